```python
import jax, jax.numpy as jnp
from jax import lax
import numpy as np

D_MODEL = 1024
BATCH = 8
SEQ = 8192
DEPTH = 1

ATTN_HEADS = 8
ATTN_KV_HEADS = 2
ATTN_HEAD_DIM = 64
ATTN_W = ATTN_HEADS * ATTN_HEAD_DIM
IDX_HEADS = 8
IDX_DIM = 64
TOPK_MAX = 256
Q_BLOCK = 128
GLA_HEADS = 4
GLA_DK = 64
GLA_DV = 128
GLA_W = GLA_HEADS * GLA_DV
GLA_GATE_RANK = 16
GLA_TAU = 16.0
GLA_CHUNK = 64
MIX_W = ATTN_W + GLA_W
D_FF = 2816
EPS = 1e-6

IN_SPLITS = (
    ATTN_HEADS * ATTN_HEAD_DIM,
    ATTN_KV_HEADS * ATTN_HEAD_DIM,
    ATTN_KV_HEADS * ATTN_HEAD_DIM,
    IDX_HEADS * IDX_DIM,
    IDX_DIM,
    IDX_HEADS,
    GLA_HEADS * GLA_DK,
    GLA_HEADS * GLA_DK,
    GLA_HEADS * GLA_DV,
    GLA_GATE_RANK,
    GLA_HEADS * GLA_DV,
)
IN_COLS = sum(IN_SPLITS)

kernel_name = "hybrid_dsa_gla_macaron_sandwich"


def rmsnorm(x, g):
    xf = x.astype(jnp.float32)
    y = xf * lax.rsqrt(jnp.mean(xf * xf, axis=-1, keepdims=True) + EPS)
    return (y * g.astype(jnp.float32)).astype(x.dtype)


def swiglu(x, w_gate, w_up, w_down):
    return (jax.nn.silu(x @ w_gate) * (x @ w_up)) @ w_down


def dsa_attention(q, k, v, iq, ik, iw):
    B, S = q.shape[0], q.shape[1]
    topk = min(TOPK_MAX, S // 4)
    nb = S // Q_BLOCK
    rep = ATTN_HEADS // ATTN_KV_HEADS
    key_pos = jnp.arange(S, dtype=jnp.int32)

    def to_blocks(t):
        return t.reshape((B, nb, Q_BLOCK) + t.shape[2:]).swapaxes(0, 1)

    def one_block(args):
        qb, iqb, iwb, q0 = args
        qpos = q0 + jnp.arange(Q_BLOCK, dtype=jnp.int32)
        causal = key_pos[None, :] <= qpos[:, None]
        logits = jnp.einsum('bthd,bsd->bths', iqb, ik,
                            preferred_element_type=jnp.float32) * (IDX_DIM ** -0.5)
        score = jnp.einsum('bths,bth->bts', jax.nn.relu(logits), iwb.astype(jnp.float32))
        score = jnp.where(causal[None], score, -jnp.inf)
        _, idx = lax.top_k(score, topk)
        valid = idx <= qpos[None, :, None]
        ks = jax.vmap(lambda kk, ii: kk[ii])(k, idx)
        vs = jax.vmap(lambda vv, ii: vv[ii])(v, idx)
        qg = qb.reshape(B, Q_BLOCK, ATTN_KV_HEADS, rep, ATTN_HEAD_DIM)
        s = jnp.einsum('btgrd,btkgd->btgrk', qg, ks,
                       preferred_element_type=jnp.float32) * (ATTN_HEAD_DIM ** -0.5)
        s = jnp.where(valid[:, :, None, None, :], s, -jnp.inf)
        p = jax.nn.softmax(s, axis=-1).astype(vs.dtype)
        o = jnp.einsum('btgrk,btkgd->btgrd', p, vs)
        return o.reshape(B, Q_BLOCK, ATTN_W)

    q_starts = jnp.arange(nb, dtype=jnp.int32) * Q_BLOCK
    out = lax.map(one_block, (to_blocks(q), to_blocks(iq), to_blocks(iw), q_starts))
    return out.swapaxes(0, 1).reshape(B, S, ATTN_W)


def gla(q, k, v, log_a, g, norm_g):
    B, S = q.shape[0], q.shape[1]
    nc = S // GLA_CHUNK
    out_dtype = v.dtype

    def chunk(t):
        return t.astype(jnp.float32).reshape(B, nc, GLA_CHUNK, GLA_HEADS, -1).transpose(0, 3, 1, 2, 4)

    qc = chunk(q) * (GLA_DK ** -0.5)
    kc, vc, ac = chunk(k), chunk(v), chunk(log_a)
    b = jnp.cumsum(ac, axis=3)
    b_last = b[:, :, :, -1:, :]
    q_dec = qc * jnp.exp(b)
    k_in = kc * jnp.exp(-b)
    k_out = kc * jnp.exp(b_last - b)
    tri = jnp.tril(jnp.ones((GLA_CHUNK, GLA_CHUNK), dtype=bool))
    A = jnp.where(tri, jnp.einsum('bhncd,bhnsd->bhncs', q_dec, k_in), 0.0)
    o_intra = jnp.einsum('bhncs,bhnsv->bhncv', A, vc)
    upd = jnp.einsum('bhnsd,bhnsv->bhndv', k_out, vc)
    decay = jnp.exp(b_last[:, :, :, 0, :])

    def step(state, inp):
        dec, u = inp
        return dec[..., None] * state + u, state

    init = jnp.zeros((B, GLA_HEADS, GLA_DK, GLA_DV), jnp.float32)
    _, s_prev = lax.scan(step, init, (decay.transpose(2, 0, 1, 3), upd.transpose(2, 0, 1, 3, 4)))
    s_prev = s_prev.transpose(1, 2, 0, 3, 4)
    o_inter = jnp.einsum('bhncd,bhndv->bhncv', q_dec, s_prev)
    o = (o_intra + o_inter).transpose(0, 2, 3, 1, 4).reshape(B, S, GLA_HEADS, GLA_DV)
    o = rmsnorm(o, norm_g).reshape(B, S, GLA_W) * jax.nn.silu(g.astype(jnp.float32))
    return o.astype(out_dtype)


def hybrid_mixer(h, w_in, w_gla_a2, b_gla_a, g_gla_norm, w_out):
    B, S = h.shape[0], h.shape[1]
    proj = h @ w_in
    offsets = [int(o) for o in np.cumsum(IN_SPLITS)[:-1]]
    (aq, ak, av, iq, ik, iw, gq, gk, gv, ga, gg) = jnp.split(proj, offsets, axis=-1)
    aq = aq.reshape(B, S, ATTN_HEADS, ATTN_HEAD_DIM)
    ak = ak.reshape(B, S, ATTN_KV_HEADS, ATTN_HEAD_DIM)
    av = av.reshape(B, S, ATTN_KV_HEADS, ATTN_HEAD_DIM)
    iq = iq.reshape(B, S, IDX_HEADS, IDX_DIM)
    iw = iw * (IDX_HEADS ** -0.5)
    o_attn = dsa_attention(aq, ak, av, iq, ik, iw)
    gate_logit = (ga @ w_gla_a2 + b_gla_a).astype(jnp.float32)
    log_a = jax.nn.log_sigmoid(gate_logit) / GLA_TAU
    o_gla = gla(gq.reshape(B, S, GLA_HEADS, GLA_DK), gk.reshape(B, S, GLA_HEADS, GLA_DK),
                gv.reshape(B, S, GLA_HEADS, GLA_DV), log_a.reshape(B, S, GLA_HEADS, GLA_DK),
                gg, g_gla_norm)
    return jnp.concatenate([o_attn, o_gla.astype(o_attn.dtype)], axis=-1) @ w_out


def setup_inputs(seed: int = 0) -> dict:
    key = jax.random.key(seed)
    ks = jax.random.split(key, 24)
    f32 = jnp.float32

    def w(k, shape, fan_in):
        return jax.random.normal(k, shape, f32) * (fan_in ** -0.5)

    def gain(k, n):
        return 1.0 + 0.05 * jax.random.normal(k, (DEPTH, n), f32)

    return {
        "x": jax.random.normal(ks[0], (BATCH, SEQ, D_MODEL), f32),
        "g_ffn1_pre": gain(ks[1], D_MODEL),
        "w_ffn1_gate": w(ks[2], (DEPTH, D_MODEL, D_FF), D_MODEL),
        "w_ffn1_up": w(ks[3], (DEPTH, D_MODEL, D_FF), D_MODEL),
        "w_ffn1_down": w(ks[4], (DEPTH, D_FF, D_MODEL), D_FF),
        "g_ffn1_post": gain(ks[5], D_MODEL),
        "g_mix_pre": gain(ks[6], D_MODEL),
        "w_in": w(ks[7], (DEPTH, D_MODEL, IN_COLS), D_MODEL),
        "w_gla_a2": w(ks[8], (DEPTH, GLA_GATE_RANK, GLA_HEADS * GLA_DK), GLA_GATE_RANK),
        "b_gla_a": 0.1 * jax.random.normal(ks[9], (DEPTH, GLA_HEADS * GLA_DK), f32),
        "g_gla_norm": gain(ks[10], GLA_DV),
        "w_out": w(ks[11], (DEPTH, MIX_W, D_MODEL), MIX_W),
        "g_mix_post": gain(ks[12], D_MODEL),
        "g_ffn2_pre": gain(ks[13], D_MODEL),
        "w_ffn2_gate": w(ks[14], (DEPTH, D_MODEL, D_FF), D_MODEL),
        "w_ffn2_up": w(ks[15], (DEPTH, D_MODEL, D_FF), D_MODEL),
        "w_ffn2_down": w(ks[16], (DEPTH, D_FF, D_MODEL), D_FF),
        "g_ffn2_post": gain(ks[17], D_MODEL),
    }


def reference(x, g_ffn1_pre, w_ffn1_gate, w_ffn1_up, w_ffn1_down, g_ffn1_post,
              g_mix_pre, w_in, w_gla_a2, b_gla_a, g_gla_norm, w_out, g_mix_post,
              g_ffn2_pre, w_ffn2_gate, w_ffn2_up, w_ffn2_down, g_ffn2_post):
    for l in range(DEPTH):
        f = swiglu(rmsnorm(x, g_ffn1_pre[l]), w_ffn1_gate[l], w_ffn1_up[l], w_ffn1_down[l])
        x = x + 0.5 * rmsnorm(f, g_ffn1_post[l])
        m = hybrid_mixer(rmsnorm(x, g_mix_pre[l]), w_in[l], w_gla_a2[l], b_gla_a[l],
                         g_gla_norm[l], w_out[l])
        x = x + rmsnorm(m, g_mix_post[l])
        f = swiglu(rmsnorm(x, g_ffn2_pre[l]), w_ffn2_gate[l], w_ffn2_up[l], w_ffn2_down[l])
        x = x + 0.5 * rmsnorm(f, g_ffn2_post[l])
    return x
```

```python
import functools

import jax
import jax.numpy as jnp
from jax import lax
from jax.experimental import pallas as pl
from jax.experimental.pallas import tpu as pltpu

ATTN_HEADS = 8
ATTN_KV_HEADS = 2
ATTN_HEAD_DIM = 64
ATTN_REP = ATTN_HEADS // ATTN_KV_HEADS
IDX_HEADS = 8
IDX_DIM = 64
TOPK_MAX = 256
GLA_HEADS = 4
GLA_DK = 64
GLA_DV = 128
GLA_GATE_RANK = 16
GLA_TAU = 16.0
GLA_CHUNK = 64
EPS = 1e-6

V7X_LANES = 128
V7X_MXU_DIM = 256
V7X_VMEM_BYTES = 64 * 2**20

FFN_ROWS = 512
FF_CHUNK = V7X_MXU_DIM
DSA_TQ = 256
DSA_ROWBLK = 128
GLA_ROWS = 256

INT_MIN = -2**31
NEG_BIG = -1e30

F32 = jnp.float32
BF16 = jnp.bfloat16
NT_DIMS = (((1,), (1,)), ((), ()))
TN_DIMS = (((0,), (0,)), ((), ()))


def _vmem_limit(nbytes):
    return int(min(nbytes * 1.25 + (8 << 20), V7X_VMEM_BYTES - (6 << 20)))


def _rms(x, g):
    return x * lax.rsqrt(jnp.mean(x * x, axis=-1, keepdims=True) + EPS) * g


def _const_spec(shape):
    nd = len(shape)
    return pl.BlockSpec(shape, lambda *_: (0,) * nd, pipeline_mode=pl.Buffered(1))


def _ffn_kernel(x_ref, gpre_ref, wg_ref, wu_ref, wd_ref, gpost_ref, o_ref, acc_ref):
    x = x_ref[...]
    xn = _rms(x, gpre_ref[...]).astype(BF16)
    acc_ref[...] = jnp.zeros_like(acc_ref)

    def body(c, carry):
        g = jnp.dot(xn, wg_ref[c], preferred_element_type=F32)
        u = jnp.dot(xn, wu_ref[c], preferred_element_type=F32)
        a = (g * jax.nn.sigmoid(g) * u).astype(BF16)
        acc_ref[...] += jnp.dot(a, wd_ref[c], preferred_element_type=F32)
        return carry

    lax.fori_loop(0, wg_ref.shape[0], body, 0)
    o_ref[...] = x + 0.5 * _rms(acc_ref[...], gpost_ref[...])


def _ffn(x, g_pre, w_gate, w_up, w_down, g_post):
    n, d = x.shape
    dff = w_gate.shape[1]
    nch = dff // FF_CHUNK
    assert nch * FF_CHUNK == dff and n % FFN_ROWS == 0
    wg = w_gate.reshape(d, nch, FF_CHUNK).transpose(1, 0, 2).astype(BF16)
    wu = w_up.reshape(d, nch, FF_CHUNK).transpose(1, 0, 2).astype(BF16)
    wd = w_down.reshape(nch, FF_CHUNK, d).astype(BF16)
    row = lambda i: (i, 0)
    est = 4 * FFN_ROWS * d * 4 + 3 * d * dff * 2 + FFN_ROWS * d * 8 + 4 * FFN_ROWS * FF_CHUNK * 4
    return pl.pallas_call(
        _ffn_kernel,
        out_shape=jax.ShapeDtypeStruct((n, d), F32),
        grid=(n // FFN_ROWS,),
        in_specs=[
            pl.BlockSpec((FFN_ROWS, d), row),
            _const_spec((1, d)),
            _const_spec(wg.shape), _const_spec(wu.shape), _const_spec(wd.shape),
            _const_spec((1, d)),
        ],
        out_specs=pl.BlockSpec((FFN_ROWS, d), row),
        scratch_shapes=[pltpu.VMEM((FFN_ROWS, d), F32)],
        compiler_params=pltpu.CompilerParams(
            dimension_semantics=("arbitrary",), vmem_limit_bytes=_vmem_limit(est)),
    )(x, g_pre.reshape(1, d), wg, wu, wd, g_post.reshape(1, d))


_Q_OFF, _IQ_OFF, _K_OFF, _V_OFF, _MISC_OFF = 0, 512, 1024, 1152, 1280
_GQ_OFF, _GK_OFF, _GV_OFF, _GG_OFF, _PROJ_COLS = 1408, 1664, 1920, 2432, 3072
_IW_LANE, _GA_LANE = 64, 72


def _inproj_kernel(x_ref, g_ref, w_ref, wa2_ref, ba_ref,
                   qh_ref, iqh_ref, kk_ref, va_ref, ik_ref, iw_ref, gqk_ref, gv_ref, gg_ref, la_ref):
    h = _rms(x_ref[...], g_ref[...]).astype(BF16)
    proj = jnp.dot(h, w_ref[...], preferred_element_type=F32)
    for i in range(ATTN_HEADS):
        qh_ref[i] = proj[:, _Q_OFF + 64 * i:_Q_OFF + 64 * (i + 1)].astype(BF16)
    for i in range(IDX_HEADS):
        iqh_ref[i] = proj[:, _IQ_OFF + 64 * i:_IQ_OFF + 64 * (i + 1)].astype(BF16)
    lane = lax.broadcasted_iota(jnp.int32, (1, V7X_LANES), 1)
    ones_col = (lane == ATTN_HEAD_DIM).astype(F32)
    for g in range(ATTN_KV_HEADS):
        kk_ref[g] = proj[:, _K_OFF + 64 * g:_K_OFF + 64 * (g + 1)].astype(BF16)
        v = proj[:, _V_OFF + 64 * g:_V_OFF + 64 * (g + 1)]
        va = jnp.concatenate([v, jnp.zeros_like(v)], axis=-1) + ones_col
        va_ref[g] = va.astype(BF16)
    misc = proj[:, _MISC_OFF:_MISC_OFF + V7X_LANES]
    ik_ref[...] = misc[:, :IDX_DIM].astype(BF16)
    iw_ref[...] = misc[:, _IW_LANE:_IW_LANE + IDX_HEADS] * (IDX_HEADS ** -0.5)
    gqk_ref[...] = proj[:, _GQ_OFF:_GV_OFF]
    gv_ref[...] = proj[:, _GV_OFF:_GG_OFF].astype(BF16)
    gg_ref[...] = proj[:, _GG_OFF:_GG_OFF + GLA_HEADS * GLA_DV]
    z = jnp.dot(misc, wa2_ref[...], preferred_element_type=F32, precision=lax.Precision.HIGHEST)
    z = z + ba_ref[...]
    log_sig = jnp.minimum(z, 0.0) - jnp.log1p(jnp.exp(-jnp.abs(z)))
    la_ref[...] = log_sig * (1.0 / GLA_TAU)


def _pack_w_in(w_in):
    d = w_in.shape[0]
    sizes = (512, 128, 128, 512, 64, 8, 256, 256, 512, 16, 512)
    offs = [0]
    for s in sizes:
        offs.append(offs[-1] + s)
    aq, ak, av, iq, ik, iw, gq, gk, gv, ga, gg = [w_in[:, offs[i]:offs[i + 1]] for i in range(len(sizes))]
    z = lambda n: jnp.zeros((d, n), w_in.dtype)
    cols = [aq * (ATTN_HEAD_DIM ** -0.5), iq * (IDX_DIM ** -0.5), ak, av, ik, iw, ga, z(40),
            gq * (GLA_DK ** -0.5), gk, gv, gg, z(128)]
    w = jnp.concatenate(cols, axis=1)
    assert w.shape[1] == _PROJ_COLS
    return w.astype(BF16)


def _inproj(x1, g_mix_pre, w_in, w_gla_a2, b_gla_a):
    n, d = x1.shape
    t = FFN_ROWS
    w = _pack_w_in(w_in)
    nqk = GLA_HEADS * GLA_DK
    wa2 = jnp.zeros((V7X_LANES, nqk), F32).at[_GA_LANE:_GA_LANE + GLA_GATE_RANK].set(w_gla_a2)
    row = lambda i: (i, 0)
    row3 = lambda i: (0, i, 0)
    out_shape = (
        jax.ShapeDtypeStruct((ATTN_HEADS, n, ATTN_HEAD_DIM), BF16),
        jax.ShapeDtypeStruct((IDX_HEADS, n, IDX_DIM), BF16),
        jax.ShapeDtypeStruct((ATTN_KV_HEADS, n, ATTN_HEAD_DIM), BF16),
        jax.ShapeDtypeStruct((ATTN_KV_HEADS, n, V7X_LANES), BF16),
        jax.ShapeDtypeStruct((n, IDX_DIM), BF16),
        jax.ShapeDtypeStruct((n, IDX_HEADS), F32),
        jax.ShapeDtypeStruct((n, 2 * nqk), F32),
        jax.ShapeDtypeStruct((n, GLA_HEADS * GLA_DV), BF16),
        jax.ShapeDtypeStruct((n, GLA_HEADS * GLA_DV), F32),
        jax.ShapeDtypeStruct((n, nqk), F32),
    )
    out_specs = (
        pl.BlockSpec((ATTN_HEADS, t, ATTN_HEAD_DIM), row3),
        pl.BlockSpec((IDX_HEADS, t, IDX_DIM), row3),
        pl.BlockSpec((ATTN_KV_HEADS, t, ATTN_HEAD_DIM), row3),
        pl.BlockSpec((ATTN_KV_HEADS, t, V7X_LANES), row3),
        pl.BlockSpec((t, IDX_DIM), row),
        pl.BlockSpec((t, IDX_HEADS), row),
        pl.BlockSpec((t, 2 * nqk), row),
        pl.BlockSpec((t, GLA_HEADS * GLA_DV), row),
        pl.BlockSpec((t, GLA_HEADS * GLA_DV), row),
        pl.BlockSpec((t, nqk), row),
    )
    est = 2 * t * d * 4 + d * _PROJ_COLS * 2 + t * _PROJ_COLS * 4 * 2 + 2 * t * 8 * 1024
    return pl.pallas_call(
        _inproj_kernel,
        out_shape=out_shape,
        grid=(n // t,),
        in_specs=[
            pl.BlockSpec((t, d), row),
            _const_spec((1, d)),
            _const_spec(w.shape),
            _const_spec(wa2.shape),
            _const_spec((1, nqk)),
        ],
        out_specs=out_specs,
        compiler_params=pltpu.CompilerParams(
            dimension_semantics=("arbitrary",), vmem_limit_bytes=_vmem_limit(est)),
    )(x1, g_mix_pre.reshape(1, d), w, wa2, b_gla_a.reshape(1, nqk))


def _dsa_kernel(qh_ref, iqh_ref, iw_ref, kk_ref, va_ref, ik_ref, o_ref,
                key_s, iwb_s, bias_s, m_s, acc_s, *, topk):
    tq = DSA_TQ
    kc = DSA_TQ
    i = pl.program_id(1)
    q0 = i * tq
    nchunks = i + 1

    row = lax.broadcasted_iota(jnp.int32, (tq, kc), 0)
    col = lax.broadcasted_iota(jnp.int32, (tq, kc), 1)

    iw = iw_ref[...]
    for h in range(IDX_HEADS):
        iwb_s[h] = jnp.broadcast_to(iw[:, h:h + 1], (tq, kc))

    def score_chunk(j, diagonal):
        ikc = ik_ref[pl.ds(pl.multiple_of(j * kc, kc), kc), :]
        sc = jnp.zeros((tq, kc), F32)
        for h in range(IDX_HEADS):
            lg = lax.dot_general(iqh_ref[h], ikc, NT_DIMS, preferred_element_type=F32)
            sc = sc + jnp.maximum(lg, 0.0) * iwb_s[h]
        bits = pltpu.bitcast(sc, jnp.int32)
        key = jnp.where(bits < 0, bits ^ 0x7FFFFFFF, bits)
        if diagonal:
            key = jnp.where(col <= row, key, INT_MIN)
        key_s[j] = key

    def score_body(j, c):
        score_chunk(j, False)
        return c

    lax.fori_loop(0, i, score_body, 0)
    score_chunk(i, True)

    rb = DSA_ROWBLK
    nrb = tq // rb
    qpos = q0 + lax.broadcasted_iota(jnp.int32, (rb, 1), 0)
    krow = [jnp.minimum(topk, qpos + r * rb + 1).astype(F32) for r in range(nrb)]

    def count(ind_fn, r):
        def body(j, acc):
            c = ind_fn(key_s[j, r * rb:(r + 1) * rb, :], j)
            return acc + c[:, :V7X_LANES] + c[:, V7X_LANES:]
        acc = lax.fori_loop(0, nchunks, body, jnp.zeros((rb, V7X_LANES), F32))
        return jnp.sum(acc, axis=-1, keepdims=True)

    def count_ge(cand, r):
        cb = jnp.broadcast_to(cand, (rb, kc))
        return count(lambda k, j: jnp.where(k >= cb, 1.0, 0.0), r)

    t = []
    for r in range(nrb):
        zero = jnp.zeros((rb, 1), jnp.int32)
        t.append(jnp.where(count_ge(zero, r) >= krow[r], zero, INT_MIN))

    def bit_body(it, ts):
        bit = lax.shift_left(jnp.int32(1), 30 - it)
        out = []
        for r in range(nrb):
            cand = ts[r] | bit
            out.append(jnp.where(count_ge(cand, r) >= krow[r], cand, ts[r]))
        return tuple(out)

    t = list(lax.fori_loop(0, 31, bit_body, tuple(t)))

    need = []
    excess = jnp.zeros((rb, 1), F32)
    for r in range(nrb):
        c_ge = count_ge(t[r], r)
        c_gt = count_ge(t[r] + 1, r)
        need.append(krow[r] - c_gt)
        excess = jnp.maximum(excess, c_ge - krow[r])

    @pl.when(jnp.max(excess) > 0.5)
    def _():
        for r in range(nrb):
            tb = jnp.broadcast_to(t[r], (rb, kc))
            colr = lax.broadcasted_iota(jnp.int32, (rb, kc), 1)

            def tied_below(u):
                ub = jnp.broadcast_to(u, (rb, kc))
                return count(
                    lambda k, j: jnp.where(k == tb, jnp.where(colr + j * kc < ub, 1.0, 0.0), 0.0), r)

            def ubit(it, u):
                cand = u | lax.shift_left(jnp.int32(1), 30 - it)
                return jnp.where(tied_below(cand) < need[r], cand, u)

            u = lax.fori_loop(0, 31, ubit, jnp.zeros((rb, 1), jnp.int32))
            ub = jnp.broadcast_to(u, (rb, kc))

            def drop(j, c):
                k = key_s[j, r * rb:(r + 1) * rb, :]
                k = jnp.where(k == tb, jnp.where(colr + j * kc > ub, INT_MIN, k), k)
                key_s[j, r * rb:(r + 1) * rb, :] = k
                return c

            lax.fori_loop(0, nchunks, drop, 0)

    m_s[...] = jnp.full_like(m_s, NEG_BIG)
    acc_s[...] = jnp.zeros_like(acc_s)
    tfull = jnp.concatenate(t, axis=0)
    tb_full = jnp.broadcast_to(tfull, (tq, kc))

    def att_body(j, c):
        koff = pl.multiple_of(j * kc, kc)
        bias_s[...] = jnp.where(key_s[j] >= tb_full, 0.0, NEG_BIG)
        for g in range(ATTN_KV_HEADS):
            kg = kk_ref[g, pl.ds(koff, kc), :]
            vg = va_ref[g, pl.ds(koff, kc), :]
            for rr in range(ATTN_REP):
                h = g * ATTN_REP + rr
                s = lax.dot_general(qh_ref[h], kg, NT_DIMS, preferred_element_type=F32) + bias_s[...]
                m_old = m_s[h]
                m_new = jnp.maximum(m_old, jnp.max(s, axis=-1, keepdims=True))
                p = jnp.exp(s - m_new)
                pv = jnp.dot(p.astype(BF16), vg, preferred_element_type=F32)
                acc_s[h] = jnp.exp(m_old - m_new) * acc_s[h] + pv
                m_s[h] = m_new
        return c

    lax.fori_loop(0, nchunks, att_body, 0)

    for h in range(ATTN_HEADS):
        a = acc_s[h]
        o = a[:, :ATTN_HEAD_DIM] / a[:, ATTN_HEAD_DIM:ATTN_HEAD_DIM + 1]
        o_ref[:, h * ATTN_HEAD_DIM:(h + 1) * ATTN_HEAD_DIM] = o.astype(o_ref.dtype)


def _dsa(qh, iqh, iw, kk, va, ik, batch, seq):
    n = batch * seq
    tq = DSA_TQ
    assert seq % tq == 0
    nq = seq // tq
    topk = min(TOPK_MAX, seq // 4)
    nch = seq // tq
    qmap = lambda b, i: (0, b * nq + i, 0)
    bmap = lambda b, i: (0, b, 0)
    est = (nch * tq * tq * 4 + IDX_HEADS * tq * tq * 4 + tq * tq * 4 + 2 * ATTN_HEADS * tq * V7X_LANES * 4
           + 2 * seq * V7X_LANES * 2 * 2 + seq * V7X_LANES * 2
           + 2 * (2 * ATTN_HEADS * tq * V7X_LANES * 2 + tq * V7X_LANES * 4 + tq * 512 * 2))
    return pl.pallas_call(
        functools.partial(_dsa_kernel, topk=topk),
        out_shape=jax.ShapeDtypeStruct((n, ATTN_HEADS * ATTN_HEAD_DIM), BF16),
        grid=(batch, nq),
        in_specs=[
            pl.BlockSpec((ATTN_HEADS, tq, ATTN_HEAD_DIM), qmap),
            pl.BlockSpec((IDX_HEADS, tq, IDX_DIM), qmap),
            pl.BlockSpec((tq, IDX_HEADS), lambda b, i: (b * nq + i, 0)),
            pl.BlockSpec((ATTN_KV_HEADS, seq, ATTN_HEAD_DIM), bmap, pipeline_mode=pl.Buffered(1)),
            pl.BlockSpec((ATTN_KV_HEADS, seq, V7X_LANES), bmap, pipeline_mode=pl.Buffered(1)),
            pl.BlockSpec((seq, IDX_DIM), lambda b, i: (b, 0), pipeline_mode=pl.Buffered(1)),
        ],
        out_specs=pl.BlockSpec((tq, ATTN_HEADS * ATTN_HEAD_DIM), lambda b, i: (b * nq + i, 0)),
        scratch_shapes=[
            pltpu.VMEM((nch, tq, tq), jnp.int32),
            pltpu.VMEM((IDX_HEADS, tq, tq), F32),
            pltpu.VMEM((tq, tq), F32),
            pltpu.VMEM((ATTN_HEADS, tq, 1), F32),
            pltpu.VMEM((ATTN_HEADS, tq, V7X_LANES), F32),
        ],
        compiler_params=pltpu.CompilerParams(
            dimension_semantics=("arbitrary", "arbitrary"), vmem_limit_bytes=_vmem_limit(est)),
    )(qh, iqh, iw, kk, va, ik)


def _gla_kernel(gqk_ref, gv_ref, la_ref, gg_ref, gn_ref, o_ref, st_s):
    @pl.when(pl.program_id(1) == 0)
    def _():
        st_s[...] = jnp.zeros_like(st_s)

    c = GLA_CHUNK
    nqk = GLA_HEADS * GLA_DK
    r_i = lax.broadcasted_iota(jnp.int32, (c, c), 0)
    c_i = lax.broadcasted_iota(jnp.int32, (c, c), 1)
    tri = r_i >= c_i
    tri_f = tri.astype(F32)
    gn = gn_ref[...]
    for ci in range(GLA_ROWS // c):
        rows = slice(ci * c, (ci + 1) * c)
        la = la_ref[rows, :]
        b = jnp.dot(tri_f, la, preferred_element_type=F32, precision=lax.Precision.HIGHEST)
        b_last = b[c - 1:c, :]
        q = gqk_ref[rows, :nqk]
        k = gqk_ref[rows, nqk:]
        q_dec = (q * jnp.exp(b)).astype(BF16)
        k_in = (k * jnp.exp(-b)).astype(BF16)
        k_out = (k * jnp.exp(b_last - b)).astype(BF16)
        decay = jnp.exp(b_last)
        for h in range(GLA_HEADS):
            hs = slice(h * GLA_DK, (h + 1) * GLA_DK)
            vs = slice(h * GLA_DV, (h + 1) * GLA_DV)
            qd, ki, ko = q_dec[:, hs], k_in[:, hs], k_out[:, hs]
            v = gv_ref[rows, vs]
            a = lax.dot_general(qd, ki, NT_DIMS, preferred_element_type=F32)
            a = jnp.where(tri, a, 0.0).astype(BF16)
            st = st_s[h]
            o = jnp.dot(a, v, preferred_element_type=F32)
            o = o + lax.dot_general(qd, st.astype(BF16), NT_DIMS, preferred_element_type=F32)
            upd = lax.dot_general(v, ko, TN_DIMS, preferred_element_type=F32)
            st_s[h] = st * decay[:, hs] + upd
            gate = gg_ref[rows, vs]
            o_ref[rows, vs] = (_rms(o, gn) * (gate * jax.nn.sigmoid(gate))).astype(o_ref.dtype)


def _gla(gqk, gv, la, gg, g_norm, batch, seq):
    n = batch * seq
    t = GLA_ROWS
    assert seq % t == 0
    ns = seq // t
    row = lambda b, i: (b * ns + i, 0)
    nqk = GLA_HEADS * GLA_DK
    nv = GLA_HEADS * GLA_DV
    return pl.pallas_call(
        _gla_kernel,
        out_shape=jax.ShapeDtypeStruct((n, nv), BF16),
        grid=(batch, ns),
        in_specs=[
            pl.BlockSpec((t, 2 * nqk), row),
            pl.BlockSpec((t, nv), row),
            pl.BlockSpec((t, nqk), row),
            pl.BlockSpec((t, nv), row),
            _const_spec((1, GLA_DV)),
        ],
        out_specs=pl.BlockSpec((t, nv), row),
        scratch_shapes=[pltpu.VMEM((GLA_HEADS, GLA_DV, GLA_DK), F32)],
        compiler_params=pltpu.CompilerParams(dimension_semantics=("arbitrary", "arbitrary")),
    )(gqk, gv, la, gg, g_norm.reshape(1, GLA_DV))


def _outproj_kernel(x_ref, oa_ref, og_ref, w_ref, g_ref, o_ref):
    m = jnp.dot(oa_ref[...], w_ref[0], preferred_element_type=F32)
    m = m + jnp.dot(og_ref[...], w_ref[1], preferred_element_type=F32)
    o_ref[...] = x_ref[...] + _rms(m, g_ref[...])


def _outproj(x1, oa, og, w_out, g_post):
    n, d = x1.shape
    t = FFN_ROWS
    half = oa.shape[1]
    assert og.shape[1] == half and w_out.shape[0] == 2 * half
    w = w_out.reshape(2, half, d).astype(BF16)
    row = lambda i: (i, 0)
    est = 4 * t * d * 4 + 4 * t * half * 2 + 2 * half * d * 2 + t * d * 4
    return pl.pallas_call(
        _outproj_kernel,
        out_shape=jax.ShapeDtypeStruct((n, d), F32),
        grid=(n // t,),
        in_specs=[
            pl.BlockSpec((t, d), row),
            pl.BlockSpec((t, half), row),
            pl.BlockSpec((t, half), row),
            _const_spec(w.shape),
            _const_spec((1, d)),
        ],
        out_specs=pl.BlockSpec((t, d), row),
        compiler_params=pltpu.CompilerParams(
            dimension_semantics=("arbitrary",), vmem_limit_bytes=_vmem_limit(est)),
    )(x1, oa, og, w, g_post.reshape(1, d))


def kernel(x, g_ffn1_pre, w_ffn1_gate, w_ffn1_up, w_ffn1_down, g_ffn1_post, g_mix_pre, w_in, w_gla_a2,
           b_gla_a, g_gla_norm, w_out, g_mix_post, g_ffn2_pre, w_ffn2_gate, w_ffn2_up, w_ffn2_down,
           g_ffn2_post):
    batch, seq, d = x.shape
    h = x.reshape(batch * seq, d)
    for l in range(g_ffn1_pre.shape[0]):
        h = _ffn(h, g_ffn1_pre[l], w_ffn1_gate[l], w_ffn1_up[l], w_ffn1_down[l], g_ffn1_post[l])
        qh, iqh, kk, va, ik, iw, gqk, gv, gg, la = _inproj(h, g_mix_pre[l], w_in[l], w_gla_a2[l], b_gla_a[l])
        oa = _dsa(qh, iqh, iw, kk, va, ik, batch, seq)
        og = _gla(gqk, gv, la, gg, g_gla_norm[l], batch, seq)
        h = _outproj(h, oa, og, w_out[l], g_mix_post[l])
        h = _ffn(h, g_ffn2_pre[l], w_ffn2_gate[l], w_ffn2_up[l], w_ffn2_down[l], g_ffn2_post[l])
    return h.reshape(batch, seq, d)
```

```python
import functools

import jax
import jax.numpy as jnp
from jax import lax
from jax.experimental import pallas as pl
from jax.experimental.pallas import tpu as pltpu

ATTN_HEADS = 8
ATTN_KV_HEADS = 2
ATTN_HEAD_DIM = 64
ATTN_REP = ATTN_HEADS // ATTN_KV_HEADS
IDX_HEADS = 8
IDX_DIM = 64
TOPK_MAX = 256
GLA_HEADS = 4
GLA_DK = 64
GLA_DV = 128
GLA_GATE_RANK = 16
GLA_TAU = 16.0
GLA_CHUNK = 64
EPS = 1e-6

V7X_LANES = 128
V7X_SUBLANES = 8
V7X_BF16_ROWS = 16
V7X_MXU_DIM = 256
V7X_VMEM_BYTES = 64 * 2**20

FFN_ROWS = 512
FF_CHUNK = V7X_MXU_DIM
DSA_TQ = 256
DSA_KC = DSA_TQ
GLA_ROWS = 256
VT_ROWS = ATTN_HEAD_DIM + V7X_BF16_ROWS

INT_MIN = -2**31
NEG_BIG = -1e30

F32 = jnp.float32
BF16 = jnp.bfloat16
NT_DIMS = (((1,), (1,)), ((), ()))
TN_DIMS = (((0,), (0,)), ((), ()))


def _vmem_limit(nbytes):
    return int(min(nbytes * 1.25 + (8 << 20), V7X_VMEM_BYTES - (6 << 20)))


def _rms(x, g):
    return x * lax.rsqrt(jnp.mean(x * x, axis=-1, keepdims=True) + EPS) * g


def _const_spec(shape):
    nd = len(shape)
    return pl.BlockSpec(shape, lambda *_: (0,) * nd, pipeline_mode=pl.Buffered(1))


def _ffn_kernel(x_ref, gpre_ref, wg_ref, wu_ref, wd_ref, gpost_ref, o_ref, acc_ref):
    x = x_ref[...]
    xn = _rms(x, gpre_ref[...]).astype(BF16)
    acc_ref[...] = jnp.zeros_like(acc_ref)

    def body(c, carry):
        g = jnp.dot(xn, wg_ref[c], preferred_element_type=F32)
        u = jnp.dot(xn, wu_ref[c], preferred_element_type=F32)
        a = (g * jax.nn.sigmoid(g) * u).astype(BF16)
        acc_ref[...] += jnp.dot(a, wd_ref[c], preferred_element_type=F32)
        return carry

    lax.fori_loop(0, wg_ref.shape[0], body, 0)
    o_ref[...] = x + 0.5 * _rms(acc_ref[...], gpost_ref[...])


def _ffn(x, g_pre, w_gate, w_up, w_down, g_post):
    n, d = x.shape
    dff = w_gate.shape[1]
    nch = dff // FF_CHUNK
    assert nch * FF_CHUNK == dff and n % FFN_ROWS == 0
    wg = w_gate.reshape(d, nch, FF_CHUNK).transpose(1, 0, 2).astype(BF16)
    wu = w_up.reshape(d, nch, FF_CHUNK).transpose(1, 0, 2).astype(BF16)
    wd = w_down.reshape(nch, FF_CHUNK, d).astype(BF16)
    row = lambda i: (i, 0)
    est = 4 * FFN_ROWS * d * 4 + 3 * d * dff * 2 + FFN_ROWS * d * 8 + 4 * FFN_ROWS * FF_CHUNK * 4
    return pl.pallas_call(
        _ffn_kernel,
        out_shape=jax.ShapeDtypeStruct((n, d), F32),
        grid=(n // FFN_ROWS,),
        in_specs=[
            pl.BlockSpec((FFN_ROWS, d), row),
            _const_spec((1, d)),
            _const_spec(wg.shape), _const_spec(wu.shape), _const_spec(wd.shape),
            _const_spec((1, d)),
        ],
        out_specs=pl.BlockSpec((FFN_ROWS, d), row),
        scratch_shapes=[pltpu.VMEM((FFN_ROWS, d), F32)],
        compiler_params=pltpu.CompilerParams(
            dimension_semantics=("arbitrary",), vmem_limit_bytes=_vmem_limit(est)),
    )(x, g_pre.reshape(1, d), wg, wu, wd, g_post.reshape(1, d))


_K_OFF, _MISC_OFF, _GQ_OFF, _GK_OFF, _GV_OFF, _GG_OFF, _TOK_COLS = 0, 128, 256, 512, 768, 1280, 1792
_GA_LANE = 72
_QT_OFF, _IQT_OFF, _VT_OFF, _IWT_OFF, _T_ROWS = 0, 512, 1024, 1152, 1168


def _inproj_kernel(x_ref, g_ref, wtok_ref, wt_ref, wa2_ref, ba_ref,
                   qt_ref, iqt_ref, vt_ref, iwt_ref, kk_ref, ik_ref, gqk_ref, gv_ref, gg_ref, la_ref):
    h = _rms(x_ref[...], g_ref[...]).astype(BF16)
    t = h.shape[0]
    pt = lax.dot_general(wt_ref[...], h, NT_DIMS, preferred_element_type=F32)
    for i in range(ATTN_HEADS):
        qt_ref[i] = pt[_QT_OFF + 64 * i:_QT_OFF + 64 * (i + 1)].astype(BF16)
    for i in range(IDX_HEADS):
        iqt_ref[i] = pt[_IQT_OFF + 64 * i:_IQT_OFF + 64 * (i + 1)].astype(BF16)
    ones_row = (lax.broadcasted_iota(jnp.int32, (V7X_BF16_ROWS, t), 0) == 0).astype(F32)
    for g in range(ATTN_KV_HEADS):
        v_t = pt[_VT_OFF + 64 * g:_VT_OFF + 64 * (g + 1)]
        vt_ref[g] = jnp.concatenate([v_t, ones_row], axis=0).astype(BF16)
    iwt_ref[...] = pt[_IWT_OFF:_IWT_OFF + IDX_HEADS] * (IDX_HEADS ** -0.5)

    proj = jnp.dot(h, wtok_ref[...], preferred_element_type=F32)
    for g in range(ATTN_KV_HEADS):
        kk_ref[g] = proj[:, _K_OFF + 64 * g:_K_OFF + 64 * (g + 1)].astype(BF16)
    misc = proj[:, _MISC_OFF:_MISC_OFF + V7X_LANES]
    ik_ref[...] = misc[:, :IDX_DIM].astype(BF16)
    gqk_ref[...] = proj[:, _GQ_OFF:_GV_OFF]
    gv_ref[...] = proj[:, _GV_OFF:_GG_OFF].astype(BF16)
    gg_ref[...] = proj[:, _GG_OFF:_GG_OFF + GLA_HEADS * GLA_DV]
    z = jnp.dot(misc, wa2_ref[...], preferred_element_type=F32, precision=lax.Precision.HIGHEST)
    z = z + ba_ref[...]
    log_sig = jnp.minimum(z, 0.0) - jnp.log1p(jnp.exp(-jnp.abs(z)))
    la_ref[...] = log_sig * (1.0 / GLA_TAU)


def _pack_w_in(w_in):
    d = w_in.shape[0]
    sizes = (512, 128, 128, 512, 64, 8, 256, 256, 512, 16, 512)
    offs = [0]
    for s in sizes:
        offs.append(offs[-1] + s)
    aq, ak, av, iq, ik, iw, gq, gk, gv, ga, gg = [w_in[:, offs[i]:offs[i + 1]] for i in range(len(sizes))]
    z = lambda n: jnp.zeros((d, n), w_in.dtype)
    w_tok = jnp.concatenate([ak, ik, z(8), ga, z(40), gq * (GLA_DK ** -0.5), gk, gv, gg], axis=1)
    w_t = jnp.concatenate([aq * (ATTN_HEAD_DIM ** -0.5), iq * (IDX_DIM ** -0.5), av, iw, z(8)], axis=1).T
    assert w_tok.shape[1] == _TOK_COLS and w_t.shape[0] == _T_ROWS
    return w_tok.astype(BF16), w_t.astype(BF16)


def _inproj(x1, g_mix_pre, w_in, w_gla_a2, b_gla_a):
    n, d = x1.shape
    t = FFN_ROWS
    w_tok, w_t = _pack_w_in(w_in)
    nqk = GLA_HEADS * GLA_DK
    nv = GLA_HEADS * GLA_DV
    wa2 = jnp.zeros((V7X_LANES, nqk), F32).at[_GA_LANE:_GA_LANE + GLA_GATE_RANK].set(w_gla_a2)
    row = lambda i: (i, 0)
    row3 = lambda i: (0, i, 0)
    col3 = lambda i: (0, 0, i)
    out_shape = (
        jax.ShapeDtypeStruct((ATTN_HEADS, ATTN_HEAD_DIM, n), BF16),
        jax.ShapeDtypeStruct((IDX_HEADS, IDX_DIM, n), BF16),
        jax.ShapeDtypeStruct((ATTN_KV_HEADS, VT_ROWS, n), BF16),
        jax.ShapeDtypeStruct((IDX_HEADS, n), F32),
        jax.ShapeDtypeStruct((ATTN_KV_HEADS, n, ATTN_HEAD_DIM), BF16),
        jax.ShapeDtypeStruct((n, IDX_DIM), BF16),
        jax.ShapeDtypeStruct((n, 2 * nqk), F32),
        jax.ShapeDtypeStruct((n, nv), BF16),
        jax.ShapeDtypeStruct((n, nv), F32),
        jax.ShapeDtypeStruct((n, nqk), F32),
    )
    out_specs = (
        pl.BlockSpec((ATTN_HEADS, ATTN_HEAD_DIM, t), col3),
        pl.BlockSpec((IDX_HEADS, IDX_DIM, t), col3),
        pl.BlockSpec((ATTN_KV_HEADS, VT_ROWS, t), col3),
        pl.BlockSpec((IDX_HEADS, t), lambda i: (0, i)),
        pl.BlockSpec((ATTN_KV_HEADS, t, ATTN_HEAD_DIM), row3),
        pl.BlockSpec((t, IDX_DIM), row),
        pl.BlockSpec((t, 2 * nqk), row),
        pl.BlockSpec((t, nv), row),
        pl.BlockSpec((t, nv), row),
        pl.BlockSpec((t, nqk), row),
    )
    est = (2 * t * d * 4 + d * (_TOK_COLS + _T_ROWS) * 2 + t * (_TOK_COLS + _T_ROWS) * 4 * 2
           + 2 * t * 8 * 1024)
    return pl.pallas_call(
        _inproj_kernel,
        out_shape=out_shape,
        grid=(n // t,),
        in_specs=[
            pl.BlockSpec((t, d), row),
            _const_spec((1, d)),
            _const_spec(w_tok.shape),
            _const_spec(w_t.shape),
            _const_spec(wa2.shape),
            _const_spec((1, nqk)),
        ],
        out_specs=out_specs,
        compiler_params=pltpu.CompilerParams(
            dimension_semantics=("arbitrary",), vmem_limit_bytes=_vmem_limit(est)),
    )(x1, g_mix_pre.reshape(1, d), w_tok, w_t, wa2, b_gla_a.reshape(1, nqk))


def _dsa_kernel(qt_ref, iqt_ref, iwt_ref, kk_ref, vt_ref, ik_ref, o_ref,
                key_s, bias_s, acc_s, ot_s, *, topk):
    tq = DSA_TQ
    kc = DSA_KC
    sub = V7X_SUBLANES
    i = pl.program_id(1)
    nchunks = i + 1

    def score_chunk(j, diagonal):
        ikc = ik_ref[pl.ds(pl.multiple_of(j * kc, kc), kc), :]
        sc = jnp.zeros((kc, tq), F32)
        for h in range(IDX_HEADS):
            lg = jnp.dot(ikc, iqt_ref[h], preferred_element_type=F32)
            sc = sc + jnp.maximum(lg, 0.0) * iwt_ref[h:h + 1, :]
        bits = pltpu.bitcast(sc, jnp.int32)
        key = jnp.where(bits < 0, bits ^ 0x7FFFFFFF, bits)
        if diagonal:
            kpos = lax.broadcasted_iota(jnp.int32, (kc, tq), 0)
            qpos = lax.broadcasted_iota(jnp.int32, (kc, tq), 1)
            key = jnp.where(kpos <= qpos, key, INT_MIN)
        key_s[j] = key

    def score_body(j, c):
        score_chunk(j, False)
        return c

    lax.fori_loop(0, i, score_body, 0)
    score_chunk(i, True)

    qidx = i * tq + lax.broadcasted_iota(jnp.int32, (1, tq), 1)
    krow = jnp.minimum(topk, qidx + 1).astype(F32)

    def count(ind_fn):
        def body(j, acc):
            parts = [acc, None, None, None]
            for s in range(kc // sub):
                ind = ind_fn(key_s[j, s * sub:(s + 1) * sub, :], j * kc + s * sub)
                p = s % 4
                parts[p] = ind if parts[p] is None else parts[p] + ind
            return (parts[0] + parts[1]) + (parts[2] + parts[3])
        acc = lax.fori_loop(0, nchunks, body, jnp.zeros((sub, tq), F32))
        return jnp.sum(acc, axis=0, keepdims=True)

    def count_ge(cand):
        cb = jnp.broadcast_to(cand, (sub, tq))
        return count(lambda k, base: jnp.where(k >= cb, 1.0, 0.0))

    zero = jnp.zeros((1, tq), jnp.int32)
    t0 = jnp.where(count_ge(zero) >= krow, zero, INT_MIN)

    def bit_body(it, t):
        cand = t | lax.shift_left(jnp.int32(1), 30 - it)
        return jnp.where(count_ge(cand) >= krow, cand, t)

    t = lax.fori_loop(0, 31, bit_body, t0)

    c_ge = count_ge(t)
    need = krow - count_ge(t + 1)
    tb8 = jnp.broadcast_to(t, (sub, tq))

    @pl.when(jnp.max(c_ge - krow) > 0.5)
    def _():
        sub_iota = lax.broadcasted_iota(jnp.int32, (sub, tq), 0)

        def tied_below(u):
            ub = jnp.broadcast_to(u, (sub, tq))
            return count(lambda k, base: jnp.where(k == tb8, jnp.where(sub_iota + base < ub, 1.0, 0.0), 0.0))

        def ubit(it, u):
            cand = u | lax.shift_left(jnp.int32(1), 30 - it)
            return jnp.where(tied_below(cand) < need, cand, u)

        u = lax.fori_loop(0, 31, ubit, jnp.zeros((1, tq), jnp.int32))
        ub_full = jnp.broadcast_to(u, (kc, tq))
        tb_full = jnp.broadcast_to(t, (kc, tq))
        kidx = lax.broadcasted_iota(jnp.int32, (kc, tq), 0)

        def drop(j, c):
            k = key_s[j]
            key_s[j] = jnp.where(k == tb_full, jnp.where(kidx + j * kc > ub_full, INT_MIN, k), k)
            return c

        lax.fori_loop(0, nchunks, drop, 0)

    acc_s[...] = jnp.zeros_like(acc_s)

    def att_body(j, ms):
        koff = pl.multiple_of(j * kc, kc)
        bias_s[...] = jnp.where(key_s[j] >= jnp.broadcast_to(t, (kc, tq)), 0.0, NEG_BIG)
        out = []
        for g in range(ATTN_KV_HEADS):
            qcat = jnp.concatenate([qt_ref[g * ATTN_REP + r] for r in range(ATTN_REP)], axis=1)
            bias = jnp.concatenate([bias_s[...]] * ATTN_REP, axis=1)
            kg = kk_ref[g, pl.ds(koff, kc), :]
            vg = vt_ref[g, :, pl.ds(koff, kc)]
            s = jnp.dot(kg, qcat, preferred_element_type=F32) + bias
            m_new = jnp.maximum(ms[g], jnp.max(s, axis=0, keepdims=True))
            p = jnp.exp(s - m_new).astype(BF16)
            pv = jnp.dot(vg, p, preferred_element_type=F32)
            acc_s[g] = jnp.exp(ms[g] - m_new) * acc_s[g] + pv
            out.append(m_new)
        return tuple(out)

    m_init = tuple(jnp.full((1, ATTN_REP * tq), NEG_BIG, F32) for _ in range(ATTN_KV_HEADS))
    lax.fori_loop(0, nchunks, att_body, m_init)

    for h in range(ATTN_HEADS):
        g, r = divmod(h, ATTN_REP)
        a = acc_s[g, :, r * tq:(r + 1) * tq]
        ot_s[h * ATTN_HEAD_DIM:(h + 1) * ATTN_HEAD_DIM, :] = (
            a[:ATTN_HEAD_DIM] / a[ATTN_HEAD_DIM:ATTN_HEAD_DIM + 1])
    o_ref[...] = ot_s[...].T.astype(o_ref.dtype)


def _dsa(qt, iqt, iwt, kk, vt, ik, batch, seq):
    n = batch * seq
    tq, kc = DSA_TQ, DSA_KC
    assert seq % tq == 0
    nq = seq // tq
    topk = min(TOPK_MAX, seq // 4)
    nch = seq // kc
    nout = ATTN_HEADS * ATTN_HEAD_DIM
    qmap = lambda b, i: (0, 0, b * nq + i)
    est = (nch * kc * tq * 4 + kc * tq * 4 + ATTN_HEADS * (VT_ROWS + 8) * tq * 4 + nout * tq * 4
           + ATTN_KV_HEADS * seq * (V7X_LANES + VT_ROWS) * 2 + seq * V7X_LANES * 2
           + 2 * (2 * ATTN_HEADS * ATTN_HEAD_DIM * tq * 2 + 8 * tq * 4 + tq * nout * 2))
    return pl.pallas_call(
        functools.partial(_dsa_kernel, topk=topk),
        out_shape=jax.ShapeDtypeStruct((n, nout), BF16),
        grid=(batch, nq),
        in_specs=[
            pl.BlockSpec((ATTN_HEADS, ATTN_HEAD_DIM, tq), qmap),
            pl.BlockSpec((IDX_HEADS, IDX_DIM, tq), qmap),
            pl.BlockSpec((IDX_HEADS, tq), lambda b, i: (0, b * nq + i)),
            pl.BlockSpec((ATTN_KV_HEADS, seq, ATTN_HEAD_DIM), lambda b, i: (0, b, 0),
                         pipeline_mode=pl.Buffered(1)),
            pl.BlockSpec((ATTN_KV_HEADS, VT_ROWS, seq), lambda b, i: (0, 0, b),
                         pipeline_mode=pl.Buffered(1)),
            pl.BlockSpec((seq, IDX_DIM), lambda b, i: (b, 0), pipeline_mode=pl.Buffered(1)),
        ],
        out_specs=pl.BlockSpec((tq, nout), lambda b, i: (b * nq + i, 0)),
        scratch_shapes=[
            pltpu.VMEM((nch, kc, tq), jnp.int32),
            pltpu.VMEM((kc, tq), F32),
            pltpu.VMEM((ATTN_KV_HEADS, VT_ROWS, ATTN_REP * tq), F32),
            pltpu.VMEM((nout, tq), F32),
        ],
        compiler_params=pltpu.CompilerParams(
            dimension_semantics=("arbitrary", "arbitrary"), vmem_limit_bytes=_vmem_limit(est)),
    )(qt, iqt, iwt, kk, vt, ik)


def _gla_kernel(gqk_ref, gv_ref, la_ref, gg_ref, gn_ref, o_ref, st_s):
    @pl.when(pl.program_id(1) == 0)
    def _():
        st_s[...] = jnp.zeros_like(st_s)

    c = GLA_CHUNK
    nqk = GLA_HEADS * GLA_DK
    r_i = lax.broadcasted_iota(jnp.int32, (c, c), 0)
    c_i = lax.broadcasted_iota(jnp.int32, (c, c), 1)
    tri = r_i >= c_i
    tri_f = tri.astype(F32)
    gn = gn_ref[...]
    for ci in range(GLA_ROWS // c):
        rows = slice(ci * c, (ci + 1) * c)
        la = la_ref[rows, :]
        b = jnp.dot(tri_f, la, preferred_element_type=F32, precision=lax.Precision.HIGHEST)
        b_last = b[c - 1:c, :]
        q = gqk_ref[rows, :nqk]
        k = gqk_ref[rows, nqk:]
        q_dec = (q * jnp.exp(b)).astype(BF16)
        k_in = (k * jnp.exp(-b)).astype(BF16)
        k_out = (k * jnp.exp(b_last - b)).astype(BF16)
        decay = jnp.exp(b_last)
        for h in range(GLA_HEADS):
            hs = slice(h * GLA_DK, (h + 1) * GLA_DK)
            vs = slice(h * GLA_DV, (h + 1) * GLA_DV)
            qd, ki, ko = q_dec[:, hs], k_in[:, hs], k_out[:, hs]
            v = gv_ref[rows, vs]
            a = lax.dot_general(qd, ki, NT_DIMS, preferred_element_type=F32)
            a = jnp.where(tri, a, 0.0).astype(BF16)
            st = st_s[h]
            o = jnp.dot(a, v, preferred_element_type=F32)
            o = o + lax.dot_general(qd, st.astype(BF16), NT_DIMS, preferred_element_type=F32)
            upd = lax.dot_general(v, ko, TN_DIMS, preferred_element_type=F32)
            st_s[h] = st * decay[:, hs] + upd
            gate = gg_ref[rows, vs]
            o_ref[rows, vs] = (_rms(o, gn) * (gate * jax.nn.sigmoid(gate))).astype(o_ref.dtype)


def _gla(gqk, gv, la, gg, g_norm, batch, seq):
    n = batch * seq
    t = GLA_ROWS
    assert seq % t == 0
    ns = seq // t
    row = lambda b, i: (b * ns + i, 0)
    nqk = GLA_HEADS * GLA_DK
    nv = GLA_HEADS * GLA_DV
    return pl.pallas_call(
        _gla_kernel,
        out_shape=jax.ShapeDtypeStruct((n, nv), BF16),
        grid=(batch, ns),
        in_specs=[
            pl.BlockSpec((t, 2 * nqk), row),
            pl.BlockSpec((t, nv), row),
            pl.BlockSpec((t, nqk), row),
            pl.BlockSpec((t, nv), row),
            _const_spec((1, GLA_DV)),
        ],
        out_specs=pl.BlockSpec((t, nv), row),
        scratch_shapes=[pltpu.VMEM((GLA_HEADS, GLA_DV, GLA_DK), F32)],
        compiler_params=pltpu.CompilerParams(dimension_semantics=("arbitrary", "arbitrary")),
    )(gqk, gv, la, gg, g_norm.reshape(1, GLA_DV))


def _outproj_kernel(x_ref, oa_ref, og_ref, w_ref, g_ref, o_ref):
    m = jnp.dot(oa_ref[...], w_ref[0], preferred_element_type=F32)
    m = m + jnp.dot(og_ref[...], w_ref[1], preferred_element_type=F32)
    o_ref[...] = x_ref[...] + _rms(m, g_ref[...])


def _outproj(x1, oa, og, w_out, g_post):
    n, d = x1.shape
    t = FFN_ROWS
    half = oa.shape[1]
    assert og.shape[1] == half and w_out.shape[0] == 2 * half
    w = w_out.reshape(2, half, d).astype(BF16)
    row = lambda i: (i, 0)
    est = 4 * t * d * 4 + 4 * t * half * 2 + 2 * half * d * 2 + t * d * 4
    return pl.pallas_call(
        _outproj_kernel,
        out_shape=jax.ShapeDtypeStruct((n, d), F32),
        grid=(n // t,),
        in_specs=[
            pl.BlockSpec((t, d), row),
            pl.BlockSpec((t, half), row),
            pl.BlockSpec((t, half), row),
            _const_spec(w.shape),
            _const_spec((1, d)),
        ],
        out_specs=pl.BlockSpec((t, d), row),
        compiler_params=pltpu.CompilerParams(
            dimension_semantics=("arbitrary",), vmem_limit_bytes=_vmem_limit(est)),
    )(x1, oa, og, w, g_post.reshape(1, d))


def kernel(x, g_ffn1_pre, w_ffn1_gate, w_ffn1_up, w_ffn1_down, g_ffn1_post, g_mix_pre, w_in, w_gla_a2,
           b_gla_a, g_gla_norm, w_out, g_mix_post, g_ffn2_pre, w_ffn2_gate, w_ffn2_up, w_ffn2_down,
           g_ffn2_post):
    batch, seq, d = x.shape
    h = x.reshape(batch * seq, d)
    for l in range(g_ffn1_pre.shape[0]):
        h = _ffn(h, g_ffn1_pre[l], w_ffn1_gate[l], w_ffn1_up[l], w_ffn1_down[l], g_ffn1_post[l])
        qt, iqt, vt, iwt, kk, ik, gqk, gv, gg, la = _inproj(h, g_mix_pre[l], w_in[l], w_gla_a2[l], b_gla_a[l])
        oa = _dsa(qt, iqt, iwt, kk, vt, ik, batch, seq)
        og = _gla(gqk, gv, la, gg, g_gla_norm[l], batch, seq)
        h = _outproj(h, oa, og, w_out[l], g_mix_post[l])
        h = _ffn(h, g_ffn2_pre[l], w_ffn2_gate[l], w_ffn2_up[l], w_ffn2_down[l], g_ffn2_post[l])
    return h.reshape(batch, seq, d)
```

```python
import functools

import jax
import jax.numpy as jnp
from jax import lax
from jax.experimental import pallas as pl
from jax.experimental.pallas import tpu as pltpu

ATTN_HEADS = 8
ATTN_KV_HEADS = 2
ATTN_HEAD_DIM = 64
ATTN_REP = ATTN_HEADS // ATTN_KV_HEADS
IDX_HEADS = 8
IDX_DIM = 64
TOPK_MAX = 256
GLA_HEADS = 4
GLA_DK = 64
GLA_DV = 128
GLA_GATE_RANK = 16
GLA_TAU = 16.0
GLA_CHUNK = 64
EPS = 1e-6

V7X_LANES = 128
V7X_SUBLANES = 8
V7X_BF16_ROWS = 16
V7X_MXU_DIM = 256
V7X_VMEM_BYTES = 64 * 2**20

FFN_ROWS = 512
FF_CHUNK = V7X_MXU_DIM
DSA_TQ = 256
DSA_KC = DSA_TQ
GLA_ROWS = 256
VT_ROWS = ATTN_HEAD_DIM + V7X_BF16_ROWS

INT_MIN = -2**31
NEG_BIG = -1e30
SOFTMAX_NORM_MIN = 2.0 ** -60
SOFTMAX_NORM_MAX = 2.0 ** 100

F32 = jnp.float32
BF16 = jnp.bfloat16
NT_DIMS = (((1,), (1,)), ((), ()))
TN_DIMS = (((0,), (0,)), ((), ()))


def _vmem_limit(nbytes):
    return int(min(nbytes * 1.25 + (8 << 20), V7X_VMEM_BYTES - (6 << 20)))


def _rms(x, g):
    return x * lax.rsqrt(jnp.mean(x * x, axis=-1, keepdims=True) + EPS) * g


def _const_spec(shape):
    nd = len(shape)
    return pl.BlockSpec(shape, lambda *_: (0,) * nd, pipeline_mode=pl.Buffered(1))


def _ffn_kernel(x_ref, gpre_ref, wg_ref, wu_ref, wd_ref, gpost_ref, o_ref, acc_ref):
    x = x_ref[...]
    xn = _rms(x, gpre_ref[...]).astype(BF16)
    acc_ref[...] = jnp.zeros_like(acc_ref)

    def body(c, carry):
        g = jnp.dot(xn, wg_ref[c], preferred_element_type=F32)
        u = jnp.dot(xn, wu_ref[c], preferred_element_type=F32)
        a = (g * jax.nn.sigmoid(g) * u).astype(BF16)
        acc_ref[...] += jnp.dot(a, wd_ref[c], preferred_element_type=F32)
        return carry

    lax.fori_loop(0, wg_ref.shape[0], body, 0)
    o_ref[...] = x + 0.5 * _rms(acc_ref[...], gpost_ref[...])


def _ffn(x, g_pre, w_gate, w_up, w_down, g_post):
    n, d = x.shape
    dff = w_gate.shape[1]
    nch = dff // FF_CHUNK
    assert nch * FF_CHUNK == dff and n % FFN_ROWS == 0
    wg = w_gate.reshape(d, nch, FF_CHUNK).transpose(1, 0, 2).astype(BF16)
    wu = w_up.reshape(d, nch, FF_CHUNK).transpose(1, 0, 2).astype(BF16)
    wd = w_down.reshape(nch, FF_CHUNK, d).astype(BF16)
    row = lambda i: (i, 0)
    est = 4 * FFN_ROWS * d * 4 + 3 * d * dff * 2 + FFN_ROWS * d * 8 + 4 * FFN_ROWS * FF_CHUNK * 4
    return pl.pallas_call(
        _ffn_kernel,
        out_shape=jax.ShapeDtypeStruct((n, d), F32),
        grid=(n // FFN_ROWS,),
        in_specs=[
            pl.BlockSpec((FFN_ROWS, d), row),
            _const_spec((1, d)),
            _const_spec(wg.shape), _const_spec(wu.shape), _const_spec(wd.shape),
            _const_spec((1, d)),
        ],
        out_specs=pl.BlockSpec((FFN_ROWS, d), row),
        scratch_shapes=[pltpu.VMEM((FFN_ROWS, d), F32)],
        compiler_params=pltpu.CompilerParams(
            dimension_semantics=("arbitrary",), vmem_limit_bytes=_vmem_limit(est)),
    )(x, g_pre.reshape(1, d), wg, wu, wd, g_post.reshape(1, d))


_K_OFF, _MISC_OFF, _GQ_OFF, _GK_OFF, _GV_OFF, _GG_OFF, _TOK_COLS = 0, 128, 256, 512, 768, 1280, 1792
_GA_LANE = 72
_QT_OFF, _IQT_OFF, _VT_OFF, _IWT_OFF, _T_ROWS = 0, 512, 1024, 1152, 1168


def _inproj_kernel(x_ref, g_ref, wtok_ref, wt_ref, wa2_ref, ba_ref,
                   qt_ref, iqt_ref, vt_ref, iwt_ref, kk_ref, ik_ref, gqk_ref, gv_ref, gg_ref, la_ref):
    h = _rms(x_ref[...], g_ref[...]).astype(BF16)
    t = h.shape[0]
    pt = lax.dot_general(wt_ref[...], h, NT_DIMS, preferred_element_type=F32)
    for i in range(ATTN_HEADS):
        qt_ref[i] = pt[_QT_OFF + 64 * i:_QT_OFF + 64 * (i + 1)].astype(BF16)
    for i in range(IDX_HEADS):
        iqt_ref[i] = pt[_IQT_OFF + 64 * i:_IQT_OFF + 64 * (i + 1)].astype(BF16)
    ones_row = (lax.broadcasted_iota(jnp.int32, (V7X_BF16_ROWS, t), 0) == 0).astype(F32)
    for g in range(ATTN_KV_HEADS):
        v_t = pt[_VT_OFF + 64 * g:_VT_OFF + 64 * (g + 1)]
        vt_ref[g] = jnp.concatenate([v_t, ones_row], axis=0).astype(BF16)
    iwt_ref[...] = pt[_IWT_OFF:_IWT_OFF + IDX_HEADS] * (IDX_HEADS ** -0.5)

    proj = jnp.dot(h, wtok_ref[...], preferred_element_type=F32)
    for g in range(ATTN_KV_HEADS):
        kk_ref[g] = proj[:, _K_OFF + 64 * g:_K_OFF + 64 * (g + 1)].astype(BF16)
    misc = proj[:, _MISC_OFF:_MISC_OFF + V7X_LANES]
    ik_ref[...] = misc[:, :IDX_DIM].astype(BF16)
    gqk_ref[...] = proj[:, _GQ_OFF:_GV_OFF]
    gv_ref[...] = proj[:, _GV_OFF:_GG_OFF].astype(BF16)
    gg_ref[...] = proj[:, _GG_OFF:_GG_OFF + GLA_HEADS * GLA_DV]
    z = jnp.dot(misc, wa2_ref[...], preferred_element_type=F32, precision=lax.Precision.HIGHEST)
    z = z + ba_ref[...]
    log_sig = jnp.minimum(z, 0.0) - jnp.log1p(jnp.exp(-jnp.abs(z)))
    la_ref[...] = log_sig * (1.0 / GLA_TAU)


def _pack_w_in(w_in):
    d = w_in.shape[0]
    sizes = (512, 128, 128, 512, 64, 8, 256, 256, 512, 16, 512)
    offs = [0]
    for s in sizes:
        offs.append(offs[-1] + s)
    aq, ak, av, iq, ik, iw, gq, gk, gv, ga, gg = [w_in[:, offs[i]:offs[i + 1]] for i in range(len(sizes))]
    z = lambda n: jnp.zeros((d, n), w_in.dtype)
    w_tok = jnp.concatenate([ak, ik, z(8), ga, z(40), gq * (GLA_DK ** -0.5), gk, gv, gg], axis=1)
    w_t = jnp.concatenate([aq * (ATTN_HEAD_DIM ** -0.5), iq * (IDX_DIM ** -0.5), av, iw, z(8)], axis=1).T
    assert w_tok.shape[1] == _TOK_COLS and w_t.shape[0] == _T_ROWS
    return w_tok.astype(BF16), w_t.astype(BF16)


def _inproj(x1, g_mix_pre, w_in, w_gla_a2, b_gla_a):
    n, d = x1.shape
    t = FFN_ROWS
    w_tok, w_t = _pack_w_in(w_in)
    nqk = GLA_HEADS * GLA_DK
    nv = GLA_HEADS * GLA_DV
    wa2 = jnp.zeros((V7X_LANES, nqk), F32).at[_GA_LANE:_GA_LANE + GLA_GATE_RANK].set(w_gla_a2)
    row = lambda i: (i, 0)
    row3 = lambda i: (0, i, 0)
    col3 = lambda i: (0, 0, i)
    out_shape = (
        jax.ShapeDtypeStruct((ATTN_HEADS, ATTN_HEAD_DIM, n), BF16),
        jax.ShapeDtypeStruct((IDX_HEADS, IDX_DIM, n), BF16),
        jax.ShapeDtypeStruct((ATTN_KV_HEADS, VT_ROWS, n), BF16),
        jax.ShapeDtypeStruct((IDX_HEADS, n), F32),
        jax.ShapeDtypeStruct((ATTN_KV_HEADS, n, ATTN_HEAD_DIM), BF16),
        jax.ShapeDtypeStruct((n, IDX_DIM), BF16),
        jax.ShapeDtypeStruct((n, 2 * nqk), F32),
        jax.ShapeDtypeStruct((n, nv), BF16),
        jax.ShapeDtypeStruct((n, nv), F32),
        jax.ShapeDtypeStruct((n, nqk), F32),
    )
    out_specs = (
        pl.BlockSpec((ATTN_HEADS, ATTN_HEAD_DIM, t), col3),
        pl.BlockSpec((IDX_HEADS, IDX_DIM, t), col3),
        pl.BlockSpec((ATTN_KV_HEADS, VT_ROWS, t), col3),
        pl.BlockSpec((IDX_HEADS, t), lambda i: (0, i)),
        pl.BlockSpec((ATTN_KV_HEADS, t, ATTN_HEAD_DIM), row3),
        pl.BlockSpec((t, IDX_DIM), row),
        pl.BlockSpec((t, 2 * nqk), row),
        pl.BlockSpec((t, nv), row),
        pl.BlockSpec((t, nv), row),
        pl.BlockSpec((t, nqk), row),
    )
    est = (2 * t * d * 4 + d * (_TOK_COLS + _T_ROWS) * 2 + t * (_TOK_COLS + _T_ROWS) * 4 * 2
           + 2 * t * 8 * 1024)
    return pl.pallas_call(
        _inproj_kernel,
        out_shape=out_shape,
        grid=(n // t,),
        in_specs=[
            pl.BlockSpec((t, d), row),
            _const_spec((1, d)),
            _const_spec(w_tok.shape),
            _const_spec(w_t.shape),
            _const_spec(wa2.shape),
            _const_spec((1, nqk)),
        ],
        out_specs=out_specs,
        compiler_params=pltpu.CompilerParams(
            dimension_semantics=("arbitrary",), vmem_limit_bytes=_vmem_limit(est)),
    )(x1, g_mix_pre.reshape(1, d), w_tok, w_t, wa2, b_gla_a.reshape(1, nqk))


def _dsa_kernel(qt_ref, iqt_ref, iwt_ref, kk_ref, vt_ref, ik_ref, o_ref,
                key_s, bias_s, acc_s, ot_s, *, topk, idx_bits):
    tq = DSA_TQ
    kc = DSA_KC
    sub = V7X_SUBLANES
    i = pl.program_id(1)
    nchunks = i + 1

    def score_chunk(j, diagonal):
        ikc = ik_ref[pl.ds(pl.multiple_of(j * kc, kc), kc), :]
        iqcat = jnp.concatenate([iqt_ref[h] for h in range(IDX_HEADS)], axis=1)
        lg = jnp.dot(ikc, iqcat, preferred_element_type=F32)
        sc = jnp.zeros((kc, tq), F32)
        for h in range(IDX_HEADS):
            sc = sc + jnp.maximum(lg[:, h * tq:(h + 1) * tq], 0.0) * iwt_ref[h:h + 1, :]
        bits = pltpu.bitcast(sc, jnp.int32)
        key = jnp.where(bits < 0, bits ^ 0x7FFFFFFF, bits)
        if diagonal:
            kpos = lax.broadcasted_iota(jnp.int32, (kc, tq), 0)
            qpos = lax.broadcasted_iota(jnp.int32, (kc, tq), 1)
            key = jnp.where(kpos <= qpos, key, INT_MIN)
        key_s[j] = key

    def score_body(j, c):
        score_chunk(j, False)
        return c

    lax.fori_loop(0, i, score_body, 0)
    score_chunk(i, True)

    qidx = i * tq + lax.broadcasted_iota(jnp.int32, (1, tq), 1)
    krow = jnp.minimum(topk, qidx + 1).astype(F32)

    def count(ind_fn):
        def body(j, acc):
            parts = [acc, None, None, None]
            for s in range(kc // sub):
                ind = ind_fn(key_s[j, s * sub:(s + 1) * sub, :], j * kc + s * sub)
                p = s % 4
                parts[p] = ind if parts[p] is None else parts[p] + ind
            return (parts[0] + parts[1]) + (parts[2] + parts[3])
        acc = lax.fori_loop(0, nchunks, body, jnp.zeros((sub, tq), F32))
        return jnp.sum(acc, axis=0, keepdims=True)

    def count_ge(cand):
        cb = jnp.broadcast_to(cand, (sub, tq))
        return count(lambda k, base: jnp.where(k >= cb, 1.0, 0.0))

    zero = jnp.zeros((1, tq), jnp.int32)
    t0 = jnp.where(count_ge(zero) >= krow, zero, INT_MIN)

    def bit_body(it, t):
        cand = t | lax.shift_left(jnp.int32(1), 30 - it)
        return jnp.where(count_ge(cand) >= krow, cand, t)

    t = lax.fori_loop(0, 31, bit_body, t0)

    c_ge = count_ge(t)
    need = krow - count_ge(t + 1)
    tb8 = jnp.broadcast_to(t, (sub, tq))

    @pl.when(jnp.max(c_ge - krow) > 0.5)
    def _():
        sub_iota = lax.broadcasted_iota(jnp.int32, (sub, tq), 0)

        def tied_below(u):
            ub = jnp.broadcast_to(u, (sub, tq))
            return count(lambda k, base: jnp.where(k == tb8, jnp.where(sub_iota + base < ub, 1.0, 0.0), 0.0))

        def ubit(it, u):
            cand = u | lax.shift_left(jnp.int32(1), idx_bits - 1 - it)
            return jnp.where(tied_below(cand) < need, cand, u)

        u = lax.fori_loop(0, idx_bits, ubit, jnp.zeros((1, tq), jnp.int32))
        ub_full = jnp.broadcast_to(u, (kc, tq))
        tb_full = jnp.broadcast_to(t, (kc, tq))
        kidx = lax.broadcasted_iota(jnp.int32, (kc, tq), 0)

        def drop(j, c):
            k = key_s[j]
            key_s[j] = jnp.where(k == tb_full, jnp.where(kidx + j * kc > ub_full, INT_MIN, k), k)
            return c

        lax.fori_loop(0, nchunks, drop, 0)

    def masked_scores(j, g):
        koff = pl.multiple_of(j * kc, kc)
        qcat = jnp.concatenate([qt_ref[g * ATTN_REP + r] for r in range(ATTN_REP)], axis=1)
        bias = jnp.concatenate([bias_s[...]] * ATTN_REP, axis=1)
        kg = kk_ref[g, pl.ds(koff, kc), :]
        vg = vt_ref[g, :, pl.ds(koff, kc)]
        return jnp.dot(kg, qcat, preferred_element_type=F32) + bias, vg

    def set_bias(j):
        bias_s[...] = jnp.where(key_s[j] >= jnp.broadcast_to(t, (kc, tq)), 0.0, NEG_BIG)

    acc_s[...] = jnp.zeros_like(acc_s)

    def att_body(j, c):
        set_bias(j)
        for g in range(ATTN_KV_HEADS):
            s, vg = masked_scores(j, g)
            acc_s[g] += jnp.dot(vg, jnp.exp(s).astype(BF16), preferred_element_type=F32)
        return c

    lax.fori_loop(0, nchunks, att_body, 0)

    norm = acc_s[:, ATTN_HEAD_DIM:ATTN_HEAD_DIM + 1, :]
    in_range = jnp.where(norm >= SOFTMAX_NORM_MIN, jnp.where(norm <= SOFTMAX_NORM_MAX, 1.0, 0.0), 0.0)

    @pl.when(jnp.min(in_range) < 0.5)
    def _():
        acc_s[...] = jnp.zeros_like(acc_s)

        def online_body(j, ms):
            set_bias(j)
            out = []
            for g in range(ATTN_KV_HEADS):
                s, vg = masked_scores(j, g)
                m_new = jnp.maximum(ms[g], jnp.max(s, axis=0, keepdims=True))
                p = jnp.exp(s - m_new).astype(BF16)
                pv = jnp.dot(vg, p, preferred_element_type=F32)
                acc_s[g] = jnp.exp(ms[g] - m_new) * acc_s[g] + pv
                out.append(m_new)
            return tuple(out)

        m_init = tuple(jnp.full((1, ATTN_REP * tq), NEG_BIG, F32) for _ in range(ATTN_KV_HEADS))
        lax.fori_loop(0, nchunks, online_body, m_init)

    for h in range(ATTN_HEADS):
        g, r = divmod(h, ATTN_REP)
        a = acc_s[g, :, r * tq:(r + 1) * tq]
        ot_s[h * ATTN_HEAD_DIM:(h + 1) * ATTN_HEAD_DIM, :] = (
            a[:ATTN_HEAD_DIM] / a[ATTN_HEAD_DIM:ATTN_HEAD_DIM + 1])
    o_ref[...] = ot_s[...].T.astype(o_ref.dtype)


def _dsa(qt, iqt, iwt, kk, vt, ik, batch, seq):
    n = batch * seq
    tq, kc = DSA_TQ, DSA_KC
    assert seq % tq == 0
    nq = seq // tq
    topk = min(TOPK_MAX, seq // 4)
    nch = seq // kc
    nout = ATTN_HEADS * ATTN_HEAD_DIM
    qmap = lambda b, i: (0, 0, b * nq + i)
    est = (nch * kc * tq * 4 + kc * tq * 4 + ATTN_HEADS * (VT_ROWS + 8) * tq * 4 + nout * tq * 4
           + ATTN_KV_HEADS * seq * (V7X_LANES + VT_ROWS) * 2 + seq * V7X_LANES * 2
           + 2 * (2 * ATTN_HEADS * ATTN_HEAD_DIM * tq * 2 + 8 * tq * 4 + tq * nout * 2))
    return pl.pallas_call(
        functools.partial(_dsa_kernel, topk=topk, idx_bits=(seq - 1).bit_length()),
        out_shape=jax.ShapeDtypeStruct((n, nout), BF16),
        grid=(batch, nq),
        in_specs=[
            pl.BlockSpec((ATTN_HEADS, ATTN_HEAD_DIM, tq), qmap),
            pl.BlockSpec((IDX_HEADS, IDX_DIM, tq), qmap),
            pl.BlockSpec((IDX_HEADS, tq), lambda b, i: (0, b * nq + i)),
            pl.BlockSpec((ATTN_KV_HEADS, seq, ATTN_HEAD_DIM), lambda b, i: (0, b, 0),
                         pipeline_mode=pl.Buffered(1)),
            pl.BlockSpec((ATTN_KV_HEADS, VT_ROWS, seq), lambda b, i: (0, 0, b),
                         pipeline_mode=pl.Buffered(1)),
            pl.BlockSpec((seq, IDX_DIM), lambda b, i: (b, 0), pipeline_mode=pl.Buffered(1)),
        ],
        out_specs=pl.BlockSpec((tq, nout), lambda b, i: (b * nq + i, 0)),
        scratch_shapes=[
            pltpu.VMEM((nch, kc, tq), jnp.int32),
            pltpu.VMEM((kc, tq), F32),
            pltpu.VMEM((ATTN_KV_HEADS, VT_ROWS, ATTN_REP * tq), F32),
            pltpu.VMEM((nout, tq), F32),
        ],
        compiler_params=pltpu.CompilerParams(
            dimension_semantics=("arbitrary", "arbitrary"), vmem_limit_bytes=_vmem_limit(est)),
    )(qt, iqt, iwt, kk, vt, ik)


def _gla_kernel(gqk_ref, gv_ref, la_ref, gg_ref, gn_ref, o_ref, st_s):
    @pl.when(pl.program_id(1) == 0)
    def _():
        st_s[...] = jnp.zeros_like(st_s)

    c = GLA_CHUNK
    nqk = GLA_HEADS * GLA_DK
    r_i = lax.broadcasted_iota(jnp.int32, (c, c), 0)
    c_i = lax.broadcasted_iota(jnp.int32, (c, c), 1)
    tri = r_i >= c_i
    tri_f = tri.astype(F32)
    gn = gn_ref[...]
    for ci in range(GLA_ROWS // c):
        rows = slice(ci * c, (ci + 1) * c)
        la = la_ref[rows, :]
        b = jnp.dot(tri_f, la, preferred_element_type=F32, precision=lax.Precision.HIGHEST)
        b_last = b[c - 1:c, :]
        q = gqk_ref[rows, :nqk]
        k = gqk_ref[rows, nqk:]
        q_dec = (q * jnp.exp(b)).astype(BF16)
        k_in = (k * jnp.exp(-b)).astype(BF16)
        k_out = (k * jnp.exp(b_last - b)).astype(BF16)
        decay = jnp.exp(b_last)
        for h in range(GLA_HEADS):
            hs = slice(h * GLA_DK, (h + 1) * GLA_DK)
            vs = slice(h * GLA_DV, (h + 1) * GLA_DV)
            qd, ki, ko = q_dec[:, hs], k_in[:, hs], k_out[:, hs]
            v = gv_ref[rows, vs]
            a = lax.dot_general(qd, ki, NT_DIMS, preferred_element_type=F32)
            a = jnp.where(tri, a, 0.0).astype(BF16)
            st = st_s[h]
            o = jnp.dot(a, v, preferred_element_type=F32)
            o = o + lax.dot_general(qd, st.astype(BF16), NT_DIMS, preferred_element_type=F32)
            upd = lax.dot_general(v, ko, TN_DIMS, preferred_element_type=F32)
            st_s[h] = st * decay[:, hs] + upd
            gate = gg_ref[rows, vs]
            o_ref[rows, vs] = (_rms(o, gn) * (gate * jax.nn.sigmoid(gate))).astype(o_ref.dtype)


def _gla(gqk, gv, la, gg, g_norm, batch, seq):
    n = batch * seq
    t = GLA_ROWS
    assert seq % t == 0
    ns = seq // t
    row = lambda b, i: (b * ns + i, 0)
    nqk = GLA_HEADS * GLA_DK
    nv = GLA_HEADS * GLA_DV
    return pl.pallas_call(
        _gla_kernel,
        out_shape=jax.ShapeDtypeStruct((n, nv), BF16),
        grid=(batch, ns),
        in_specs=[
            pl.BlockSpec((t, 2 * nqk), row),
            pl.BlockSpec((t, nv), row),
            pl.BlockSpec((t, nqk), row),
            pl.BlockSpec((t, nv), row),
            _const_spec((1, GLA_DV)),
        ],
        out_specs=pl.BlockSpec((t, nv), row),
        scratch_shapes=[pltpu.VMEM((GLA_HEADS, GLA_DV, GLA_DK), F32)],
        compiler_params=pltpu.CompilerParams(dimension_semantics=("arbitrary", "arbitrary")),
    )(gqk, gv, la, gg, g_norm.reshape(1, GLA_DV))


def _outproj_kernel(x_ref, oa_ref, og_ref, w_ref, g_ref, o_ref):
    m = jnp.dot(oa_ref[...], w_ref[0], preferred_element_type=F32)
    m = m + jnp.dot(og_ref[...], w_ref[1], preferred_element_type=F32)
    o_ref[...] = x_ref[...] + _rms(m, g_ref[...])


def _outproj(x1, oa, og, w_out, g_post):
    n, d = x1.shape
    t = FFN_ROWS
    half = oa.shape[1]
    assert og.shape[1] == half and w_out.shape[0] == 2 * half
    w = w_out.reshape(2, half, d).astype(BF16)
    row = lambda i: (i, 0)
    est = 4 * t * d * 4 + 4 * t * half * 2 + 2 * half * d * 2 + t * d * 4
    return pl.pallas_call(
        _outproj_kernel,
        out_shape=jax.ShapeDtypeStruct((n, d), F32),
        grid=(n // t,),
        in_specs=[
            pl.BlockSpec((t, d), row),
            pl.BlockSpec((t, half), row),
            pl.BlockSpec((t, half), row),
            _const_spec(w.shape),
            _const_spec((1, d)),
        ],
        out_specs=pl.BlockSpec((t, d), row),
        compiler_params=pltpu.CompilerParams(
            dimension_semantics=("arbitrary",), vmem_limit_bytes=_vmem_limit(est)),
    )(x1, oa, og, w, g_post.reshape(1, d))


def kernel(x, g_ffn1_pre, w_ffn1_gate, w_ffn1_up, w_ffn1_down, g_ffn1_post, g_mix_pre, w_in, w_gla_a2,
           b_gla_a, g_gla_norm, w_out, g_mix_post, g_ffn2_pre, w_ffn2_gate, w_ffn2_up, w_ffn2_down,
           g_ffn2_post):
    batch, seq, d = x.shape
    h = x.reshape(batch * seq, d)
    for l in range(g_ffn1_pre.shape[0]):
        h = _ffn(h, g_ffn1_pre[l], w_ffn1_gate[l], w_ffn1_up[l], w_ffn1_down[l], g_ffn1_post[l])
        qt, iqt, vt, iwt, kk, ik, gqk, gv, gg, la = _inproj(h, g_mix_pre[l], w_in[l], w_gla_a2[l], b_gla_a[l])
        oa = _dsa(qt, iqt, iwt, kk, vt, ik, batch, seq)
        og = _gla(gqk, gv, la, gg, g_gla_norm[l], batch, seq)
        h = _outproj(h, oa, og, w_out[l], g_mix_post[l])
        h = _ffn(h, g_ffn2_pre[l], w_ffn2_gate[l], w_ffn2_up[l], w_ffn2_down[l], g_ffn2_post[l])
    return h.reshape(batch, seq, d)
```

```python
import functools

import jax
import jax.numpy as jnp
from jax import lax
from jax.experimental import pallas as pl
from jax.experimental.pallas import tpu as pltpu

ATTN_HEADS = 8
ATTN_KV_HEADS = 2
ATTN_HEAD_DIM = 64
ATTN_REP = ATTN_HEADS // ATTN_KV_HEADS
IDX_HEADS = 8
IDX_DIM = 64
TOPK_MAX = 256
GLA_HEADS = 4
GLA_DK = 64
GLA_DV = 128
GLA_GATE_RANK = 16
GLA_TAU = 16.0
GLA_CHUNK = 64
EPS = 1e-6

V7X_LANES = 128
V7X_SUBLANES = 8
V7X_BF16_ROWS = 16
V7X_MXU_DIM = 256
V7X_VMEM_BYTES = 64 * 2**20

FFN_ROWS = 512
FF_CHUNK = V7X_MXU_DIM
DSA_TQ = 256
DSA_KC = DSA_TQ
GLA_ROWS = 256
VT_ROWS = ATTN_HEAD_DIM + V7X_BF16_ROWS

INT_MIN = -2**31
INT16_MIN, INT16_MAX = -2**15, 2**15 - 1
NEG_BIG = -1e30
SOFTMAX_NORM_MIN = 2.0 ** -60
SOFTMAX_NORM_MAX = 2.0 ** 100

F32 = jnp.float32
BF16 = jnp.bfloat16
NT_DIMS = (((1,), (1,)), ((), ()))
TN_DIMS = (((0,), (0,)), ((), ()))


def _vmem_limit(nbytes):
    return int(min(nbytes * 1.25 + (8 << 20), V7X_VMEM_BYTES - (6 << 20)))


def _rms(x, g):
    return x * lax.rsqrt(jnp.mean(x * x, axis=-1, keepdims=True) + EPS) * g


def _const_spec(shape):
    nd = len(shape)
    return pl.BlockSpec(shape, lambda *_: (0,) * nd, pipeline_mode=pl.Buffered(1))


def _ffn_kernel(x_ref, gpre_ref, wg_ref, wu_ref, wd_ref, gpost_ref, o_ref, acc_ref):
    x = x_ref[...]
    xn = _rms(x, gpre_ref[...]).astype(BF16)
    acc_ref[...] = jnp.zeros_like(acc_ref)

    def body(c, carry):
        g = jnp.dot(xn, wg_ref[c], preferred_element_type=F32)
        u = jnp.dot(xn, wu_ref[c], preferred_element_type=F32)
        a = (g * jax.nn.sigmoid(g) * u).astype(BF16)
        acc_ref[...] += jnp.dot(a, wd_ref[c], preferred_element_type=F32)
        return carry

    lax.fori_loop(0, wg_ref.shape[0], body, 0)
    o_ref[...] = x + 0.5 * _rms(acc_ref[...], gpost_ref[...])


def _ffn(x, g_pre, w_gate, w_up, w_down, g_post):
    n, d = x.shape
    dff = w_gate.shape[1]
    nch = dff // FF_CHUNK
    assert nch * FF_CHUNK == dff and n % FFN_ROWS == 0
    wg = w_gate.reshape(d, nch, FF_CHUNK).transpose(1, 0, 2).astype(BF16)
    wu = w_up.reshape(d, nch, FF_CHUNK).transpose(1, 0, 2).astype(BF16)
    wd = w_down.reshape(nch, FF_CHUNK, d).astype(BF16)
    row = lambda i: (i, 0)
    est = 4 * FFN_ROWS * d * 4 + 3 * d * dff * 2 + FFN_ROWS * d * 8 + 4 * FFN_ROWS * FF_CHUNK * 4
    return pl.pallas_call(
        _ffn_kernel,
        out_shape=jax.ShapeDtypeStruct((n, d), F32),
        grid=(n // FFN_ROWS,),
        in_specs=[
            pl.BlockSpec((FFN_ROWS, d), row),
            _const_spec((1, d)),
            _const_spec(wg.shape), _const_spec(wu.shape), _const_spec(wd.shape),
            _const_spec((1, d)),
        ],
        out_specs=pl.BlockSpec((FFN_ROWS, d), row),
        scratch_shapes=[pltpu.VMEM((FFN_ROWS, d), F32)],
        compiler_params=pltpu.CompilerParams(
            dimension_semantics=("arbitrary",), vmem_limit_bytes=_vmem_limit(est)),
    )(x, g_pre.reshape(1, d), wg, wu, wd, g_post.reshape(1, d))


_K_OFF, _MISC_OFF, _GQ_OFF, _GK_OFF, _GV_OFF, _GG_OFF, _TOK_COLS = 0, 128, 256, 512, 768, 1280, 1792
_GA_LANE = 72
_QT_OFF, _IQT_OFF, _VT_OFF, _IWT_OFF, _T_ROWS = 0, 512, 1024, 1152, 1168


def _inproj_kernel(x_ref, g_ref, wtok_ref, wt_ref, wa2_ref, ba_ref,
                   qt_ref, iqt_ref, vt_ref, iwt_ref, kk_ref, ik_ref, gqk_ref, gv_ref, gg_ref, la_ref):
    h = _rms(x_ref[...], g_ref[...]).astype(BF16)
    t = h.shape[0]
    pt = lax.dot_general(wt_ref[...], h, NT_DIMS, preferred_element_type=F32)
    for i in range(ATTN_HEADS):
        qt_ref[i] = pt[_QT_OFF + 64 * i:_QT_OFF + 64 * (i + 1)].astype(BF16)
    for i in range(IDX_HEADS):
        iqt_ref[i] = pt[_IQT_OFF + 64 * i:_IQT_OFF + 64 * (i + 1)].astype(BF16)
    ones_row = (lax.broadcasted_iota(jnp.int32, (V7X_BF16_ROWS, t), 0) == 0).astype(F32)
    for g in range(ATTN_KV_HEADS):
        v_t = pt[_VT_OFF + 64 * g:_VT_OFF + 64 * (g + 1)]
        vt_ref[g] = jnp.concatenate([v_t, ones_row], axis=0).astype(BF16)
    iwt_ref[...] = pt[_IWT_OFF:_IWT_OFF + IDX_HEADS] * (IDX_HEADS ** -0.5)

    proj = jnp.dot(h, wtok_ref[...], preferred_element_type=F32)
    for g in range(ATTN_KV_HEADS):
        kk_ref[g] = proj[:, _K_OFF + 64 * g:_K_OFF + 64 * (g + 1)].astype(BF16)
    misc = proj[:, _MISC_OFF:_MISC_OFF + V7X_LANES]
    ik_ref[...] = misc[:, :IDX_DIM].astype(BF16)
    gqk_ref[...] = proj[:, _GQ_OFF:_GV_OFF]
    gv_ref[...] = proj[:, _GV_OFF:_GG_OFF].astype(BF16)
    gg_ref[...] = proj[:, _GG_OFF:_GG_OFF + GLA_HEADS * GLA_DV]
    z = jnp.dot(misc, wa2_ref[...], preferred_element_type=F32, precision=lax.Precision.HIGHEST)
    z = z + ba_ref[...]
    log_sig = jnp.minimum(z, 0.0) - jnp.log1p(jnp.exp(-jnp.abs(z)))
    la_ref[...] = log_sig * (1.0 / GLA_TAU)


def _pack_w_in(w_in):
    d = w_in.shape[0]
    sizes = (512, 128, 128, 512, 64, 8, 256, 256, 512, 16, 512)
    offs = [0]
    for s in sizes:
        offs.append(offs[-1] + s)
    aq, ak, av, iq, ik, iw, gq, gk, gv, ga, gg = [w_in[:, offs[i]:offs[i + 1]] for i in range(len(sizes))]
    z = lambda n: jnp.zeros((d, n), w_in.dtype)
    w_tok = jnp.concatenate([ak, ik, z(8), ga, z(40), gq * (GLA_DK ** -0.5), gk, gv, gg], axis=1)
    w_t = jnp.concatenate([aq * (ATTN_HEAD_DIM ** -0.5), iq * (IDX_DIM ** -0.5), av, iw, z(8)], axis=1).T
    assert w_tok.shape[1] == _TOK_COLS and w_t.shape[0] == _T_ROWS
    return w_tok.astype(BF16), w_t.astype(BF16)


def _inproj(x1, g_mix_pre, w_in, w_gla_a2, b_gla_a):
    n, d = x1.shape
    t = FFN_ROWS
    w_tok, w_t = _pack_w_in(w_in)
    nqk = GLA_HEADS * GLA_DK
    nv = GLA_HEADS * GLA_DV
    wa2 = jnp.zeros((V7X_LANES, nqk), F32).at[_GA_LANE:_GA_LANE + GLA_GATE_RANK].set(w_gla_a2)
    row = lambda i: (i, 0)
    row3 = lambda i: (0, i, 0)
    col3 = lambda i: (0, 0, i)
    out_shape = (
        jax.ShapeDtypeStruct((ATTN_HEADS, ATTN_HEAD_DIM, n), BF16),
        jax.ShapeDtypeStruct((IDX_HEADS, IDX_DIM, n), BF16),
        jax.ShapeDtypeStruct((ATTN_KV_HEADS, VT_ROWS, n), BF16),
        jax.ShapeDtypeStruct((IDX_HEADS, n), F32),
        jax.ShapeDtypeStruct((ATTN_KV_HEADS, n, ATTN_HEAD_DIM), BF16),
        jax.ShapeDtypeStruct((n, IDX_DIM), BF16),
        jax.ShapeDtypeStruct((n, 2 * nqk), F32),
        jax.ShapeDtypeStruct((n, nv), BF16),
        jax.ShapeDtypeStruct((n, nv), F32),
        jax.ShapeDtypeStruct((n, nqk), F32),
    )
    out_specs = (
        pl.BlockSpec((ATTN_HEADS, ATTN_HEAD_DIM, t), col3),
        pl.BlockSpec((IDX_HEADS, IDX_DIM, t), col3),
        pl.BlockSpec((ATTN_KV_HEADS, VT_ROWS, t), col3),
        pl.BlockSpec((IDX_HEADS, t), lambda i: (0, i)),
        pl.BlockSpec((ATTN_KV_HEADS, t, ATTN_HEAD_DIM), row3),
        pl.BlockSpec((t, IDX_DIM), row),
        pl.BlockSpec((t, 2 * nqk), row),
        pl.BlockSpec((t, nv), row),
        pl.BlockSpec((t, nv), row),
        pl.BlockSpec((t, nqk), row),
    )
    est = (2 * t * d * 4 + d * (_TOK_COLS + _T_ROWS) * 2 + t * (_TOK_COLS + _T_ROWS) * 4 * 2
           + 2 * t * 8 * 1024)
    return pl.pallas_call(
        _inproj_kernel,
        out_shape=out_shape,
        grid=(n // t,),
        in_specs=[
            pl.BlockSpec((t, d), row),
            _const_spec((1, d)),
            _const_spec(w_tok.shape),
            _const_spec(w_t.shape),
            _const_spec(wa2.shape),
            _const_spec((1, nqk)),
        ],
        out_specs=out_specs,
        compiler_params=pltpu.CompilerParams(
            dimension_semantics=("arbitrary",), vmem_limit_bytes=_vmem_limit(est)),
    )(x1, g_mix_pre.reshape(1, d), w_tok, w_t, wa2, b_gla_a.reshape(1, nqk))


def _dsa_kernel(qt_ref, iqt_ref, iwt_ref, kk_ref, vt_ref, ik_ref, o_ref,
                key_s, half_s, bias_s, acc_s, ot_s, *, topk, idx_bits):
    tq = DSA_TQ
    kc = DSA_KC
    sub = V7X_SUBLANES
    i = pl.program_id(1)
    nchunks = i + 1

    def score_chunk(j, diagonal):
        ikc = ik_ref[pl.ds(pl.multiple_of(j * kc, kc), kc), :]
        iqcat = jnp.concatenate([iqt_ref[h] for h in range(IDX_HEADS)], axis=1)
        lg = jnp.dot(ikc, iqcat, preferred_element_type=F32)
        sc = jnp.zeros((kc, tq), F32)
        for h in range(IDX_HEADS):
            sc = sc + jnp.maximum(lg[:, h * tq:(h + 1) * tq], 0.0) * iwt_ref[h:h + 1, :]
        bits = pltpu.bitcast(sc, jnp.int32)
        key = jnp.where(bits < 0, bits ^ 0x7FFFFFFF, bits)
        if diagonal:
            kpos = lax.broadcasted_iota(jnp.int32, (kc, tq), 0)
            qpos = lax.broadcasted_iota(jnp.int32, (kc, tq), 1)
            key = jnp.where(kpos <= qpos, key, INT_MIN)
        key_s[j] = key
        half_s[j] = lax.shift_right_arithmetic(key, 16).astype(jnp.int16)

    def score_body(j, c):
        score_chunk(j, False)
        return c

    lax.fori_loop(0, i, score_body, 0)
    score_chunk(i, True)

    qidx = i * tq + lax.broadcasted_iota(jnp.int32, (1, tq), 1)
    krow = jnp.minimum(topk, qidx + 1).astype(F32)

    def count(ind_fn):
        def body(j, acc):
            parts = [acc, None, None, None]
            for s in range(kc // sub):
                ind = ind_fn(key_s[j, s * sub:(s + 1) * sub, :], j * kc + s * sub)
                p = s % 4
                parts[p] = ind if parts[p] is None else parts[p] + ind
            return (parts[0] + parts[1]) + (parts[2] + parts[3])
        acc = lax.fori_loop(0, nchunks, body, jnp.zeros((sub, tq), F32))
        return jnp.sum(acc, axis=0, keepdims=True)

    def count_ge(cand):
        cb = jnp.broadcast_to(cand, (sub, tq))
        return count(lambda k, base: jnp.where(k >= cb, 1.0, 0.0))

    def search16():
        rows16 = V7X_BF16_ROWS

        def count16(cand):
            cb = jnp.broadcast_to(cand, (rows16, tq)).astype(jnp.int16)
            one, nil = jnp.int16(1), jnp.int16(0)

            def body(j, acc):
                parts = [acc, None, None, None]
                for s in range(kc // rows16):
                    ind = jnp.where(half_s[j, s * rows16:(s + 1) * rows16, :] >= cb, one, nil)
                    p = s % 4
                    parts[p] = ind if parts[p] is None else parts[p] + ind
                return (parts[0] + parts[1]) + (parts[2] + parts[3])

            acc = lax.fori_loop(0, nchunks, body, jnp.zeros((rows16, tq), jnp.int16))
            return jnp.sum(acc.astype(F32), axis=0, keepdims=True)

        zero = jnp.zeros((1, tq), jnp.int32)
        t0 = jnp.where(count16(zero) >= krow, zero, INT16_MIN)

        def bit_body(it, th):
            cand = th | lax.shift_left(jnp.int32(1), 14 - it)
            return jnp.where(count16(cand) >= krow, cand, th)

        return lax.fori_loop(0, 15, bit_body, t0)

    t_hi = search16()
    thb = jnp.broadcast_to(t_hi, (kc, tq))

    def low_halves(j, c):
        k = key_s[j]
        hi = lax.shift_right_arithmetic(k, 16)
        lo = (k & 0xFFFF) + INT16_MIN
        e = jnp.where(hi == thb, lo, jnp.where(hi > thb, INT16_MAX, INT16_MIN))
        half_s[j] = e.astype(jnp.int16)
        return c

    lax.fori_loop(0, nchunks, low_halves, 0)
    t_lo = search16()
    t = lax.shift_left(t_hi, 16) | (t_lo - INT16_MIN)

    c_ge = count_ge(t)
    need = krow - count_ge(t + 1)
    tb8 = jnp.broadcast_to(t, (sub, tq))

    @pl.when(jnp.max(c_ge - krow) > 0.5)
    def _():
        sub_iota = lax.broadcasted_iota(jnp.int32, (sub, tq), 0)

        def tied_below(u):
            ub = jnp.broadcast_to(u, (sub, tq))
            return count(lambda k, base: jnp.where(k == tb8, jnp.where(sub_iota + base < ub, 1.0, 0.0), 0.0))

        def ubit(it, u):
            cand = u | lax.shift_left(jnp.int32(1), idx_bits - 1 - it)
            return jnp.where(tied_below(cand) < need, cand, u)

        u = lax.fori_loop(0, idx_bits, ubit, jnp.zeros((1, tq), jnp.int32))
        ub_full = jnp.broadcast_to(u, (kc, tq))
        tb_full = jnp.broadcast_to(t, (kc, tq))
        kidx = lax.broadcasted_iota(jnp.int32, (kc, tq), 0)

        def drop(j, c):
            k = key_s[j]
            key_s[j] = jnp.where(k == tb_full, jnp.where(kidx + j * kc > ub_full, INT_MIN, k), k)
            return c

        lax.fori_loop(0, nchunks, drop, 0)

    def masked_scores(j, g):
        koff = pl.multiple_of(j * kc, kc)
        qcat = jnp.concatenate([qt_ref[g * ATTN_REP + r] for r in range(ATTN_REP)], axis=1)
        bias = jnp.concatenate([bias_s[...]] * ATTN_REP, axis=1)
        kg = kk_ref[g, pl.ds(koff, kc), :]
        vg = vt_ref[g, :, pl.ds(koff, kc)]
        return jnp.dot(kg, qcat, preferred_element_type=F32) + bias, vg

    def set_bias(j):
        bias_s[...] = jnp.where(key_s[j] >= jnp.broadcast_to(t, (kc, tq)), 0.0, NEG_BIG)

    acc_s[...] = jnp.zeros_like(acc_s)

    def att_body(j, c):
        set_bias(j)
        for g in range(ATTN_KV_HEADS):
            s, vg = masked_scores(j, g)
            acc_s[g] += jnp.dot(vg, jnp.exp(s).astype(BF16), preferred_element_type=F32)
        return c

    lax.fori_loop(0, nchunks, att_body, 0)

    norm = acc_s[:, ATTN_HEAD_DIM:ATTN_HEAD_DIM + 1, :]
    in_range = jnp.where(norm >= SOFTMAX_NORM_MIN, jnp.where(norm <= SOFTMAX_NORM_MAX, 1.0, 0.0), 0.0)

    @pl.when(jnp.min(in_range) < 0.5)
    def _():
        acc_s[...] = jnp.zeros_like(acc_s)

        def online_body(j, ms):
            set_bias(j)
            out = []
            for g in range(ATTN_KV_HEADS):
                s, vg = masked_scores(j, g)
                m_new = jnp.maximum(ms[g], jnp.max(s, axis=0, keepdims=True))
                p = jnp.exp(s - m_new).astype(BF16)
                pv = jnp.dot(vg, p, preferred_element_type=F32)
                acc_s[g] = jnp.exp(ms[g] - m_new) * acc_s[g] + pv
                out.append(m_new)
            return tuple(out)

        m_init = tuple(jnp.full((1, ATTN_REP * tq), NEG_BIG, F32) for _ in range(ATTN_KV_HEADS))
        lax.fori_loop(0, nchunks, online_body, m_init)

    for h in range(ATTN_HEADS):
        g, r = divmod(h, ATTN_REP)
        a = acc_s[g, :, r * tq:(r + 1) * tq]
        ot_s[h * ATTN_HEAD_DIM:(h + 1) * ATTN_HEAD_DIM, :] = (
            a[:ATTN_HEAD_DIM] / a[ATTN_HEAD_DIM:ATTN_HEAD_DIM + 1])
    o_ref[...] = ot_s[...].T.astype(o_ref.dtype)


def _dsa(qt, iqt, iwt, kk, vt, ik, batch, seq):
    n = batch * seq
    tq, kc = DSA_TQ, DSA_KC
    assert seq % tq == 0
    nq = seq // tq
    topk = min(TOPK_MAX, seq // 4)
    nch = seq // kc
    nout = ATTN_HEADS * ATTN_HEAD_DIM
    qmap = lambda b, i: (0, 0, b * nq + i)
    est = (nch * kc * tq * 6 + kc * tq * 4 + ATTN_HEADS * (VT_ROWS + 8) * tq * 4 + nout * tq * 4
           + ATTN_KV_HEADS * seq * (V7X_LANES + VT_ROWS) * 2 + seq * V7X_LANES * 2
           + 2 * (2 * ATTN_HEADS * ATTN_HEAD_DIM * tq * 2 + 8 * tq * 4 + tq * nout * 2))
    return pl.pallas_call(
        functools.partial(_dsa_kernel, topk=topk, idx_bits=(seq - 1).bit_length()),
        out_shape=jax.ShapeDtypeStruct((n, nout), BF16),
        grid=(batch, nq),
        in_specs=[
            pl.BlockSpec((ATTN_HEADS, ATTN_HEAD_DIM, tq), qmap),
            pl.BlockSpec((IDX_HEADS, IDX_DIM, tq), qmap),
            pl.BlockSpec((IDX_HEADS, tq), lambda b, i: (0, b * nq + i)),
            pl.BlockSpec((ATTN_KV_HEADS, seq, ATTN_HEAD_DIM), lambda b, i: (0, b, 0),
                         pipeline_mode=pl.Buffered(1)),
            pl.BlockSpec((ATTN_KV_HEADS, VT_ROWS, seq), lambda b, i: (0, 0, b),
                         pipeline_mode=pl.Buffered(1)),
            pl.BlockSpec((seq, IDX_DIM), lambda b, i: (b, 0), pipeline_mode=pl.Buffered(1)),
        ],
        out_specs=pl.BlockSpec((tq, nout), lambda b, i: (b * nq + i, 0)),
        scratch_shapes=[
            pltpu.VMEM((nch, kc, tq), jnp.int32),
            pltpu.VMEM((nch, kc, tq), jnp.int16),
            pltpu.VMEM((kc, tq), F32),
            pltpu.VMEM((ATTN_KV_HEADS, VT_ROWS, ATTN_REP * tq), F32),
            pltpu.VMEM((nout, tq), F32),
        ],
        compiler_params=pltpu.CompilerParams(
            dimension_semantics=("arbitrary", "arbitrary"), vmem_limit_bytes=_vmem_limit(est)),
    )(qt, iqt, iwt, kk, vt, ik)


def _gla_kernel(gqk_ref, gv_ref, la_ref, gg_ref, gn_ref, o_ref, st_s):
    @pl.when(pl.program_id(1) == 0)
    def _():
        st_s[...] = jnp.zeros_like(st_s)

    c = GLA_CHUNK
    nqk = GLA_HEADS * GLA_DK
    r_i = lax.broadcasted_iota(jnp.int32, (c, c), 0)
    c_i = lax.broadcasted_iota(jnp.int32, (c, c), 1)
    tri = r_i >= c_i
    tri_f = tri.astype(F32)
    gn = gn_ref[...]
    for ci in range(GLA_ROWS // c):
        rows = slice(ci * c, (ci + 1) * c)
        la = la_ref[rows, :]
        b = jnp.dot(tri_f, la, preferred_element_type=F32, precision=lax.Precision.HIGHEST)
        b_last = b[c - 1:c, :]
        q = gqk_ref[rows, :nqk]
        k = gqk_ref[rows, nqk:]
        q_dec = (q * jnp.exp(b)).astype(BF16)
        k_in = (k * jnp.exp(-b)).astype(BF16)
        k_out = (k * jnp.exp(b_last - b)).astype(BF16)
        decay = jnp.exp(b_last)
        for h in range(GLA_HEADS):
            hs = slice(h * GLA_DK, (h + 1) * GLA_DK)
            vs = slice(h * GLA_DV, (h + 1) * GLA_DV)
            qd, ki, ko = q_dec[:, hs], k_in[:, hs], k_out[:, hs]
            v = gv_ref[rows, vs]
            a = lax.dot_general(qd, ki, NT_DIMS, preferred_element_type=F32)
            a = jnp.where(tri, a, 0.0).astype(BF16)
            st = st_s[h]
            o = jnp.dot(a, v, preferred_element_type=F32)
            o = o + lax.dot_general(qd, st.astype(BF16), NT_DIMS, preferred_element_type=F32)
            upd = lax.dot_general(v, ko, TN_DIMS, preferred_element_type=F32)
            st_s[h] = st * decay[:, hs] + upd
            gate = gg_ref[rows, vs]
            o_ref[rows, vs] = (_rms(o, gn) * (gate * jax.nn.sigmoid(gate))).astype(o_ref.dtype)


def _gla(gqk, gv, la, gg, g_norm, batch, seq):
    n = batch * seq
    t = GLA_ROWS
    assert seq % t == 0
    ns = seq // t
    row = lambda b, i: (b * ns + i, 0)
    nqk = GLA_HEADS * GLA_DK
    nv = GLA_HEADS * GLA_DV
    return pl.pallas_call(
        _gla_kernel,
        out_shape=jax.ShapeDtypeStruct((n, nv), BF16),
        grid=(batch, ns),
        in_specs=[
            pl.BlockSpec((t, 2 * nqk), row),
            pl.BlockSpec((t, nv), row),
            pl.BlockSpec((t, nqk), row),
            pl.BlockSpec((t, nv), row),
            _const_spec((1, GLA_DV)),
        ],
        out_specs=pl.BlockSpec((t, nv), row),
        scratch_shapes=[pltpu.VMEM((GLA_HEADS, GLA_DV, GLA_DK), F32)],
        compiler_params=pltpu.CompilerParams(dimension_semantics=("arbitrary", "arbitrary")),
    )(gqk, gv, la, gg, g_norm.reshape(1, GLA_DV))


def _outproj_kernel(x_ref, oa_ref, og_ref, w_ref, g_ref, o_ref):
    m = jnp.dot(oa_ref[...], w_ref[0], preferred_element_type=F32)
    m = m + jnp.dot(og_ref[...], w_ref[1], preferred_element_type=F32)
    o_ref[...] = x_ref[...] + _rms(m, g_ref[...])


def _outproj(x1, oa, og, w_out, g_post):
    n, d = x1.shape
    t = FFN_ROWS
    half = oa.shape[1]
    assert og.shape[1] == half and w_out.shape[0] == 2 * half
    w = w_out.reshape(2, half, d).astype(BF16)
    row = lambda i: (i, 0)
    est = 4 * t * d * 4 + 4 * t * half * 2 + 2 * half * d * 2 + t * d * 4
    return pl.pallas_call(
        _outproj_kernel,
        out_shape=jax.ShapeDtypeStruct((n, d), F32),
        grid=(n // t,),
        in_specs=[
            pl.BlockSpec((t, d), row),
            pl.BlockSpec((t, half), row),
            pl.BlockSpec((t, half), row),
            _const_spec(w.shape),
            _const_spec((1, d)),
        ],
        out_specs=pl.BlockSpec((t, d), row),
        compiler_params=pltpu.CompilerParams(
            dimension_semantics=("arbitrary",), vmem_limit_bytes=_vmem_limit(est)),
    )(x1, oa, og, w, g_post.reshape(1, d))


def kernel(x, g_ffn1_pre, w_ffn1_gate, w_ffn1_up, w_ffn1_down, g_ffn1_post, g_mix_pre, w_in, w_gla_a2,
           b_gla_a, g_gla_norm, w_out, g_mix_post, g_ffn2_pre, w_ffn2_gate, w_ffn2_up, w_ffn2_down,
           g_ffn2_post):
    batch, seq, d = x.shape
    h = x.reshape(batch * seq, d)
    for l in range(g_ffn1_pre.shape[0]):
        h = _ffn(h, g_ffn1_pre[l], w_ffn1_gate[l], w_ffn1_up[l], w_ffn1_down[l], g_ffn1_post[l])
        qt, iqt, vt, iwt, kk, ik, gqk, gv, gg, la = _inproj(h, g_mix_pre[l], w_in[l], w_gla_a2[l], b_gla_a[l])
        oa = _dsa(qt, iqt, iwt, kk, vt, ik, batch, seq)
        og = _gla(gqk, gv, la, gg, g_gla_norm[l], batch, seq)
        h = _outproj(h, oa, og, w_out[l], g_mix_post[l])
        h = _ffn(h, g_ffn2_pre[l], w_ffn2_gate[l], w_ffn2_up[l], w_ffn2_down[l], g_ffn2_post[l])
    return h.reshape(batch, seq, d)
```

```python
import functools

import jax
import jax.numpy as jnp
from jax import lax
from jax.experimental import pallas as pl
from jax.experimental.pallas import tpu as pltpu

ATTN_HEADS = 8
ATTN_KV_HEADS = 2
ATTN_HEAD_DIM = 64
ATTN_REP = ATTN_HEADS // ATTN_KV_HEADS
IDX_HEADS = 8
IDX_DIM = 64
TOPK_MAX = 256
GLA_HEADS = 4
GLA_DK = 64
GLA_DV = 128
GLA_GATE_RANK = 16
GLA_TAU = 16.0
GLA_CHUNK = 64
EPS = 1e-6

V7X_LANES = 128
V7X_SUBLANES = 8
V7X_BF16_ROWS = 16
V7X_MXU_DIM = 256
V7X_VMEM_BYTES = 64 * 2**20

FFN_ROWS = 512
FF_CHUNK = V7X_MXU_DIM
DSA_TQ = 256
DSA_KC = DSA_TQ
GLA_ROWS = 256
VT_ROWS = ATTN_HEAD_DIM + V7X_BF16_ROWS

INT_MIN = -2**31
INT16_MIN, INT16_MAX = -2**15, 2**15 - 1
NEG_BIG = -1e30
SOFTMAX_NORM_MIN = 2.0 ** -60
SOFTMAX_NORM_MAX = 2.0 ** 100

F32 = jnp.float32
BF16 = jnp.bfloat16
NT_DIMS = (((1,), (1,)), ((), ()))
TN_DIMS = (((0,), (0,)), ((), ()))


def _vmem_limit(nbytes):
    return int(min(nbytes * 1.25 + (8 << 20), V7X_VMEM_BYTES - (6 << 20)))


def _rms(x, g):
    return x * lax.rsqrt(jnp.mean(x * x, axis=-1, keepdims=True) + EPS) * g


def _const_spec(shape):
    nd = len(shape)
    return pl.BlockSpec(shape, lambda *_: (0,) * nd, pipeline_mode=pl.Buffered(1))


def _ffn_kernel(x_ref, gpre_ref, wg_ref, wu_ref, wd_ref, gpost_ref, o_ref, acc_ref):
    x = x_ref[...]
    xn = _rms(x, gpre_ref[...]).astype(BF16)
    acc_ref[...] = jnp.zeros_like(acc_ref)

    def body(c, carry):
        g = jnp.dot(xn, wg_ref[c], preferred_element_type=F32)
        u = jnp.dot(xn, wu_ref[c], preferred_element_type=F32)
        a = (g * jax.nn.sigmoid(g) * u).astype(BF16)
        acc_ref[...] += jnp.dot(a, wd_ref[c], preferred_element_type=F32)
        return carry

    lax.fori_loop(0, wg_ref.shape[0], body, 0)
    o_ref[...] = x + 0.5 * _rms(acc_ref[...], gpost_ref[...])


def _ffn(x, g_pre, w_gate, w_up, w_down, g_post):
    n, d = x.shape
    dff = w_gate.shape[1]
    nch = dff // FF_CHUNK
    assert nch * FF_CHUNK == dff and n % FFN_ROWS == 0
    wg = w_gate.reshape(d, nch, FF_CHUNK).transpose(1, 0, 2).astype(BF16)
    wu = w_up.reshape(d, nch, FF_CHUNK).transpose(1, 0, 2).astype(BF16)
    wd = w_down.reshape(nch, FF_CHUNK, d).astype(BF16)
    row = lambda i: (i, 0)
    est = 4 * FFN_ROWS * d * 4 + 3 * d * dff * 2 + FFN_ROWS * d * 8 + 4 * FFN_ROWS * FF_CHUNK * 4
    return pl.pallas_call(
        _ffn_kernel,
        out_shape=jax.ShapeDtypeStruct((n, d), F32),
        grid=(n // FFN_ROWS,),
        in_specs=[
            pl.BlockSpec((FFN_ROWS, d), row),
            _const_spec((1, d)),
            _const_spec(wg.shape), _const_spec(wu.shape), _const_spec(wd.shape),
            _const_spec((1, d)),
        ],
        out_specs=pl.BlockSpec((FFN_ROWS, d), row),
        scratch_shapes=[pltpu.VMEM((FFN_ROWS, d), F32)],
        compiler_params=pltpu.CompilerParams(
            dimension_semantics=("arbitrary",), vmem_limit_bytes=_vmem_limit(est)),
    )(x, g_pre.reshape(1, d), wg, wu, wd, g_post.reshape(1, d))


_K_OFF, _MISC_OFF, _GQ_OFF, _GK_OFF, _GV_OFF, _GG_OFF, _TOK_COLS = 0, 128, 256, 512, 768, 1280, 1792
_GA_LANE = 72
_QT_OFF, _IQT_OFF, _VT_OFF, _IWT_OFF, _T_ROWS = 0, 512, 1024, 1152, 1168


def _inproj_kernel(x_ref, g_ref, wtok_ref, wt_ref, wa2_ref, ba_ref,
                   qt_ref, iqt_ref, vt_ref, iwt_ref, kk_ref, ik_ref, gqk_ref, gv_ref, gg_ref, la_ref):
    h = _rms(x_ref[...], g_ref[...]).astype(BF16)
    t = h.shape[0]
    pt = lax.dot_general(wt_ref[...], h, NT_DIMS, preferred_element_type=F32)
    for i in range(ATTN_HEADS):
        qt_ref[i] = pt[_QT_OFF + 64 * i:_QT_OFF + 64 * (i + 1)].astype(BF16)
    for i in range(IDX_HEADS):
        iqt_ref[i] = pt[_IQT_OFF + 64 * i:_IQT_OFF + 64 * (i + 1)].astype(BF16)
    ones_row = (lax.broadcasted_iota(jnp.int32, (V7X_BF16_ROWS, t), 0) == 0).astype(F32)
    for g in range(ATTN_KV_HEADS):
        v_t = pt[_VT_OFF + 64 * g:_VT_OFF + 64 * (g + 1)]
        vt_ref[g] = jnp.concatenate([v_t, ones_row], axis=0).astype(BF16)
    iwt_ref[...] = pt[_IWT_OFF:_IWT_OFF + IDX_HEADS] * (IDX_HEADS ** -0.5)

    proj = jnp.dot(h, wtok_ref[...], preferred_element_type=F32)
    for g in range(ATTN_KV_HEADS):
        kk_ref[g] = proj[:, _K_OFF + 64 * g:_K_OFF + 64 * (g + 1)].astype(BF16)
    misc = proj[:, _MISC_OFF:_MISC_OFF + V7X_LANES]
    ik_ref[...] = misc[:, :IDX_DIM].astype(BF16)
    gqk_ref[...] = proj[:, _GQ_OFF:_GV_OFF]
    gv_ref[...] = proj[:, _GV_OFF:_GG_OFF].astype(BF16)
    gg_ref[...] = proj[:, _GG_OFF:_GG_OFF + GLA_HEADS * GLA_DV]
    z = jnp.dot(misc, wa2_ref[...], preferred_element_type=F32, precision=lax.Precision.HIGHEST)
    z = z + ba_ref[...]
    log_sig = jnp.minimum(z, 0.0) - jnp.log1p(jnp.exp(-jnp.abs(z)))
    la_ref[...] = log_sig * (1.0 / GLA_TAU)


def _pack_w_in(w_in):
    d = w_in.shape[0]
    sizes = (512, 128, 128, 512, 64, 8, 256, 256, 512, 16, 512)
    offs = [0]
    for s in sizes:
        offs.append(offs[-1] + s)
    aq, ak, av, iq, ik, iw, gq, gk, gv, ga, gg = [w_in[:, offs[i]:offs[i + 1]] for i in range(len(sizes))]
    z = lambda n: jnp.zeros((d, n), w_in.dtype)
    w_tok = jnp.concatenate([ak, ik, z(8), ga, z(40), gq * (GLA_DK ** -0.5), gk, gv, gg], axis=1)
    w_t = jnp.concatenate([aq * (ATTN_HEAD_DIM ** -0.5), iq * (IDX_DIM ** -0.5), av, iw, z(8)], axis=1).T
    assert w_tok.shape[1] == _TOK_COLS and w_t.shape[0] == _T_ROWS
    return w_tok.astype(BF16), w_t.astype(BF16)


def _inproj(x1, g_mix_pre, w_in, w_gla_a2, b_gla_a):
    n, d = x1.shape
    t = FFN_ROWS
    w_tok, w_t = _pack_w_in(w_in)
    nqk = GLA_HEADS * GLA_DK
    nv = GLA_HEADS * GLA_DV
    wa2 = jnp.zeros((V7X_LANES, nqk), F32).at[_GA_LANE:_GA_LANE + GLA_GATE_RANK].set(w_gla_a2)
    row = lambda i: (i, 0)
    row3 = lambda i: (0, i, 0)
    col3 = lambda i: (0, 0, i)
    out_shape = (
        jax.ShapeDtypeStruct((ATTN_HEADS, ATTN_HEAD_DIM, n), BF16),
        jax.ShapeDtypeStruct((IDX_HEADS, IDX_DIM, n), BF16),
        jax.ShapeDtypeStruct((ATTN_KV_HEADS, VT_ROWS, n), BF16),
        jax.ShapeDtypeStruct((IDX_HEADS, n), F32),
        jax.ShapeDtypeStruct((ATTN_KV_HEADS, n, ATTN_HEAD_DIM), BF16),
        jax.ShapeDtypeStruct((n, IDX_DIM), BF16),
        jax.ShapeDtypeStruct((n, 2 * nqk), F32),
        jax.ShapeDtypeStruct((n, nv), BF16),
        jax.ShapeDtypeStruct((n, nv), F32),
        jax.ShapeDtypeStruct((n, nqk), F32),
    )
    out_specs = (
        pl.BlockSpec((ATTN_HEADS, ATTN_HEAD_DIM, t), col3),
        pl.BlockSpec((IDX_HEADS, IDX_DIM, t), col3),
        pl.BlockSpec((ATTN_KV_HEADS, VT_ROWS, t), col3),
        pl.BlockSpec((IDX_HEADS, t), lambda i: (0, i)),
        pl.BlockSpec((ATTN_KV_HEADS, t, ATTN_HEAD_DIM), row3),
        pl.BlockSpec((t, IDX_DIM), row),
        pl.BlockSpec((t, 2 * nqk), row),
        pl.BlockSpec((t, nv), row),
        pl.BlockSpec((t, nv), row),
        pl.BlockSpec((t, nqk), row),
    )
    est = (2 * t * d * 4 + d * (_TOK_COLS + _T_ROWS) * 2 + t * (_TOK_COLS + _T_ROWS) * 4 * 2
           + 2 * t * 8 * 1024)
    return pl.pallas_call(
        _inproj_kernel,
        out_shape=out_shape,
        grid=(n // t,),
        in_specs=[
            pl.BlockSpec((t, d), row),
            _const_spec((1, d)),
            _const_spec(w_tok.shape),
            _const_spec(w_t.shape),
            _const_spec(wa2.shape),
            _const_spec((1, nqk)),
        ],
        out_specs=out_specs,
        compiler_params=pltpu.CompilerParams(
            dimension_semantics=("arbitrary",), vmem_limit_bytes=_vmem_limit(est)),
    )(x1, g_mix_pre.reshape(1, d), w_tok, w_t, wa2, b_gla_a.reshape(1, nqk))


def _dsa_kernel(qt_ref, iqt_ref, iwt_ref, kk_ref, vt_ref, ik_ref, o_ref,
                key_s, half_s, bias_s, acc_s, ot_s, *, topk, idx_bits):
    tq = DSA_TQ
    kc = DSA_KC
    sub = V7X_SUBLANES
    i = pl.program_id(1)
    nchunks = i + 1

    def score_chunk(j, diagonal):
        ikc = ik_ref[pl.ds(pl.multiple_of(j * kc, kc), kc), :]
        iqcat = jnp.concatenate([iqt_ref[h] for h in range(IDX_HEADS)], axis=1)
        lg = jnp.dot(ikc, iqcat, preferred_element_type=F32)
        sc = jnp.zeros((kc, tq), F32)
        for h in range(IDX_HEADS):
            sc = sc + jnp.maximum(lg[:, h * tq:(h + 1) * tq], 0.0) * iwt_ref[h:h + 1, :]
        bits = pltpu.bitcast(sc, jnp.int32)
        key = jnp.where(bits < 0, bits ^ 0x7FFFFFFF, bits)
        if diagonal:
            kpos = lax.broadcasted_iota(jnp.int32, (kc, tq), 0)
            qpos = lax.broadcasted_iota(jnp.int32, (kc, tq), 1)
            key = jnp.where(kpos <= qpos, key, INT_MIN)
        key_s[j] = key
        half_s[j] = lax.shift_right_arithmetic(key, 16).astype(jnp.int16)

    def for_chunks_paired(n, fn):
        def pair(jj, c):
            fn(2 * jj)
            fn(2 * jj + 1)
            return c

        lax.fori_loop(0, lax.shift_right_logical(n, 1), pair, 0)

        @pl.when((n & 1) == 1)
        def _():
            fn(n - 1)

    for_chunks_paired(i, lambda j: score_chunk(j, False))
    score_chunk(i, True)

    qidx = i * tq + lax.broadcasted_iota(jnp.int32, (1, tq), 1)
    krow = jnp.minimum(topk, qidx + 1).astype(F32)

    def count(ind_fn):
        def body(j, acc):
            parts = [acc, None, None, None]
            for s in range(kc // sub):
                ind = ind_fn(key_s[j, s * sub:(s + 1) * sub, :], j * kc + s * sub)
                p = s % 4
                parts[p] = ind if parts[p] is None else parts[p] + ind
            return (parts[0] + parts[1]) + (parts[2] + parts[3])
        acc = lax.fori_loop(0, nchunks, body, jnp.zeros((sub, tq), F32))
        return jnp.sum(acc, axis=0, keepdims=True)

    def count_ge(cand):
        cb = jnp.broadcast_to(cand, (sub, tq))
        return count(lambda k, base: jnp.where(k >= cb, 1.0, 0.0))

    def search16():
        rows16 = V7X_BF16_ROWS

        def count16(cand):
            cb = jnp.broadcast_to(cand, (rows16, tq)).astype(jnp.int16)
            one, nil = jnp.int16(1), jnp.int16(0)

            def body(j, acc):
                parts = [acc, None, None, None]
                for s in range(kc // rows16):
                    ind = jnp.where(half_s[j, s * rows16:(s + 1) * rows16, :] >= cb, one, nil)
                    p = s % 4
                    parts[p] = ind if parts[p] is None else parts[p] + ind
                return (parts[0] + parts[1]) + (parts[2] + parts[3])

            acc = lax.fori_loop(0, nchunks, body, jnp.zeros((rows16, tq), jnp.int16))
            return jnp.sum(acc.astype(F32), axis=0, keepdims=True)

        zero = jnp.zeros((1, tq), jnp.int32)
        t0 = jnp.where(count16(zero) >= krow, zero, INT16_MIN)

        def bit_body(it, th):
            cand = th | lax.shift_left(jnp.int32(1), 14 - it)
            return jnp.where(count16(cand) >= krow, cand, th)

        return lax.fori_loop(0, 15, bit_body, t0)

    t_hi = search16()
    thb = jnp.broadcast_to(t_hi, (kc, tq))

    def low_halves(j, c):
        k = key_s[j]
        hi = lax.shift_right_arithmetic(k, 16)
        lo = (k & 0xFFFF) + INT16_MIN
        e = jnp.where(hi == thb, lo, jnp.where(hi > thb, INT16_MAX, INT16_MIN))
        half_s[j] = e.astype(jnp.int16)
        return c

    lax.fori_loop(0, nchunks, low_halves, 0)
    t_lo = search16()
    t = lax.shift_left(t_hi, 16) | (t_lo - INT16_MIN)

    c_ge = count_ge(t)
    need = krow - count_ge(t + 1)
    tb8 = jnp.broadcast_to(t, (sub, tq))

    @pl.when(jnp.max(c_ge - krow) > 0.5)
    def _():
        sub_iota = lax.broadcasted_iota(jnp.int32, (sub, tq), 0)

        def tied_below(u):
            ub = jnp.broadcast_to(u, (sub, tq))
            return count(lambda k, base: jnp.where(k == tb8, jnp.where(sub_iota + base < ub, 1.0, 0.0), 0.0))

        def ubit(it, u):
            cand = u | lax.shift_left(jnp.int32(1), idx_bits - 1 - it)
            return jnp.where(tied_below(cand) < need, cand, u)

        u = lax.fori_loop(0, idx_bits, ubit, jnp.zeros((1, tq), jnp.int32))
        ub_full = jnp.broadcast_to(u, (kc, tq))
        tb_full = jnp.broadcast_to(t, (kc, tq))
        kidx = lax.broadcasted_iota(jnp.int32, (kc, tq), 0)

        def drop(j, c):
            k = key_s[j]
            key_s[j] = jnp.where(k == tb_full, jnp.where(kidx + j * kc > ub_full, INT_MIN, k), k)
            return c

        lax.fori_loop(0, nchunks, drop, 0)

    def masked_scores(j, g, slot=0):
        koff = pl.multiple_of(j * kc, kc)
        qcat = jnp.concatenate([qt_ref[g * ATTN_REP + r] for r in range(ATTN_REP)], axis=1)
        bias = jnp.concatenate([bias_s[slot]] * ATTN_REP, axis=1)
        kg = kk_ref[g, pl.ds(koff, kc), :]
        vg = vt_ref[g, :, pl.ds(koff, kc)]
        return jnp.dot(kg, qcat, preferred_element_type=F32) + bias, vg

    def set_bias(j, slot=0):
        bias_s[slot] = jnp.where(key_s[j] >= jnp.broadcast_to(t, (kc, tq)), 0.0, NEG_BIG)

    acc_s[...] = jnp.zeros_like(acc_s)

    def att_chunk(j, slot):
        set_bias(j, slot)
        for g in range(ATTN_KV_HEADS):
            s, vg = masked_scores(j, g, slot)
            acc_s[g] += jnp.dot(vg, jnp.exp(s).astype(BF16), preferred_element_type=F32)

    def att_pair(jj, c):
        work = []
        for slot in range(2):
            j = 2 * jj + slot
            set_bias(j, slot)
            for g in range(ATTN_KV_HEADS):
                work.append((g, masked_scores(j, g, slot)))
        for g, (s, vg) in work:
            acc_s[g] += jnp.dot(vg, jnp.exp(s).astype(BF16), preferred_element_type=F32)
        return c

    lax.fori_loop(0, lax.shift_right_logical(nchunks, 1), att_pair, 0)

    @pl.when((nchunks & 1) == 1)
    def _():
        att_chunk(nchunks - 1, 0)

    norm = acc_s[:, ATTN_HEAD_DIM:ATTN_HEAD_DIM + 1, :]
    in_range = jnp.where(norm >= SOFTMAX_NORM_MIN, jnp.where(norm <= SOFTMAX_NORM_MAX, 1.0, 0.0), 0.0)

    @pl.when(jnp.min(in_range) < 0.5)
    def _():
        acc_s[...] = jnp.zeros_like(acc_s)

        def online_body(j, ms):
            set_bias(j)
            out = []
            for g in range(ATTN_KV_HEADS):
                s, vg = masked_scores(j, g)
                m_new = jnp.maximum(ms[g], jnp.max(s, axis=0, keepdims=True))
                p = jnp.exp(s - m_new).astype(BF16)
                pv = jnp.dot(vg, p, preferred_element_type=F32)
                acc_s[g] = jnp.exp(ms[g] - m_new) * acc_s[g] + pv
                out.append(m_new)
            return tuple(out)

        m_init = tuple(jnp.full((1, ATTN_REP * tq), NEG_BIG, F32) for _ in range(ATTN_KV_HEADS))
        lax.fori_loop(0, nchunks, online_body, m_init)

    for h in range(ATTN_HEADS):
        g, r = divmod(h, ATTN_REP)
        a = acc_s[g, :, r * tq:(r + 1) * tq]
        ot_s[h * ATTN_HEAD_DIM:(h + 1) * ATTN_HEAD_DIM, :] = (
            a[:ATTN_HEAD_DIM] / a[ATTN_HEAD_DIM:ATTN_HEAD_DIM + 1])
    o_ref[...] = ot_s[...].T.astype(o_ref.dtype)


def _dsa(qt, iqt, iwt, kk, vt, ik, batch, seq):
    n = batch * seq
    tq, kc = DSA_TQ, DSA_KC
    assert seq % tq == 0
    nq = seq // tq
    topk = min(TOPK_MAX, seq // 4)
    nch = seq // kc
    nout = ATTN_HEADS * ATTN_HEAD_DIM
    qmap = lambda b, i: (0, 0, b * nq + i)
    est = (nch * kc * tq * 6 + kc * tq * 4 + ATTN_HEADS * (VT_ROWS + 8) * tq * 4 + nout * tq * 4
           + ATTN_KV_HEADS * seq * (V7X_LANES + VT_ROWS) * 2 + seq * V7X_LANES * 2
           + 2 * (2 * ATTN_HEADS * ATTN_HEAD_DIM * tq * 2 + 8 * tq * 4 + tq * nout * 2))
    return pl.pallas_call(
        functools.partial(_dsa_kernel, topk=topk, idx_bits=(seq - 1).bit_length()),
        out_shape=jax.ShapeDtypeStruct((n, nout), BF16),
        grid=(batch, nq),
        in_specs=[
            pl.BlockSpec((ATTN_HEADS, ATTN_HEAD_DIM, tq), qmap),
            pl.BlockSpec((IDX_HEADS, IDX_DIM, tq), qmap),
            pl.BlockSpec((IDX_HEADS, tq), lambda b, i: (0, b * nq + i)),
            pl.BlockSpec((ATTN_KV_HEADS, seq, ATTN_HEAD_DIM), lambda b, i: (0, b, 0),
                         pipeline_mode=pl.Buffered(1)),
            pl.BlockSpec((ATTN_KV_HEADS, VT_ROWS, seq), lambda b, i: (0, 0, b),
                         pipeline_mode=pl.Buffered(1)),
            pl.BlockSpec((seq, IDX_DIM), lambda b, i: (b, 0), pipeline_mode=pl.Buffered(1)),
        ],
        out_specs=pl.BlockSpec((tq, nout), lambda b, i: (b * nq + i, 0)),
        scratch_shapes=[
            pltpu.VMEM((nch, kc, tq), jnp.int32),
            pltpu.VMEM((nch, kc, tq), jnp.int16),
            pltpu.VMEM((2, kc, tq), F32),
            pltpu.VMEM((ATTN_KV_HEADS, VT_ROWS, ATTN_REP * tq), F32),
            pltpu.VMEM((nout, tq), F32),
        ],
        compiler_params=pltpu.CompilerParams(
            dimension_semantics=("arbitrary", "arbitrary"), vmem_limit_bytes=_vmem_limit(est)),
    )(qt, iqt, iwt, kk, vt, ik)


def _gla_kernel(gqk_ref, gv_ref, la_ref, gg_ref, gn_ref, o_ref, st_s):
    @pl.when(pl.program_id(1) == 0)
    def _():
        st_s[...] = jnp.zeros_like(st_s)

    c = GLA_CHUNK
    nqk = GLA_HEADS * GLA_DK
    r_i = lax.broadcasted_iota(jnp.int32, (c, c), 0)
    c_i = lax.broadcasted_iota(jnp.int32, (c, c), 1)
    tri = r_i >= c_i
    tri_f = tri.astype(F32)
    gn = gn_ref[...]
    for ci in range(GLA_ROWS // c):
        rows = slice(ci * c, (ci + 1) * c)
        la = la_ref[rows, :]
        b = jnp.dot(tri_f, la, preferred_element_type=F32, precision=lax.Precision.HIGHEST)
        b_last = b[c - 1:c, :]
        q = gqk_ref[rows, :nqk]
        k = gqk_ref[rows, nqk:]
        q_dec = (q * jnp.exp(b)).astype(BF16)
        k_in = (k * jnp.exp(-b)).astype(BF16)
        k_out = (k * jnp.exp(b_last - b)).astype(BF16)
        decay = jnp.exp(b_last)
        for h in range(GLA_HEADS):
            hs = slice(h * GLA_DK, (h + 1) * GLA_DK)
            vs = slice(h * GLA_DV, (h + 1) * GLA_DV)
            qd, ki, ko = q_dec[:, hs], k_in[:, hs], k_out[:, hs]
            v = gv_ref[rows, vs]
            a = lax.dot_general(qd, ki, NT_DIMS, preferred_element_type=F32)
            a = jnp.where(tri, a, 0.0).astype(BF16)
            st = st_s[h]
            o = jnp.dot(a, v, preferred_element_type=F32)
            o = o + lax.dot_general(qd, st.astype(BF16), NT_DIMS, preferred_element_type=F32)
            upd = lax.dot_general(v, ko, TN_DIMS, preferred_element_type=F32)
            st_s[h] = st * decay[:, hs] + upd
            gate = gg_ref[rows, vs]
            o_ref[rows, vs] = (_rms(o, gn) * (gate * jax.nn.sigmoid(gate))).astype(o_ref.dtype)


def _gla(gqk, gv, la, gg, g_norm, batch, seq):
    n = batch * seq
    t = GLA_ROWS
    assert seq % t == 0
    ns = seq // t
    row = lambda b, i: (b * ns + i, 0)
    nqk = GLA_HEADS * GLA_DK
    nv = GLA_HEADS * GLA_DV
    return pl.pallas_call(
        _gla_kernel,
        out_shape=jax.ShapeDtypeStruct((n, nv), BF16),
        grid=(batch, ns),
        in_specs=[
            pl.BlockSpec((t, 2 * nqk), row),
            pl.BlockSpec((t, nv), row),
            pl.BlockSpec((t, nqk), row),
            pl.BlockSpec((t, nv), row),
            _const_spec((1, GLA_DV)),
        ],
        out_specs=pl.BlockSpec((t, nv), row),
        scratch_shapes=[pltpu.VMEM((GLA_HEADS, GLA_DV, GLA_DK), F32)],
        compiler_params=pltpu.CompilerParams(dimension_semantics=("arbitrary", "arbitrary")),
    )(gqk, gv, la, gg, g_norm.reshape(1, GLA_DV))


def _outproj_kernel(x_ref, oa_ref, og_ref, w_ref, g_ref, o_ref):
    m = jnp.dot(oa_ref[...], w_ref[0], preferred_element_type=F32)
    m = m + jnp.dot(og_ref[...], w_ref[1], preferred_element_type=F32)
    o_ref[...] = x_ref[...] + _rms(m, g_ref[...])


def _outproj(x1, oa, og, w_out, g_post):
    n, d = x1.shape
    t = FFN_ROWS
    half = oa.shape[1]
    assert og.shape[1] == half and w_out.shape[0] == 2 * half
    w = w_out.reshape(2, half, d).astype(BF16)
    row = lambda i: (i, 0)
    est = 4 * t * d * 4 + 4 * t * half * 2 + 2 * half * d * 2 + t * d * 4
    return pl.pallas_call(
        _outproj_kernel,
        out_shape=jax.ShapeDtypeStruct((n, d), F32),
        grid=(n // t,),
        in_specs=[
            pl.BlockSpec((t, d), row),
            pl.BlockSpec((t, half), row),
            pl.BlockSpec((t, half), row),
            _const_spec(w.shape),
            _const_spec((1, d)),
        ],
        out_specs=pl.BlockSpec((t, d), row),
        compiler_params=pltpu.CompilerParams(
            dimension_semantics=("arbitrary",), vmem_limit_bytes=_vmem_limit(est)),
    )(x1, oa, og, w, g_post.reshape(1, d))


def kernel(x, g_ffn1_pre, w_ffn1_gate, w_ffn1_up, w_ffn1_down, g_ffn1_post, g_mix_pre, w_in, w_gla_a2,
           b_gla_a, g_gla_norm, w_out, g_mix_post, g_ffn2_pre, w_ffn2_gate, w_ffn2_up, w_ffn2_down,
           g_ffn2_post):
    batch, seq, d = x.shape
    h = x.reshape(batch * seq, d)
    for l in range(g_ffn1_pre.shape[0]):
        h = _ffn(h, g_ffn1_pre[l], w_ffn1_gate[l], w_ffn1_up[l], w_ffn1_down[l], g_ffn1_post[l])
        qt, iqt, vt, iwt, kk, ik, gqk, gv, gg, la = _inproj(h, g_mix_pre[l], w_in[l], w_gla_a2[l], b_gla_a[l])
        oa = _dsa(qt, iqt, iwt, kk, vt, ik, batch, seq)
        og = _gla(gqk, gv, la, gg, g_gla_norm[l], batch, seq)
        h = _outproj(h, oa, og, w_out[l], g_mix_post[l])
        h = _ffn(h, g_ffn2_pre[l], w_ffn2_gate[l], w_ffn2_up[l], w_ffn2_down[l], g_ffn2_post[l])
    return h.reshape(batch, seq, d)
```

```python
import functools

import jax
import jax.numpy as jnp
from jax import lax
from jax.experimental import pallas as pl
from jax.experimental.pallas import tpu as pltpu

ATTN_HEADS = 8
ATTN_KV_HEADS = 2
ATTN_HEAD_DIM = 64
ATTN_REP = ATTN_HEADS // ATTN_KV_HEADS
IDX_HEADS = 8
IDX_DIM = 64
TOPK_MAX = 256
GLA_HEADS = 4
GLA_DK = 64
GLA_DV = 128
GLA_GATE_RANK = 16
GLA_TAU = 16.0
GLA_CHUNK = 64
EPS = 1e-6

V7X_LANES = 128
V7X_SUBLANES = 8
V7X_BF16_ROWS = 16
V7X_MXU_DIM = 256
V7X_VMEM_BYTES = 64 * 2**20

FFN_ROWS = 512
FF_CHUNK = V7X_MXU_DIM
DSA_TQ = 256
DSA_KC = DSA_TQ
GLA_ROWS = 256
VT_ROWS = ATTN_HEAD_DIM + V7X_BF16_ROWS

INT_MIN = -2**31
INT16_MIN, INT16_MAX = -2**15, 2**15 - 1
NEG_BIG = -1e30
SOFTMAX_NORM_MIN = 2.0 ** -60
SOFTMAX_NORM_MAX = 2.0 ** 100

F32 = jnp.float32
BF16 = jnp.bfloat16
NT_DIMS = (((1,), (1,)), ((), ()))
TN_DIMS = (((0,), (0,)), ((), ()))


def _vmem_limit(nbytes):
    return int(min(nbytes * 1.25 + (8 << 20), V7X_VMEM_BYTES - (6 << 20)))


def _rms(x, g):
    return x * lax.rsqrt(jnp.mean(x * x, axis=-1, keepdims=True) + EPS) * g


def _const_spec(shape):
    nd = len(shape)
    return pl.BlockSpec(shape, lambda *_: (0,) * nd, pipeline_mode=pl.Buffered(1))


def _ffn_kernel(x_ref, gpre_ref, wg_ref, wu_ref, wd_ref, gpost_ref, o_ref):
    x = x_ref[...]
    xn = _rms(x, gpre_ref[...]).astype(BF16)

    def gate_up(c):
        return (jnp.dot(xn, wg_ref[c], preferred_element_type=F32),
                jnp.dot(xn, wu_ref[c], preferred_element_type=F32))

    nch = wg_ref.shape[0]
    acc = None
    nxt = gate_up(0)
    for c in range(nch):
        g, u = nxt
        if c + 1 < nch:
            nxt = gate_up(c + 1)
        a = (g * jax.nn.sigmoid(g) * u).astype(BF16)
        d = jnp.dot(a, wd_ref[c], preferred_element_type=F32)
        acc = d if acc is None else acc + d
    o_ref[...] = x + 0.5 * _rms(acc, gpost_ref[...])


def _ffn(x, g_pre, w_gate, w_up, w_down, g_post):
    n, d = x.shape
    dff = w_gate.shape[1]
    nch = dff // FF_CHUNK
    assert nch * FF_CHUNK == dff and n % FFN_ROWS == 0
    wg = w_gate.reshape(d, nch, FF_CHUNK).transpose(1, 0, 2).astype(BF16)
    wu = w_up.reshape(d, nch, FF_CHUNK).transpose(1, 0, 2).astype(BF16)
    wd = w_down.reshape(nch, FF_CHUNK, d).astype(BF16)
    row = lambda i: (i, 0)
    est = 4 * FFN_ROWS * d * 4 + 3 * d * dff * 2 + FFN_ROWS * d * 8 + 4 * FFN_ROWS * FF_CHUNK * 4
    return pl.pallas_call(
        _ffn_kernel,
        out_shape=jax.ShapeDtypeStruct((n, d), F32),
        grid=(n // FFN_ROWS,),
        in_specs=[
            pl.BlockSpec((FFN_ROWS, d), row),
            _const_spec((1, d)),
            _const_spec(wg.shape), _const_spec(wu.shape), _const_spec(wd.shape),
            _const_spec((1, d)),
        ],
        out_specs=pl.BlockSpec((FFN_ROWS, d), row),
        compiler_params=pltpu.CompilerParams(
            dimension_semantics=("arbitrary",), vmem_limit_bytes=_vmem_limit(est)),
    )(x, g_pre.reshape(1, d), wg, wu, wd, g_post.reshape(1, d))


_K_OFF, _MISC_OFF, _GQ_OFF, _GK_OFF, _GV_OFF, _GG_OFF, _TOK_COLS = 0, 128, 256, 512, 768, 1280, 1792
_GA_LANE = 72
_QT_OFF, _IQT_OFF, _VT_OFF, _IWT_OFF, _T_ROWS = 0, 512, 1024, 1152, 1168


def _inproj_kernel(x_ref, g_ref, wtok_ref, wt_ref, wa2_ref, ba_ref,
                   qt_ref, iqt_ref, vt_ref, iwt_ref, kk_ref, ik_ref, gqk_ref, gv_ref, gg_ref, la_ref):
    h = _rms(x_ref[...], g_ref[...]).astype(BF16)
    t = h.shape[0]
    pt = lax.dot_general(wt_ref[...], h, NT_DIMS, preferred_element_type=F32)
    for i in range(ATTN_HEADS):
        qt_ref[i] = pt[_QT_OFF + 64 * i:_QT_OFF + 64 * (i + 1)].astype(BF16)
    for i in range(IDX_HEADS):
        iqt_ref[i] = pt[_IQT_OFF + 64 * i:_IQT_OFF + 64 * (i + 1)].astype(BF16)
    ones_row = (lax.broadcasted_iota(jnp.int32, (V7X_BF16_ROWS, t), 0) == 0).astype(F32)
    for g in range(ATTN_KV_HEADS):
        v_t = pt[_VT_OFF + 64 * g:_VT_OFF + 64 * (g + 1)]
        vt_ref[g] = jnp.concatenate([v_t, ones_row], axis=0).astype(BF16)
    iwt_ref[...] = pt[_IWT_OFF:_IWT_OFF + IDX_HEADS] * (IDX_HEADS ** -0.5)

    proj = jnp.dot(h, wtok_ref[...], preferred_element_type=F32)
    for g in range(ATTN_KV_HEADS):
        kk_ref[g] = proj[:, _K_OFF + 64 * g:_K_OFF + 64 * (g + 1)].astype(BF16)
    misc = proj[:, _MISC_OFF:_MISC_OFF + V7X_LANES]
    ik_ref[...] = misc[:, :IDX_DIM].astype(BF16)
    gqk_ref[...] = proj[:, _GQ_OFF:_GV_OFF]
    gv_ref[...] = proj[:, _GV_OFF:_GG_OFF].astype(BF16)
    gg_ref[...] = proj[:, _GG_OFF:_GG_OFF + GLA_HEADS * GLA_DV]
    z = jnp.dot(misc, wa2_ref[...], preferred_element_type=F32, precision=lax.Precision.HIGHEST)
    z = z + ba_ref[...]
    log_sig = jnp.minimum(z, 0.0) - jnp.log1p(jnp.exp(-jnp.abs(z)))
    la_ref[...] = log_sig * (1.0 / GLA_TAU)


def _pack_w_in(w_in):
    d = w_in.shape[0]
    sizes = (512, 128, 128, 512, 64, 8, 256, 256, 512, 16, 512)
    offs = [0]
    for s in sizes:
        offs.append(offs[-1] + s)
    aq, ak, av, iq, ik, iw, gq, gk, gv, ga, gg = [w_in[:, offs[i]:offs[i + 1]] for i in range(len(sizes))]
    z = lambda n: jnp.zeros((d, n), w_in.dtype)
    w_tok = jnp.concatenate([ak, ik, z(8), ga, z(40), gq * (GLA_DK ** -0.5), gk, gv, gg], axis=1)
    w_t = jnp.concatenate([aq * (ATTN_HEAD_DIM ** -0.5), iq * (IDX_DIM ** -0.5), av, iw, z(8)], axis=1).T
    assert w_tok.shape[1] == _TOK_COLS and w_t.shape[0] == _T_ROWS
    return w_tok.astype(BF16), w_t.astype(BF16)


def _inproj(x1, g_mix_pre, w_in, w_gla_a2, b_gla_a):
    n, d = x1.shape
    t = FFN_ROWS
    w_tok, w_t = _pack_w_in(w_in)
    nqk = GLA_HEADS * GLA_DK
    nv = GLA_HEADS * GLA_DV
    wa2 = jnp.zeros((V7X_LANES, nqk), F32).at[_GA_LANE:_GA_LANE + GLA_GATE_RANK].set(w_gla_a2)
    row = lambda i: (i, 0)
    row3 = lambda i: (0, i, 0)
    col3 = lambda i: (0, 0, i)
    out_shape = (
        jax.ShapeDtypeStruct((ATTN_HEADS, ATTN_HEAD_DIM, n), BF16),
        jax.ShapeDtypeStruct((IDX_HEADS, IDX_DIM, n), BF16),
        jax.ShapeDtypeStruct((ATTN_KV_HEADS, VT_ROWS, n), BF16),
        jax.ShapeDtypeStruct((IDX_HEADS, n), F32),
        jax.ShapeDtypeStruct((ATTN_KV_HEADS, n, ATTN_HEAD_DIM), BF16),
        jax.ShapeDtypeStruct((n, IDX_DIM), BF16),
        jax.ShapeDtypeStruct((n, 2 * nqk), F32),
        jax.ShapeDtypeStruct((n, nv), BF16),
        jax.ShapeDtypeStruct((n, nv), F32),
        jax.ShapeDtypeStruct((n, nqk), F32),
    )
    out_specs = (
        pl.BlockSpec((ATTN_HEADS, ATTN_HEAD_DIM, t), col3),
        pl.BlockSpec((IDX_HEADS, IDX_DIM, t), col3),
        pl.BlockSpec((ATTN_KV_HEADS, VT_ROWS, t), col3),
        pl.BlockSpec((IDX_HEADS, t), lambda i: (0, i)),
        pl.BlockSpec((ATTN_KV_HEADS, t, ATTN_HEAD_DIM), row3),
        pl.BlockSpec((t, IDX_DIM), row),
        pl.BlockSpec((t, 2 * nqk), row),
        pl.BlockSpec((t, nv), row),
        pl.BlockSpec((t, nv), row),
        pl.BlockSpec((t, nqk), row),
    )
    est = (2 * t * d * 4 + d * (_TOK_COLS + _T_ROWS) * 2 + t * (_TOK_COLS + _T_ROWS) * 4 * 2
           + 2 * t * 8 * 1024)
    return pl.pallas_call(
        _inproj_kernel,
        out_shape=out_shape,
        grid=(n // t,),
        in_specs=[
            pl.BlockSpec((t, d), row),
            _const_spec((1, d)),
            _const_spec(w_tok.shape),
            _const_spec(w_t.shape),
            _const_spec(wa2.shape),
            _const_spec((1, nqk)),
        ],
        out_specs=out_specs,
        compiler_params=pltpu.CompilerParams(
            dimension_semantics=("arbitrary",), vmem_limit_bytes=_vmem_limit(est)),
    )(x1, g_mix_pre.reshape(1, d), w_tok, w_t, wa2, b_gla_a.reshape(1, nqk))


def _dsa_kernel(qt_ref, iqt_ref, iwt_ref, kk_ref, vt_ref, ik_ref, o_ref,
                key_s, half_s, bias_s, acc_s, ot_s, *, topk, idx_bits):
    tq = DSA_TQ
    kc = DSA_KC
    sub = V7X_SUBLANES
    i = pl.program_id(1)
    nchunks = i + 1

    def score_logits(j):
        ikc = ik_ref[pl.ds(pl.multiple_of(j * kc, kc), kc), :]
        iqcat = jnp.concatenate([iqt_ref[h] for h in range(IDX_HEADS)], axis=1)
        return jnp.dot(ikc, iqcat, preferred_element_type=F32)

    def score_keys(j, lg, diagonal):
        sc = jnp.zeros((kc, tq), F32)
        for h in range(IDX_HEADS):
            sc = sc + jnp.maximum(lg[:, h * tq:(h + 1) * tq], 0.0) * iwt_ref[h:h + 1, :]
        bits = pltpu.bitcast(sc, jnp.int32)
        key = jnp.where(bits < 0, bits ^ 0x7FFFFFFF, bits)
        if diagonal:
            kpos = lax.broadcasted_iota(jnp.int32, (kc, tq), 0)
            qpos = lax.broadcasted_iota(jnp.int32, (kc, tq), 1)
            key = jnp.where(kpos <= qpos, key, INT_MIN)
        key_s[j] = key
        half_s[j] = lax.shift_right_arithmetic(key, 16).astype(jnp.int16)

    def score_pair(jj, c):
        lgs = [score_logits(2 * jj + s) for s in range(2)]
        for s in range(2):
            score_keys(2 * jj + s, lgs[s], False)
        return c

    lax.fori_loop(0, lax.shift_right_logical(i, 1), score_pair, 0)

    @pl.when((i & 1) == 1)
    def _():
        score_keys(i - 1, score_logits(i - 1), False)

    score_keys(i, score_logits(i), True)

    qidx = i * tq + lax.broadcasted_iota(jnp.int32, (1, tq), 1)
    krow = jnp.minimum(topk, qidx + 1).astype(F32)

    def count(ind_fn):
        def body(j, acc):
            parts = [acc, None, None, None]
            for s in range(kc // sub):
                ind = ind_fn(key_s[j, s * sub:(s + 1) * sub, :], j * kc + s * sub)
                p = s % 4
                parts[p] = ind if parts[p] is None else parts[p] + ind
            return (parts[0] + parts[1]) + (parts[2] + parts[3])
        acc = lax.fori_loop(0, nchunks, body, jnp.zeros((sub, tq), F32))
        return jnp.sum(acc, axis=0, keepdims=True)

    def count_ge(cand):
        cb = jnp.broadcast_to(cand, (sub, tq))
        return count(lambda k, base: jnp.where(k >= cb, 1.0, 0.0))

    def search16():
        rows16 = V7X_BF16_ROWS

        def count16(cand):
            cb = jnp.broadcast_to(cand, (rows16, tq)).astype(jnp.int16)
            one, nil = jnp.int16(1), jnp.int16(0)

            def body(j, acc):
                parts = [acc, None, None, None]
                for s in range(kc // rows16):
                    ind = jnp.where(half_s[j, s * rows16:(s + 1) * rows16, :] >= cb, one, nil)
                    p = s % 4
                    parts[p] = ind if parts[p] is None else parts[p] + ind
                return (parts[0] + parts[1]) + (parts[2] + parts[3])

            acc = lax.fori_loop(0, nchunks, body, jnp.zeros((rows16, tq), jnp.int16))
            return jnp.sum(acc.astype(F32), axis=0, keepdims=True)

        zero = jnp.zeros((1, tq), jnp.int32)
        t0 = jnp.where(count16(zero) >= krow, zero, INT16_MIN)

        def bit_body(it, th):
            cand = th | lax.shift_left(jnp.int32(1), 14 - it)
            return jnp.where(count16(cand) >= krow, cand, th)

        return lax.fori_loop(0, 15, bit_body, t0)

    t_hi = search16()
    thb = jnp.broadcast_to(t_hi, (kc, tq))

    def low_halves(j, c):
        k = key_s[j]
        hi = lax.shift_right_arithmetic(k, 16)
        lo = (k & 0xFFFF) + INT16_MIN
        e = jnp.where(hi == thb, lo, jnp.where(hi > thb, INT16_MAX, INT16_MIN))
        half_s[j] = e.astype(jnp.int16)
        return c

    lax.fori_loop(0, nchunks, low_halves, 0)
    t_lo = search16()
    t = lax.shift_left(t_hi, 16) | (t_lo - INT16_MIN)

    c_ge = count_ge(t)
    need = krow - count_ge(t + 1)
    tb8 = jnp.broadcast_to(t, (sub, tq))

    @pl.when(jnp.max(c_ge - krow) > 0.5)
    def _():
        sub_iota = lax.broadcasted_iota(jnp.int32, (sub, tq), 0)

        def tied_below(u):
            ub = jnp.broadcast_to(u, (sub, tq))
            return count(lambda k, base: jnp.where(k == tb8, jnp.where(sub_iota + base < ub, 1.0, 0.0), 0.0))

        def ubit(it, u):
            cand = u | lax.shift_left(jnp.int32(1), idx_bits - 1 - it)
            return jnp.where(tied_below(cand) < need, cand, u)

        u = lax.fori_loop(0, idx_bits, ubit, jnp.zeros((1, tq), jnp.int32))
        ub_full = jnp.broadcast_to(u, (kc, tq))
        tb_full = jnp.broadcast_to(t, (kc, tq))
        kidx = lax.broadcasted_iota(jnp.int32, (kc, tq), 0)

        def drop(j, c):
            k = key_s[j]
            key_s[j] = jnp.where(k == tb_full, jnp.where(kidx + j * kc > ub_full, INT_MIN, k), k)
            return c

        lax.fori_loop(0, nchunks, drop, 0)

    def masked_scores(j, g, slot=0):
        koff = pl.multiple_of(j * kc, kc)
        qcat = jnp.concatenate([qt_ref[g * ATTN_REP + r] for r in range(ATTN_REP)], axis=1)
        bias = jnp.concatenate([bias_s[slot]] * ATTN_REP, axis=1)
        kg = kk_ref[g, pl.ds(koff, kc), :]
        vg = vt_ref[g, :, pl.ds(koff, kc)]
        return jnp.dot(kg, qcat, preferred_element_type=F32) + bias, vg

    def set_bias(j, slot=0):
        bias_s[slot] = jnp.where(key_s[j] >= jnp.broadcast_to(t, (kc, tq)), 0.0, NEG_BIG)

    acc_s[...] = jnp.zeros_like(acc_s)

    def att_chunk(j, slot):
        set_bias(j, slot)
        for g in range(ATTN_KV_HEADS):
            s, vg = masked_scores(j, g, slot)
            acc_s[g] += jnp.dot(vg, jnp.exp(s).astype(BF16), preferred_element_type=F32)

    def att_pair(jj, c):
        work = []
        for slot in range(2):
            j = 2 * jj + slot
            set_bias(j, slot)
            for g in range(ATTN_KV_HEADS):
                work.append((g, masked_scores(j, g, slot)))
        for g, (s, vg) in work:
            acc_s[g] += jnp.dot(vg, jnp.exp(s).astype(BF16), preferred_element_type=F32)
        return c

    lax.fori_loop(0, lax.shift_right_logical(nchunks, 1), att_pair, 0)

    @pl.when((nchunks & 1) == 1)
    def _():
        att_chunk(nchunks - 1, 0)

    norm = acc_s[:, ATTN_HEAD_DIM:ATTN_HEAD_DIM + 1, :]
    in_range = jnp.where(norm >= SOFTMAX_NORM_MIN, jnp.where(norm <= SOFTMAX_NORM_MAX, 1.0, 0.0), 0.0)

    @pl.when(jnp.min(in_range) < 0.5)
    def _():
        acc_s[...] = jnp.zeros_like(acc_s)

        def online_body(j, ms):
            set_bias(j)
            out = []
            for g in range(ATTN_KV_HEADS):
                s, vg = masked_scores(j, g)
                m_new = jnp.maximum(ms[g], jnp.max(s, axis=0, keepdims=True))
                p = jnp.exp(s - m_new).astype(BF16)
                pv = jnp.dot(vg, p, preferred_element_type=F32)
                acc_s[g] = jnp.exp(ms[g] - m_new) * acc_s[g] + pv
                out.append(m_new)
            return tuple(out)

        m_init = tuple(jnp.full((1, ATTN_REP * tq), NEG_BIG, F32) for _ in range(ATTN_KV_HEADS))
        lax.fori_loop(0, nchunks, online_body, m_init)

    for h in range(ATTN_HEADS):
        g, r = divmod(h, ATTN_REP)
        a = acc_s[g, :, r * tq:(r + 1) * tq]
        ot_s[h * ATTN_HEAD_DIM:(h + 1) * ATTN_HEAD_DIM, :] = (
            a[:ATTN_HEAD_DIM] / a[ATTN_HEAD_DIM:ATTN_HEAD_DIM + 1])
    o_ref[...] = ot_s[...].T.astype(o_ref.dtype)


def _dsa(qt, iqt, iwt, kk, vt, ik, batch, seq):
    n = batch * seq
    tq, kc = DSA_TQ, DSA_KC
    assert seq % tq == 0
    nq = seq // tq
    topk = min(TOPK_MAX, seq // 4)
    nch = seq // kc
    nout = ATTN_HEADS * ATTN_HEAD_DIM
    qmap = lambda b, i: (0, 0, b * nq + i)
    est = (nch * kc * tq * 6 + kc * tq * 4 + ATTN_HEADS * (VT_ROWS + 8) * tq * 4 + nout * tq * 4
           + ATTN_KV_HEADS * seq * (V7X_LANES + VT_ROWS) * 2 + seq * V7X_LANES * 2
           + 2 * (2 * ATTN_HEADS * ATTN_HEAD_DIM * tq * 2 + 8 * tq * 4 + tq * nout * 2))
    return pl.pallas_call(
        functools.partial(_dsa_kernel, topk=topk, idx_bits=(seq - 1).bit_length()),
        out_shape=jax.ShapeDtypeStruct((n, nout), BF16),
        grid=(batch, nq),
        in_specs=[
            pl.BlockSpec((ATTN_HEADS, ATTN_HEAD_DIM, tq), qmap),
            pl.BlockSpec((IDX_HEADS, IDX_DIM, tq), qmap),
            pl.BlockSpec((IDX_HEADS, tq), lambda b, i: (0, b * nq + i)),
            pl.BlockSpec((ATTN_KV_HEADS, seq, ATTN_HEAD_DIM), lambda b, i: (0, b, 0),
                         pipeline_mode=pl.Buffered(1)),
            pl.BlockSpec((ATTN_KV_HEADS, VT_ROWS, seq), lambda b, i: (0, 0, b),
                         pipeline_mode=pl.Buffered(1)),
            pl.BlockSpec((seq, IDX_DIM), lambda b, i: (b, 0), pipeline_mode=pl.Buffered(1)),
        ],
        out_specs=pl.BlockSpec((tq, nout), lambda b, i: (b * nq + i, 0)),
        scratch_shapes=[
            pltpu.VMEM((nch, kc, tq), jnp.int32),
            pltpu.VMEM((nch, kc, tq), jnp.int16),
            pltpu.VMEM((2, kc, tq), F32),
            pltpu.VMEM((ATTN_KV_HEADS, VT_ROWS, ATTN_REP * tq), F32),
            pltpu.VMEM((nout, tq), F32),
        ],
        compiler_params=pltpu.CompilerParams(
            dimension_semantics=("arbitrary", "arbitrary"), vmem_limit_bytes=_vmem_limit(est)),
    )(qt, iqt, iwt, kk, vt, ik)


def _gla_kernel(gqk_ref, gv_ref, la_ref, gg_ref, gn_ref, o_ref, st_s):
    @pl.when(pl.program_id(1) == 0)
    def _():
        st_s[...] = jnp.zeros_like(st_s)

    c = GLA_CHUNK
    nqk = GLA_HEADS * GLA_DK
    r_i = lax.broadcasted_iota(jnp.int32, (c, c), 0)
    c_i = lax.broadcasted_iota(jnp.int32, (c, c), 1)
    tri = r_i >= c_i
    tri_f = tri.astype(F32)
    gn = gn_ref[...]
    for ci in range(GLA_ROWS // c):
        rows = slice(ci * c, (ci + 1) * c)
        la = la_ref[rows, :]
        b = jnp.dot(tri_f, la, preferred_element_type=F32, precision=lax.Precision.HIGHEST)
        b_last = b[c - 1:c, :]
        q = gqk_ref[rows, :nqk]
        k = gqk_ref[rows, nqk:]
        q_dec = (q * jnp.exp(b)).astype(BF16)
        k_in = (k * jnp.exp(-b)).astype(BF16)
        k_out = (k * jnp.exp(b_last - b)).astype(BF16)
        decay = jnp.exp(b_last)
        for h in range(GLA_HEADS):
            hs = slice(h * GLA_DK, (h + 1) * GLA_DK)
            vs = slice(h * GLA_DV, (h + 1) * GLA_DV)
            qd, ki, ko = q_dec[:, hs], k_in[:, hs], k_out[:, hs]
            v = gv_ref[rows, vs]
            a = lax.dot_general(qd, ki, NT_DIMS, preferred_element_type=F32)
            a = jnp.where(tri, a, 0.0).astype(BF16)
            st = st_s[h]
            o = jnp.dot(a, v, preferred_element_type=F32)
            o = o + lax.dot_general(qd, st.astype(BF16), NT_DIMS, preferred_element_type=F32)
            upd = lax.dot_general(v, ko, TN_DIMS, preferred_element_type=F32)
            st_s[h] = st * decay[:, hs] + upd
            gate = gg_ref[rows, vs]
            o_ref[rows, vs] = (_rms(o, gn) * (gate * jax.nn.sigmoid(gate))).astype(o_ref.dtype)


def _gla(gqk, gv, la, gg, g_norm, batch, seq):
    n = batch * seq
    t = GLA_ROWS
    assert seq % t == 0
    ns = seq // t
    row = lambda b, i: (b * ns + i, 0)
    nqk = GLA_HEADS * GLA_DK
    nv = GLA_HEADS * GLA_DV
    return pl.pallas_call(
        _gla_kernel,
        out_shape=jax.ShapeDtypeStruct((n, nv), BF16),
        grid=(batch, ns),
        in_specs=[
            pl.BlockSpec((t, 2 * nqk), row),
            pl.BlockSpec((t, nv), row),
            pl.BlockSpec((t, nqk), row),
            pl.BlockSpec((t, nv), row),
            _const_spec((1, GLA_DV)),
        ],
        out_specs=pl.BlockSpec((t, nv), row),
        scratch_shapes=[pltpu.VMEM((GLA_HEADS, GLA_DV, GLA_DK), F32)],
        compiler_params=pltpu.CompilerParams(dimension_semantics=("arbitrary", "arbitrary")),
    )(gqk, gv, la, gg, g_norm.reshape(1, GLA_DV))


def _outproj_kernel(x_ref, oa_ref, og_ref, w_ref, g_ref, o_ref):
    m = jnp.dot(oa_ref[...], w_ref[0], preferred_element_type=F32)
    m = m + jnp.dot(og_ref[...], w_ref[1], preferred_element_type=F32)
    o_ref[...] = x_ref[...] + _rms(m, g_ref[...])


def _outproj(x1, oa, og, w_out, g_post):
    n, d = x1.shape
    t = FFN_ROWS
    half = oa.shape[1]
    assert og.shape[1] == half and w_out.shape[0] == 2 * half
    w = w_out.reshape(2, half, d).astype(BF16)
    row = lambda i: (i, 0)
    est = 4 * t * d * 4 + 4 * t * half * 2 + 2 * half * d * 2 + t * d * 4
    return pl.pallas_call(
        _outproj_kernel,
        out_shape=jax.ShapeDtypeStruct((n, d), F32),
        grid=(n // t,),
        in_specs=[
            pl.BlockSpec((t, d), row),
            pl.BlockSpec((t, half), row),
            pl.BlockSpec((t, half), row),
            _const_spec(w.shape),
            _const_spec((1, d)),
        ],
        out_specs=pl.BlockSpec((t, d), row),
        compiler_params=pltpu.CompilerParams(
            dimension_semantics=("arbitrary",), vmem_limit_bytes=_vmem_limit(est)),
    )(x1, oa, og, w, g_post.reshape(1, d))


def kernel(x, g_ffn1_pre, w_ffn1_gate, w_ffn1_up, w_ffn1_down, g_ffn1_post, g_mix_pre, w_in, w_gla_a2,
           b_gla_a, g_gla_norm, w_out, g_mix_post, g_ffn2_pre, w_ffn2_gate, w_ffn2_up, w_ffn2_down,
           g_ffn2_post):
    batch, seq, d = x.shape
    h = x.reshape(batch * seq, d)
    for l in range(g_ffn1_pre.shape[0]):
        h = _ffn(h, g_ffn1_pre[l], w_ffn1_gate[l], w_ffn1_up[l], w_ffn1_down[l], g_ffn1_post[l])
        qt, iqt, vt, iwt, kk, ik, gqk, gv, gg, la = _inproj(h, g_mix_pre[l], w_in[l], w_gla_a2[l], b_gla_a[l])
        oa = _dsa(qt, iqt, iwt, kk, vt, ik, batch, seq)
        og = _gla(gqk, gv, la, gg, g_gla_norm[l], batch, seq)
        h = _outproj(h, oa, og, w_out[l], g_mix_post[l])
        h = _ffn(h, g_ffn2_pre[l], w_ffn2_gate[l], w_ffn2_up[l], w_ffn2_down[l], g_ffn2_post[l])
    return h.reshape(batch, seq, d)
```

```python
import functools

import jax
import jax.numpy as jnp
from jax import lax
from jax.experimental import pallas as pl
from jax.experimental.pallas import tpu as pltpu

ATTN_HEADS = 8
ATTN_KV_HEADS = 2
ATTN_HEAD_DIM = 64
ATTN_REP = ATTN_HEADS // ATTN_KV_HEADS
IDX_HEADS = 8
IDX_DIM = 64
TOPK_MAX = 256
GLA_HEADS = 4
GLA_DK = 64
GLA_DV = 128
GLA_GATE_RANK = 16
GLA_TAU = 16.0
GLA_CHUNK = 64
EPS = 1e-6

V7X_LANES = 128
V7X_SUBLANES = 8
V7X_BF16_ROWS = 16
V7X_MXU_DIM = 256
V7X_VMEM_BYTES = 64 * 2**20

FFN_ROWS = 512
FF_CHUNK = V7X_MXU_DIM
DSA_TQ = 256
DSA_KC = DSA_TQ
GLA_ROWS = 256
VT_ROWS = ATTN_HEAD_DIM + V7X_BF16_ROWS

INT_MIN = -2**31
INT16_MIN, INT16_MAX = -2**15, 2**15 - 1
NEG_BIG = -1e30
SOFTMAX_NORM_MIN = 2.0 ** -60
SOFTMAX_NORM_MAX = 2.0 ** 100

F32 = jnp.float32
BF16 = jnp.bfloat16
NT_DIMS = (((1,), (1,)), ((), ()))
TN_DIMS = (((0,), (0,)), ((), ()))


def _vmem_limit(nbytes):
    return int(min(nbytes * 1.25 + (8 << 20), V7X_VMEM_BYTES - (6 << 20)))


def _rms(x, g):
    return x * lax.rsqrt(jnp.mean(x * x, axis=-1, keepdims=True) + EPS) * g


def _const_spec(shape):
    nd = len(shape)
    return pl.BlockSpec(shape, lambda *_: (0,) * nd, pipeline_mode=pl.Buffered(1))


def _ffn_kernel(x_ref, gpre_ref, wg_ref, wu_ref, wd_ref, gpost_ref, o_ref):
    x = x_ref[...]
    xn = _rms(x, gpre_ref[...]).astype(BF16)

    def gate_up(c):
        return (jnp.dot(xn, wg_ref[c], preferred_element_type=F32),
                jnp.dot(xn, wu_ref[c], preferred_element_type=F32))

    nch = wg_ref.shape[0]
    acc = None
    nxt = gate_up(0)
    for c in range(nch):
        g, u = nxt
        if c + 1 < nch:
            nxt = gate_up(c + 1)
        a = (g * jax.nn.sigmoid(g) * u).astype(BF16)
        d = jnp.dot(a, wd_ref[c], preferred_element_type=F32)
        acc = d if acc is None else acc + d
    o_ref[...] = x + 0.5 * _rms(acc, gpost_ref[...])


def _ffn(x, g_pre, w_gate, w_up, w_down, g_post):
    n, d = x.shape
    dff = w_gate.shape[1]
    nch = dff // FF_CHUNK
    assert nch * FF_CHUNK == dff and n % FFN_ROWS == 0
    wg = w_gate.reshape(d, nch, FF_CHUNK).transpose(1, 0, 2).astype(BF16)
    wu = w_up.reshape(d, nch, FF_CHUNK).transpose(1, 0, 2).astype(BF16)
    wd = w_down.reshape(nch, FF_CHUNK, d).astype(BF16)
    row = lambda i: (i, 0)
    est = 4 * FFN_ROWS * d * 4 + 3 * d * dff * 2 + FFN_ROWS * d * 8 + 4 * FFN_ROWS * FF_CHUNK * 4
    return pl.pallas_call(
        _ffn_kernel,
        out_shape=jax.ShapeDtypeStruct((n, d), F32),
        grid=(n // FFN_ROWS,),
        in_specs=[
            pl.BlockSpec((FFN_ROWS, d), row),
            _const_spec((1, d)),
            _const_spec(wg.shape), _const_spec(wu.shape), _const_spec(wd.shape),
            _const_spec((1, d)),
        ],
        out_specs=pl.BlockSpec((FFN_ROWS, d), row),
        compiler_params=pltpu.CompilerParams(
            dimension_semantics=("arbitrary",), vmem_limit_bytes=_vmem_limit(est)),
    )(x, g_pre.reshape(1, d), wg, wu, wd, g_post.reshape(1, d))


_K_OFF, _MISC_OFF, _GQ_OFF, _GK_OFF, _GV_OFF, _GG_OFF, _TOK_COLS = 0, 128, 256, 512, 768, 1280, 1792
_GA_LANE = 72
_QT_OFF, _IQT_OFF, _VT_OFF, _IWT_OFF, _T_ROWS = 0, 512, 1024, 1152, 1168


def _inproj_kernel(x_ref, g_ref, wtok_ref, wt_ref, wa2_ref, ba_ref,
                   qt_ref, iqt_ref, vt_ref, iwt_ref, kk_ref, ik_ref, gqk_ref, gv_ref, gg_ref, la_ref):
    h = _rms(x_ref[...], g_ref[...]).astype(BF16)
    t = h.shape[0]
    pt = lax.dot_general(wt_ref[...], h, NT_DIMS, preferred_element_type=F32)
    for i in range(ATTN_HEADS):
        qt_ref[i] = pt[_QT_OFF + 64 * i:_QT_OFF + 64 * (i + 1)].astype(BF16)
    for i in range(IDX_HEADS):
        iqt_ref[i] = pt[_IQT_OFF + 64 * i:_IQT_OFF + 64 * (i + 1)].astype(BF16)
    ones_row = (lax.broadcasted_iota(jnp.int32, (V7X_BF16_ROWS, t), 0) == 0).astype(F32)
    for g in range(ATTN_KV_HEADS):
        v_t = pt[_VT_OFF + 64 * g:_VT_OFF + 64 * (g + 1)]
        vt_ref[g] = jnp.concatenate([v_t, ones_row], axis=0).astype(BF16)
    iwt_ref[...] = pt[_IWT_OFF:_IWT_OFF + IDX_HEADS] * (IDX_HEADS ** -0.5)

    proj = jnp.dot(h, wtok_ref[...], preferred_element_type=F32)
    for g in range(ATTN_KV_HEADS):
        kk_ref[g] = proj[:, _K_OFF + 64 * g:_K_OFF + 64 * (g + 1)].astype(BF16)
    misc = proj[:, _MISC_OFF:_MISC_OFF + V7X_LANES]
    ik_ref[...] = misc[:, :IDX_DIM].astype(BF16)
    gqk_ref[...] = proj[:, _GQ_OFF:_GV_OFF]
    gv_ref[...] = proj[:, _GV_OFF:_GG_OFF].astype(BF16)
    gg_ref[...] = proj[:, _GG_OFF:_GG_OFF + GLA_HEADS * GLA_DV]
    z = jnp.dot(misc, wa2_ref[...], preferred_element_type=F32, precision=lax.Precision.HIGHEST)
    z = z + ba_ref[...]
    log_sig = jnp.minimum(z, 0.0) - jnp.log1p(jnp.exp(-jnp.abs(z)))
    la_ref[...] = log_sig * (1.0 / GLA_TAU)


def _pack_w_in(w_in):
    d = w_in.shape[0]
    sizes = (512, 128, 128, 512, 64, 8, 256, 256, 512, 16, 512)
    offs = [0]
    for s in sizes:
        offs.append(offs[-1] + s)
    aq, ak, av, iq, ik, iw, gq, gk, gv, ga, gg = [w_in[:, offs[i]:offs[i + 1]] for i in range(len(sizes))]
    z = lambda n: jnp.zeros((d, n), w_in.dtype)
    w_tok = jnp.concatenate([ak, ik, z(8), ga, z(40), gq * (GLA_DK ** -0.5), gk, gv, gg], axis=1)
    w_t = jnp.concatenate([aq * (ATTN_HEAD_DIM ** -0.5), iq * (IDX_DIM ** -0.5), av, iw, z(8)], axis=1).T
    assert w_tok.shape[1] == _TOK_COLS and w_t.shape[0] == _T_ROWS
    return w_tok.astype(BF16), w_t.astype(BF16)


def _inproj(x1, g_mix_pre, w_in, w_gla_a2, b_gla_a):
    n, d = x1.shape
    t = FFN_ROWS
    w_tok, w_t = _pack_w_in(w_in)
    nqk = GLA_HEADS * GLA_DK
    nv = GLA_HEADS * GLA_DV
    wa2 = jnp.zeros((V7X_LANES, nqk), F32).at[_GA_LANE:_GA_LANE + GLA_GATE_RANK].set(w_gla_a2)
    row = lambda i: (i, 0)
    row3 = lambda i: (0, i, 0)
    col3 = lambda i: (0, 0, i)
    out_shape = (
        jax.ShapeDtypeStruct((ATTN_HEADS, ATTN_HEAD_DIM, n), BF16),
        jax.ShapeDtypeStruct((IDX_HEADS, IDX_DIM, n), BF16),
        jax.ShapeDtypeStruct((ATTN_KV_HEADS, VT_ROWS, n), BF16),
        jax.ShapeDtypeStruct((IDX_HEADS, n), F32),
        jax.ShapeDtypeStruct((ATTN_KV_HEADS, n, ATTN_HEAD_DIM), BF16),
        jax.ShapeDtypeStruct((n, IDX_DIM), BF16),
        jax.ShapeDtypeStruct((n, 2 * nqk), F32),
        jax.ShapeDtypeStruct((n, nv), BF16),
        jax.ShapeDtypeStruct((n, nv), F32),
        jax.ShapeDtypeStruct((n, nqk), F32),
    )
    out_specs = (
        pl.BlockSpec((ATTN_HEADS, ATTN_HEAD_DIM, t), col3),
        pl.BlockSpec((IDX_HEADS, IDX_DIM, t), col3),
        pl.BlockSpec((ATTN_KV_HEADS, VT_ROWS, t), col3),
        pl.BlockSpec((IDX_HEADS, t), lambda i: (0, i)),
        pl.BlockSpec((ATTN_KV_HEADS, t, ATTN_HEAD_DIM), row3),
        pl.BlockSpec((t, IDX_DIM), row),
        pl.BlockSpec((t, 2 * nqk), row),
        pl.BlockSpec((t, nv), row),
        pl.BlockSpec((t, nv), row),
        pl.BlockSpec((t, nqk), row),
    )
    est = (2 * t * d * 4 + d * (_TOK_COLS + _T_ROWS) * 2 + t * (_TOK_COLS + _T_ROWS) * 4 * 2
           + 2 * t * 8 * 1024)
    return pl.pallas_call(
        _inproj_kernel,
        out_shape=out_shape,
        grid=(n // t,),
        in_specs=[
            pl.BlockSpec((t, d), row),
            _const_spec((1, d)),
            _const_spec(w_tok.shape),
            _const_spec(w_t.shape),
            _const_spec(wa2.shape),
            _const_spec((1, nqk)),
        ],
        out_specs=out_specs,
        compiler_params=pltpu.CompilerParams(
            dimension_semantics=("arbitrary",), vmem_limit_bytes=_vmem_limit(est)),
    )(x1, g_mix_pre.reshape(1, d), w_tok, w_t, wa2, b_gla_a.reshape(1, nqk))


def _dsa_kernel(qt_ref, iqt_ref, iwt_ref, kk_ref, vt_ref, ik_ref, o_ref,
                key_s, half_s, low_s, bias_s, acc_s, ot_s, *, topk, idx_bits):
    tq = DSA_TQ
    kc = DSA_KC
    sub = V7X_SUBLANES
    i = pl.program_id(1)
    nchunks = i + 1

    def score_logits(j):
        ikc = ik_ref[pl.ds(pl.multiple_of(j * kc, kc), kc), :]
        iqcat = jnp.concatenate([iqt_ref[h] for h in range(IDX_HEADS)], axis=1)
        return jnp.dot(ikc, iqcat, preferred_element_type=F32)

    def score_keys(j, lg, diagonal):
        sc = jnp.zeros((kc, tq), F32)
        for h in range(IDX_HEADS):
            sc = sc + jnp.maximum(lg[:, h * tq:(h + 1) * tq], 0.0) * iwt_ref[h:h + 1, :]
        bits = pltpu.bitcast(sc, jnp.int32)
        key = jnp.where(bits < 0, bits ^ 0x7FFFFFFF, bits)
        if diagonal:
            kpos = lax.broadcasted_iota(jnp.int32, (kc, tq), 0)
            qpos = lax.broadcasted_iota(jnp.int32, (kc, tq), 1)
            key = jnp.where(kpos <= qpos, key, INT_MIN)
        key_s[j] = key
        half_s[j] = lax.shift_right_arithmetic(key, 16).astype(jnp.int16)
        low_s[j] = ((key & 0xFFFF) + INT16_MIN).astype(jnp.int16)

    def score_pair(jj, c):
        lgs = [score_logits(2 * jj + s) for s in range(2)]
        for s in range(2):
            score_keys(2 * jj + s, lgs[s], False)
        return c

    lax.fori_loop(0, lax.shift_right_logical(i, 1), score_pair, 0)

    @pl.when((i & 1) == 1)
    def _():
        score_keys(i - 1, score_logits(i - 1), False)

    score_keys(i, score_logits(i), True)

    qidx = i * tq + lax.broadcasted_iota(jnp.int32, (1, tq), 1)
    krow = jnp.minimum(topk, qidx + 1).astype(F32)

    def count(ind_fn):
        def body(j, acc):
            parts = [acc, None, None, None]
            for s in range(kc // sub):
                ind = ind_fn(key_s[j, s * sub:(s + 1) * sub, :], j * kc + s * sub)
                p = s % 4
                parts[p] = ind if parts[p] is None else parts[p] + ind
            return (parts[0] + parts[1]) + (parts[2] + parts[3])
        acc = lax.fori_loop(0, nchunks, body, jnp.zeros((sub, tq), F32))
        return jnp.sum(acc, axis=0, keepdims=True)

    def count_ge(cand):
        cb = jnp.broadcast_to(cand, (sub, tq))
        return count(lambda k, base: jnp.where(k >= cb, 1.0, 0.0))

    rows16 = V7X_BF16_ROWS

    def count16(cand):
        cb = jnp.broadcast_to(cand, (rows16, tq)).astype(jnp.int16)
        one, nil = jnp.int16(1), jnp.int16(0)

        def body(j, acc):
            parts = [acc, None, None, None]
            for s in range(kc // rows16):
                ind = jnp.where(half_s[j, s * rows16:(s + 1) * rows16, :] >= cb, one, nil)
                p = s % 4
                parts[p] = ind if parts[p] is None else parts[p] + ind
            return (parts[0] + parts[1]) + (parts[2] + parts[3])

        acc = lax.fori_loop(0, nchunks, body, jnp.zeros((rows16, tq), jnp.int16))
        return jnp.sum(acc.astype(F32), axis=0, keepdims=True)

    def search16(count_at_min):
        zero = jnp.zeros((1, tq), jnp.int32)
        c0 = count16(zero)
        start = (jnp.where(c0 >= krow, zero, INT16_MIN), jnp.where(c0 >= krow, c0, count_at_min))

        def bit_body(it, carry):
            th, cth = carry
            cand = th | lax.shift_left(jnp.int32(1), 14 - it)
            c = count16(cand)
            return jnp.where(c >= krow, cand, th), jnp.where(c >= krow, c, cth)

        return lax.fori_loop(0, 15, bit_body, start)

    t_hi, c_hi = search16(jnp.full((1, tq), 1.0, F32) * (nchunks * kc).astype(F32))
    th16 = jnp.broadcast_to(t_hi, (kc, tq)).astype(jnp.int16)

    def low_halves(j, c):
        hi = half_s[j]
        half_s[j] = jnp.where(hi == th16, low_s[j],
                              jnp.where(hi > th16, jnp.int16(INT16_MAX), jnp.int16(INT16_MIN)))
        return c

    lax.fori_loop(0, nchunks, low_halves, 0)
    t_lo, c_ge = search16(c_hi)
    t = lax.shift_left(t_hi, 16) | (t_lo - INT16_MIN)

    saturated = jnp.max(jnp.where(t_lo == INT16_MAX, 1.0, 0.0)) > 0.5
    c_gt = lax.cond(saturated, lambda: count_ge(t + 1), lambda: count16(jnp.minimum(t_lo + 1, INT16_MAX)))
    need = krow - c_gt
    tb8 = jnp.broadcast_to(t, (sub, tq))

    @pl.when(jnp.max(c_ge - krow) > 0.5)
    def _():
        sub_iota = lax.broadcasted_iota(jnp.int32, (sub, tq), 0)

        def tied_below(u):
            ub = jnp.broadcast_to(u, (sub, tq))
            return count(lambda k, base: jnp.where(k == tb8, jnp.where(sub_iota + base < ub, 1.0, 0.0), 0.0))

        def ubit(it, u):
            cand = u | lax.shift_left(jnp.int32(1), idx_bits - 1 - it)
            return jnp.where(tied_below(cand) < need, cand, u)

        u = lax.fori_loop(0, idx_bits, ubit, jnp.zeros((1, tq), jnp.int32))
        ub_full = jnp.broadcast_to(u, (kc, tq))
        tb_full = jnp.broadcast_to(t, (kc, tq))
        kidx = lax.broadcasted_iota(jnp.int32, (kc, tq), 0)

        def drop(j, c):
            k = key_s[j]
            key_s[j] = jnp.where(k == tb_full, jnp.where(kidx + j * kc > ub_full, INT_MIN, k), k)
            return c

        lax.fori_loop(0, nchunks, drop, 0)

    def masked_scores(j, g, slot=0):
        koff = pl.multiple_of(j * kc, kc)
        qcat = jnp.concatenate([qt_ref[g * ATTN_REP + r] for r in range(ATTN_REP)], axis=1)
        bias = jnp.concatenate([bias_s[slot]] * ATTN_REP, axis=1)
        kg = kk_ref[g, pl.ds(koff, kc), :]
        vg = vt_ref[g, :, pl.ds(koff, kc)]
        return jnp.dot(kg, qcat, preferred_element_type=F32) + bias, vg

    def set_bias(j, slot=0):
        bias_s[slot] = jnp.where(key_s[j] >= jnp.broadcast_to(t, (kc, tq)), 0.0, NEG_BIG)

    acc_s[...] = jnp.zeros_like(acc_s)

    def att_chunk(j, slot):
        set_bias(j, slot)
        for g in range(ATTN_KV_HEADS):
            s, vg = masked_scores(j, g, slot)
            acc_s[g] += jnp.dot(vg, jnp.exp(s).astype(BF16), preferred_element_type=F32)

    def att_pair(jj, c):
        work = []
        for slot in range(2):
            j = 2 * jj + slot
            set_bias(j, slot)
            for g in range(ATTN_KV_HEADS):
                work.append((g, masked_scores(j, g, slot)))
        for g, (s, vg) in work:
            acc_s[g] += jnp.dot(vg, jnp.exp(s).astype(BF16), preferred_element_type=F32)
        return c

    lax.fori_loop(0, lax.shift_right_logical(nchunks, 1), att_pair, 0)

    @pl.when((nchunks & 1) == 1)
    def _():
        att_chunk(nchunks - 1, 0)

    norm = acc_s[:, ATTN_HEAD_DIM:ATTN_HEAD_DIM + 1, :]
    in_range = jnp.where(norm >= SOFTMAX_NORM_MIN, jnp.where(norm <= SOFTMAX_NORM_MAX, 1.0, 0.0), 0.0)

    @pl.when(jnp.min(in_range) < 0.5)
    def _():
        acc_s[...] = jnp.zeros_like(acc_s)

        def online_body(j, ms):
            set_bias(j)
            out = []
            for g in range(ATTN_KV_HEADS):
                s, vg = masked_scores(j, g)
                m_new = jnp.maximum(ms[g], jnp.max(s, axis=0, keepdims=True))
                p = jnp.exp(s - m_new).astype(BF16)
                pv = jnp.dot(vg, p, preferred_element_type=F32)
                acc_s[g] = jnp.exp(ms[g] - m_new) * acc_s[g] + pv
                out.append(m_new)
            return tuple(out)

        m_init = tuple(jnp.full((1, ATTN_REP * tq), NEG_BIG, F32) for _ in range(ATTN_KV_HEADS))
        lax.fori_loop(0, nchunks, online_body, m_init)

    for h in range(ATTN_HEADS):
        g, r = divmod(h, ATTN_REP)
        a = acc_s[g, :, r * tq:(r + 1) * tq]
        ot_s[h * ATTN_HEAD_DIM:(h + 1) * ATTN_HEAD_DIM, :] = (
            a[:ATTN_HEAD_DIM] / a[ATTN_HEAD_DIM:ATTN_HEAD_DIM + 1])
    o_ref[...] = ot_s[...].T.astype(o_ref.dtype)


def _dsa(qt, iqt, iwt, kk, vt, ik, batch, seq):
    n = batch * seq
    tq, kc = DSA_TQ, DSA_KC
    assert seq % tq == 0
    nq = seq // tq
    topk = min(TOPK_MAX, seq // 4)
    nch = seq // kc
    nout = ATTN_HEADS * ATTN_HEAD_DIM
    qmap = lambda b, i: (0, 0, b * nq + i)
    est = (nch * kc * tq * 8 + 2 * kc * tq * 4 + ATTN_HEADS * (VT_ROWS + 8) * tq * 4 + nout * tq * 4
           + ATTN_KV_HEADS * seq * (V7X_LANES + VT_ROWS) * 2 + seq * V7X_LANES * 2
           + 2 * (2 * ATTN_HEADS * ATTN_HEAD_DIM * tq * 2 + 8 * tq * 4 + tq * nout * 2))
    return pl.pallas_call(
        functools.partial(_dsa_kernel, topk=topk, idx_bits=(seq - 1).bit_length()),
        out_shape=jax.ShapeDtypeStruct((n, nout), BF16),
        grid=(batch, nq),
        in_specs=[
            pl.BlockSpec((ATTN_HEADS, ATTN_HEAD_DIM, tq), qmap),
            pl.BlockSpec((IDX_HEADS, IDX_DIM, tq), qmap),
            pl.BlockSpec((IDX_HEADS, tq), lambda b, i: (0, b * nq + i)),
            pl.BlockSpec((ATTN_KV_HEADS, seq, ATTN_HEAD_DIM), lambda b, i: (0, b, 0),
                         pipeline_mode=pl.Buffered(1)),
            pl.BlockSpec((ATTN_KV_HEADS, VT_ROWS, seq), lambda b, i: (0, 0, b),
                         pipeline_mode=pl.Buffered(1)),
            pl.BlockSpec((seq, IDX_DIM), lambda b, i: (b, 0), pipeline_mode=pl.Buffered(1)),
        ],
        out_specs=pl.BlockSpec((tq, nout), lambda b, i: (b * nq + i, 0)),
        scratch_shapes=[
            pltpu.VMEM((nch, kc, tq), jnp.int32),
            pltpu.VMEM((nch, kc, tq), jnp.int16),
            pltpu.VMEM((nch, kc, tq), jnp.int16),
            pltpu.VMEM((2, kc, tq), F32),
            pltpu.VMEM((ATTN_KV_HEADS, VT_ROWS, ATTN_REP * tq), F32),
            pltpu.VMEM((nout, tq), F32),
        ],
        compiler_params=pltpu.CompilerParams(
            dimension_semantics=("arbitrary", "arbitrary"), vmem_limit_bytes=_vmem_limit(est)),
    )(qt, iqt, iwt, kk, vt, ik)


def _gla_kernel(gqk_ref, gv_ref, la_ref, gg_ref, gn_ref, o_ref, st_s):
    @pl.when(pl.program_id(1) == 0)
    def _():
        st_s[...] = jnp.zeros_like(st_s)

    c = GLA_CHUNK
    nqk = GLA_HEADS * GLA_DK
    r_i = lax.broadcasted_iota(jnp.int32, (c, c), 0)
    c_i = lax.broadcasted_iota(jnp.int32, (c, c), 1)
    tri = r_i >= c_i
    tri_f = tri.astype(F32)
    gn = gn_ref[...]
    for ci in range(GLA_ROWS // c):
        rows = slice(ci * c, (ci + 1) * c)
        la = la_ref[rows, :]
        b = jnp.dot(tri_f, la, preferred_element_type=F32, precision=lax.Precision.HIGHEST)
        b_last = b[c - 1:c, :]
        q = gqk_ref[rows, :nqk]
        k = gqk_ref[rows, nqk:]
        q_dec = (q * jnp.exp(b)).astype(BF16)
        k_in = (k * jnp.exp(-b)).astype(BF16)
        k_out = (k * jnp.exp(b_last - b)).astype(BF16)
        decay = jnp.exp(b_last)
        for h in range(GLA_HEADS):
            hs = slice(h * GLA_DK, (h + 1) * GLA_DK)
            vs = slice(h * GLA_DV, (h + 1) * GLA_DV)
            qd, ki, ko = q_dec[:, hs], k_in[:, hs], k_out[:, hs]
            v = gv_ref[rows, vs]
            a = lax.dot_general(qd, ki, NT_DIMS, preferred_element_type=F32)
            a = jnp.where(tri, a, 0.0).astype(BF16)
            st = st_s[h]
            o = jnp.dot(a, v, preferred_element_type=F32)
            o = o + lax.dot_general(qd, st.astype(BF16), NT_DIMS, preferred_element_type=F32)
            upd = lax.dot_general(v, ko, TN_DIMS, preferred_element_type=F32)
            st_s[h] = st * decay[:, hs] + upd
            gate = gg_ref[rows, vs]
            o_ref[rows, vs] = (_rms(o, gn) * (gate * jax.nn.sigmoid(gate))).astype(o_ref.dtype)


def _gla(gqk, gv, la, gg, g_norm, batch, seq):
    n = batch * seq
    t = GLA_ROWS
    assert seq % t == 0
    ns = seq // t
    row = lambda b, i: (b * ns + i, 0)
    nqk = GLA_HEADS * GLA_DK
    nv = GLA_HEADS * GLA_DV
    return pl.pallas_call(
        _gla_kernel,
        out_shape=jax.ShapeDtypeStruct((n, nv), BF16),
        grid=(batch, ns),
        in_specs=[
            pl.BlockSpec((t, 2 * nqk), row),
            pl.BlockSpec((t, nv), row),
            pl.BlockSpec((t, nqk), row),
            pl.BlockSpec((t, nv), row),
            _const_spec((1, GLA_DV)),
        ],
        out_specs=pl.BlockSpec((t, nv), row),
        scratch_shapes=[pltpu.VMEM((GLA_HEADS, GLA_DV, GLA_DK), F32)],
        compiler_params=pltpu.CompilerParams(dimension_semantics=("arbitrary", "arbitrary")),
    )(gqk, gv, la, gg, g_norm.reshape(1, GLA_DV))


def _outproj_kernel(x_ref, oa_ref, og_ref, w_ref, g_ref, o_ref):
    m = jnp.dot(oa_ref[...], w_ref[0], preferred_element_type=F32)
    m = m + jnp.dot(og_ref[...], w_ref[1], preferred_element_type=F32)
    o_ref[...] = x_ref[...] + _rms(m, g_ref[...])


def _outproj(x1, oa, og, w_out, g_post):
    n, d = x1.shape
    t = FFN_ROWS
    half = oa.shape[1]
    assert og.shape[1] == half and w_out.shape[0] == 2 * half
    w = w_out.reshape(2, half, d).astype(BF16)
    row = lambda i: (i, 0)
    est = 4 * t * d * 4 + 4 * t * half * 2 + 2 * half * d * 2 + t * d * 4
    return pl.pallas_call(
        _outproj_kernel,
        out_shape=jax.ShapeDtypeStruct((n, d), F32),
        grid=(n // t,),
        in_specs=[
            pl.BlockSpec((t, d), row),
            pl.BlockSpec((t, half), row),
            pl.BlockSpec((t, half), row),
            _const_spec(w.shape),
            _const_spec((1, d)),
        ],
        out_specs=pl.BlockSpec((t, d), row),
        compiler_params=pltpu.CompilerParams(
            dimension_semantics=("arbitrary",), vmem_limit_bytes=_vmem_limit(est)),
    )(x1, oa, og, w, g_post.reshape(1, d))


def kernel(x, g_ffn1_pre, w_ffn1_gate, w_ffn1_up, w_ffn1_down, g_ffn1_post, g_mix_pre, w_in, w_gla_a2,
           b_gla_a, g_gla_norm, w_out, g_mix_post, g_ffn2_pre, w_ffn2_gate, w_ffn2_up, w_ffn2_down,
           g_ffn2_post):
    batch, seq, d = x.shape
    h = x.reshape(batch * seq, d)
    for l in range(g_ffn1_pre.shape[0]):
        h = _ffn(h, g_ffn1_pre[l], w_ffn1_gate[l], w_ffn1_up[l], w_ffn1_down[l], g_ffn1_post[l])
        qt, iqt, vt, iwt, kk, ik, gqk, gv, gg, la = _inproj(h, g_mix_pre[l], w_in[l], w_gla_a2[l], b_gla_a[l])
        oa = _dsa(qt, iqt, iwt, kk, vt, ik, batch, seq)
        og = _gla(gqk, gv, la, gg, g_gla_norm[l], batch, seq)
        h = _outproj(h, oa, og, w_out[l], g_mix_post[l])
        h = _ffn(h, g_ffn2_pre[l], w_ffn2_gate[l], w_ffn2_up[l], w_ffn2_down[l], g_ffn2_post[l])
    return h.reshape(batch, seq, d)
```

```python
import functools

import jax
import jax.numpy as jnp
from jax import lax
from jax.experimental import pallas as pl
from jax.experimental.pallas import tpu as pltpu

ATTN_HEADS = 8
ATTN_KV_HEADS = 2
ATTN_HEAD_DIM = 64
ATTN_REP = ATTN_HEADS // ATTN_KV_HEADS
IDX_HEADS = 8
IDX_DIM = 64
TOPK_MAX = 256
GLA_HEADS = 4
GLA_DK = 64
GLA_DV = 128
GLA_GATE_RANK = 16
GLA_TAU = 16.0
GLA_CHUNK = 64
EPS = 1e-6

V7X_LANES = 128
V7X_SUBLANES = 8
V7X_BF16_ROWS = 16
V7X_MXU_DIM = 256
V7X_VMEM_BYTES = 64 * 2**20

FFN_ROWS = 512
FF_CHUNK = V7X_MXU_DIM
DSA_TQ = 256
DSA_KC = DSA_TQ
GLA_ROWS = 512
VT_ROWS = ATTN_HEAD_DIM + V7X_BF16_ROWS

INT_MIN = -2**31
INT16_MIN, INT16_MAX = -2**15, 2**15 - 1
NEG_BIG = -1e30
SOFTMAX_NORM_MIN = 2.0 ** -60
SOFTMAX_NORM_MAX = 2.0 ** 100

F32 = jnp.float32
BF16 = jnp.bfloat16
NT_DIMS = (((1,), (1,)), ((), ()))
TN_DIMS = (((0,), (0,)), ((), ()))


def _vmem_limit(nbytes):
    return int(min(nbytes * 1.25 + (8 << 20), V7X_VMEM_BYTES - (6 << 20)))


def _rms(x, g):
    return x * lax.rsqrt(jnp.mean(x * x, axis=-1, keepdims=True) + EPS) * g


def _const_spec(shape):
    nd = len(shape)
    return pl.BlockSpec(shape, lambda *_: (0,) * nd, pipeline_mode=pl.Buffered(1))


def _ffn_kernel(x_ref, gpre_ref, wg_ref, wu_ref, wd_ref, gpost_ref, o_ref):
    o_ref[...] = _ffn_residual(x_ref[...], gpre_ref, wg_ref, wu_ref, wd_ref, gpost_ref)


def _mix_ffn_kernel(x_ref, oa_ref, og_ref, wo_ref, gmix_ref, gpre_ref, wg_ref, wu_ref, wd_ref, gpost_ref, o_ref):
    m = jnp.dot(oa_ref[...], wo_ref[0], preferred_element_type=F32)
    m = m + jnp.dot(og_ref[...], wo_ref[1], preferred_element_type=F32)
    x2 = x_ref[...] + _rms(m, gmix_ref[...])
    o_ref[...] = _ffn_residual(x2, gpre_ref, wg_ref, wu_ref, wd_ref, gpost_ref)


def _ffn_residual(x, gpre_ref, wg_ref, wu_ref, wd_ref, gpost_ref):
    xn = _rms(x, gpre_ref[...]).astype(BF16)

    def gate_up(c):
        return (jnp.dot(xn, wg_ref[c], preferred_element_type=F32),
                jnp.dot(xn, wu_ref[c], preferred_element_type=F32))

    nch = wg_ref.shape[0]
    acc = None
    nxt = gate_up(0)
    for c in range(nch):
        g, u = nxt
        if c + 1 < nch:
            nxt = gate_up(c + 1)
        a = (g * jax.nn.sigmoid(g) * u).astype(BF16)
        d = jnp.dot(a, wd_ref[c], preferred_element_type=F32)
        acc = d if acc is None else acc + d
    return x + 0.5 * _rms(acc, gpost_ref[...])


def _ffn(x, g_pre, w_gate, w_up, w_down, g_post, mix=None):
    n, d = x.shape
    dff = w_gate.shape[1]
    nch = dff // FF_CHUNK
    assert nch * FF_CHUNK == dff and n % FFN_ROWS == 0
    wg = w_gate.reshape(d, nch, FF_CHUNK).transpose(1, 0, 2).astype(BF16)
    wu = w_up.reshape(d, nch, FF_CHUNK).transpose(1, 0, 2).astype(BF16)
    wd = w_down.reshape(nch, FF_CHUNK, d).astype(BF16)
    row = lambda i: (i, 0)
    x_spec = pl.BlockSpec((FFN_ROWS, d), row)
    ffn_specs = [_const_spec((1, d)), _const_spec(wg.shape), _const_spec(wu.shape), _const_spec(wd.shape),
                 _const_spec((1, d))]
    ffn_args = (g_pre.reshape(1, d), wg, wu, wd, g_post.reshape(1, d))
    est = 4 * FFN_ROWS * d * 4 + 3 * d * dff * 2 + FFN_ROWS * d * 8 + 4 * FFN_ROWS * FF_CHUNK * 4
    if mix is None:
        body, in_specs, args = _ffn_kernel, [x_spec] + ffn_specs, (x,) + ffn_args
    else:
        oa, og, w_out, g_mix = mix
        half = oa.shape[1]
        assert og.shape[1] == half and w_out.shape == (2 * half, d)
        wo = w_out.reshape(2, half, d).astype(BF16)
        o_spec = pl.BlockSpec((FFN_ROWS, half), row)
        body = _mix_ffn_kernel
        in_specs = [x_spec, o_spec, o_spec, _const_spec(wo.shape), _const_spec((1, d))] + ffn_specs
        args = (x, oa, og, wo, g_mix.reshape(1, d)) + ffn_args
        est += 4 * FFN_ROWS * half * 2 + 2 * half * d * 2 + FFN_ROWS * d * 4
    return pl.pallas_call(
        body,
        out_shape=jax.ShapeDtypeStruct((n, d), F32),
        grid=(n // FFN_ROWS,),
        in_specs=in_specs,
        out_specs=x_spec,
        compiler_params=pltpu.CompilerParams(
            dimension_semantics=("arbitrary",), vmem_limit_bytes=_vmem_limit(est)),
    )(*args)


_K_OFF, _MISC_OFF, _GQ_OFF, _GK_OFF, _GV_OFF, _GG_OFF, _TOK_COLS = 0, 128, 256, 512, 768, 1280, 1792
_GA_LANE = 72
_QT_OFF, _IQT_OFF, _VT_OFF, _IWT_OFF, _T_ROWS = 0, 512, 1024, 1152, 1168


def _inproj_kernel(x_ref, g_ref, wtok_ref, wt_ref, wa2_ref, ba_ref,
                   qt_ref, iqt_ref, vt_ref, iwt_ref, kk_ref, ik_ref, gqk_ref, gv_ref, gg_ref, la_ref):
    h = _rms(x_ref[...], g_ref[...]).astype(BF16)
    t = h.shape[0]
    pt = lax.dot_general(wt_ref[...], h, NT_DIMS, preferred_element_type=F32)
    for i in range(ATTN_HEADS):
        qt_ref[i] = pt[_QT_OFF + 64 * i:_QT_OFF + 64 * (i + 1)].astype(BF16)
    for i in range(IDX_HEADS):
        iqt_ref[i] = pt[_IQT_OFF + 64 * i:_IQT_OFF + 64 * (i + 1)].astype(BF16)
    ones_row = (lax.broadcasted_iota(jnp.int32, (V7X_BF16_ROWS, t), 0) == 0).astype(F32)
    for g in range(ATTN_KV_HEADS):
        v_t = pt[_VT_OFF + 64 * g:_VT_OFF + 64 * (g + 1)]
        vt_ref[g] = jnp.concatenate([v_t, ones_row], axis=0).astype(BF16)
    iwt_ref[...] = pt[_IWT_OFF:_IWT_OFF + IDX_HEADS] * (IDX_HEADS ** -0.5)

    proj = jnp.dot(h, wtok_ref[...], preferred_element_type=F32)
    for g in range(ATTN_KV_HEADS):
        kk_ref[g] = proj[:, _K_OFF + 64 * g:_K_OFF + 64 * (g + 1)].astype(BF16)
    misc = proj[:, _MISC_OFF:_MISC_OFF + V7X_LANES]
    ik_ref[...] = misc[:, :IDX_DIM].astype(BF16)
    gqk_ref[...] = proj[:, _GQ_OFF:_GV_OFF]
    gv_ref[...] = proj[:, _GV_OFF:_GG_OFF].astype(BF16)
    gg_ref[...] = proj[:, _GG_OFF:_GG_OFF + GLA_HEADS * GLA_DV]
    z = jnp.dot(misc, wa2_ref[...], preferred_element_type=F32, precision=lax.Precision.HIGHEST)
    z = z + ba_ref[...]
    log_sig = jnp.minimum(z, 0.0) - jnp.log1p(jnp.exp(-jnp.abs(z)))
    la_ref[...] = log_sig * (1.0 / GLA_TAU)


def _pack_w_in(w_in):
    d = w_in.shape[0]
    sizes = (512, 128, 128, 512, 64, 8, 256, 256, 512, 16, 512)
    offs = [0]
    for s in sizes:
        offs.append(offs[-1] + s)
    aq, ak, av, iq, ik, iw, gq, gk, gv, ga, gg = [w_in[:, offs[i]:offs[i + 1]] for i in range(len(sizes))]
    z = lambda n: jnp.zeros((d, n), w_in.dtype)
    w_tok = jnp.concatenate([ak, ik, z(8), ga, z(40), gq * (GLA_DK ** -0.5), gk, gv, gg], axis=1)
    w_t = jnp.concatenate([aq * (ATTN_HEAD_DIM ** -0.5), iq * (IDX_DIM ** -0.5), av, iw, z(8)], axis=1).T
    assert w_tok.shape[1] == _TOK_COLS and w_t.shape[0] == _T_ROWS
    return w_tok.astype(BF16), w_t.astype(BF16)


def _inproj(x1, g_mix_pre, w_in, w_gla_a2, b_gla_a):
    n, d = x1.shape
    t = FFN_ROWS
    w_tok, w_t = _pack_w_in(w_in)
    nqk = GLA_HEADS * GLA_DK
    nv = GLA_HEADS * GLA_DV
    wa2 = jnp.zeros((V7X_LANES, nqk), F32).at[_GA_LANE:_GA_LANE + GLA_GATE_RANK].set(w_gla_a2)
    row = lambda i: (i, 0)
    row3 = lambda i: (0, i, 0)
    col3 = lambda i: (0, 0, i)
    out_shape = (
        jax.ShapeDtypeStruct((ATTN_HEADS, ATTN_HEAD_DIM, n), BF16),
        jax.ShapeDtypeStruct((IDX_HEADS, IDX_DIM, n), BF16),
        jax.ShapeDtypeStruct((ATTN_KV_HEADS, VT_ROWS, n), BF16),
        jax.ShapeDtypeStruct((IDX_HEADS, n), F32),
        jax.ShapeDtypeStruct((ATTN_KV_HEADS, n, ATTN_HEAD_DIM), BF16),
        jax.ShapeDtypeStruct((n, IDX_DIM), BF16),
        jax.ShapeDtypeStruct((n, 2 * nqk), F32),
        jax.ShapeDtypeStruct((n, nv), BF16),
        jax.ShapeDtypeStruct((n, nv), F32),
        jax.ShapeDtypeStruct((n, nqk), F32),
    )
    out_specs = (
        pl.BlockSpec((ATTN_HEADS, ATTN_HEAD_DIM, t), col3),
        pl.BlockSpec((IDX_HEADS, IDX_DIM, t), col3),
        pl.BlockSpec((ATTN_KV_HEADS, VT_ROWS, t), col3),
        pl.BlockSpec((IDX_HEADS, t), lambda i: (0, i)),
        pl.BlockSpec((ATTN_KV_HEADS, t, ATTN_HEAD_DIM), row3),
        pl.BlockSpec((t, IDX_DIM), row),
        pl.BlockSpec((t, 2 * nqk), row),
        pl.BlockSpec((t, nv), row),
        pl.BlockSpec((t, nv), row),
        pl.BlockSpec((t, nqk), row),
    )
    est = (2 * t * d * 4 + d * (_TOK_COLS + _T_ROWS) * 2 + t * (_TOK_COLS + _T_ROWS) * 4 * 2
           + 2 * t * 8 * 1024)
    return pl.pallas_call(
        _inproj_kernel,
        out_shape=out_shape,
        grid=(n // t,),
        in_specs=[
            pl.BlockSpec((t, d), row),
            _const_spec((1, d)),
            _const_spec(w_tok.shape),
            _const_spec(w_t.shape),
            _const_spec(wa2.shape),
            _const_spec((1, nqk)),
        ],
        out_specs=out_specs,
        compiler_params=pltpu.CompilerParams(
            dimension_semantics=("arbitrary",), vmem_limit_bytes=_vmem_limit(est)),
    )(x1, g_mix_pre.reshape(1, d), w_tok, w_t, wa2, b_gla_a.reshape(1, nqk))


def _dsa_kernel(qt_ref, iqt_ref, iwt_ref, kk_ref, vt_ref, ik_ref, o_ref,
                key_s, half_s, low_s, bias_s, acc_s, ot_s, *, topk, idx_bits):
    tq = DSA_TQ
    kc = DSA_KC
    sub = V7X_SUBLANES
    i = pl.program_id(1)
    nchunks = i + 1

    def score_logits(j):
        ikc = ik_ref[pl.ds(pl.multiple_of(j * kc, kc), kc), :]
        iqcat = jnp.concatenate([iqt_ref[h] for h in range(IDX_HEADS)], axis=1)
        return jnp.dot(ikc, iqcat, preferred_element_type=F32)

    def score_keys(j, lg, diagonal):
        sc = jnp.zeros((kc, tq), F32)
        for h in range(IDX_HEADS):
            sc = sc + jnp.maximum(lg[:, h * tq:(h + 1) * tq], 0.0) * iwt_ref[h:h + 1, :]
        bits = pltpu.bitcast(sc, jnp.int32)
        key = jnp.where(bits < 0, bits ^ 0x7FFFFFFF, bits)
        if diagonal:
            kpos = lax.broadcasted_iota(jnp.int32, (kc, tq), 0)
            qpos = lax.broadcasted_iota(jnp.int32, (kc, tq), 1)
            key = jnp.where(kpos <= qpos, key, INT_MIN)
        key_s[j] = key
        half_s[j] = lax.shift_right_arithmetic(key, 16).astype(jnp.int16)
        low_s[j] = key.astype(jnp.int16) ^ jnp.int16(INT16_MIN)

    def score_pair(jj, c):
        lgs = [score_logits(2 * jj + s) for s in range(2)]
        for s in range(2):
            score_keys(2 * jj + s, lgs[s], False)
        return c

    lax.fori_loop(0, lax.shift_right_logical(i, 1), score_pair, 0)

    @pl.when((i & 1) == 1)
    def _():
        score_keys(i - 1, score_logits(i - 1), False)

    score_keys(i, score_logits(i), True)

    qidx = i * tq + lax.broadcasted_iota(jnp.int32, (1, tq), 1)
    krow = jnp.minimum(topk, qidx + 1).astype(F32)

    def count(ind_fn):
        def body(j, acc):
            parts = [acc, None, None, None]
            for s in range(kc // sub):
                ind = ind_fn(key_s[j, s * sub:(s + 1) * sub, :], j * kc + s * sub)
                p = s % 4
                parts[p] = ind if parts[p] is None else parts[p] + ind
            return (parts[0] + parts[1]) + (parts[2] + parts[3])
        acc = lax.fori_loop(0, nchunks, body, jnp.zeros((sub, tq), F32))
        return jnp.sum(acc, axis=0, keepdims=True)

    def count_ge(cand):
        cb = jnp.broadcast_to(cand, (sub, tq))
        return count(lambda k, base: jnp.where(k >= cb, 1.0, 0.0))

    rows16 = V7X_BF16_ROWS

    def count16(cand):
        cb = jnp.broadcast_to(cand, (rows16, tq)).astype(jnp.int16)
        one, nil = jnp.int16(1), jnp.int16(0)

        def body(j, acc):
            parts = [acc, None, None, None]
            for s in range(kc // rows16):
                ind = jnp.where(half_s[j, s * rows16:(s + 1) * rows16, :] >= cb, one, nil)
                p = s % 4
                parts[p] = ind if parts[p] is None else parts[p] + ind
            return (parts[0] + parts[1]) + (parts[2] + parts[3])

        acc = lax.fori_loop(0, nchunks, body, jnp.zeros((rows16, tq), jnp.int16))
        return jnp.sum(acc.astype(F32), axis=0, keepdims=True)

    def search16(count_at_min):
        zero = jnp.zeros((1, tq), jnp.int32)
        c0 = count16(zero)
        start = (jnp.where(c0 >= krow, zero, INT16_MIN), jnp.where(c0 >= krow, c0, count_at_min))

        def bit_body(it, carry):
            th, cth = carry
            cand = th | lax.shift_left(jnp.int32(1), 14 - it)
            c = count16(cand)
            return jnp.where(c >= krow, cand, th), jnp.where(c >= krow, c, cth)

        return lax.fori_loop(0, 15, bit_body, start)

    t_hi, c_hi = search16(jnp.full((1, tq), 1.0, F32) * (nchunks * kc).astype(F32))
    th16 = jnp.broadcast_to(t_hi, (kc, tq)).astype(jnp.int16)

    def low_halves(j, c):
        hi = half_s[j]
        half_s[j] = jnp.where(hi == th16, low_s[j],
                              jnp.where(hi > th16, jnp.int16(INT16_MAX), jnp.int16(INT16_MIN)))
        return c

    lax.fori_loop(0, nchunks, low_halves, 0)
    t_lo, c_ge = search16(c_hi)
    t = lax.shift_left(t_hi, 16) | (t_lo - INT16_MIN)

    saturated = jnp.max(jnp.where(t_lo == INT16_MAX, 1.0, 0.0)) > 0.5
    c_gt = lax.cond(saturated, lambda: count_ge(t + 1), lambda: count16(jnp.minimum(t_lo + 1, INT16_MAX)))
    need = krow - c_gt
    tb8 = jnp.broadcast_to(t, (sub, tq))

    @pl.when(jnp.max(c_ge - krow) > 0.5)
    def _():
        sub_iota = lax.broadcasted_iota(jnp.int32, (sub, tq), 0)

        def tied_below(u):
            ub = jnp.broadcast_to(u, (sub, tq))
            return count(lambda k, base: jnp.where(k == tb8, jnp.where(sub_iota + base < ub, 1.0, 0.0), 0.0))

        def ubit(it, u):
            cand = u | lax.shift_left(jnp.int32(1), idx_bits - 1 - it)
            return jnp.where(tied_below(cand) < need, cand, u)

        u = lax.fori_loop(0, idx_bits, ubit, jnp.zeros((1, tq), jnp.int32))
        ub_full = jnp.broadcast_to(u, (kc, tq))
        tb_full = jnp.broadcast_to(t, (kc, tq))
        kidx = lax.broadcasted_iota(jnp.int32, (kc, tq), 0)

        def drop(j, c):
            k = key_s[j]
            key_s[j] = jnp.where(k == tb_full, jnp.where(kidx + j * kc > ub_full, INT_MIN, k), k)
            return c

        lax.fori_loop(0, nchunks, drop, 0)

    def masked_scores(j, g, slot=0):
        koff = pl.multiple_of(j * kc, kc)
        qcat = jnp.concatenate([qt_ref[g * ATTN_REP + r] for r in range(ATTN_REP)], axis=1)
        bias = jnp.concatenate([bias_s[slot]] * ATTN_REP, axis=1)
        kg = kk_ref[g, pl.ds(koff, kc), :]
        vg = vt_ref[g, :, pl.ds(koff, kc)]
        return jnp.dot(kg, qcat, preferred_element_type=F32) + bias, vg

    def set_bias(j, slot=0):
        bias_s[slot] = jnp.where(key_s[j] >= jnp.broadcast_to(t, (kc, tq)), 0.0, NEG_BIG)

    acc_s[...] = jnp.zeros_like(acc_s)

    def att_chunk(j, slot):
        set_bias(j, slot)
        for g in range(ATTN_KV_HEADS):
            s, vg = masked_scores(j, g, slot)
            acc_s[g] += jnp.dot(vg, jnp.exp(s).astype(BF16), preferred_element_type=F32)

    def att_pair(jj, c):
        work = []
        for slot in range(2):
            j = 2 * jj + slot
            set_bias(j, slot)
            for g in range(ATTN_KV_HEADS):
                work.append((g, masked_scores(j, g, slot)))
        for g, (s, vg) in work:
            acc_s[g] += jnp.dot(vg, jnp.exp(s).astype(BF16), preferred_element_type=F32)
        return c

    lax.fori_loop(0, lax.shift_right_logical(nchunks, 1), att_pair, 0)

    @pl.when((nchunks & 1) == 1)
    def _():
        att_chunk(nchunks - 1, 0)

    norm = acc_s[:, ATTN_HEAD_DIM:ATTN_HEAD_DIM + 1, :]
    in_range = jnp.where(norm >= SOFTMAX_NORM_MIN, jnp.where(norm <= SOFTMAX_NORM_MAX, 1.0, 0.0), 0.0)

    @pl.when(jnp.min(in_range) < 0.5)
    def _():
        acc_s[...] = jnp.zeros_like(acc_s)

        def online_body(j, ms):
            set_bias(j)
            out = []
            for g in range(ATTN_KV_HEADS):
                s, vg = masked_scores(j, g)
                m_new = jnp.maximum(ms[g], jnp.max(s, axis=0, keepdims=True))
                p = jnp.exp(s - m_new).astype(BF16)
                pv = jnp.dot(vg, p, preferred_element_type=F32)
                acc_s[g] = jnp.exp(ms[g] - m_new) * acc_s[g] + pv
                out.append(m_new)
            return tuple(out)

        m_init = tuple(jnp.full((1, ATTN_REP * tq), NEG_BIG, F32) for _ in range(ATTN_KV_HEADS))
        lax.fori_loop(0, nchunks, online_body, m_init)

    for h in range(ATTN_HEADS):
        g, r = divmod(h, ATTN_REP)
        a = acc_s[g, :, r * tq:(r + 1) * tq]
        ot_s[h * ATTN_HEAD_DIM:(h + 1) * ATTN_HEAD_DIM, :] = (
            a[:ATTN_HEAD_DIM] / a[ATTN_HEAD_DIM:ATTN_HEAD_DIM + 1])
    o_ref[...] = ot_s[...].T.astype(o_ref.dtype)


def _dsa(qt, iqt, iwt, kk, vt, ik, batch, seq):
    n = batch * seq
    tq, kc = DSA_TQ, DSA_KC
    assert seq % tq == 0
    nq = seq // tq
    topk = min(TOPK_MAX, seq // 4)
    nch = seq // kc
    nout = ATTN_HEADS * ATTN_HEAD_DIM
    qmap = lambda b, i: (0, 0, b * nq + i)
    est = (nch * kc * tq * 8 + 2 * kc * tq * 4 + ATTN_HEADS * (VT_ROWS + 8) * tq * 4 + nout * tq * 4
           + ATTN_KV_HEADS * seq * (V7X_LANES + VT_ROWS) * 2 + seq * V7X_LANES * 2
           + 2 * (2 * ATTN_HEADS * ATTN_HEAD_DIM * tq * 2 + 8 * tq * 4 + tq * nout * 2))
    return pl.pallas_call(
        functools.partial(_dsa_kernel, topk=topk, idx_bits=(seq - 1).bit_length()),
        out_shape=jax.ShapeDtypeStruct((n, nout), BF16),
        grid=(batch, nq),
        in_specs=[
            pl.BlockSpec((ATTN_HEADS, ATTN_HEAD_DIM, tq), qmap),
            pl.BlockSpec((IDX_HEADS, IDX_DIM, tq), qmap),
            pl.BlockSpec((IDX_HEADS, tq), lambda b, i: (0, b * nq + i)),
            pl.BlockSpec((ATTN_KV_HEADS, seq, ATTN_HEAD_DIM), lambda b, i: (0, b, 0),
                         pipeline_mode=pl.Buffered(1)),
            pl.BlockSpec((ATTN_KV_HEADS, VT_ROWS, seq), lambda b, i: (0, 0, b),
                         pipeline_mode=pl.Buffered(1)),
            pl.BlockSpec((seq, IDX_DIM), lambda b, i: (b, 0), pipeline_mode=pl.Buffered(1)),
        ],
        out_specs=pl.BlockSpec((tq, nout), lambda b, i: (b * nq + i, 0)),
        scratch_shapes=[
            pltpu.VMEM((nch, kc, tq), jnp.int32),
            pltpu.VMEM((nch, kc, tq), jnp.int16),
            pltpu.VMEM((nch, kc, tq), jnp.int16),
            pltpu.VMEM((2, kc, tq), F32),
            pltpu.VMEM((ATTN_KV_HEADS, VT_ROWS, ATTN_REP * tq), F32),
            pltpu.VMEM((nout, tq), F32),
        ],
        compiler_params=pltpu.CompilerParams(
            dimension_semantics=("arbitrary", "arbitrary"), vmem_limit_bytes=_vmem_limit(est)),
    )(qt, iqt, iwt, kk, vt, ik)


def _gla_kernel(gqk_ref, gv_ref, la_ref, gg_ref, gn_ref, o_ref, st_s):
    @pl.when(pl.program_id(1) == 0)
    def _():
        st_s[...] = jnp.zeros_like(st_s)

    c = GLA_CHUNK
    nqk = GLA_HEADS * GLA_DK
    r_i = lax.broadcasted_iota(jnp.int32, (c, c), 0)
    c_i = lax.broadcasted_iota(jnp.int32, (c, c), 1)
    tri = r_i >= c_i
    tri_f = tri.astype(F32)
    gn = gn_ref[...]
    nchunk = GLA_ROWS // c
    heads = range(GLA_HEADS)
    hs = [slice(h * GLA_DK, (h + 1) * GLA_DK) for h in heads]
    vs = [slice(h * GLA_DV, (h + 1) * GLA_DV) for h in heads]

    local = []
    for ci in range(nchunk):
        rows = slice(ci * c, (ci + 1) * c)
        la = la_ref[rows, :]
        b = jnp.dot(tri_f, la, preferred_element_type=F32, precision=lax.Precision.HIGHEST)
        b_last = b[c - 1:c, :]
        q = gqk_ref[rows, :nqk]
        k = gqk_ref[rows, nqk:]
        q_dec = (q * jnp.exp(b)).astype(BF16)
        k_in = (k * jnp.exp(-b)).astype(BF16)
        k_out = (k * jnp.exp(b_last - b)).astype(BF16)
        decay = jnp.exp(b_last)
        intra, upd = [], []
        for h in heads:
            v = gv_ref[rows, vs[h]]
            a = lax.dot_general(q_dec[:, hs[h]], k_in[:, hs[h]], NT_DIMS, preferred_element_type=F32)
            a = jnp.where(tri, a, 0.0).astype(BF16)
            intra.append(jnp.dot(a, v, preferred_element_type=F32))
            upd.append(lax.dot_general(v, k_out[:, hs[h]], TN_DIMS, preferred_element_type=F32))
        local.append((rows, q_dec, decay, intra, upd))

    st = [st_s[h] for h in heads]
    for rows, q_dec, decay, intra, upd in local:
        for h in heads:
            o = intra[h] + lax.dot_general(q_dec[:, hs[h]], st[h].astype(BF16), NT_DIMS,
                                           preferred_element_type=F32)
            st[h] = st[h] * decay[:, hs[h]] + upd[h]
            gate = gg_ref[rows, vs[h]]
            o_ref[rows, vs[h]] = (_rms(o, gn) * (gate * jax.nn.sigmoid(gate))).astype(o_ref.dtype)
    for h in heads:
        st_s[h] = st[h]


def _gla(gqk, gv, la, gg, g_norm, batch, seq):
    n = batch * seq
    t = GLA_ROWS
    assert seq % t == 0
    ns = seq // t
    row = lambda b, i: (b * ns + i, 0)
    nqk = GLA_HEADS * GLA_DK
    nv = GLA_HEADS * GLA_DV
    return pl.pallas_call(
        _gla_kernel,
        out_shape=jax.ShapeDtypeStruct((n, nv), BF16),
        grid=(batch, ns),
        in_specs=[
            pl.BlockSpec((t, 2 * nqk), row),
            pl.BlockSpec((t, nv), row),
            pl.BlockSpec((t, nqk), row),
            pl.BlockSpec((t, nv), row),
            _const_spec((1, GLA_DV)),
        ],
        out_specs=pl.BlockSpec((t, nv), row),
        scratch_shapes=[pltpu.VMEM((GLA_HEADS, GLA_DV, GLA_DK), F32)],
        compiler_params=pltpu.CompilerParams(dimension_semantics=("arbitrary", "arbitrary")),
    )(gqk, gv, la, gg, g_norm.reshape(1, GLA_DV))


def kernel(x, g_ffn1_pre, w_ffn1_gate, w_ffn1_up, w_ffn1_down, g_ffn1_post, g_mix_pre, w_in, w_gla_a2,
           b_gla_a, g_gla_norm, w_out, g_mix_post, g_ffn2_pre, w_ffn2_gate, w_ffn2_up, w_ffn2_down,
           g_ffn2_post):
    batch, seq, d = x.shape
    h = x.reshape(batch * seq, d)
    for l in range(g_ffn1_pre.shape[0]):
        h = _ffn(h, g_ffn1_pre[l], w_ffn1_gate[l], w_ffn1_up[l], w_ffn1_down[l], g_ffn1_post[l])
        qt, iqt, vt, iwt, kk, ik, gqk, gv, gg, la = _inproj(h, g_mix_pre[l], w_in[l], w_gla_a2[l], b_gla_a[l])
        oa = _dsa(qt, iqt, iwt, kk, vt, ik, batch, seq)
        og = _gla(gqk, gv, la, gg, g_gla_norm[l], batch, seq)
        h = _ffn(h, g_ffn2_pre[l], w_ffn2_gate[l], w_ffn2_up[l], w_ffn2_down[l], g_ffn2_post[l],
                 mix=(oa, og, w_out[l], g_mix_post[l]))
    return h.reshape(batch, seq, d)
```

```python
import functools

import jax
import jax.numpy as jnp
from jax import lax
from jax.experimental import pallas as pl
from jax.experimental.pallas import tpu as pltpu

ATTN_HEADS = 8
ATTN_KV_HEADS = 2
ATTN_HEAD_DIM = 64
ATTN_REP = ATTN_HEADS // ATTN_KV_HEADS
IDX_HEADS = 8
IDX_DIM = 64
TOPK_MAX = 256
GLA_HEADS = 4
GLA_DK = 64
GLA_DV = 128
GLA_GATE_RANK = 16
GLA_TAU = 16.0
GLA_CHUNK = 64
EPS = 1e-6

V7X_LANES = 128
V7X_SUBLANES = 8
V7X_BF16_ROWS = 16
V7X_MXU_DIM = 256
V7X_VMEM_BYTES = 64 * 2**20

FFN_ROWS = 512
FF_CHUNK = V7X_MXU_DIM
DSA_TQ = 256
DSA_KC = DSA_TQ
DSA_GROUP = 4
GLA_ROWS = 512
VT_ROWS = ATTN_HEAD_DIM + V7X_BF16_ROWS

INT_MIN = -2**31
INT16_MIN, INT16_MAX = -2**15, 2**15 - 1
NEG_BIG = -1e30
SOFTMAX_NORM_MIN = 2.0 ** -60
SOFTMAX_NORM_MAX = 2.0 ** 100

F32 = jnp.float32
BF16 = jnp.bfloat16
NT_DIMS = (((1,), (1,)), ((), ()))
TN_DIMS = (((0,), (0,)), ((), ()))


def _vmem_limit(nbytes):
    return int(min(nbytes * 1.25 + (8 << 20), V7X_VMEM_BYTES - (6 << 20)))


def _rms(x, g):
    return x * lax.rsqrt(jnp.mean(x * x, axis=-1, keepdims=True) + EPS) * g


def _const_spec(shape):
    nd = len(shape)
    return pl.BlockSpec(shape, lambda *_: (0,) * nd, pipeline_mode=pl.Buffered(1))


def _ffn_kernel(x_ref, gpre_ref, wg_ref, wu_ref, wd_ref, gpost_ref, o_ref):
    o_ref[...] = _ffn_residual(x_ref[...], gpre_ref, wg_ref, wu_ref, wd_ref, gpost_ref)


def _mix_ffn_kernel(x_ref, oa_ref, og_ref, wo_ref, gmix_ref, gpre_ref, wg_ref, wu_ref, wd_ref, gpost_ref, o_ref):
    m = jnp.dot(oa_ref[...], wo_ref[0], preferred_element_type=F32)
    m = m + jnp.dot(og_ref[...], wo_ref[1], preferred_element_type=F32)
    x2 = x_ref[...] + _rms(m, gmix_ref[...])
    o_ref[...] = _ffn_residual(x2, gpre_ref, wg_ref, wu_ref, wd_ref, gpost_ref)


def _ffn_residual(x, gpre_ref, wg_ref, wu_ref, wd_ref, gpost_ref):
    xn = _rms(x, gpre_ref[...]).astype(BF16)

    def gate_up(c):
        return (jnp.dot(xn, wg_ref[c], preferred_element_type=F32),
                jnp.dot(xn, wu_ref[c], preferred_element_type=F32))

    nch = wg_ref.shape[0]
    acc = None
    nxt = gate_up(0)
    for c in range(nch):
        g, u = nxt
        if c + 1 < nch:
            nxt = gate_up(c + 1)
        a = (g * jax.nn.sigmoid(g) * u).astype(BF16)
        d = jnp.dot(a, wd_ref[c], preferred_element_type=F32)
        acc = d if acc is None else acc + d
    return x + 0.5 * _rms(acc, gpost_ref[...])


def _ffn(x, g_pre, w_gate, w_up, w_down, g_post, mix=None):
    n, d = x.shape
    dff = w_gate.shape[1]
    nch = dff // FF_CHUNK
    assert nch * FF_CHUNK == dff and n % FFN_ROWS == 0
    wg = w_gate.reshape(d, nch, FF_CHUNK).transpose(1, 0, 2).astype(BF16)
    wu = w_up.reshape(d, nch, FF_CHUNK).transpose(1, 0, 2).astype(BF16)
    wd = w_down.reshape(nch, FF_CHUNK, d).astype(BF16)
    row = lambda i: (i, 0)
    x_spec = pl.BlockSpec((FFN_ROWS, d), row)
    ffn_specs = [_const_spec((1, d)), _const_spec(wg.shape), _const_spec(wu.shape), _const_spec(wd.shape),
                 _const_spec((1, d))]
    ffn_args = (g_pre.reshape(1, d), wg, wu, wd, g_post.reshape(1, d))
    est = 4 * FFN_ROWS * d * 4 + 3 * d * dff * 2 + FFN_ROWS * d * 8 + 4 * FFN_ROWS * FF_CHUNK * 4
    if mix is None:
        body, in_specs, args = _ffn_kernel, [x_spec] + ffn_specs, (x,) + ffn_args
    else:
        oa, og, w_out, g_mix = mix
        half = oa.shape[1]
        assert og.shape[1] == half and w_out.shape == (2 * half, d)
        wo = w_out.reshape(2, half, d).astype(BF16)
        o_spec = pl.BlockSpec((FFN_ROWS, half), row)
        body = _mix_ffn_kernel
        in_specs = [x_spec, o_spec, o_spec, _const_spec(wo.shape), _const_spec((1, d))] + ffn_specs
        args = (x, oa, og, wo, g_mix.reshape(1, d)) + ffn_args
        est += 4 * FFN_ROWS * half * 2 + 2 * half * d * 2 + FFN_ROWS * d * 4
    return pl.pallas_call(
        body,
        out_shape=jax.ShapeDtypeStruct((n, d), F32),
        grid=(n // FFN_ROWS,),
        in_specs=in_specs,
        out_specs=x_spec,
        compiler_params=pltpu.CompilerParams(
            dimension_semantics=("arbitrary",), vmem_limit_bytes=_vmem_limit(est)),
    )(*args)


_K_OFF, _MISC_OFF, _GQ_OFF, _GK_OFF, _GV_OFF, _GG_OFF, _TOK_COLS = 0, 128, 256, 512, 768, 1280, 1792
_GA_LANE = 72
_QT_OFF, _IQT_OFF, _VT_OFF, _IWT_OFF, _T_ROWS = 0, 512, 1024, 1152, 1168


def _inproj_kernel(x_ref, g_ref, wtok_ref, wt_ref, wa2_ref, ba_ref,
                   qt_ref, iqt_ref, vt_ref, iwt_ref, kk_ref, ik_ref, gqk_ref, gv_ref, gg_ref, la_ref):
    h = _rms(x_ref[...], g_ref[...]).astype(BF16)
    t = h.shape[0]
    pt = lax.dot_general(wt_ref[...], h, NT_DIMS, preferred_element_type=F32)
    for i in range(ATTN_HEADS):
        qt_ref[i] = pt[_QT_OFF + 64 * i:_QT_OFF + 64 * (i + 1)].astype(BF16)
    for i in range(IDX_HEADS):
        iqt_ref[i] = pt[_IQT_OFF + 64 * i:_IQT_OFF + 64 * (i + 1)].astype(BF16)
    ones_row = (lax.broadcasted_iota(jnp.int32, (V7X_BF16_ROWS, t), 0) == 0).astype(F32)
    for g in range(ATTN_KV_HEADS):
        v_t = pt[_VT_OFF + 64 * g:_VT_OFF + 64 * (g + 1)]
        vt_ref[g] = jnp.concatenate([v_t, ones_row], axis=0).astype(BF16)
    iwt_ref[...] = pt[_IWT_OFF:_IWT_OFF + IDX_HEADS] * (IDX_HEADS ** -0.5)

    proj = jnp.dot(h, wtok_ref[...], preferred_element_type=F32)
    for g in range(ATTN_KV_HEADS):
        kk_ref[g] = proj[:, _K_OFF + 64 * g:_K_OFF + 64 * (g + 1)].astype(BF16)
    misc = proj[:, _MISC_OFF:_MISC_OFF + V7X_LANES]
    ik_ref[...] = misc[:, :IDX_DIM].astype(BF16)
    gqk_ref[...] = proj[:, _GQ_OFF:_GV_OFF]
    gv_ref[...] = proj[:, _GV_OFF:_GG_OFF].astype(BF16)
    gg_ref[...] = proj[:, _GG_OFF:_GG_OFF + GLA_HEADS * GLA_DV]
    z = jnp.dot(misc, wa2_ref[...], preferred_element_type=F32, precision=lax.Precision.HIGHEST)
    z = z + ba_ref[...]
    log_sig = jnp.minimum(z, 0.0) - jnp.log1p(jnp.exp(-jnp.abs(z)))
    la_ref[...] = log_sig * (1.0 / GLA_TAU)


def _pack_w_in(w_in):
    d = w_in.shape[0]
    sizes = (512, 128, 128, 512, 64, 8, 256, 256, 512, 16, 512)
    offs = [0]
    for s in sizes:
        offs.append(offs[-1] + s)
    aq, ak, av, iq, ik, iw, gq, gk, gv, ga, gg = [w_in[:, offs[i]:offs[i + 1]] for i in range(len(sizes))]
    z = lambda n: jnp.zeros((d, n), w_in.dtype)
    w_tok = jnp.concatenate([ak, ik, z(8), ga, z(40), gq * (GLA_DK ** -0.5), gk, gv, gg], axis=1)
    w_t = jnp.concatenate([aq * (ATTN_HEAD_DIM ** -0.5), iq * (IDX_DIM ** -0.5), av, iw, z(8)], axis=1).T
    assert w_tok.shape[1] == _TOK_COLS and w_t.shape[0] == _T_ROWS
    return w_tok.astype(BF16), w_t.astype(BF16)


def _inproj(x1, g_mix_pre, w_in, w_gla_a2, b_gla_a):
    n, d = x1.shape
    t = FFN_ROWS
    w_tok, w_t = _pack_w_in(w_in)
    nqk = GLA_HEADS * GLA_DK
    nv = GLA_HEADS * GLA_DV
    wa2 = jnp.zeros((V7X_LANES, nqk), F32).at[_GA_LANE:_GA_LANE + GLA_GATE_RANK].set(w_gla_a2)
    row = lambda i: (i, 0)
    row3 = lambda i: (0, i, 0)
    col3 = lambda i: (0, 0, i)
    out_shape = (
        jax.ShapeDtypeStruct((ATTN_HEADS, ATTN_HEAD_DIM, n), BF16),
        jax.ShapeDtypeStruct((IDX_HEADS, IDX_DIM, n), BF16),
        jax.ShapeDtypeStruct((ATTN_KV_HEADS, VT_ROWS, n), BF16),
        jax.ShapeDtypeStruct((IDX_HEADS, n), F32),
        jax.ShapeDtypeStruct((ATTN_KV_HEADS, n, ATTN_HEAD_DIM), BF16),
        jax.ShapeDtypeStruct((n, IDX_DIM), BF16),
        jax.ShapeDtypeStruct((n, 2 * nqk), F32),
        jax.ShapeDtypeStruct((n, nv), BF16),
        jax.ShapeDtypeStruct((n, nv), F32),
        jax.ShapeDtypeStruct((n, nqk), F32),
    )
    out_specs = (
        pl.BlockSpec((ATTN_HEADS, ATTN_HEAD_DIM, t), col3),
        pl.BlockSpec((IDX_HEADS, IDX_DIM, t), col3),
        pl.BlockSpec((ATTN_KV_HEADS, VT_ROWS, t), col3),
        pl.BlockSpec((IDX_HEADS, t), lambda i: (0, i)),
        pl.BlockSpec((ATTN_KV_HEADS, t, ATTN_HEAD_DIM), row3),
        pl.BlockSpec((t, IDX_DIM), row),
        pl.BlockSpec((t, 2 * nqk), row),
        pl.BlockSpec((t, nv), row),
        pl.BlockSpec((t, nv), row),
        pl.BlockSpec((t, nqk), row),
    )
    est = (2 * t * d * 4 + d * (_TOK_COLS + _T_ROWS) * 2 + t * (_TOK_COLS + _T_ROWS) * 4 * 2
           + 2 * t * 8 * 1024)
    return pl.pallas_call(
        _inproj_kernel,
        out_shape=out_shape,
        grid=(n // t,),
        in_specs=[
            pl.BlockSpec((t, d), row),
            _const_spec((1, d)),
            _const_spec(w_tok.shape),
            _const_spec(w_t.shape),
            _const_spec(wa2.shape),
            _const_spec((1, nqk)),
        ],
        out_specs=out_specs,
        compiler_params=pltpu.CompilerParams(
            dimension_semantics=("arbitrary",), vmem_limit_bytes=_vmem_limit(est)),
    )(x1, g_mix_pre.reshape(1, d), w_tok, w_t, wa2, b_gla_a.reshape(1, nqk))


def _dsa_kernel(qt_ref, iqt_ref, iwt_ref, kk_ref, vt_ref, ik_ref, o_ref,
                key_s, half_s, low_s, bias_s, acc_s, ot_s, *, topk, idx_bits):
    tq = DSA_TQ
    kc = DSA_KC
    sub = V7X_SUBLANES
    i = pl.program_id(1)
    nchunks = i + 1

    def score_logits(j):
        ikc = ik_ref[pl.ds(pl.multiple_of(j * kc, kc), kc), :]
        iqcat = jnp.concatenate([iqt_ref[h] for h in range(IDX_HEADS)], axis=1)
        return jnp.dot(ikc, iqcat, preferred_element_type=F32)

    def score_keys(j, lg, diagonal):
        sc = jnp.zeros((kc, tq), F32)
        for h in range(IDX_HEADS):
            sc = sc + jnp.maximum(lg[:, h * tq:(h + 1) * tq], 0.0) * iwt_ref[h:h + 1, :]
        bits = pltpu.bitcast(sc, jnp.int32)
        key = jnp.where(bits < 0, bits ^ 0x7FFFFFFF, bits)
        if diagonal:
            kpos = lax.broadcasted_iota(jnp.int32, (kc, tq), 0)
            qpos = lax.broadcasted_iota(jnp.int32, (kc, tq), 1)
            key = jnp.where(kpos <= qpos, key, INT_MIN)
        key_s[j] = key
        half_s[j] = lax.shift_right_arithmetic(key, 16).astype(jnp.int16)
        low_s[j] = key.astype(jnp.int16) ^ jnp.int16(INT16_MIN)

    def sweep_chunks(n, group_fn):
        gsz = DSA_GROUP
        assert gsz == 4

        def trip(q, c):
            group_fn(q * gsz, gsz)
            return c

        lax.fori_loop(0, lax.shift_right_logical(n, 2), trip, 0)
        done = n & -gsz

        @pl.when((n & 2) != 0)
        def _():
            group_fn(done, 2)

        @pl.when((n & 1) != 0)
        def _():
            group_fn(done + (n & 2), 1)

    def score_group(j0, count):
        lgs = [score_logits(j0 + s) for s in range(count)]
        for s in range(count):
            score_keys(j0 + s, lgs[s], False)

    sweep_chunks(i, score_group)
    score_keys(i, score_logits(i), True)

    npairs = lax.shift_right_logical(nchunks + 1, 1)

    @pl.when((nchunks & 1) == 1)
    def _():
        half_s[nchunks] = jnp.full((kc, tq), INT16_MIN, jnp.int16)
        low_s[nchunks] = jnp.full((kc, tq), INT16_MIN, jnp.int16)

    qidx = i * tq + lax.broadcasted_iota(jnp.int32, (1, tq), 1)
    krow = jnp.minimum(topk, qidx + 1).astype(F32)

    def count(ind_fn):
        def body(j, acc):
            parts = [acc, None, None, None]
            for s in range(kc // sub):
                ind = ind_fn(key_s[j, s * sub:(s + 1) * sub, :], j * kc + s * sub)
                p = s % 4
                parts[p] = ind if parts[p] is None else parts[p] + ind
            return (parts[0] + parts[1]) + (parts[2] + parts[3])
        acc = lax.fori_loop(0, nchunks, body, jnp.zeros((sub, tq), F32))
        return jnp.sum(acc, axis=0, keepdims=True)

    def count_ge(cand):
        cb = jnp.broadcast_to(cand, (sub, tq))
        return count(lambda k, base: jnp.where(k >= cb, 1.0, 0.0))

    rows16 = V7X_BF16_ROWS

    def count16(cand):
        cb = jnp.broadcast_to(cand, (rows16, tq)).astype(jnp.int16)
        one, nil = jnp.int16(1), jnp.int16(0)

        def body(jj, acc):
            parts = [acc, None, None, None]
            for s in range(2 * kc // rows16):
                half, r = divmod(s, kc // rows16)
                ind = jnp.where(half_s[2 * jj + half, r * rows16:(r + 1) * rows16, :] >= cb, one, nil)
                p = s % 4
                parts[p] = ind if parts[p] is None else parts[p] + ind
            return (parts[0] + parts[1]) + (parts[2] + parts[3])

        acc = lax.fori_loop(0, npairs, body, jnp.zeros((rows16, tq), jnp.int16))
        return jnp.sum(acc.astype(F32), axis=0, keepdims=True)

    def search16(count_at_min):
        zero = jnp.zeros((1, tq), jnp.int32)
        c0 = count16(zero)
        start = (jnp.where(c0 >= krow, zero, INT16_MIN), jnp.where(c0 >= krow, c0, count_at_min))

        def bit_body(it, carry):
            th, cth = carry
            cand = th | lax.shift_left(jnp.int32(1), 14 - it)
            c = count16(cand)
            return jnp.where(c >= krow, cand, th), jnp.where(c >= krow, c, cth)

        return lax.fori_loop(0, 15, bit_body, start)

    t_hi, c_hi = search16(jnp.full((1, tq), 1.0, F32) * (nchunks * kc).astype(F32))
    th16 = jnp.broadcast_to(t_hi, (kc, tq)).astype(jnp.int16)

    def low_halves(jj, c):
        for j in (2 * jj, 2 * jj + 1):
            hi = half_s[j]
            half_s[j] = jnp.where(hi == th16, low_s[j],
                                  jnp.where(hi > th16, jnp.int16(INT16_MAX), jnp.int16(INT16_MIN)))
        return c

    lax.fori_loop(0, npairs, low_halves, 0)
    t_lo, c_ge = search16(c_hi)
    t = lax.shift_left(t_hi, 16) | (t_lo - INT16_MIN)

    saturated = jnp.max(jnp.where(t_lo == INT16_MAX, 1.0, 0.0)) > 0.5
    c_gt = lax.cond(saturated, lambda: count_ge(t + 1), lambda: count16(jnp.minimum(t_lo + 1, INT16_MAX)))
    need = krow - c_gt
    tb8 = jnp.broadcast_to(t, (sub, tq))

    @pl.when(jnp.max(c_ge - krow) > 0.5)
    def _():
        sub_iota = lax.broadcasted_iota(jnp.int32, (sub, tq), 0)

        def tied_below(u):
            ub = jnp.broadcast_to(u, (sub, tq))
            return count(lambda k, base: jnp.where(k == tb8, jnp.where(sub_iota + base < ub, 1.0, 0.0), 0.0))

        def ubit(it, u):
            cand = u | lax.shift_left(jnp.int32(1), idx_bits - 1 - it)
            return jnp.where(tied_below(cand) < need, cand, u)

        u = lax.fori_loop(0, idx_bits, ubit, jnp.zeros((1, tq), jnp.int32))
        ub_full = jnp.broadcast_to(u, (kc, tq))
        tb_full = jnp.broadcast_to(t, (kc, tq))
        kidx = lax.broadcasted_iota(jnp.int32, (kc, tq), 0)

        def drop(j, c):
            k = key_s[j]
            key_s[j] = jnp.where(k == tb_full, jnp.where(kidx + j * kc > ub_full, INT_MIN, k), k)
            return c

        lax.fori_loop(0, nchunks, drop, 0)

    def masked_scores(j, g, slot=0):
        koff = pl.multiple_of(j * kc, kc)
        qcat = jnp.concatenate([qt_ref[g * ATTN_REP + r] for r in range(ATTN_REP)], axis=1)
        bias = jnp.concatenate([bias_s[slot]] * ATTN_REP, axis=1)
        kg = kk_ref[g, pl.ds(koff, kc), :]
        vg = vt_ref[g, :, pl.ds(koff, kc)]
        return jnp.dot(kg, qcat, preferred_element_type=F32) + bias, vg

    def set_bias(j, slot=0):
        bias_s[slot] = jnp.where(key_s[j] >= jnp.broadcast_to(t, (kc, tq)), 0.0, NEG_BIG)

    acc_s[...] = jnp.zeros_like(acc_s)

    def att_group(j0, count):
        work = []
        for slot in range(count):
            set_bias(j0 + slot, slot)
            for g in range(ATTN_KV_HEADS):
                work.append((g, masked_scores(j0 + slot, g, slot)))
        for g, (s, vg) in work:
            acc_s[g] += jnp.dot(vg, jnp.exp(s).astype(BF16), preferred_element_type=F32)

    sweep_chunks(nchunks, att_group)

    norm = acc_s[:, ATTN_HEAD_DIM:ATTN_HEAD_DIM + 1, :]
    in_range = jnp.where(norm >= SOFTMAX_NORM_MIN, jnp.where(norm <= SOFTMAX_NORM_MAX, 1.0, 0.0), 0.0)

    @pl.when(jnp.min(in_range) < 0.5)
    def _():
        acc_s[...] = jnp.zeros_like(acc_s)

        def online_body(j, ms):
            set_bias(j)
            out = []
            for g in range(ATTN_KV_HEADS):
                s, vg = masked_scores(j, g)
                m_new = jnp.maximum(ms[g], jnp.max(s, axis=0, keepdims=True))
                p = jnp.exp(s - m_new).astype(BF16)
                pv = jnp.dot(vg, p, preferred_element_type=F32)
                acc_s[g] = jnp.exp(ms[g] - m_new) * acc_s[g] + pv
                out.append(m_new)
            return tuple(out)

        m_init = tuple(jnp.full((1, ATTN_REP * tq), NEG_BIG, F32) for _ in range(ATTN_KV_HEADS))
        lax.fori_loop(0, nchunks, online_body, m_init)

    for h in range(ATTN_HEADS):
        g, r = divmod(h, ATTN_REP)
        a = acc_s[g, :, r * tq:(r + 1) * tq]
        ot_s[h * ATTN_HEAD_DIM:(h + 1) * ATTN_HEAD_DIM, :] = (
            a[:ATTN_HEAD_DIM] / a[ATTN_HEAD_DIM:ATTN_HEAD_DIM + 1])
    o_ref[...] = ot_s[...].T.astype(o_ref.dtype)


def _dsa(qt, iqt, iwt, kk, vt, ik, batch, seq):
    n = batch * seq
    tq, kc = DSA_TQ, DSA_KC
    assert seq % tq == 0
    nq = seq // tq
    topk = min(TOPK_MAX, seq // 4)
    nch = seq // kc
    nout = ATTN_HEADS * ATTN_HEAD_DIM
    qmap = lambda b, i: (0, 0, b * nq + i)
    est = (nch * kc * tq * 8 + 3 * DSA_GROUP * ATTN_REP * kc * tq * 4 + ATTN_HEADS * (VT_ROWS + 8) * tq * 4 + nout * tq * 4
           + ATTN_KV_HEADS * seq * (V7X_LANES + VT_ROWS) * 2 + seq * V7X_LANES * 2
           + 2 * (2 * ATTN_HEADS * ATTN_HEAD_DIM * tq * 2 + 8 * tq * 4 + tq * nout * 2))
    return pl.pallas_call(
        functools.partial(_dsa_kernel, topk=topk, idx_bits=(seq - 1).bit_length()),
        out_shape=jax.ShapeDtypeStruct((n, nout), BF16),
        grid=(batch, nq),
        in_specs=[
            pl.BlockSpec((ATTN_HEADS, ATTN_HEAD_DIM, tq), qmap),
            pl.BlockSpec((IDX_HEADS, IDX_DIM, tq), qmap),
            pl.BlockSpec((IDX_HEADS, tq), lambda b, i: (0, b * nq + i)),
            pl.BlockSpec((ATTN_KV_HEADS, seq, ATTN_HEAD_DIM), lambda b, i: (0, b, 0),
                         pipeline_mode=pl.Buffered(1)),
            pl.BlockSpec((ATTN_KV_HEADS, VT_ROWS, seq), lambda b, i: (0, 0, b),
                         pipeline_mode=pl.Buffered(1)),
            pl.BlockSpec((seq, IDX_DIM), lambda b, i: (b, 0), pipeline_mode=pl.Buffered(1)),
        ],
        out_specs=pl.BlockSpec((tq, nout), lambda b, i: (b * nq + i, 0)),
        scratch_shapes=[
            pltpu.VMEM((nch, kc, tq), jnp.int32),
            pltpu.VMEM((nch, kc, tq), jnp.int16),
            pltpu.VMEM((nch, kc, tq), jnp.int16),
            pltpu.VMEM((DSA_GROUP, kc, tq), F32),
            pltpu.VMEM((ATTN_KV_HEADS, VT_ROWS, ATTN_REP * tq), F32),
            pltpu.VMEM((nout, tq), F32),
        ],
        compiler_params=pltpu.CompilerParams(
            dimension_semantics=("arbitrary", "arbitrary"), vmem_limit_bytes=_vmem_limit(est)),
    )(qt, iqt, iwt, kk, vt, ik)


def _gla_kernel(gqk_ref, gv_ref, la_ref, gg_ref, gn_ref, o_ref, st_s):
    @pl.when(pl.program_id(1) == 0)
    def _():
        st_s[...] = jnp.zeros_like(st_s)

    c = GLA_CHUNK
    nqk = GLA_HEADS * GLA_DK
    r_i = lax.broadcasted_iota(jnp.int32, (c, c), 0)
    c_i = lax.broadcasted_iota(jnp.int32, (c, c), 1)
    tri = r_i >= c_i
    tri_f = tri.astype(F32)
    gn = gn_ref[...]
    nchunk = GLA_ROWS // c
    heads = range(GLA_HEADS)
    hs = [slice(h * GLA_DK, (h + 1) * GLA_DK) for h in heads]
    vs = [slice(h * GLA_DV, (h + 1) * GLA_DV) for h in heads]

    local = []
    for ci in range(nchunk):
        rows = slice(ci * c, (ci + 1) * c)
        la = la_ref[rows, :]
        b = jnp.dot(tri_f, la, preferred_element_type=F32, precision=lax.Precision.HIGHEST)
        b_last = b[c - 1:c, :]
        q = gqk_ref[rows, :nqk]
        k = gqk_ref[rows, nqk:]
        q_dec = (q * jnp.exp(b)).astype(BF16)
        k_in = (k * jnp.exp(-b)).astype(BF16)
        k_out = (k * jnp.exp(b_last - b)).astype(BF16)
        decay = jnp.exp(b_last)
        intra, upd = [], []
        for h in heads:
            v = gv_ref[rows, vs[h]]
            a = lax.dot_general(q_dec[:, hs[h]], k_in[:, hs[h]], NT_DIMS, preferred_element_type=F32)
            a = jnp.where(tri, a, 0.0).astype(BF16)
            intra.append(jnp.dot(a, v, preferred_element_type=F32))
            upd.append(lax.dot_general(v, k_out[:, hs[h]], TN_DIMS, preferred_element_type=F32))
        local.append((rows, q_dec, decay, intra, upd))

    st = [st_s[h] for h in heads]
    for rows, q_dec, decay, intra, upd in local:
        for h in heads:
            o = intra[h] + lax.dot_general(q_dec[:, hs[h]], st[h].astype(BF16), NT_DIMS,
                                           preferred_element_type=F32)
            st[h] = st[h] * decay[:, hs[h]] + upd[h]
            gate = gg_ref[rows, vs[h]]
            o_ref[rows, vs[h]] = (_rms(o, gn) * (gate * jax.nn.sigmoid(gate))).astype(o_ref.dtype)
    for h in heads:
        st_s[h] = st[h]


def _gla(gqk, gv, la, gg, g_norm, batch, seq):
    n = batch * seq
    t = GLA_ROWS
    assert seq % t == 0
    ns = seq // t
    row = lambda b, i: (b * ns + i, 0)
    nqk = GLA_HEADS * GLA_DK
    nv = GLA_HEADS * GLA_DV
    return pl.pallas_call(
        _gla_kernel,
        out_shape=jax.ShapeDtypeStruct((n, nv), BF16),
        grid=(batch, ns),
        in_specs=[
            pl.BlockSpec((t, 2 * nqk), row),
            pl.BlockSpec((t, nv), row),
            pl.BlockSpec((t, nqk), row),
            pl.BlockSpec((t, nv), row),
            _const_spec((1, GLA_DV)),
        ],
        out_specs=pl.BlockSpec((t, nv), row),
        scratch_shapes=[pltpu.VMEM((GLA_HEADS, GLA_DV, GLA_DK), F32)],
        compiler_params=pltpu.CompilerParams(dimension_semantics=("arbitrary", "arbitrary")),
    )(gqk, gv, la, gg, g_norm.reshape(1, GLA_DV))


def kernel(x, g_ffn1_pre, w_ffn1_gate, w_ffn1_up, w_ffn1_down, g_ffn1_post, g_mix_pre, w_in, w_gla_a2,
           b_gla_a, g_gla_norm, w_out, g_mix_post, g_ffn2_pre, w_ffn2_gate, w_ffn2_up, w_ffn2_down,
           g_ffn2_post):
    batch, seq, d = x.shape
    h = x.reshape(batch * seq, d)
    for l in range(g_ffn1_pre.shape[0]):
        h = _ffn(h, g_ffn1_pre[l], w_ffn1_gate[l], w_ffn1_up[l], w_ffn1_down[l], g_ffn1_post[l])
        qt, iqt, vt, iwt, kk, ik, gqk, gv, gg, la = _inproj(h, g_mix_pre[l], w_in[l], w_gla_a2[l], b_gla_a[l])
        oa = _dsa(qt, iqt, iwt, kk, vt, ik, batch, seq)
        og = _gla(gqk, gv, la, gg, g_gla_norm[l], batch, seq)
        h = _ffn(h, g_ffn2_pre[l], w_ffn2_gate[l], w_ffn2_up[l], w_ffn2_down[l], g_ffn2_post[l],
                 mix=(oa, og, w_out[l], g_mix_post[l]))
    return h.reshape(batch, seq, d)
```

```python
import functools

import jax
import jax.numpy as jnp
from jax import lax
from jax.experimental import pallas as pl
from jax.experimental.pallas import tpu as pltpu

ATTN_HEADS = 8
ATTN_KV_HEADS = 2
ATTN_HEAD_DIM = 64
ATTN_REP = ATTN_HEADS // ATTN_KV_HEADS
IDX_HEADS = 8
IDX_DIM = 64
TOPK_MAX = 256
GLA_HEADS = 4
GLA_DK = 64
GLA_DV = 128
GLA_GATE_RANK = 16
GLA_TAU = 16.0
GLA_CHUNK = 64
EPS = 1e-6

V7X_LANES = 128
V7X_SUBLANES = 8
V7X_BF16_ROWS = 16
V7X_MXU_DIM = 256
V7X_VMEM_BYTES = 64 * 2**20

FFN_ROWS = 512
FF_CHUNK = V7X_MXU_DIM
DSA_TQ = 256
DSA_KC = DSA_TQ
DSA_SCORE_GROUP = 4
DSA_ATT_GROUP = 4
DSA_ATT_LEAD = 3
GLA_ROWS = 512
VT_ROWS = ATTN_HEAD_DIM + V7X_BF16_ROWS

INT_MIN = -2**31
INT16_MIN, INT16_MAX = -2**15, 2**15 - 1
NEG_BIG = -1e30
SOFTMAX_NORM_MIN = 2.0 ** -60
SOFTMAX_NORM_MAX = 2.0 ** 100

F32 = jnp.float32
BF16 = jnp.bfloat16
NT_DIMS = (((1,), (1,)), ((), ()))
TN_DIMS = (((0,), (0,)), ((), ()))


def _vmem_limit(nbytes):
    return int(min(nbytes * 1.25 + (8 << 20), V7X_VMEM_BYTES - (6 << 20)))


def _rms(x, g):
    return x * lax.rsqrt(jnp.mean(x * x, axis=-1, keepdims=True) + EPS) * g


def _const_spec(shape):
    nd = len(shape)
    return pl.BlockSpec(shape, lambda *_: (0,) * nd, pipeline_mode=pl.Buffered(1))


def _ffn_kernel(x_ref, gpre_ref, wg_ref, wu_ref, wd_ref, gpost_ref, o_ref):
    o_ref[...] = _ffn_residual(x_ref[...], gpre_ref, wg_ref, wu_ref, wd_ref, gpost_ref)


def _mix_ffn_kernel(x_ref, oa_ref, og_ref, wo_ref, gmix_ref, gpre_ref, wg_ref, wu_ref, wd_ref, gpost_ref, o_ref):
    m = jnp.dot(oa_ref[...], wo_ref[0], preferred_element_type=F32)
    m = m + jnp.dot(og_ref[...], wo_ref[1], preferred_element_type=F32)
    x2 = x_ref[...] + _rms(m, gmix_ref[...])
    o_ref[...] = _ffn_residual(x2, gpre_ref, wg_ref, wu_ref, wd_ref, gpost_ref)


def _ffn_residual(x, gpre_ref, wg_ref, wu_ref, wd_ref, gpost_ref):
    xn = _rms(x, gpre_ref[...]).astype(BF16)

    def cols(c):
        return slice(c * FF_CHUNK, (c + 1) * FF_CHUNK)

    def gate_up(c):
        return (jnp.dot(xn, wg_ref[:, cols(c)], preferred_element_type=F32),
                jnp.dot(xn, wu_ref[:, cols(c)], preferred_element_type=F32))

    nch = wg_ref.shape[1] // FF_CHUNK
    acc = None
    nxt = gate_up(0)
    for c in range(nch):
        g, u = nxt
        if c + 1 < nch:
            nxt = gate_up(c + 1)
        a = (g * jax.nn.sigmoid(g) * u).astype(BF16)
        d = jnp.dot(a, wd_ref[cols(c), :], preferred_element_type=F32)
        acc = d if acc is None else acc + d
    return x + 0.5 * _rms(acc, gpost_ref[...])


def _ffn_operands(d, g_pre, w_gate, w_up, w_down, g_post):
    dff = w_gate.shape[1]
    assert dff % FF_CHUNK == 0
    wg, wu, wd = w_gate.astype(BF16), w_up.astype(BF16), w_down.astype(BF16)
    specs = [_const_spec((1, d)), _const_spec(wg.shape), _const_spec(wu.shape), _const_spec(wd.shape),
             _const_spec((1, d))]
    args = (g_pre.reshape(1, d), wg, wu, wd, g_post.reshape(1, d))
    est = 4 * FFN_ROWS * d * 4 + 3 * d * dff * 2 + FFN_ROWS * d * 8 + 4 * FFN_ROWS * FF_CHUNK * 4
    return specs, args, est


def _ffn(x, g_pre, w_gate, w_up, w_down, g_post):
    n, d = x.shape
    assert n % FFN_ROWS == 0
    ffn_specs, ffn_args, est = _ffn_operands(d, g_pre, w_gate, w_up, w_down, g_post)
    x_spec = pl.BlockSpec((FFN_ROWS, d), lambda i: (i, 0))
    return pl.pallas_call(
        _ffn_kernel,
        out_shape=jax.ShapeDtypeStruct((n, d), F32),
        grid=(n // FFN_ROWS,),
        in_specs=[x_spec] + ffn_specs,
        out_specs=x_spec,
        compiler_params=pltpu.CompilerParams(
            dimension_semantics=("arbitrary",), vmem_limit_bytes=_vmem_limit(est)),
    )(x, *ffn_args)


def _mix_ffn(x, oa, og, w_out, g_mix, g_pre, w_gate, w_up, w_down, g_post):
    n, d = x.shape
    assert n % FFN_ROWS == 0
    ffn_specs, ffn_args, est = _ffn_operands(d, g_pre, w_gate, w_up, w_down, g_post)
    half = oa.shape[1]
    assert og.shape[1] == half and w_out.shape == (2 * half, d)
    wo = w_out.reshape(2, half, d).astype(BF16)
    row = lambda i: (i, 0)
    x_spec = pl.BlockSpec((FFN_ROWS, d), row)
    o_spec = pl.BlockSpec((FFN_ROWS, half), row)
    est += 4 * FFN_ROWS * half * 2 + 2 * half * d * 2 + FFN_ROWS * d * 4
    return pl.pallas_call(
        _mix_ffn_kernel,
        out_shape=jax.ShapeDtypeStruct((n, d), F32),
        grid=(n // FFN_ROWS,),
        in_specs=[x_spec, o_spec, o_spec, _const_spec(wo.shape), _const_spec((1, d))] + ffn_specs,
        out_specs=x_spec,
        compiler_params=pltpu.CompilerParams(
            dimension_semantics=("arbitrary",), vmem_limit_bytes=_vmem_limit(est)),
    )(x, oa, og, wo, g_mix.reshape(1, d), *ffn_args)


_K_OFF, _MISC_OFF, _GQ_OFF, _GK_OFF, _GV_OFF, _GG_OFF, _TOK_COLS = 0, 128, 256, 512, 768, 1280, 1792
_GA_LANE = 72
_QT_OFF, _IQT_OFF, _VT_OFF, _IWT_OFF, _T_ROWS = 0, 512, 1024, 1152, 1168


def _inproj_kernel(x_ref, g_ref, wtok_ref, wt_ref, wa2_ref, ba_ref,
                   qt_ref, iqt_ref, vt_ref, iwt_ref, kk_ref, ik_ref, gqk_ref, gv_ref, gg_ref, la_ref):
    h = _rms(x_ref[...], g_ref[...]).astype(BF16)
    t = h.shape[0]
    pt = lax.dot_general(wt_ref[...], h, NT_DIMS, preferred_element_type=F32)
    for i in range(ATTN_HEADS):
        qt_ref[i] = pt[_QT_OFF + 64 * i:_QT_OFF + 64 * (i + 1)].astype(BF16)
    for i in range(IDX_HEADS):
        iqt_ref[i] = pt[_IQT_OFF + 64 * i:_IQT_OFF + 64 * (i + 1)].astype(BF16)
    ones_row = (lax.broadcasted_iota(jnp.int32, (V7X_BF16_ROWS, t), 0) == 0).astype(F32)
    for g in range(ATTN_KV_HEADS):
        v_t = pt[_VT_OFF + 64 * g:_VT_OFF + 64 * (g + 1)]
        vt_ref[g] = jnp.concatenate([v_t, ones_row], axis=0).astype(BF16)
    iwt_ref[...] = pt[_IWT_OFF:_IWT_OFF + IDX_HEADS] * (IDX_HEADS ** -0.5)

    proj = jnp.dot(h, wtok_ref[...], preferred_element_type=F32)
    for g in range(ATTN_KV_HEADS):
        kk_ref[g] = proj[:, _K_OFF + 64 * g:_K_OFF + 64 * (g + 1)].astype(BF16)
    misc = proj[:, _MISC_OFF:_MISC_OFF + V7X_LANES]
    ik_ref[...] = misc[:, :IDX_DIM].astype(BF16)
    gqk_ref[...] = proj[:, _GQ_OFF:_GV_OFF]
    gv_ref[...] = proj[:, _GV_OFF:_GG_OFF].astype(BF16)
    gg_ref[...] = proj[:, _GG_OFF:_GG_OFF + GLA_HEADS * GLA_DV]
    z = jnp.dot(misc, wa2_ref[...], preferred_element_type=F32, precision=lax.Precision.HIGHEST)
    z = z + ba_ref[...]
    log_sig = jnp.minimum(z, 0.0) - jnp.log1p(jnp.exp(-jnp.abs(z)))
    la_ref[...] = log_sig * (1.0 / GLA_TAU)


def _pack_w_in(w_in):
    d = w_in.shape[0]
    sizes = (512, 128, 128, 512, 64, 8, 256, 256, 512, 16, 512)
    offs = [0]
    for s in sizes:
        offs.append(offs[-1] + s)
    aq, ak, av, iq, ik, iw, gq, gk, gv, ga, gg = [w_in[:, offs[i]:offs[i + 1]] for i in range(len(sizes))]
    z = lambda n: jnp.zeros((d, n), w_in.dtype)
    w_tok = jnp.concatenate([ak, ik, z(8), ga, z(40), gq * (GLA_DK ** -0.5), gk, gv, gg], axis=1)
    w_t = jnp.concatenate([aq * (ATTN_HEAD_DIM ** -0.5), iq * (IDX_DIM ** -0.5), av, iw, z(8)], axis=1).T
    assert w_tok.shape[1] == _TOK_COLS and w_t.shape[0] == _T_ROWS
    return w_tok.astype(BF16), w_t.astype(BF16)


def _inproj(x1, g_mix_pre, w_in, w_gla_a2, b_gla_a):
    n, d = x1.shape
    t = FFN_ROWS
    assert n % t == 0
    w_tok, w_t = _pack_w_in(w_in)
    nqk = GLA_HEADS * GLA_DK
    nv = GLA_HEADS * GLA_DV
    wa2 = jnp.zeros((V7X_LANES, nqk), F32).at[_GA_LANE:_GA_LANE + GLA_GATE_RANK].set(w_gla_a2)
    row = lambda i: (i, 0)
    row3 = lambda i: (0, i, 0)
    col3 = lambda i: (0, 0, i)
    out_shape = (
        jax.ShapeDtypeStruct((ATTN_HEADS, ATTN_HEAD_DIM, n), BF16),
        jax.ShapeDtypeStruct((IDX_HEADS, IDX_DIM, n), BF16),
        jax.ShapeDtypeStruct((ATTN_KV_HEADS, VT_ROWS, n), BF16),
        jax.ShapeDtypeStruct((IDX_HEADS, n), F32),
        jax.ShapeDtypeStruct((ATTN_KV_HEADS, n, ATTN_HEAD_DIM), BF16),
        jax.ShapeDtypeStruct((n, IDX_DIM), BF16),
        jax.ShapeDtypeStruct((n, 2 * nqk), F32),
        jax.ShapeDtypeStruct((n, nv), BF16),
        jax.ShapeDtypeStruct((n, nv), F32),
        jax.ShapeDtypeStruct((n, nqk), F32),
    )
    out_specs = (
        pl.BlockSpec((ATTN_HEADS, ATTN_HEAD_DIM, t), col3),
        pl.BlockSpec((IDX_HEADS, IDX_DIM, t), col3),
        pl.BlockSpec((ATTN_KV_HEADS, VT_ROWS, t), col3),
        pl.BlockSpec((IDX_HEADS, t), lambda i: (0, i)),
        pl.BlockSpec((ATTN_KV_HEADS, t, ATTN_HEAD_DIM), row3),
        pl.BlockSpec((t, IDX_DIM), row),
        pl.BlockSpec((t, 2 * nqk), row),
        pl.BlockSpec((t, nv), row),
        pl.BlockSpec((t, nv), row),
        pl.BlockSpec((t, nqk), row),
    )
    est = (2 * t * d * 4 + d * (_TOK_COLS + _T_ROWS) * 2 + t * (_TOK_COLS + _T_ROWS) * 4 * 2
           + 2 * t * 8 * 1024)
    return pl.pallas_call(
        _inproj_kernel,
        out_shape=out_shape,
        grid=(n // t,),
        in_specs=[
            pl.BlockSpec((t, d), row),
            _const_spec((1, d)),
            _const_spec(w_tok.shape),
            _const_spec(w_t.shape),
            _const_spec(wa2.shape),
            _const_spec((1, nqk)),
        ],
        out_specs=out_specs,
        compiler_params=pltpu.CompilerParams(
            dimension_semantics=("arbitrary",), vmem_limit_bytes=_vmem_limit(est)),
    )(x1, g_mix_pre.reshape(1, d), w_tok, w_t, wa2, b_gla_a.reshape(1, nqk))


def _dsa_kernel(qt_ref, iqt_ref, iwt_ref, kk_ref, vt_ref, ik_ref, o_ref,
                key_s, half_s, low_s, bias_s, acc_s, ot_s, *, topk, idx_bits):
    tq = DSA_TQ
    kc = DSA_KC
    sub = V7X_SUBLANES
    i = pl.program_id(1)
    nchunks = i + 1

    def score_logits(j):
        ikc = ik_ref[pl.ds(pl.multiple_of(j * kc, kc), kc), :]
        iqcat = jnp.concatenate([iqt_ref[h] for h in range(IDX_HEADS)], axis=1)
        return jnp.dot(ikc, iqcat, preferred_element_type=F32)

    def score_keys(j, lg, diagonal):
        sc = jnp.zeros((kc, tq), F32)
        for h in range(IDX_HEADS):
            sc = sc + jnp.maximum(lg[:, h * tq:(h + 1) * tq], 0.0) * iwt_ref[h:h + 1, :]
        bits = pltpu.bitcast(sc, jnp.int32)
        key = jnp.where(bits < 0, bits ^ 0x7FFFFFFF, bits)
        if diagonal:
            kpos = lax.broadcasted_iota(jnp.int32, (kc, tq), 0)
            qpos = lax.broadcasted_iota(jnp.int32, (kc, tq), 1)
            key = jnp.where(kpos <= qpos, key, INT_MIN)
        key_s[j] = key
        half_s[j] = lax.shift_right_arithmetic(key, 16).astype(jnp.int16)
        low_s[j] = key.astype(jnp.int16) ^ jnp.int16(INT16_MIN)

    def sweep_chunks(n, group_fn, gsz):
        assert gsz & (gsz - 1) == 0

        def trip(q, c):
            group_fn(q * gsz, gsz)
            return c

        lax.fori_loop(0, lax.shift_right_logical(n, gsz.bit_length() - 1), trip, 0)
        done = n & -gsz
        part = gsz // 2
        while part:
            pl.when((n & part) != 0)(functools.partial(group_fn, done, part))
            done = done + (n & part)
            part //= 2

    def score_group(j0, count):
        lgs = [score_logits(j0 + s) for s in range(count)]
        for s in range(count):
            score_keys(j0 + s, lgs[s], False)

    sweep_chunks(i, score_group, DSA_SCORE_GROUP)
    score_keys(i, score_logits(i), True)

    npairs = lax.shift_right_logical(nchunks + 1, 1)

    @pl.when((nchunks & 1) == 1)
    def _():
        half_s[nchunks] = jnp.full((kc, tq), INT16_MIN, jnp.int16)
        low_s[nchunks] = jnp.full((kc, tq), INT16_MIN, jnp.int16)

    qidx = i * tq + lax.broadcasted_iota(jnp.int32, (1, tq), 1)
    krow = jnp.minimum(topk, qidx + 1).astype(F32)

    def count(ind_fn):
        def body(j, acc):
            parts = [acc, None, None, None]
            for s in range(kc // sub):
                ind = ind_fn(key_s[j, s * sub:(s + 1) * sub, :], j * kc + s * sub)
                p = s % 4
                parts[p] = ind if parts[p] is None else parts[p] + ind
            return (parts[0] + parts[1]) + (parts[2] + parts[3])
        acc = lax.fori_loop(0, nchunks, body, jnp.zeros((sub, tq), F32))
        return jnp.sum(acc, axis=0, keepdims=True)

    def count_ge(cand):
        cb = jnp.broadcast_to(cand, (sub, tq))
        return count(lambda k, base: jnp.where(k >= cb, 1.0, 0.0))

    rows16 = V7X_BF16_ROWS

    def count16(cand):
        cb = jnp.broadcast_to(cand, (rows16, tq)).astype(jnp.int16)
        one, nil = jnp.int16(1), jnp.int16(0)

        def body(jj, acc):
            parts = [acc, None, None, None]
            for s in range(2 * kc // rows16):
                half, r = divmod(s, kc // rows16)
                ind = jnp.where(half_s[2 * jj + half, r * rows16:(r + 1) * rows16, :] >= cb, one, nil)
                p = s % 4
                parts[p] = ind if parts[p] is None else parts[p] + ind
            return (parts[0] + parts[1]) + (parts[2] + parts[3])

        acc = lax.fori_loop(0, npairs, body, jnp.zeros((rows16, tq), jnp.int16))
        return jnp.sum(acc.astype(F32), axis=0, keepdims=True)

    def search16(count_at_min):
        zero = jnp.zeros((1, tq), jnp.int32)
        c0 = count16(zero)
        start = (jnp.where(c0 >= krow, zero, INT16_MIN), jnp.where(c0 >= krow, c0, count_at_min))

        def bit_body(it, carry):
            th, cth = carry
            cand = th | lax.shift_left(jnp.int32(1), 14 - it)
            c = count16(cand)
            return jnp.where(c >= krow, cand, th), jnp.where(c >= krow, c, cth)

        return lax.fori_loop(0, 15, bit_body, start)

    t_hi, c_hi = search16(jnp.full((1, tq), 1.0, F32) * (nchunks * kc).astype(F32))
    th16 = jnp.broadcast_to(t_hi, (kc, tq)).astype(jnp.int16)

    def low_halves(jj, c):
        for j in (2 * jj, 2 * jj + 1):
            hi = half_s[j]
            half_s[j] = jnp.where(hi == th16, low_s[j],
                                  jnp.where(hi > th16, jnp.int16(INT16_MAX), jnp.int16(INT16_MIN)))
        return c

    lax.fori_loop(0, npairs, low_halves, 0)
    t_lo, c_ge = search16(c_hi)
    t = lax.shift_left(t_hi, 16) | (t_lo - INT16_MIN)

    saturated = jnp.max(jnp.where(t_lo == INT16_MAX, 1.0, 0.0)) > 0.5
    c_gt = lax.cond(saturated, lambda: count_ge(t + 1), lambda: count16(jnp.minimum(t_lo + 1, INT16_MAX)))
    need = krow - c_gt
    tb8 = jnp.broadcast_to(t, (sub, tq))

    @pl.when(jnp.max(c_ge - krow) > 0.5)
    def _():
        sub_iota = lax.broadcasted_iota(jnp.int32, (sub, tq), 0)

        def tied_below(u):
            ub = jnp.broadcast_to(u, (sub, tq))
            return count(lambda k, base: jnp.where(k == tb8, jnp.where(sub_iota + base < ub, 1.0, 0.0), 0.0))

        def ubit(it, u):
            cand = u | lax.shift_left(jnp.int32(1), idx_bits - 1 - it)
            return jnp.where(tied_below(cand) < need, cand, u)

        u = lax.fori_loop(0, idx_bits, ubit, jnp.zeros((1, tq), jnp.int32))
        ub_full = jnp.broadcast_to(u, (kc, tq))
        tb_full = jnp.broadcast_to(t, (kc, tq))
        kidx = lax.broadcasted_iota(jnp.int32, (kc, tq), 0)

        def drop(j, c):
            k = key_s[j]
            key_s[j] = jnp.where(k == tb_full, jnp.where(kidx + j * kc > ub_full, INT_MIN, k), k)
            return c

        lax.fori_loop(0, nchunks, drop, 0)

    def masked_scores(j, g, slot=0):
        koff = pl.multiple_of(j * kc, kc)
        qcat = jnp.concatenate([qt_ref[g * ATTN_REP + r] for r in range(ATTN_REP)], axis=1)
        bias = jnp.concatenate([bias_s[slot]] * ATTN_REP, axis=1)
        kg = kk_ref[g, pl.ds(koff, kc), :]
        vg = vt_ref[g, :, pl.ds(koff, kc)]
        return jnp.dot(kg, qcat, preferred_element_type=F32) + bias, vg

    def set_bias(j, slot=0):
        bias_s[slot] = jnp.where(key_s[j] >= jnp.broadcast_to(t, (kc, tq)), 0.0, NEG_BIG)

    acc_s[...] = jnp.zeros_like(acc_s)

    def att_group(j0, count):
        units = [(slot, g) for slot in range(count) for g in range(ATTN_KV_HEADS)]
        for slot in range(count):
            set_bias(j0 + slot, slot)
        scores = {}

        def issue(u):
            slot, g = units[u]
            scores[u] = masked_scores(j0 + slot, g, slot)

        for u in range(min(DSA_ATT_LEAD, len(units))):
            issue(u)
        for u, (slot, g) in enumerate(units):
            s, vg = scores.pop(u)
            acc_s[g] += jnp.dot(vg, jnp.exp(s).astype(BF16), preferred_element_type=F32)
            if u + DSA_ATT_LEAD < len(units):
                issue(u + DSA_ATT_LEAD)

    sweep_chunks(nchunks, att_group, DSA_ATT_GROUP)

    norm = acc_s[:, ATTN_HEAD_DIM:ATTN_HEAD_DIM + 1, :]
    in_range = jnp.where(norm >= SOFTMAX_NORM_MIN, jnp.where(norm <= SOFTMAX_NORM_MAX, 1.0, 0.0), 0.0)

    @pl.when(jnp.min(in_range) < 0.5)
    def _():
        acc_s[...] = jnp.zeros_like(acc_s)

        def online_body(j, ms):
            set_bias(j)
            out = []
            for g in range(ATTN_KV_HEADS):
                s, vg = masked_scores(j, g)
                m_new = jnp.maximum(ms[g], jnp.max(s, axis=0, keepdims=True))
                p = jnp.exp(s - m_new).astype(BF16)
                pv = jnp.dot(vg, p, preferred_element_type=F32)
                acc_s[g] = jnp.exp(ms[g] - m_new) * acc_s[g] + pv
                out.append(m_new)
            return tuple(out)

        m_init = tuple(jnp.full((1, ATTN_REP * tq), NEG_BIG, F32) for _ in range(ATTN_KV_HEADS))
        lax.fori_loop(0, nchunks, online_body, m_init)

    for h in range(ATTN_HEADS):
        g, r = divmod(h, ATTN_REP)
        a = acc_s[g, :, r * tq:(r + 1) * tq]
        ot_s[h * ATTN_HEAD_DIM:(h + 1) * ATTN_HEAD_DIM, :] = (
            a[:ATTN_HEAD_DIM] / a[ATTN_HEAD_DIM:ATTN_HEAD_DIM + 1])
    o_ref[...] = ot_s[...].T.astype(o_ref.dtype)


def _dsa(qt, iqt, iwt, kk, vt, ik, batch, seq):
    n = batch * seq
    tq, kc = DSA_TQ, DSA_KC
    assert seq % tq == 0
    nq = seq // tq
    topk = min(TOPK_MAX, seq // 4)
    nch = seq // kc
    nout = ATTN_HEADS * ATTN_HEAD_DIM
    qmap = lambda b, i: (0, 0, b * nq + i)
    est = (nch * kc * tq * 8 + 3 * DSA_ATT_GROUP * ATTN_REP * kc * tq * 4 + ATTN_HEADS * (VT_ROWS + 8) * tq * 4 + nout * tq * 4
           + ATTN_KV_HEADS * seq * (V7X_LANES + VT_ROWS) * 2 + seq * V7X_LANES * 2
           + 2 * (2 * ATTN_HEADS * ATTN_HEAD_DIM * tq * 2 + 8 * tq * 4 + tq * nout * 2))
    return pl.pallas_call(
        functools.partial(_dsa_kernel, topk=topk, idx_bits=(seq - 1).bit_length()),
        out_shape=jax.ShapeDtypeStruct((n, nout), BF16),
        grid=(batch, nq),
        in_specs=[
            pl.BlockSpec((ATTN_HEADS, ATTN_HEAD_DIM, tq), qmap),
            pl.BlockSpec((IDX_HEADS, IDX_DIM, tq), qmap),
            pl.BlockSpec((IDX_HEADS, tq), lambda b, i: (0, b * nq + i)),
            pl.BlockSpec((ATTN_KV_HEADS, seq, ATTN_HEAD_DIM), lambda b, i: (0, b, 0),
                         pipeline_mode=pl.Buffered(1)),
            pl.BlockSpec((ATTN_KV_HEADS, VT_ROWS, seq), lambda b, i: (0, 0, b),
                         pipeline_mode=pl.Buffered(1)),
            pl.BlockSpec((seq, IDX_DIM), lambda b, i: (b, 0), pipeline_mode=pl.Buffered(1)),
        ],
        out_specs=pl.BlockSpec((tq, nout), lambda b, i: (b * nq + i, 0)),
        scratch_shapes=[
            pltpu.VMEM((nch, kc, tq), jnp.int32),
            pltpu.VMEM((nch, kc, tq), jnp.int16),
            pltpu.VMEM((nch, kc, tq), jnp.int16),
            pltpu.VMEM((DSA_ATT_GROUP, kc, tq), F32),
            pltpu.VMEM((ATTN_KV_HEADS, VT_ROWS, ATTN_REP * tq), F32),
            pltpu.VMEM((nout, tq), F32),
        ],
        compiler_params=pltpu.CompilerParams(
            dimension_semantics=("arbitrary", "arbitrary"), vmem_limit_bytes=_vmem_limit(est)),
    )(qt, iqt, iwt, kk, vt, ik)


def _gla_kernel(gqk_ref, gv_ref, la_ref, gg_ref, gn_ref, o_ref, st_s):
    @pl.when(pl.program_id(1) == 0)
    def _():
        st_s[...] = jnp.zeros_like(st_s)

    c = GLA_CHUNK
    nqk = GLA_HEADS * GLA_DK
    r_i = lax.broadcasted_iota(jnp.int32, (c, c), 0)
    c_i = lax.broadcasted_iota(jnp.int32, (c, c), 1)
    tri = r_i >= c_i
    tri_f = tri.astype(F32)
    gn = gn_ref[...]
    nchunk = GLA_ROWS // c
    heads = range(GLA_HEADS)
    hs = [slice(h * GLA_DK, (h + 1) * GLA_DK) for h in heads]
    vs = [slice(h * GLA_DV, (h + 1) * GLA_DV) for h in heads]

    local = []
    for ci in range(nchunk):
        rows = slice(ci * c, (ci + 1) * c)
        la = la_ref[rows, :]
        b = jnp.dot(tri_f, la, preferred_element_type=F32, precision=lax.Precision.HIGHEST)
        b_last = b[c - 1:c, :]
        q = gqk_ref[rows, :nqk]
        k = gqk_ref[rows, nqk:]
        q_dec = (q * jnp.exp(b)).astype(BF16)
        k_in = (k * jnp.exp(-b)).astype(BF16)
        k_out = (k * jnp.exp(b_last - b)).astype(BF16)
        decay = jnp.exp(b_last)
        intra, upd = [], []
        for h in heads:
            v = gv_ref[rows, vs[h]]
            a = lax.dot_general(q_dec[:, hs[h]], k_in[:, hs[h]], NT_DIMS, preferred_element_type=F32)
            a = jnp.where(tri, a, 0.0).astype(BF16)
            intra.append(jnp.dot(a, v, preferred_element_type=F32))
            upd.append(lax.dot_general(v, k_out[:, hs[h]], TN_DIMS, preferred_element_type=F32))
        local.append((rows, q_dec, decay, intra, upd))

    st = [st_s[h] for h in heads]
    for rows, q_dec, decay, intra, upd in local:
        for h in heads:
            o = intra[h] + lax.dot_general(q_dec[:, hs[h]], st[h].astype(BF16), NT_DIMS,
                                           preferred_element_type=F32)
            st[h] = st[h] * decay[:, hs[h]] + upd[h]
            gate = gg_ref[rows, vs[h]]
            o_ref[rows, vs[h]] = (_rms(o, gn) * (gate * jax.nn.sigmoid(gate))).astype(o_ref.dtype)
    for h in heads:
        st_s[h] = st[h]


def _gla(gqk, gv, la, gg, g_norm, batch, seq):
    n = batch * seq
    t = GLA_ROWS
    assert seq % t == 0
    ns = seq // t
    row = lambda b, i: (b * ns + i, 0)
    nqk = GLA_HEADS * GLA_DK
    nv = GLA_HEADS * GLA_DV
    return pl.pallas_call(
        _gla_kernel,
        out_shape=jax.ShapeDtypeStruct((n, nv), BF16),
        grid=(batch, ns),
        in_specs=[
            pl.BlockSpec((t, 2 * nqk), row),
            pl.BlockSpec((t, nv), row),
            pl.BlockSpec((t, nqk), row),
            pl.BlockSpec((t, nv), row),
            _const_spec((1, GLA_DV)),
        ],
        out_specs=pl.BlockSpec((t, nv), row),
        scratch_shapes=[pltpu.VMEM((GLA_HEADS, GLA_DV, GLA_DK), F32)],
        compiler_params=pltpu.CompilerParams(dimension_semantics=("arbitrary", "arbitrary")),
    )(gqk, gv, la, gg, g_norm.reshape(1, GLA_DV))


def kernel(x, g_ffn1_pre, w_ffn1_gate, w_ffn1_up, w_ffn1_down, g_ffn1_post, g_mix_pre, w_in, w_gla_a2,
           b_gla_a, g_gla_norm, w_out, g_mix_post, g_ffn2_pre, w_ffn2_gate, w_ffn2_up, w_ffn2_down,
           g_ffn2_post):
    batch, seq, d = x.shape
    h = x.reshape(batch * seq, d)
    for l in range(g_ffn1_pre.shape[0]):
        h = _ffn(h, g_ffn1_pre[l], w_ffn1_gate[l], w_ffn1_up[l], w_ffn1_down[l], g_ffn1_post[l])
        qt, iqt, vt, iwt, kk, ik, gqk, gv, gg, la = _inproj(h, g_mix_pre[l], w_in[l], w_gla_a2[l], b_gla_a[l])
        oa = _dsa(qt, iqt, iwt, kk, vt, ik, batch, seq)
        og = _gla(gqk, gv, la, gg, g_gla_norm[l], batch, seq)
        h = _mix_ffn(h, oa, og, w_out[l], g_mix_post[l],
                     g_ffn2_pre[l], w_ffn2_gate[l], w_ffn2_up[l], w_ffn2_down[l], g_ffn2_post[l])
    return h.reshape(batch, seq, d)
```

```python
import functools

import jax
import jax.numpy as jnp
from jax import lax
from jax.experimental import pallas as pl
from jax.experimental.pallas import tpu as pltpu

ATTN_HEADS = 8
ATTN_KV_HEADS = 2
ATTN_HEAD_DIM = 64
ATTN_REP = ATTN_HEADS // ATTN_KV_HEADS
IDX_HEADS = 8
IDX_DIM = 64
TOPK_MAX = 256
GLA_HEADS = 4
GLA_DK = 64
GLA_DV = 128
GLA_GATE_RANK = 16
GLA_TAU = 16.0
GLA_CHUNK = 64
EPS = 1e-6

V7X_LANES = 128
V7X_SUBLANES = 8
V7X_BF16_ROWS = 16
V7X_MXU_DIM = 256
V7X_VMEM_BYTES = 64 * 2**20

FFN_ROWS = 512
FF_CHUNK = V7X_MXU_DIM
DSA_TQ = 256
DSA_KC = DSA_TQ
DSA_SCORE_GROUP = 4
DSA_COUNT_GROUP = 8
DSA_ATT_GROUP = 4
DSA_ATT_LEAD = 3
GLA_ROWS = 512
VT_ROWS = ATTN_HEAD_DIM + V7X_BF16_ROWS

INT_MIN = -2**31
INT16_MIN, INT16_MAX = -2**15, 2**15 - 1
NEG_BIG = -1e30
SOFTMAX_NORM_MIN = 2.0 ** -60
SOFTMAX_NORM_MAX = 2.0 ** 100

F32 = jnp.float32
BF16 = jnp.bfloat16
NT_DIMS = (((1,), (1,)), ((), ()))
TN_DIMS = (((0,), (0,)), ((), ()))


def _vmem_limit(nbytes):
    return int(min(nbytes * 1.25 + (8 << 20), V7X_VMEM_BYTES - (6 << 20)))


def _rms(x, g):
    return x * lax.rsqrt(jnp.mean(x * x, axis=-1, keepdims=True) + EPS) * g


def _const_spec(shape):
    nd = len(shape)
    return pl.BlockSpec(shape, lambda *_: (0,) * nd, pipeline_mode=pl.Buffered(1))


def _ffn_kernel(x_ref, gpre_ref, wg_ref, wu_ref, wd_ref, gpost_ref, o_ref):
    o_ref[...] = _ffn_residual(x_ref[...], gpre_ref, wg_ref, wu_ref, wd_ref, gpost_ref)


def _mix_ffn_kernel(x_ref, oa_ref, og_ref, wo_ref, gmix_ref, gpre_ref, wg_ref, wu_ref, wd_ref, gpost_ref, o_ref):
    m = jnp.dot(oa_ref[...], wo_ref[0], preferred_element_type=F32)
    m = m + jnp.dot(og_ref[...], wo_ref[1], preferred_element_type=F32)
    x2 = x_ref[...] + _rms(m, gmix_ref[...])
    o_ref[...] = _ffn_residual(x2, gpre_ref, wg_ref, wu_ref, wd_ref, gpost_ref)


def _ffn_residual(x, gpre_ref, wg_ref, wu_ref, wd_ref, gpost_ref):
    xn = _rms(x, gpre_ref[...]).astype(BF16)

    def cols(c):
        return slice(c * FF_CHUNK, (c + 1) * FF_CHUNK)

    def gate_up(c):
        return (jnp.dot(xn, wg_ref[:, cols(c)], preferred_element_type=F32),
                jnp.dot(xn, wu_ref[:, cols(c)], preferred_element_type=F32))

    nch = wg_ref.shape[1] // FF_CHUNK
    acc = None
    nxt = gate_up(0)
    for c in range(nch):
        g, u = nxt
        if c + 1 < nch:
            nxt = gate_up(c + 1)
        a = (g * jax.nn.sigmoid(g) * u).astype(BF16)
        d = jnp.dot(a, wd_ref[cols(c), :], preferred_element_type=F32)
        acc = d if acc is None else acc + d
    return x + 0.5 * _rms(acc, gpost_ref[...])


def _ffn_operands(d, g_pre, w_gate, w_up, w_down, g_post):
    dff = w_gate.shape[1]
    assert dff % FF_CHUNK == 0
    wg, wu, wd = w_gate.astype(BF16), w_up.astype(BF16), w_down.astype(BF16)
    specs = [_const_spec((1, d)), _const_spec(wg.shape), _const_spec(wu.shape), _const_spec(wd.shape),
             _const_spec((1, d))]
    args = (g_pre.reshape(1, d), wg, wu, wd, g_post.reshape(1, d))
    est = 4 * FFN_ROWS * d * 4 + 3 * d * dff * 2 + FFN_ROWS * d * 8 + 4 * FFN_ROWS * FF_CHUNK * 4
    return specs, args, est


def _ffn(x, g_pre, w_gate, w_up, w_down, g_post):
    n, d = x.shape
    assert n % FFN_ROWS == 0
    ffn_specs, ffn_args, est = _ffn_operands(d, g_pre, w_gate, w_up, w_down, g_post)
    x_spec = pl.BlockSpec((FFN_ROWS, d), lambda i: (i, 0))
    return pl.pallas_call(
        _ffn_kernel,
        out_shape=jax.ShapeDtypeStruct((n, d), F32),
        grid=(n // FFN_ROWS,),
        in_specs=[x_spec] + ffn_specs,
        out_specs=x_spec,
        compiler_params=pltpu.CompilerParams(
            dimension_semantics=("arbitrary",), vmem_limit_bytes=_vmem_limit(est)),
    )(x, *ffn_args)


def _mix_ffn(x, oa, og, w_out, g_mix, g_pre, w_gate, w_up, w_down, g_post):
    n, d = x.shape
    assert n % FFN_ROWS == 0
    ffn_specs, ffn_args, est = _ffn_operands(d, g_pre, w_gate, w_up, w_down, g_post)
    half = oa.shape[1]
    assert og.shape[1] == half and w_out.shape == (2 * half, d)
    wo = w_out.reshape(2, half, d).astype(BF16)
    row = lambda i: (i, 0)
    x_spec = pl.BlockSpec((FFN_ROWS, d), row)
    o_spec = pl.BlockSpec((FFN_ROWS, half), row)
    est += 4 * FFN_ROWS * half * 2 + 2 * half * d * 2 + FFN_ROWS * d * 4
    return pl.pallas_call(
        _mix_ffn_kernel,
        out_shape=jax.ShapeDtypeStruct((n, d), F32),
        grid=(n // FFN_ROWS,),
        in_specs=[x_spec, o_spec, o_spec, _const_spec(wo.shape), _const_spec((1, d))] + ffn_specs,
        out_specs=x_spec,
        compiler_params=pltpu.CompilerParams(
            dimension_semantics=("arbitrary",), vmem_limit_bytes=_vmem_limit(est)),
    )(x, oa, og, wo, g_mix.reshape(1, d), *ffn_args)


_K_OFF, _MISC_OFF, _GQ_OFF, _GK_OFF, _GV_OFF, _GG_OFF, _TOK_COLS = 0, 128, 256, 512, 768, 1280, 1792
_GA_LANE = 72
_QT_OFF, _IQT_OFF, _VT_OFF, _IWT_OFF, _T_ROWS = 0, 512, 1024, 1152, 1168


def _inproj_kernel(x_ref, g_ref, wtok_ref, wt_ref, wa2_ref, ba_ref,
                   qt_ref, iqt_ref, vt_ref, iwt_ref, kk_ref, ik_ref, gqk_ref, gv_ref, gg_ref, la_ref):
    h = _rms(x_ref[...], g_ref[...]).astype(BF16)
    t = h.shape[0]
    pt = lax.dot_general(wt_ref[...], h, NT_DIMS, preferred_element_type=F32)
    for i in range(ATTN_HEADS):
        qt_ref[i] = pt[_QT_OFF + 64 * i:_QT_OFF + 64 * (i + 1)].astype(BF16)
    for i in range(IDX_HEADS):
        iqt_ref[i] = pt[_IQT_OFF + 64 * i:_IQT_OFF + 64 * (i + 1)].astype(BF16)
    ones_row = (lax.broadcasted_iota(jnp.int32, (V7X_BF16_ROWS, t), 0) == 0).astype(F32)
    for g in range(ATTN_KV_HEADS):
        v_t = pt[_VT_OFF + 64 * g:_VT_OFF + 64 * (g + 1)]
        vt_ref[g] = jnp.concatenate([v_t, ones_row], axis=0).astype(BF16)
    iwt_ref[...] = pt[_IWT_OFF:_IWT_OFF + IDX_HEADS] * (IDX_HEADS ** -0.5)

    proj = jnp.dot(h, wtok_ref[...], preferred_element_type=F32)
    for g in range(ATTN_KV_HEADS):
        kk_ref[g] = proj[:, _K_OFF + 64 * g:_K_OFF + 64 * (g + 1)].astype(BF16)
    misc = proj[:, _MISC_OFF:_MISC_OFF + V7X_LANES]
    ik_ref[...] = misc[:, :IDX_DIM].astype(BF16)
    gqk_ref[...] = proj[:, _GQ_OFF:_GV_OFF]
    gv_ref[...] = proj[:, _GV_OFF:_GG_OFF].astype(BF16)
    gg_ref[...] = proj[:, _GG_OFF:_GG_OFF + GLA_HEADS * GLA_DV]
    z = jnp.dot(misc, wa2_ref[...], preferred_element_type=F32, precision=lax.Precision.HIGHEST)
    z = z + ba_ref[...]
    log_sig = jnp.minimum(z, 0.0) - jnp.log1p(jnp.exp(-jnp.abs(z)))
    la_ref[...] = log_sig * (1.0 / GLA_TAU)


def _pack_w_in(w_in):
    d = w_in.shape[0]
    sizes = (512, 128, 128, 512, 64, 8, 256, 256, 512, 16, 512)
    offs = [0]
    for s in sizes:
        offs.append(offs[-1] + s)
    aq, ak, av, iq, ik, iw, gq, gk, gv, ga, gg = [w_in[:, offs[i]:offs[i + 1]] for i in range(len(sizes))]
    z = lambda n: jnp.zeros((d, n), w_in.dtype)
    w_tok = jnp.concatenate([ak, ik, z(8), ga, z(40), gq * (GLA_DK ** -0.5), gk, gv, gg], axis=1)
    w_t = jnp.concatenate([aq * (ATTN_HEAD_DIM ** -0.5), iq * (IDX_DIM ** -0.5), av, iw, z(8)], axis=1).T
    assert w_tok.shape[1] == _TOK_COLS and w_t.shape[0] == _T_ROWS
    return w_tok.astype(BF16), w_t.astype(BF16)


def _inproj(x1, g_mix_pre, w_in, w_gla_a2, b_gla_a):
    n, d = x1.shape
    t = FFN_ROWS
    assert n % t == 0
    w_tok, w_t = _pack_w_in(w_in)
    nqk = GLA_HEADS * GLA_DK
    nv = GLA_HEADS * GLA_DV
    wa2 = jnp.zeros((V7X_LANES, nqk), F32).at[_GA_LANE:_GA_LANE + GLA_GATE_RANK].set(w_gla_a2)
    row = lambda i: (i, 0)
    row3 = lambda i: (0, i, 0)
    col3 = lambda i: (0, 0, i)
    out_shape = (
        jax.ShapeDtypeStruct((ATTN_HEADS, ATTN_HEAD_DIM, n), BF16),
        jax.ShapeDtypeStruct((IDX_HEADS, IDX_DIM, n), BF16),
        jax.ShapeDtypeStruct((ATTN_KV_HEADS, VT_ROWS, n), BF16),
        jax.ShapeDtypeStruct((IDX_HEADS, n), F32),
        jax.ShapeDtypeStruct((ATTN_KV_HEADS, n, ATTN_HEAD_DIM), BF16),
        jax.ShapeDtypeStruct((n, IDX_DIM), BF16),
        jax.ShapeDtypeStruct((n, 2 * nqk), F32),
        jax.ShapeDtypeStruct((n, nv), BF16),
        jax.ShapeDtypeStruct((n, nv), F32),
        jax.ShapeDtypeStruct((n, nqk), F32),
    )
    out_specs = (
        pl.BlockSpec((ATTN_HEADS, ATTN_HEAD_DIM, t), col3),
        pl.BlockSpec((IDX_HEADS, IDX_DIM, t), col3),
        pl.BlockSpec((ATTN_KV_HEADS, VT_ROWS, t), col3),
        pl.BlockSpec((IDX_HEADS, t), lambda i: (0, i)),
        pl.BlockSpec((ATTN_KV_HEADS, t, ATTN_HEAD_DIM), row3),
        pl.BlockSpec((t, IDX_DIM), row),
        pl.BlockSpec((t, 2 * nqk), row),
        pl.BlockSpec((t, nv), row),
        pl.BlockSpec((t, nv), row),
        pl.BlockSpec((t, nqk), row),
    )
    est = (2 * t * d * 4 + d * (_TOK_COLS + _T_ROWS) * 2 + t * (_TOK_COLS + _T_ROWS) * 4 * 2
           + 2 * t * 8 * 1024)
    return pl.pallas_call(
        _inproj_kernel,
        out_shape=out_shape,
        grid=(n // t,),
        in_specs=[
            pl.BlockSpec((t, d), row),
            _const_spec((1, d)),
            _const_spec(w_tok.shape),
            _const_spec(w_t.shape),
            _const_spec(wa2.shape),
            _const_spec((1, nqk)),
        ],
        out_specs=out_specs,
        compiler_params=pltpu.CompilerParams(
            dimension_semantics=("arbitrary",), vmem_limit_bytes=_vmem_limit(est)),
    )(x1, g_mix_pre.reshape(1, d), w_tok, w_t, wa2, b_gla_a.reshape(1, nqk))


def _dsa_kernel(qt_ref, iqt_ref, iwt_ref, kk_ref, vt_ref, ik_ref, o_ref,
                key_s, half_s, low_s, cnt_s, bias_s, acc_s, ot_s, *, topk, idx_bits):
    tq = DSA_TQ
    kc = DSA_KC
    sub = V7X_SUBLANES
    i = pl.program_id(1)
    nchunks = i + 1

    def score_logits(j):
        ikc = ik_ref[pl.ds(pl.multiple_of(j * kc, kc), kc), :]
        iqcat = jnp.concatenate([iqt_ref[h] for h in range(IDX_HEADS)], axis=1)
        return jnp.dot(ikc, iqcat, preferred_element_type=F32)

    def score_keys(j, lg, diagonal):
        sc = jnp.zeros((kc, tq), F32)
        for h in range(IDX_HEADS):
            sc = sc + jnp.maximum(lg[:, h * tq:(h + 1) * tq], 0.0) * iwt_ref[h:h + 1, :]
        bits = pltpu.bitcast(sc, jnp.int32)
        key = jnp.where(bits < 0, bits ^ 0x7FFFFFFF, bits)
        if diagonal:
            kpos = lax.broadcasted_iota(jnp.int32, (kc, tq), 0)
            qpos = lax.broadcasted_iota(jnp.int32, (kc, tq), 1)
            key = jnp.where(kpos <= qpos, key, INT_MIN)
        key_s[j] = key
        half_s[j] = lax.shift_right_arithmetic(key, 16).astype(jnp.int16)
        low_s[j] = key.astype(jnp.int16) ^ jnp.int16(INT16_MIN)

    def sweep_chunks(n, group_fn, gsz):
        assert gsz & (gsz - 1) == 0

        def trip(q, c):
            group_fn(q * gsz, gsz)
            return c

        lax.fori_loop(0, lax.shift_right_logical(n, gsz.bit_length() - 1), trip, 0)
        done = n & -gsz
        part = gsz // 2
        while part:
            pl.when((n & part) != 0)(functools.partial(group_fn, done, part))
            done = done + (n & part)
            part //= 2

    def score_group(j0, count):
        lgs = [score_logits(j0 + s) for s in range(count)]
        for s in range(count):
            score_keys(j0 + s, lgs[s], False)

    sweep_chunks(i, score_group, DSA_SCORE_GROUP)
    score_keys(i, score_logits(i), True)

    qidx = i * tq + lax.broadcasted_iota(jnp.int32, (1, tq), 1)
    krow = jnp.minimum(topk, qidx + 1).astype(F32)

    def count(ind_fn):
        def body(j, acc):
            parts = [acc, None, None, None]
            for s in range(kc // sub):
                ind = ind_fn(key_s[j, s * sub:(s + 1) * sub, :], j * kc + s * sub)
                p = s % 4
                parts[p] = ind if parts[p] is None else parts[p] + ind
            return (parts[0] + parts[1]) + (parts[2] + parts[3])
        acc = lax.fori_loop(0, nchunks, body, jnp.zeros((sub, tq), F32))
        return jnp.sum(acc, axis=0, keepdims=True)

    def count_ge(cand):
        cb = jnp.broadcast_to(cand, (sub, tq))
        return count(lambda k, base: jnp.where(k >= cb, 1.0, 0.0))

    rows16 = V7X_BF16_ROWS

    def count16(cand):
        cb = jnp.broadcast_to(cand, (rows16, tq)).astype(jnp.int16)
        one, nil = jnp.int16(1), jnp.int16(0)
        cnt_s[...] = jnp.zeros_like(cnt_s)

        def group(j0, count):
            parts = [None] * 4
            for s in range(count * kc // rows16):
                c, r = divmod(s, kc // rows16)
                ind = jnp.where(half_s[j0 + c, r * rows16:(r + 1) * rows16, :] >= cb, one, nil)
                p = s % 4
                parts[p] = ind if parts[p] is None else parts[p] + ind
            cnt_s[...] += (parts[0] + parts[1]) + (parts[2] + parts[3])

        sweep_chunks(nchunks, group, DSA_COUNT_GROUP)
        return jnp.sum(cnt_s[...].astype(F32), axis=0, keepdims=True)

    def search16(count_at_min):
        zero = jnp.zeros((1, tq), jnp.int32)
        c0 = count16(zero)
        start = (jnp.where(c0 >= krow, zero, INT16_MIN), jnp.where(c0 >= krow, c0, count_at_min))

        def bit_body(it, carry):
            th, cth = carry
            cand = th | lax.shift_left(jnp.int32(1), 14 - it)
            c = count16(cand)
            return jnp.where(c >= krow, cand, th), jnp.where(c >= krow, c, cth)

        return lax.fori_loop(0, 15, bit_body, start)

    t_hi, c_hi = search16(jnp.full((1, tq), 1.0, F32) * (nchunks * kc).astype(F32))
    th16 = jnp.broadcast_to(t_hi, (kc, tq)).astype(jnp.int16)

    def low_halves(j0, count):
        for j in range(count):
            hi = half_s[j0 + j]
            half_s[j0 + j] = jnp.where(hi == th16, low_s[j0 + j],
                                       jnp.where(hi > th16, jnp.int16(INT16_MAX), jnp.int16(INT16_MIN)))

    sweep_chunks(nchunks, low_halves, DSA_SCORE_GROUP)
    t_lo, c_ge = search16(c_hi)
    t = lax.shift_left(t_hi, 16) | (t_lo - INT16_MIN)

    saturated = jnp.max(jnp.where(t_lo == INT16_MAX, 1.0, 0.0)) > 0.5
    c_gt = lax.cond(saturated, lambda: count_ge(t + 1), lambda: count16(jnp.minimum(t_lo + 1, INT16_MAX)))
    need = krow - c_gt
    tb8 = jnp.broadcast_to(t, (sub, tq))

    @pl.when(jnp.max(c_ge - krow) > 0.5)
    def _():
        sub_iota = lax.broadcasted_iota(jnp.int32, (sub, tq), 0)

        def tied_below(u):
            ub = jnp.broadcast_to(u, (sub, tq))
            return count(lambda k, base: jnp.where(k == tb8, jnp.where(sub_iota + base < ub, 1.0, 0.0), 0.0))

        def ubit(it, u):
            cand = u | lax.shift_left(jnp.int32(1), idx_bits - 1 - it)
            return jnp.where(tied_below(cand) < need, cand, u)

        u = lax.fori_loop(0, idx_bits, ubit, jnp.zeros((1, tq), jnp.int32))
        ub_full = jnp.broadcast_to(u, (kc, tq))
        tb_full = jnp.broadcast_to(t, (kc, tq))
        kidx = lax.broadcasted_iota(jnp.int32, (kc, tq), 0)

        def drop(j, c):
            k = key_s[j]
            key_s[j] = jnp.where(k == tb_full, jnp.where(kidx + j * kc > ub_full, INT_MIN, k), k)
            return c

        lax.fori_loop(0, nchunks, drop, 0)

    def masked_scores(j, g, slot=0):
        koff = pl.multiple_of(j * kc, kc)
        qcat = jnp.concatenate([qt_ref[g * ATTN_REP + r] for r in range(ATTN_REP)], axis=1)
        bias = jnp.concatenate([bias_s[slot]] * ATTN_REP, axis=1)
        kg = kk_ref[g, pl.ds(koff, kc), :]
        vg = vt_ref[g, :, pl.ds(koff, kc)]
        return jnp.dot(kg, qcat, preferred_element_type=F32) + bias, vg

    def set_bias(j, slot=0):
        bias_s[slot] = jnp.where(key_s[j] >= jnp.broadcast_to(t, (kc, tq)), 0.0, NEG_BIG)

    acc_s[...] = jnp.zeros_like(acc_s)

    def att_group(j0, count):
        units = [(slot, g) for slot in range(count) for g in range(ATTN_KV_HEADS)]
        for slot in range(count):
            set_bias(j0 + slot, slot)
        scores = {}

        def issue(u):
            slot, g = units[u]
            scores[u] = masked_scores(j0 + slot, g, slot)

        for u in range(min(DSA_ATT_LEAD, len(units))):
            issue(u)
        for u, (slot, g) in enumerate(units):
            s, vg = scores.pop(u)
            acc_s[g] += jnp.dot(vg, jnp.exp(s).astype(BF16), preferred_element_type=F32)
            if u + DSA_ATT_LEAD < len(units):
                issue(u + DSA_ATT_LEAD)

    sweep_chunks(nchunks, att_group, DSA_ATT_GROUP)

    norm = acc_s[:, ATTN_HEAD_DIM:ATTN_HEAD_DIM + 1, :]
    in_range = jnp.where(norm >= SOFTMAX_NORM_MIN, jnp.where(norm <= SOFTMAX_NORM_MAX, 1.0, 0.0), 0.0)

    @pl.when(jnp.min(in_range) < 0.5)
    def _():
        acc_s[...] = jnp.zeros_like(acc_s)

        def online_body(j, ms):
            set_bias(j)
            out = []
            for g in range(ATTN_KV_HEADS):
                s, vg = masked_scores(j, g)
                m_new = jnp.maximum(ms[g], jnp.max(s, axis=0, keepdims=True))
                p = jnp.exp(s - m_new).astype(BF16)
                pv = jnp.dot(vg, p, preferred_element_type=F32)
                acc_s[g] = jnp.exp(ms[g] - m_new) * acc_s[g] + pv
                out.append(m_new)
            return tuple(out)

        m_init = tuple(jnp.full((1, ATTN_REP * tq), NEG_BIG, F32) for _ in range(ATTN_KV_HEADS))
        lax.fori_loop(0, nchunks, online_body, m_init)

    for h in range(ATTN_HEADS):
        g, r = divmod(h, ATTN_REP)
        a = acc_s[g, :, r * tq:(r + 1) * tq]
        ot_s[h * ATTN_HEAD_DIM:(h + 1) * ATTN_HEAD_DIM, :] = (
            a[:ATTN_HEAD_DIM] / a[ATTN_HEAD_DIM:ATTN_HEAD_DIM + 1])
    o_ref[...] = ot_s[...].T.astype(o_ref.dtype)


def _dsa(qt, iqt, iwt, kk, vt, ik, batch, seq):
    n = batch * seq
    tq, kc = DSA_TQ, DSA_KC
    assert seq % tq == 0
    nq = seq // tq
    topk = min(TOPK_MAX, seq // 4)
    nch = seq // kc
    nout = ATTN_HEADS * ATTN_HEAD_DIM
    qmap = lambda b, i: (0, 0, b * nq + i)
    est = (nch * kc * tq * 8 + 3 * DSA_ATT_GROUP * ATTN_REP * kc * tq * 4 + ATTN_HEADS * (VT_ROWS + 8) * tq * 4 + nout * tq * 4
           + ATTN_KV_HEADS * seq * (V7X_LANES + VT_ROWS) * 2 + seq * V7X_LANES * 2
           + 2 * (2 * ATTN_HEADS * ATTN_HEAD_DIM * tq * 2 + 8 * tq * 4 + tq * nout * 2))
    return pl.pallas_call(
        functools.partial(_dsa_kernel, topk=topk, idx_bits=(seq - 1).bit_length()),
        out_shape=jax.ShapeDtypeStruct((n, nout), BF16),
        grid=(batch, nq),
        in_specs=[
            pl.BlockSpec((ATTN_HEADS, ATTN_HEAD_DIM, tq), qmap),
            pl.BlockSpec((IDX_HEADS, IDX_DIM, tq), qmap),
            pl.BlockSpec((IDX_HEADS, tq), lambda b, i: (0, b * nq + i)),
            pl.BlockSpec((ATTN_KV_HEADS, seq, ATTN_HEAD_DIM), lambda b, i: (0, b, 0),
                         pipeline_mode=pl.Buffered(1)),
            pl.BlockSpec((ATTN_KV_HEADS, VT_ROWS, seq), lambda b, i: (0, 0, b),
                         pipeline_mode=pl.Buffered(1)),
            pl.BlockSpec((seq, IDX_DIM), lambda b, i: (b, 0), pipeline_mode=pl.Buffered(1)),
        ],
        out_specs=pl.BlockSpec((tq, nout), lambda b, i: (b * nq + i, 0)),
        scratch_shapes=[
            pltpu.VMEM((nch, kc, tq), jnp.int32),
            pltpu.VMEM((nch, kc, tq), jnp.int16),
            pltpu.VMEM((nch, kc, tq), jnp.int16),
            pltpu.VMEM((V7X_BF16_ROWS, tq), jnp.int16),
            pltpu.VMEM((DSA_ATT_GROUP, kc, tq), F32),
            pltpu.VMEM((ATTN_KV_HEADS, VT_ROWS, ATTN_REP * tq), F32),
            pltpu.VMEM((nout, tq), F32),
        ],
        compiler_params=pltpu.CompilerParams(
            dimension_semantics=("arbitrary", "arbitrary"), vmem_limit_bytes=_vmem_limit(est)),
    )(qt, iqt, iwt, kk, vt, ik)


def _gla_kernel(gqk_ref, gv_ref, la_ref, gg_ref, gn_ref, o_ref, st_s):
    @pl.when(pl.program_id(1) == 0)
    def _():
        st_s[...] = jnp.zeros_like(st_s)

    c = GLA_CHUNK
    nqk = GLA_HEADS * GLA_DK
    r_i = lax.broadcasted_iota(jnp.int32, (c, c), 0)
    c_i = lax.broadcasted_iota(jnp.int32, (c, c), 1)
    tri = r_i >= c_i
    tri_f = tri.astype(F32)
    gn = gn_ref[...]
    nchunk = GLA_ROWS // c
    heads = range(GLA_HEADS)
    hs = [slice(h * GLA_DK, (h + 1) * GLA_DK) for h in heads]
    vs = [slice(h * GLA_DV, (h + 1) * GLA_DV) for h in heads]

    local = []
    for ci in range(nchunk):
        rows = slice(ci * c, (ci + 1) * c)
        la = la_ref[rows, :]
        b = jnp.dot(tri_f, la, preferred_element_type=F32, precision=lax.Precision.HIGHEST)
        b_last = b[c - 1:c, :]
        q = gqk_ref[rows, :nqk]
        k = gqk_ref[rows, nqk:]
        q_dec = (q * jnp.exp(b)).astype(BF16)
        k_in = (k * jnp.exp(-b)).astype(BF16)
        k_out = (k * jnp.exp(b_last - b)).astype(BF16)
        decay = jnp.exp(b_last)
        intra, upd = [], []
        for h in heads:
            v = gv_ref[rows, vs[h]]
            a = lax.dot_general(q_dec[:, hs[h]], k_in[:, hs[h]], NT_DIMS, preferred_element_type=F32)
            a = jnp.where(tri, a, 0.0).astype(BF16)
            intra.append(jnp.dot(a, v, preferred_element_type=F32))
            upd.append(lax.dot_general(v, k_out[:, hs[h]], TN_DIMS, preferred_element_type=F32))
        local.append((rows, q_dec, decay, intra, upd))

    st = [st_s[h] for h in heads]
    for rows, q_dec, decay, intra, upd in local:
        for h in heads:
            o = intra[h] + lax.dot_general(q_dec[:, hs[h]], st[h].astype(BF16), NT_DIMS,
                                           preferred_element_type=F32)
            st[h] = st[h] * decay[:, hs[h]] + upd[h]
            gate = gg_ref[rows, vs[h]]
            o_ref[rows, vs[h]] = (_rms(o, gn) * (gate * jax.nn.sigmoid(gate))).astype(o_ref.dtype)
    for h in heads:
        st_s[h] = st[h]


def _gla(gqk, gv, la, gg, g_norm, batch, seq):
    n = batch * seq
    t = GLA_ROWS
    assert seq % t == 0
    ns = seq // t
    row = lambda b, i: (b * ns + i, 0)
    nqk = GLA_HEADS * GLA_DK
    nv = GLA_HEADS * GLA_DV
    return pl.pallas_call(
        _gla_kernel,
        out_shape=jax.ShapeDtypeStruct((n, nv), BF16),
        grid=(batch, ns),
        in_specs=[
            pl.BlockSpec((t, 2 * nqk), row),
            pl.BlockSpec((t, nv), row),
            pl.BlockSpec((t, nqk), row),
            pl.BlockSpec((t, nv), row),
            _const_spec((1, GLA_DV)),
        ],
        out_specs=pl.BlockSpec((t, nv), row),
        scratch_shapes=[pltpu.VMEM((GLA_HEADS, GLA_DV, GLA_DK), F32)],
        compiler_params=pltpu.CompilerParams(dimension_semantics=("arbitrary", "arbitrary")),
    )(gqk, gv, la, gg, g_norm.reshape(1, GLA_DV))


def kernel(x, g_ffn1_pre, w_ffn1_gate, w_ffn1_up, w_ffn1_down, g_ffn1_post, g_mix_pre, w_in, w_gla_a2,
           b_gla_a, g_gla_norm, w_out, g_mix_post, g_ffn2_pre, w_ffn2_gate, w_ffn2_up, w_ffn2_down,
           g_ffn2_post):
    batch, seq, d = x.shape
    h = x.reshape(batch * seq, d)
    for l in range(g_ffn1_pre.shape[0]):
        h = _ffn(h, g_ffn1_pre[l], w_ffn1_gate[l], w_ffn1_up[l], w_ffn1_down[l], g_ffn1_post[l])
        qt, iqt, vt, iwt, kk, ik, gqk, gv, gg, la = _inproj(h, g_mix_pre[l], w_in[l], w_gla_a2[l], b_gla_a[l])
        oa = _dsa(qt, iqt, iwt, kk, vt, ik, batch, seq)
        og = _gla(gqk, gv, la, gg, g_gla_norm[l], batch, seq)
        h = _mix_ffn(h, oa, og, w_out[l], g_mix_post[l],
                     g_ffn2_pre[l], w_ffn2_gate[l], w_ffn2_up[l], w_ffn2_down[l], g_ffn2_post[l])
    return h.reshape(batch, seq, d)
```

```python
import functools

import jax
import jax.numpy as jnp
from jax import lax
from jax.experimental import pallas as pl
from jax.experimental.pallas import tpu as pltpu

ATTN_HEADS = 8
ATTN_KV_HEADS = 2
ATTN_HEAD_DIM = 64
ATTN_REP = ATTN_HEADS // ATTN_KV_HEADS
IDX_HEADS = 8
IDX_DIM = 64
TOPK_MAX = 256
GLA_HEADS = 4
GLA_DK = 64
GLA_DV = 128
GLA_GATE_RANK = 16
GLA_TAU = 16.0
GLA_CHUNK = 64
EPS = 1e-6

V7X_LANES = 128
V7X_SUBLANES = 8
V7X_BF16_ROWS = 16
V7X_MXU_DIM = 256
V7X_VMEM_BYTES = 64 * 2**20

FFN_ROWS = 512
FF_CHUNK = V7X_MXU_DIM
DSA_TQ = 256
DSA_KC = DSA_TQ
DSA_SCORE_GROUP = 4
DSA_COUNT_GROUP = 8
DSA_ATT_GROUP = 4
DSA_ATT_LEAD = 3
GLA_ROWS = 512
VT_ROWS = ATTN_HEAD_DIM + V7X_BF16_ROWS

INT_MIN = -2**31
INT16_MIN, INT16_MAX = -2**15, 2**15 - 1
NEG_BIG = -1e30
SOFTMAX_NORM_MIN = 2.0 ** -60
SOFTMAX_NORM_MAX = 2.0 ** 100

F32 = jnp.float32
BF16 = jnp.bfloat16
NT_DIMS = (((1,), (1,)), ((), ()))
TN_DIMS = (((0,), (0,)), ((), ()))


def _vmem_limit(nbytes):
    return int(min(nbytes * 1.25 + (8 << 20), V7X_VMEM_BYTES - (6 << 20)))


def _rms(x, g):
    return x * lax.rsqrt(jnp.mean(x * x, axis=-1, keepdims=True) + EPS) * g


def _const_spec(shape):
    nd = len(shape)
    return pl.BlockSpec(shape, lambda *_: (0,) * nd, pipeline_mode=pl.Buffered(1))


def _ffn_kernel(x_ref, gpre_ref, wg_ref, wu_ref, wd_ref, gpost_ref, o_ref):
    o_ref[...] = _ffn_residual(x_ref[...], gpre_ref, wg_ref, wu_ref, wd_ref, gpost_ref)


def _mix_ffn_kernel(x_ref, oa_ref, og_ref, wo_ref, gmix_ref, gpre_ref, wg_ref, wu_ref, wd_ref, gpost_ref, o_ref):
    m = jnp.dot(oa_ref[...], wo_ref[0], preferred_element_type=F32)
    m = m + jnp.dot(og_ref[...], wo_ref[1], preferred_element_type=F32)
    x2 = x_ref[...] + _rms(m, gmix_ref[...])
    o_ref[...] = _ffn_residual(x2, gpre_ref, wg_ref, wu_ref, wd_ref, gpost_ref)


def _ffn_residual(x, gpre_ref, wg_ref, wu_ref, wd_ref, gpost_ref):
    xn = _rms(x, gpre_ref[...]).astype(BF16)

    def cols(c):
        return slice(c * FF_CHUNK, (c + 1) * FF_CHUNK)

    def gate_up(c):
        return (jnp.dot(xn, wg_ref[:, cols(c)], preferred_element_type=F32),
                jnp.dot(xn, wu_ref[:, cols(c)], preferred_element_type=F32))

    nch = wg_ref.shape[1] // FF_CHUNK
    acc = None
    nxt = gate_up(0)
    for c in range(nch):
        g, u = nxt
        if c + 1 < nch:
            nxt = gate_up(c + 1)
        a = (g * jax.nn.sigmoid(g) * u).astype(BF16)
        d = jnp.dot(a, wd_ref[cols(c), :], preferred_element_type=F32)
        acc = d if acc is None else acc + d
    return x + 0.5 * _rms(acc, gpost_ref[...])


def _ffn_operands(d, g_pre, w_gate, w_up, w_down, g_post):
    dff = w_gate.shape[1]
    assert dff % FF_CHUNK == 0
    wg, wu, wd = w_gate.astype(BF16), w_up.astype(BF16), w_down.astype(BF16)
    specs = [_const_spec((1, d)), _const_spec(wg.shape), _const_spec(wu.shape), _const_spec(wd.shape),
             _const_spec((1, d))]
    args = (g_pre.reshape(1, d), wg, wu, wd, g_post.reshape(1, d))
    est = 4 * FFN_ROWS * d * 4 + 3 * d * dff * 2 + FFN_ROWS * d * 8 + 4 * FFN_ROWS * FF_CHUNK * 4
    return specs, args, est


def _ffn(x, g_pre, w_gate, w_up, w_down, g_post):
    n, d = x.shape
    assert n % FFN_ROWS == 0
    ffn_specs, ffn_args, est = _ffn_operands(d, g_pre, w_gate, w_up, w_down, g_post)
    x_spec = pl.BlockSpec((FFN_ROWS, d), lambda i: (i, 0))
    return pl.pallas_call(
        _ffn_kernel,
        out_shape=jax.ShapeDtypeStruct((n, d), F32),
        grid=(n // FFN_ROWS,),
        in_specs=[x_spec] + ffn_specs,
        out_specs=x_spec,
        compiler_params=pltpu.CompilerParams(
            dimension_semantics=("arbitrary",), vmem_limit_bytes=_vmem_limit(est)),
    )(x, *ffn_args)


def _mix_ffn(x, oa, og, w_out, g_mix, g_pre, w_gate, w_up, w_down, g_post):
    n, d = x.shape
    assert n % FFN_ROWS == 0
    ffn_specs, ffn_args, est = _ffn_operands(d, g_pre, w_gate, w_up, w_down, g_post)
    half = oa.shape[1]
    assert og.shape[1] == half and w_out.shape == (2 * half, d)
    wo = w_out.reshape(2, half, d).astype(BF16)
    row = lambda i: (i, 0)
    x_spec = pl.BlockSpec((FFN_ROWS, d), row)
    o_spec = pl.BlockSpec((FFN_ROWS, half), row)
    est += 4 * FFN_ROWS * half * 2 + 2 * half * d * 2 + FFN_ROWS * d * 4
    return pl.pallas_call(
        _mix_ffn_kernel,
        out_shape=jax.ShapeDtypeStruct((n, d), F32),
        grid=(n // FFN_ROWS,),
        in_specs=[x_spec, o_spec, o_spec, _const_spec(wo.shape), _const_spec((1, d))] + ffn_specs,
        out_specs=x_spec,
        compiler_params=pltpu.CompilerParams(
            dimension_semantics=("arbitrary",), vmem_limit_bytes=_vmem_limit(est)),
    )(x, oa, og, wo, g_mix.reshape(1, d), *ffn_args)


_K_OFF, _MISC_OFF, _GQ_OFF, _GK_OFF, _GV_OFF, _GG_OFF, _TOK_COLS = 0, 128, 256, 512, 768, 1280, 1792
_GA_LANE = 72
_QT_OFF, _IQT_OFF, _VT_OFF, _IWT_OFF, _T_ROWS = 0, 512, 1024, 1152, 1168


def _inproj_kernel(x_ref, g_ref, wtok_ref, wt_ref, wa2_ref, ba_ref,
                   qt_ref, iqt_ref, vt_ref, iwt_ref, kk_ref, ik_ref, gqk_ref, gv_ref, gg_ref, la_ref):
    h = _rms(x_ref[...], g_ref[...]).astype(BF16)
    t = h.shape[0]
    pt = lax.dot_general(wt_ref[...], h, NT_DIMS, preferred_element_type=F32)
    for i in range(ATTN_HEADS):
        qt_ref[i] = pt[_QT_OFF + 64 * i:_QT_OFF + 64 * (i + 1)].astype(BF16)
    for i in range(IDX_HEADS):
        iqt_ref[i] = pt[_IQT_OFF + 64 * i:_IQT_OFF + 64 * (i + 1)].astype(BF16)
    ones_row = (lax.broadcasted_iota(jnp.int32, (V7X_BF16_ROWS, t), 0) == 0).astype(F32)
    for g in range(ATTN_KV_HEADS):
        v_t = pt[_VT_OFF + 64 * g:_VT_OFF + 64 * (g + 1)]
        vt_ref[g] = jnp.concatenate([v_t, ones_row], axis=0).astype(BF16)
    iwt_ref[...] = pt[_IWT_OFF:_IWT_OFF + IDX_HEADS] * (IDX_HEADS ** -0.5)

    proj = jnp.dot(h, wtok_ref[...], preferred_element_type=F32)
    for g in range(ATTN_KV_HEADS):
        kk_ref[g] = proj[:, _K_OFF + 64 * g:_K_OFF + 64 * (g + 1)].astype(BF16)
    misc = proj[:, _MISC_OFF:_MISC_OFF + V7X_LANES]
    ik_ref[...] = misc[:, :IDX_DIM].astype(BF16)
    gqk_ref[...] = proj[:, _GQ_OFF:_GV_OFF]
    gv_ref[...] = proj[:, _GV_OFF:_GG_OFF].astype(BF16)
    gg_ref[...] = proj[:, _GG_OFF:_GG_OFF + GLA_HEADS * GLA_DV]
    z = jnp.dot(misc, wa2_ref[...], preferred_element_type=F32, precision=lax.Precision.HIGHEST)
    z = z + ba_ref[...]
    log_sig = jnp.minimum(z, 0.0) - jnp.log1p(jnp.exp(-jnp.abs(z)))
    la_ref[...] = log_sig * (1.0 / GLA_TAU)


def _pack_w_in(w_in):
    d = w_in.shape[0]
    sizes = (512, 128, 128, 512, 64, 8, 256, 256, 512, 16, 512)
    offs = [0]
    for s in sizes:
        offs.append(offs[-1] + s)
    aq, ak, av, iq, ik, iw, gq, gk, gv, ga, gg = [w_in[:, offs[i]:offs[i + 1]] for i in range(len(sizes))]
    z = lambda n: jnp.zeros((d, n), w_in.dtype)
    w_tok = jnp.concatenate([ak, ik, z(8), ga, z(40), gq * (GLA_DK ** -0.5), gk, gv, gg], axis=1)
    w_t = jnp.concatenate([aq * (ATTN_HEAD_DIM ** -0.5), iq * (IDX_DIM ** -0.5), av, iw, z(8)], axis=1).T
    assert w_tok.shape[1] == _TOK_COLS and w_t.shape[0] == _T_ROWS
    return w_tok.astype(BF16), w_t.astype(BF16)


def _inproj(x1, g_mix_pre, w_in, w_gla_a2, b_gla_a):
    n, d = x1.shape
    t = FFN_ROWS
    assert n % t == 0
    w_tok, w_t = _pack_w_in(w_in)
    nqk = GLA_HEADS * GLA_DK
    nv = GLA_HEADS * GLA_DV
    wa2 = jnp.zeros((V7X_LANES, nqk), F32).at[_GA_LANE:_GA_LANE + GLA_GATE_RANK].set(w_gla_a2)
    row = lambda i: (i, 0)
    row3 = lambda i: (0, i, 0)
    col3 = lambda i: (0, 0, i)
    out_shape = (
        jax.ShapeDtypeStruct((ATTN_HEADS, ATTN_HEAD_DIM, n), BF16),
        jax.ShapeDtypeStruct((IDX_HEADS, IDX_DIM, n), BF16),
        jax.ShapeDtypeStruct((ATTN_KV_HEADS, VT_ROWS, n), BF16),
        jax.ShapeDtypeStruct((IDX_HEADS, n), F32),
        jax.ShapeDtypeStruct((ATTN_KV_HEADS, n, ATTN_HEAD_DIM), BF16),
        jax.ShapeDtypeStruct((n, IDX_DIM), BF16),
        jax.ShapeDtypeStruct((n, 2 * nqk), F32),
        jax.ShapeDtypeStruct((n, nv), BF16),
        jax.ShapeDtypeStruct((n, nv), F32),
        jax.ShapeDtypeStruct((n, nqk), F32),
    )
    out_specs = (
        pl.BlockSpec((ATTN_HEADS, ATTN_HEAD_DIM, t), col3),
        pl.BlockSpec((IDX_HEADS, IDX_DIM, t), col3),
        pl.BlockSpec((ATTN_KV_HEADS, VT_ROWS, t), col3),
        pl.BlockSpec((IDX_HEADS, t), lambda i: (0, i)),
        pl.BlockSpec((ATTN_KV_HEADS, t, ATTN_HEAD_DIM), row3),
        pl.BlockSpec((t, IDX_DIM), row),
        pl.BlockSpec((t, 2 * nqk), row),
        pl.BlockSpec((t, nv), row),
        pl.BlockSpec((t, nv), row),
        pl.BlockSpec((t, nqk), row),
    )
    est = (2 * t * d * 4 + d * (_TOK_COLS + _T_ROWS) * 2 + t * (_TOK_COLS + _T_ROWS) * 4 * 2
           + 2 * t * 8 * 1024)
    return pl.pallas_call(
        _inproj_kernel,
        out_shape=out_shape,
        grid=(n // t,),
        in_specs=[
            pl.BlockSpec((t, d), row),
            _const_spec((1, d)),
            _const_spec(w_tok.shape),
            _const_spec(w_t.shape),
            _const_spec(wa2.shape),
            _const_spec((1, nqk)),
        ],
        out_specs=out_specs,
        compiler_params=pltpu.CompilerParams(
            dimension_semantics=("arbitrary",), vmem_limit_bytes=_vmem_limit(est)),
    )(x1, g_mix_pre.reshape(1, d), w_tok, w_t, wa2, b_gla_a.reshape(1, nqk))


def _dsa_kernel(qt_ref, iqt_ref, iwt_ref, kk_ref, vt_ref, ik_ref, o_ref,
                key_s, half_s, low_s, cnt_s, bias_s, acc_s, ot_s, *, topk, idx_bits):
    tq = DSA_TQ
    kc = DSA_KC
    sub = V7X_SUBLANES
    i = pl.program_id(1)
    nchunks = i + 1

    def score_logits(j):
        ikc = ik_ref[pl.ds(pl.multiple_of(j * kc, kc), kc), :]
        iqcat = jnp.concatenate([iqt_ref[h] for h in range(IDX_HEADS)], axis=1)
        return jnp.dot(ikc, iqcat, preferred_element_type=F32)

    def score_keys(j, lg, diagonal):
        sc = jnp.zeros((kc, tq), F32)
        for h in range(IDX_HEADS):
            sc = sc + jnp.maximum(lg[:, h * tq:(h + 1) * tq], 0.0) * iwt_ref[h:h + 1, :]
        bits = pltpu.bitcast(sc, jnp.int32)
        key = jnp.where(bits < 0, bits ^ 0x7FFFFFFF, bits)
        if diagonal:
            kpos = lax.broadcasted_iota(jnp.int32, (kc, tq), 0)
            qpos = lax.broadcasted_iota(jnp.int32, (kc, tq), 1)
            key = jnp.where(kpos <= qpos, key, INT_MIN)
        key_s[j] = key
        half_s[j] = lax.shift_right_arithmetic(key, 16).astype(jnp.int16)
        low_s[j] = key.astype(jnp.int16) ^ jnp.int16(INT16_MIN)

    def sweep_chunks(n, group_fn, gsz):
        assert gsz & (gsz - 1) == 0

        def trip(q, c):
            group_fn(q * gsz, gsz)
            return c

        lax.fori_loop(0, lax.shift_right_logical(n, gsz.bit_length() - 1), trip, 0)
        done = n & -gsz
        part = gsz // 2
        while part:
            pl.when((n & part) != 0)(functools.partial(group_fn, done, part))
            done = done + (n & part)
            part //= 2

    def score_group(j0, count):
        lgs = [score_logits(j0 + s) for s in range(count)]
        for s in range(count):
            score_keys(j0 + s, lgs[s], False)

    sweep_chunks(i, score_group, DSA_SCORE_GROUP)
    score_keys(i, score_logits(i), True)

    qidx = i * tq + lax.broadcasted_iota(jnp.int32, (1, tq), 1)
    krow = jnp.minimum(topk, qidx + 1).astype(F32)

    def count(ind_fn):
        def body(j, acc):
            parts = [acc, None, None, None]
            for s in range(kc // sub):
                ind = ind_fn(key_s[j, s * sub:(s + 1) * sub, :], j * kc + s * sub)
                p = s % 4
                parts[p] = ind if parts[p] is None else parts[p] + ind
            return (parts[0] + parts[1]) + (parts[2] + parts[3])
        acc = lax.fori_loop(0, nchunks, body, jnp.zeros((sub, tq), F32))
        return jnp.sum(acc, axis=0, keepdims=True)

    def count_ge(cand):
        cb = jnp.broadcast_to(cand, (sub, tq))
        return count(lambda k, base: jnp.where(k >= cb, 1.0, 0.0))

    rows16 = V7X_BF16_ROWS

    def count16(cand, src=half_s, below=False):
        cb = jnp.broadcast_to(cand, (rows16, tq)).astype(jnp.int16)
        one, nil = jnp.int16(1), jnp.int16(0)
        cnt_s[...] = jnp.zeros_like(cnt_s)

        def group(j0, count):
            parts = [None] * 4
            for s in range(count * kc // rows16):
                c, r = divmod(s, kc // rows16)
                x = src[j0 + c, r * rows16:(r + 1) * rows16, :]
                ind = jnp.where(x < cb, one, nil) if below else jnp.where(x >= cb, one, nil)
                p = s % 4
                parts[p] = ind if parts[p] is None else parts[p] + ind
            cnt_s[...] += (parts[0] + parts[1]) + (parts[2] + parts[3])

        sweep_chunks(nchunks, group, DSA_COUNT_GROUP)
        return jnp.sum(cnt_s[...].astype(F32), axis=0, keepdims=True)

    def search16(count_at_min):
        zero = jnp.zeros((1, tq), jnp.int32)
        c0 = count16(zero)
        start = (jnp.where(c0 >= krow, zero, INT16_MIN), jnp.where(c0 >= krow, c0, count_at_min))

        def bit_body(it, carry):
            th, cth = carry
            cand = th | lax.shift_left(jnp.int32(1), 14 - it)
            c = count16(cand)
            return jnp.where(c >= krow, cand, th), jnp.where(c >= krow, c, cth)

        return lax.fori_loop(0, 15, bit_body, start)

    t_hi, c_hi = search16(jnp.full((1, tq), 1.0, F32) * (nchunks * kc).astype(F32))
    th16 = jnp.broadcast_to(t_hi, (kc, tq)).astype(jnp.int16)

    def low_halves(j0, count):
        for j in range(count):
            hi = half_s[j0 + j]
            half_s[j0 + j] = jnp.where(hi == th16, low_s[j0 + j],
                                       jnp.where(hi > th16, jnp.int16(INT16_MAX), jnp.int16(INT16_MIN)))

    sweep_chunks(nchunks, low_halves, DSA_SCORE_GROUP)
    t_lo, c_ge = search16(c_hi)
    t = lax.shift_left(t_hi, 16) | (t_lo - INT16_MIN)

    saturated = jnp.max(jnp.where(t_lo == INT16_MAX, 1.0, 0.0)) > 0.5
    c_gt = lax.cond(saturated, lambda: count_ge(t + 1), lambda: count16(jnp.minimum(t_lo + 1, INT16_MAX)))
    need = krow - c_gt

    @pl.when(jnp.max(c_ge - krow) > 0.5)
    def _():
        assert idx_bits < 16
        tb_full = jnp.broadcast_to(t, (kc, tq))
        kidx = lax.broadcasted_iota(jnp.int32, (kc, tq), 0)

        def tied_index(j0, count):
            for s in range(count):
                j = j0 + s
                low_s[j] = jnp.where(key_s[j] == tb_full, kidx + j * kc, INT16_MAX).astype(jnp.int16)

        sweep_chunks(nchunks, tied_index, DSA_SCORE_GROUP)

        def ubit(it, u):
            cand = u | lax.shift_left(jnp.int32(1), idx_bits - 1 - it)
            return jnp.where(count16(cand, low_s, below=True) < need, cand, u)

        u = lax.fori_loop(0, idx_bits, ubit, jnp.zeros((1, tq), jnp.int32))
        ub_full = jnp.broadcast_to(u, (kc, tq))

        def drop(j0, count):
            for s in range(count):
                z = low_s[j0 + s].astype(jnp.int32)
                k = key_s[j0 + s]
                key_s[j0 + s] = jnp.where(z > ub_full, jnp.where(z < INT16_MAX, INT_MIN, k), k)

        sweep_chunks(nchunks, drop, DSA_SCORE_GROUP)

    def masked_scores(j, g, slot=0):
        koff = pl.multiple_of(j * kc, kc)
        qcat = jnp.concatenate([qt_ref[g * ATTN_REP + r] for r in range(ATTN_REP)], axis=1)
        bias = jnp.concatenate([bias_s[slot]] * ATTN_REP, axis=1)
        kg = kk_ref[g, pl.ds(koff, kc), :]
        vg = vt_ref[g, :, pl.ds(koff, kc)]
        return jnp.dot(kg, qcat, preferred_element_type=F32) + bias, vg

    def set_bias(j, slot=0):
        bias_s[slot] = jnp.where(key_s[j] >= jnp.broadcast_to(t, (kc, tq)), 0.0, NEG_BIG)

    acc_s[...] = jnp.zeros_like(acc_s)

    def att_group(j0, count):
        units = [(slot, g) for slot in range(count) for g in range(ATTN_KV_HEADS)]
        for slot in range(count):
            set_bias(j0 + slot, slot)
        scores = {}

        def issue(u):
            slot, g = units[u]
            scores[u] = masked_scores(j0 + slot, g, slot)

        for u in range(min(DSA_ATT_LEAD, len(units))):
            issue(u)
        for u, (slot, g) in enumerate(units):
            s, vg = scores.pop(u)
            acc_s[g] += jnp.dot(vg, jnp.exp(s).astype(BF16), preferred_element_type=F32)
            if u + DSA_ATT_LEAD < len(units):
                issue(u + DSA_ATT_LEAD)

    sweep_chunks(nchunks, att_group, DSA_ATT_GROUP)

    norm = acc_s[:, ATTN_HEAD_DIM:ATTN_HEAD_DIM + 1, :]
    in_range = jnp.where(norm >= SOFTMAX_NORM_MIN, jnp.where(norm <= SOFTMAX_NORM_MAX, 1.0, 0.0), 0.0)

    @pl.when(jnp.min(in_range) < 0.5)
    def _():
        acc_s[...] = jnp.zeros_like(acc_s)

        def online_body(j, ms):
            set_bias(j)
            out = []
            for g in range(ATTN_KV_HEADS):
                s, vg = masked_scores(j, g)
                m_new = jnp.maximum(ms[g], jnp.max(s, axis=0, keepdims=True))
                p = jnp.exp(s - m_new).astype(BF16)
                pv = jnp.dot(vg, p, preferred_element_type=F32)
                acc_s[g] = jnp.exp(ms[g] - m_new) * acc_s[g] + pv
                out.append(m_new)
            return tuple(out)

        m_init = tuple(jnp.full((1, ATTN_REP * tq), NEG_BIG, F32) for _ in range(ATTN_KV_HEADS))
        lax.fori_loop(0, nchunks, online_body, m_init)

    for h in range(ATTN_HEADS):
        g, r = divmod(h, ATTN_REP)
        a = acc_s[g, :, r * tq:(r + 1) * tq]
        ot_s[h * ATTN_HEAD_DIM:(h + 1) * ATTN_HEAD_DIM, :] = (
            a[:ATTN_HEAD_DIM] / a[ATTN_HEAD_DIM:ATTN_HEAD_DIM + 1])
    o_ref[...] = ot_s[...].T.astype(o_ref.dtype)


def _dsa(qt, iqt, iwt, kk, vt, ik, batch, seq):
    n = batch * seq
    tq, kc = DSA_TQ, DSA_KC
    assert seq % tq == 0
    nq = seq // tq
    topk = min(TOPK_MAX, seq // 4)
    nch = seq // kc
    nout = ATTN_HEADS * ATTN_HEAD_DIM
    qmap = lambda b, i: (0, 0, b * nq + i)
    est = (nch * kc * tq * 8 + 3 * DSA_ATT_GROUP * ATTN_REP * kc * tq * 4 + ATTN_HEADS * (VT_ROWS + 8) * tq * 4 + nout * tq * 4
           + ATTN_KV_HEADS * seq * (V7X_LANES + VT_ROWS) * 2 + seq * V7X_LANES * 2
           + 2 * (2 * ATTN_HEADS * ATTN_HEAD_DIM * tq * 2 + 8 * tq * 4 + tq * nout * 2))
    return pl.pallas_call(
        functools.partial(_dsa_kernel, topk=topk, idx_bits=(seq - 1).bit_length()),
        out_shape=jax.ShapeDtypeStruct((n, nout), BF16),
        grid=(batch, nq),
        in_specs=[
            pl.BlockSpec((ATTN_HEADS, ATTN_HEAD_DIM, tq), qmap),
            pl.BlockSpec((IDX_HEADS, IDX_DIM, tq), qmap),
            pl.BlockSpec((IDX_HEADS, tq), lambda b, i: (0, b * nq + i)),
            pl.BlockSpec((ATTN_KV_HEADS, seq, ATTN_HEAD_DIM), lambda b, i: (0, b, 0),
                         pipeline_mode=pl.Buffered(1)),
            pl.BlockSpec((ATTN_KV_HEADS, VT_ROWS, seq), lambda b, i: (0, 0, b),
                         pipeline_mode=pl.Buffered(1)),
            pl.BlockSpec((seq, IDX_DIM), lambda b, i: (b, 0), pipeline_mode=pl.Buffered(1)),
        ],
        out_specs=pl.BlockSpec((tq, nout), lambda b, i: (b * nq + i, 0)),
        scratch_shapes=[
            pltpu.VMEM((nch, kc, tq), jnp.int32),
            pltpu.VMEM((nch, kc, tq), jnp.int16),
            pltpu.VMEM((nch, kc, tq), jnp.int16),
            pltpu.VMEM((V7X_BF16_ROWS, tq), jnp.int16),
            pltpu.VMEM((DSA_ATT_GROUP, kc, tq), F32),
            pltpu.VMEM((ATTN_KV_HEADS, VT_ROWS, ATTN_REP * tq), F32),
            pltpu.VMEM((nout, tq), F32),
        ],
        compiler_params=pltpu.CompilerParams(
            dimension_semantics=("arbitrary", "arbitrary"), vmem_limit_bytes=_vmem_limit(est)),
    )(qt, iqt, iwt, kk, vt, ik)


def _gla_kernel(gqk_ref, gv_ref, la_ref, gg_ref, gn_ref, o_ref, st_s):
    @pl.when(pl.program_id(1) == 0)
    def _():
        st_s[...] = jnp.zeros_like(st_s)

    c = GLA_CHUNK
    nqk = GLA_HEADS * GLA_DK
    r_i = lax.broadcasted_iota(jnp.int32, (c, c), 0)
    c_i = lax.broadcasted_iota(jnp.int32, (c, c), 1)
    tri = r_i >= c_i
    tri_f = tri.astype(F32)
    gn = gn_ref[...]
    nchunk = GLA_ROWS // c
    heads = range(GLA_HEADS)
    hs = [slice(h * GLA_DK, (h + 1) * GLA_DK) for h in heads]
    vs = [slice(h * GLA_DV, (h + 1) * GLA_DV) for h in heads]

    local = []
    for ci in range(nchunk):
        rows = slice(ci * c, (ci + 1) * c)
        la = la_ref[rows, :]
        b = jnp.dot(tri_f, la, preferred_element_type=F32, precision=lax.Precision.HIGHEST)
        b_last = b[c - 1:c, :]
        q = gqk_ref[rows, :nqk]
        k = gqk_ref[rows, nqk:]
        q_dec = (q * jnp.exp(b)).astype(BF16)
        k_in = (k * jnp.exp(-b)).astype(BF16)
        k_out = (k * jnp.exp(b_last - b)).astype(BF16)
        decay = jnp.exp(b_last)
        intra, upd = [], []
        for h in heads:
            v = gv_ref[rows, vs[h]]
            a = lax.dot_general(q_dec[:, hs[h]], k_in[:, hs[h]], NT_DIMS, preferred_element_type=F32)
            a = jnp.where(tri, a, 0.0).astype(BF16)
            intra.append(jnp.dot(a, v, preferred_element_type=F32))
            upd.append(lax.dot_general(v, k_out[:, hs[h]], TN_DIMS, preferred_element_type=F32))
        local.append((rows, q_dec, decay, intra, upd))

    st = [st_s[h] for h in heads]
    for rows, q_dec, decay, intra, upd in local:
        for h in heads:
            o = intra[h] + lax.dot_general(q_dec[:, hs[h]], st[h].astype(BF16), NT_DIMS,
                                           preferred_element_type=F32)
            st[h] = st[h] * decay[:, hs[h]] + upd[h]
            gate = gg_ref[rows, vs[h]]
            o_ref[rows, vs[h]] = (_rms(o, gn) * (gate * jax.nn.sigmoid(gate))).astype(o_ref.dtype)
    for h in heads:
        st_s[h] = st[h]


def _gla(gqk, gv, la, gg, g_norm, batch, seq):
    n = batch * seq
    t = GLA_ROWS
    assert seq % t == 0
    ns = seq // t
    row = lambda b, i: (b * ns + i, 0)
    nqk = GLA_HEADS * GLA_DK
    nv = GLA_HEADS * GLA_DV
    return pl.pallas_call(
        _gla_kernel,
        out_shape=jax.ShapeDtypeStruct((n, nv), BF16),
        grid=(batch, ns),
        in_specs=[
            pl.BlockSpec((t, 2 * nqk), row),
            pl.BlockSpec((t, nv), row),
            pl.BlockSpec((t, nqk), row),
            pl.BlockSpec((t, nv), row),
            _const_spec((1, GLA_DV)),
        ],
        out_specs=pl.BlockSpec((t, nv), row),
        scratch_shapes=[pltpu.VMEM((GLA_HEADS, GLA_DV, GLA_DK), F32)],
        compiler_params=pltpu.CompilerParams(dimension_semantics=("arbitrary", "arbitrary")),
    )(gqk, gv, la, gg, g_norm.reshape(1, GLA_DV))


def kernel(x, g_ffn1_pre, w_ffn1_gate, w_ffn1_up, w_ffn1_down, g_ffn1_post, g_mix_pre, w_in, w_gla_a2,
           b_gla_a, g_gla_norm, w_out, g_mix_post, g_ffn2_pre, w_ffn2_gate, w_ffn2_up, w_ffn2_down,
           g_ffn2_post):
    batch, seq, d = x.shape
    h = x.reshape(batch * seq, d)
    for l in range(g_ffn1_pre.shape[0]):
        h = _ffn(h, g_ffn1_pre[l], w_ffn1_gate[l], w_ffn1_up[l], w_ffn1_down[l], g_ffn1_post[l])
        qt, iqt, vt, iwt, kk, ik, gqk, gv, gg, la = _inproj(h, g_mix_pre[l], w_in[l], w_gla_a2[l], b_gla_a[l])
        oa = _dsa(qt, iqt, iwt, kk, vt, ik, batch, seq)
        og = _gla(gqk, gv, la, gg, g_gla_norm[l], batch, seq)
        h = _mix_ffn(h, oa, og, w_out[l], g_mix_post[l],
                     g_ffn2_pre[l], w_ffn2_gate[l], w_ffn2_up[l], w_ffn2_down[l], g_ffn2_post[l])
    return h.reshape(batch, seq, d)
```

```python
import functools

import jax
import jax.numpy as jnp
from jax import lax
from jax.experimental import pallas as pl
from jax.experimental.pallas import tpu as pltpu

ATTN_HEADS = 8
ATTN_KV_HEADS = 2
ATTN_HEAD_DIM = 64
ATTN_REP = ATTN_HEADS // ATTN_KV_HEADS
IDX_HEADS = 8
IDX_DIM = 64
TOPK_MAX = 256
GLA_HEADS = 4
GLA_DK = 64
GLA_DV = 128
GLA_GATE_RANK = 16
GLA_TAU = 16.0
GLA_CHUNK = 64
EPS = 1e-6

V7X_LANES = 128
V7X_SUBLANES = 8
V7X_BF16_ROWS = 16
V7X_MXU_DIM = 256
V7X_VMEM_BYTES = 64 * 2**20

FFN_ROWS = 512
FF_CHUNK = V7X_MXU_DIM
DSA_TQ = 256
DSA_KC = DSA_TQ
DSA_SCORE_GROUP = 4
DSA_COUNT_GROUP = 8
DSA_ATT_GROUP = 4
DSA_ATT_LEAD = 3
GLA_ROWS = 512
VT_ROWS = ATTN_HEAD_DIM + V7X_BF16_ROWS

INT_MIN = -2**31
INT16_MIN, INT16_MAX = -2**15, 2**15 - 1
NEG_BIG = -1e30
SOFTMAX_NORM_MIN = 2.0 ** -60
SOFTMAX_NORM_MAX = 2.0 ** 100

F32 = jnp.float32
BF16 = jnp.bfloat16
NT_DIMS = (((1,), (1,)), ((), ()))
TN_DIMS = (((0,), (0,)), ((), ()))


def _vmem_limit(nbytes):
    return int(min(nbytes * 1.25 + (8 << 20), V7X_VMEM_BYTES - (6 << 20)))


def _rms(x, g):
    return x * lax.rsqrt(jnp.mean(x * x, axis=-1, keepdims=True) + EPS) * g


def _const_spec(shape):
    nd = len(shape)
    return pl.BlockSpec(shape, lambda *_: (0,) * nd, pipeline_mode=pl.Buffered(1))


def _ffn_kernel(x_ref, gpre_ref, wg_ref, wu_ref, wd_ref, gpost_ref, o_ref):
    o_ref[...] = _ffn_residual(x_ref[...], gpre_ref, wg_ref, wu_ref, wd_ref, gpost_ref)


def _mix_ffn_kernel(x_ref, oa_ref, og_ref, wo_ref, gmix_ref, gpre_ref, wg_ref, wu_ref, wd_ref, gpost_ref, o_ref):
    m = jnp.dot(oa_ref[...], wo_ref[0], preferred_element_type=F32)
    m = m + jnp.dot(og_ref[...], wo_ref[1], preferred_element_type=F32)
    x2 = x_ref[...] + _rms(m, gmix_ref[...])
    o_ref[...] = _ffn_residual(x2, gpre_ref, wg_ref, wu_ref, wd_ref, gpost_ref)


def _ffn_residual(x, gpre_ref, wg_ref, wu_ref, wd_ref, gpost_ref):
    xn = _rms(x, gpre_ref[...]).astype(BF16)

    def cols(c):
        return slice(c * FF_CHUNK, (c + 1) * FF_CHUNK)

    def gate_up(c):
        return (jnp.dot(xn, wg_ref[:, cols(c)], preferred_element_type=F32),
                jnp.dot(xn, wu_ref[:, cols(c)], preferred_element_type=F32))

    nch = wg_ref.shape[1] // FF_CHUNK
    acc = None
    nxt = gate_up(0)
    for c in range(nch):
        g, u = nxt
        if c + 1 < nch:
            nxt = gate_up(c + 1)
        a = (g * jax.nn.sigmoid(g) * u).astype(BF16)
        d = jnp.dot(a, wd_ref[cols(c), :], preferred_element_type=F32)
        acc = d if acc is None else acc + d
    return x + 0.5 * _rms(acc, gpost_ref[...])


def _ffn_operands(d, g_pre, w_gate, w_up, w_down, g_post):
    dff = w_gate.shape[1]
    assert dff % FF_CHUNK == 0
    wg, wu, wd = w_gate.astype(BF16), w_up.astype(BF16), w_down.astype(BF16)
    specs = [_const_spec((1, d)), _const_spec(wg.shape), _const_spec(wu.shape), _const_spec(wd.shape),
             _const_spec((1, d))]
    args = (g_pre.reshape(1, d), wg, wu, wd, g_post.reshape(1, d))
    est = 4 * FFN_ROWS * d * 4 + 3 * d * dff * 2 + FFN_ROWS * d * 8 + 4 * FFN_ROWS * FF_CHUNK * 4
    return specs, args, est


def _ffn(x, g_pre, w_gate, w_up, w_down, g_post):
    n, d = x.shape
    assert n % FFN_ROWS == 0
    ffn_specs, ffn_args, est = _ffn_operands(d, g_pre, w_gate, w_up, w_down, g_post)
    x_spec = pl.BlockSpec((FFN_ROWS, d), lambda i: (i, 0))
    return pl.pallas_call(
        _ffn_kernel,
        out_shape=jax.ShapeDtypeStruct((n, d), F32),
        grid=(n // FFN_ROWS,),
        in_specs=[x_spec] + ffn_specs,
        out_specs=x_spec,
        compiler_params=pltpu.CompilerParams(
            dimension_semantics=("arbitrary",), vmem_limit_bytes=_vmem_limit(est)),
    )(x, *ffn_args)


def _mix_ffn(x, oa, og, w_out, g_mix, g_pre, w_gate, w_up, w_down, g_post):
    n, d = x.shape
    assert n % FFN_ROWS == 0
    ffn_specs, ffn_args, est = _ffn_operands(d, g_pre, w_gate, w_up, w_down, g_post)
    half = oa.shape[1]
    assert og.shape[1] == half and w_out.shape == (2 * half, d)
    wo = w_out.reshape(2, half, d).astype(BF16)
    row = lambda i: (i, 0)
    x_spec = pl.BlockSpec((FFN_ROWS, d), row)
    o_spec = pl.BlockSpec((FFN_ROWS, half), row)
    est += 4 * FFN_ROWS * half * 2 + 2 * half * d * 2 + FFN_ROWS * d * 4
    return pl.pallas_call(
        _mix_ffn_kernel,
        out_shape=jax.ShapeDtypeStruct((n, d), F32),
        grid=(n // FFN_ROWS,),
        in_specs=[x_spec, o_spec, o_spec, _const_spec(wo.shape), _const_spec((1, d))] + ffn_specs,
        out_specs=x_spec,
        compiler_params=pltpu.CompilerParams(
            dimension_semantics=("arbitrary",), vmem_limit_bytes=_vmem_limit(est)),
    )(x, oa, og, wo, g_mix.reshape(1, d), *ffn_args)


_K_OFF, _MISC_OFF, _GQ_OFF, _GK_OFF, _GV_OFF, _GG_OFF, _TOK_COLS = 0, 128, 256, 512, 768, 1280, 1792
_GA_LANE = 72
_QT_OFF, _IQT_OFF, _VT_OFF, _IWT_OFF, _T_ROWS = 0, 512, 1024, 1152, 1168


def _inproj_kernel(x_ref, g_ref, wtok_ref, wt_ref, wa2_ref, ba_ref,
                   qt_ref, iqt_ref, vt_ref, iwt_ref, kk_ref, ik_ref, gqk_ref, gv_ref, gg_ref, la_ref):
    h = _rms(x_ref[...], g_ref[...]).astype(BF16)
    t = h.shape[0]
    pt = lax.dot_general(wt_ref[...], h, NT_DIMS, preferred_element_type=F32)
    for i in range(ATTN_HEADS):
        qt_ref[i] = pt[_QT_OFF + 64 * i:_QT_OFF + 64 * (i + 1)].astype(BF16)
    for i in range(IDX_HEADS):
        iqt_ref[i] = pt[_IQT_OFF + 64 * i:_IQT_OFF + 64 * (i + 1)].astype(BF16)
    ones_row = (lax.broadcasted_iota(jnp.int32, (V7X_BF16_ROWS, t), 0) == 0).astype(F32)
    for g in range(ATTN_KV_HEADS):
        v_t = pt[_VT_OFF + 64 * g:_VT_OFF + 64 * (g + 1)]
        vt_ref[g] = jnp.concatenate([v_t, ones_row], axis=0).astype(BF16)
    iwt_ref[...] = pt[_IWT_OFF:_IWT_OFF + IDX_HEADS] * (IDX_HEADS ** -0.5)

    proj = jnp.dot(h, wtok_ref[...], preferred_element_type=F32)
    for g in range(ATTN_KV_HEADS):
        kk_ref[g] = proj[:, _K_OFF + 64 * g:_K_OFF + 64 * (g + 1)].astype(BF16)
    misc = proj[:, _MISC_OFF:_MISC_OFF + V7X_LANES]
    ik_ref[...] = misc[:, :IDX_DIM].astype(BF16)
    gqk_ref[...] = proj[:, _GQ_OFF:_GV_OFF]
    gv_ref[...] = proj[:, _GV_OFF:_GG_OFF].astype(BF16)
    gg_ref[...] = proj[:, _GG_OFF:_GG_OFF + GLA_HEADS * GLA_DV]
    z = jnp.dot(misc, wa2_ref[...], preferred_element_type=F32, precision=lax.Precision.HIGHEST)
    z = z + ba_ref[...]
    log_sig = jnp.minimum(z, 0.0) - jnp.log1p(jnp.exp(-jnp.abs(z)))
    la_ref[...] = log_sig * (1.0 / GLA_TAU)


def _pack_w_in(w_in):
    d = w_in.shape[0]
    sizes = (512, 128, 128, 512, 64, 8, 256, 256, 512, 16, 512)
    offs = [0]
    for s in sizes:
        offs.append(offs[-1] + s)
    aq, ak, av, iq, ik, iw, gq, gk, gv, ga, gg = [w_in[:, offs[i]:offs[i + 1]] for i in range(len(sizes))]
    z = lambda n: jnp.zeros((d, n), w_in.dtype)
    w_tok = jnp.concatenate([ak, ik, z(8), ga, z(40), gq * (GLA_DK ** -0.5), gk, gv, gg], axis=1)
    w_t = jnp.concatenate([aq * (ATTN_HEAD_DIM ** -0.5), iq * (IDX_DIM ** -0.5), av, iw, z(8)], axis=1).T
    assert w_tok.shape[1] == _TOK_COLS and w_t.shape[0] == _T_ROWS
    return w_tok.astype(BF16), w_t.astype(BF16)


def _inproj(x1, g_mix_pre, w_in, w_gla_a2, b_gla_a):
    n, d = x1.shape
    t = FFN_ROWS
    assert n % t == 0
    w_tok, w_t = _pack_w_in(w_in)
    nqk = GLA_HEADS * GLA_DK
    nv = GLA_HEADS * GLA_DV
    wa2 = jnp.zeros((V7X_LANES, nqk), F32).at[_GA_LANE:_GA_LANE + GLA_GATE_RANK].set(w_gla_a2)
    row = lambda i: (i, 0)
    row3 = lambda i: (0, i, 0)
    col3 = lambda i: (0, 0, i)
    out_shape = (
        jax.ShapeDtypeStruct((ATTN_HEADS, ATTN_HEAD_DIM, n), BF16),
        jax.ShapeDtypeStruct((IDX_HEADS, IDX_DIM, n), BF16),
        jax.ShapeDtypeStruct((ATTN_KV_HEADS, VT_ROWS, n), BF16),
        jax.ShapeDtypeStruct((IDX_HEADS, n), F32),
        jax.ShapeDtypeStruct((ATTN_KV_HEADS, n, ATTN_HEAD_DIM), BF16),
        jax.ShapeDtypeStruct((n, IDX_DIM), BF16),
        jax.ShapeDtypeStruct((n, 2 * nqk), F32),
        jax.ShapeDtypeStruct((n, nv), BF16),
        jax.ShapeDtypeStruct((n, nv), F32),
        jax.ShapeDtypeStruct((n, nqk), F32),
    )
    out_specs = (
        pl.BlockSpec((ATTN_HEADS, ATTN_HEAD_DIM, t), col3),
        pl.BlockSpec((IDX_HEADS, IDX_DIM, t), col3),
        pl.BlockSpec((ATTN_KV_HEADS, VT_ROWS, t), col3),
        pl.BlockSpec((IDX_HEADS, t), lambda i: (0, i)),
        pl.BlockSpec((ATTN_KV_HEADS, t, ATTN_HEAD_DIM), row3),
        pl.BlockSpec((t, IDX_DIM), row),
        pl.BlockSpec((t, 2 * nqk), row),
        pl.BlockSpec((t, nv), row),
        pl.BlockSpec((t, nv), row),
        pl.BlockSpec((t, nqk), row),
    )
    est = (2 * t * d * 4 + d * (_TOK_COLS + _T_ROWS) * 2 + t * (_TOK_COLS + _T_ROWS) * 4 * 2
           + 2 * t * 8 * 1024)
    return pl.pallas_call(
        _inproj_kernel,
        out_shape=out_shape,
        grid=(n // t,),
        in_specs=[
            pl.BlockSpec((t, d), row),
            _const_spec((1, d)),
            _const_spec(w_tok.shape),
            _const_spec(w_t.shape),
            _const_spec(wa2.shape),
            _const_spec((1, nqk)),
        ],
        out_specs=out_specs,
        compiler_params=pltpu.CompilerParams(
            dimension_semantics=("arbitrary",), vmem_limit_bytes=_vmem_limit(est)),
    )(x1, g_mix_pre.reshape(1, d), w_tok, w_t, wa2, b_gla_a.reshape(1, nqk))


def _dsa_kernel(qt_ref, iqt_ref, iwt_ref, kk_ref, vt_ref, ik_ref, o_ref,
                key_s, half_s, low_s, cnt_s, seen_s, bias_s, acc_s, ot_s, *, topk):
    tq = DSA_TQ
    kc = DSA_KC
    sub = V7X_SUBLANES
    i = pl.program_id(1)
    nchunks = i + 1

    def score_logits(j):
        ikc = ik_ref[pl.ds(pl.multiple_of(j * kc, kc), kc), :]
        iqcat = jnp.concatenate([iqt_ref[h] for h in range(IDX_HEADS)], axis=1)
        return jnp.dot(ikc, iqcat, preferred_element_type=F32)

    def score_keys(j, lg, diagonal):
        sc = jnp.zeros((kc, tq), F32)
        for h in range(IDX_HEADS):
            sc = sc + jnp.maximum(lg[:, h * tq:(h + 1) * tq], 0.0) * iwt_ref[h:h + 1, :]
        bits = pltpu.bitcast(sc, jnp.int32)
        key = jnp.where(bits < 0, bits ^ 0x7FFFFFFF, bits)
        if diagonal:
            kpos = lax.broadcasted_iota(jnp.int32, (kc, tq), 0)
            qpos = lax.broadcasted_iota(jnp.int32, (kc, tq), 1)
            key = jnp.where(kpos <= qpos, key, INT_MIN)
        key_s[j] = key
        half_s[j] = lax.shift_right_arithmetic(key, 16).astype(jnp.int16)
        low_s[j] = key.astype(jnp.int16) ^ jnp.int16(INT16_MIN)

    def sweep_chunks(n, group_fn, gsz):
        assert gsz & (gsz - 1) == 0

        def trip(q, c):
            group_fn(q * gsz, gsz)
            return c

        lax.fori_loop(0, lax.shift_right_logical(n, gsz.bit_length() - 1), trip, 0)
        done = n & -gsz
        part = gsz // 2
        while part:
            pl.when((n & part) != 0)(functools.partial(group_fn, done, part))
            done = done + (n & part)
            part //= 2

    def score_group(j0, count):
        lgs = [score_logits(j0 + s) for s in range(count)]
        for s in range(count):
            score_keys(j0 + s, lgs[s], False)

    sweep_chunks(i, score_group, DSA_SCORE_GROUP)
    score_keys(i, score_logits(i), True)

    qidx = i * tq + lax.broadcasted_iota(jnp.int32, (1, tq), 1)
    krow = jnp.minimum(topk, qidx + 1).astype(F32)

    def count(ind_fn):
        def body(j, acc):
            parts = [acc, None, None, None]
            for s in range(kc // sub):
                ind = ind_fn(key_s[j, s * sub:(s + 1) * sub, :], j * kc + s * sub)
                p = s % 4
                parts[p] = ind if parts[p] is None else parts[p] + ind
            return (parts[0] + parts[1]) + (parts[2] + parts[3])
        acc = lax.fori_loop(0, nchunks, body, jnp.zeros((sub, tq), F32))
        return jnp.sum(acc, axis=0, keepdims=True)

    def count_ge(cand):
        cb = jnp.broadcast_to(cand, (sub, tq))
        return count(lambda k, base: jnp.where(k >= cb, 1.0, 0.0))

    rows16 = V7X_BF16_ROWS

    def count16(cand):
        cb = jnp.broadcast_to(cand, (rows16, tq)).astype(jnp.int16)
        one, nil = jnp.int16(1), jnp.int16(0)
        cnt_s[...] = jnp.zeros_like(cnt_s)

        def group(j0, count):
            parts = [None] * 4
            for s in range(count * kc // rows16):
                c, r = divmod(s, kc // rows16)
                ind = jnp.where(half_s[j0 + c, r * rows16:(r + 1) * rows16, :] >= cb, one, nil)
                p = s % 4
                parts[p] = ind if parts[p] is None else parts[p] + ind
            cnt_s[...] += (parts[0] + parts[1]) + (parts[2] + parts[3])

        sweep_chunks(nchunks, group, DSA_COUNT_GROUP)
        return jnp.sum(cnt_s[...].astype(F32), axis=0, keepdims=True)

    def search16(count_at_min):
        zero = jnp.zeros((1, tq), jnp.int32)
        c0 = count16(zero)
        start = (jnp.where(c0 >= krow, zero, INT16_MIN), jnp.where(c0 >= krow, c0, count_at_min))

        def bit_body(it, carry):
            th, cth = carry
            cand = th | lax.shift_left(jnp.int32(1), 14 - it)
            c = count16(cand)
            return jnp.where(c >= krow, cand, th), jnp.where(c >= krow, c, cth)

        return lax.fori_loop(0, 15, bit_body, start)

    t_hi, c_hi = search16(jnp.full((1, tq), 1.0, F32) * (nchunks * kc).astype(F32))
    th16 = jnp.broadcast_to(t_hi, (kc, tq)).astype(jnp.int16)

    def low_halves(j0, count):
        for j in range(count):
            hi = half_s[j0 + j]
            half_s[j0 + j] = jnp.where(hi == th16, low_s[j0 + j],
                                       jnp.where(hi > th16, jnp.int16(INT16_MAX), jnp.int16(INT16_MIN)))

    sweep_chunks(nchunks, low_halves, DSA_SCORE_GROUP)
    t_lo, c_ge = search16(c_hi)
    t = lax.shift_left(t_hi, 16) | (t_lo - INT16_MIN)

    saturated = jnp.max(jnp.where(t_lo == INT16_MAX, 1.0, 0.0)) > 0.5
    c_gt = lax.cond(saturated, lambda: count_ge(t + 1), lambda: count16(jnp.minimum(t_lo + 1, INT16_MAX)))
    need = krow - c_gt

    @pl.when(jnp.max(c_ge - krow) > 0.5)
    def _():
        tb_full = jnp.broadcast_to(t, (kc, tq))
        tri = (lax.broadcasted_iota(jnp.int32, (kc, kc), 0)
               >= lax.broadcasted_iota(jnp.int32, (kc, kc), 1)).astype(BF16)

        seen_s[...] = jnp.zeros_like(seen_s)

        def drop_surplus(j0, count):
            ks = [key_s[j0 + s] for s in range(count)]
            prefix = [jnp.dot(tri, jnp.where(k == tb_full, 1.0, 0.0).astype(BF16), preferred_element_type=F32)
                      for k in ks]
            seen = seen_s[...]
            for s in range(count):
                rank = seen + prefix[s]
                key_s[j0 + s] = jnp.where(ks[s] == tb_full, jnp.where(rank > need, INT_MIN, ks[s]), ks[s])
                seen = rank[kc - 1:kc, :]
            seen_s[...] = seen

        sweep_chunks(nchunks, drop_surplus, DSA_SCORE_GROUP)

    def masked_scores(j, g, slot=0):
        koff = pl.multiple_of(j * kc, kc)
        qcat = jnp.concatenate([qt_ref[g * ATTN_REP + r] for r in range(ATTN_REP)], axis=1)
        bias = jnp.concatenate([bias_s[slot]] * ATTN_REP, axis=1)
        kg = kk_ref[g, pl.ds(koff, kc), :]
        vg = vt_ref[g, :, pl.ds(koff, kc)]
        return jnp.dot(kg, qcat, preferred_element_type=F32) + bias, vg

    def set_bias(j, slot=0):
        bias_s[slot] = jnp.where(key_s[j] >= jnp.broadcast_to(t, (kc, tq)), 0.0, NEG_BIG)

    acc_s[...] = jnp.zeros_like(acc_s)

    def att_group(j0, count):
        units = [(slot, g) for slot in range(count) for g in range(ATTN_KV_HEADS)]
        for slot in range(count):
            set_bias(j0 + slot, slot)
        scores = {}

        def issue(u):
            slot, g = units[u]
            scores[u] = masked_scores(j0 + slot, g, slot)

        for u in range(min(DSA_ATT_LEAD, len(units))):
            issue(u)
        for u, (slot, g) in enumerate(units):
            s, vg = scores.pop(u)
            acc_s[g] += jnp.dot(vg, jnp.exp(s).astype(BF16), preferred_element_type=F32)
            if u + DSA_ATT_LEAD < len(units):
                issue(u + DSA_ATT_LEAD)

    sweep_chunks(nchunks, att_group, DSA_ATT_GROUP)

    norm = acc_s[:, ATTN_HEAD_DIM:ATTN_HEAD_DIM + 1, :]
    in_range = jnp.where(norm >= SOFTMAX_NORM_MIN, jnp.where(norm <= SOFTMAX_NORM_MAX, 1.0, 0.0), 0.0)

    @pl.when(jnp.min(in_range) < 0.5)
    def _():
        acc_s[...] = jnp.zeros_like(acc_s)

        def online_body(j, ms):
            set_bias(j)
            out = []
            for g in range(ATTN_KV_HEADS):
                s, vg = masked_scores(j, g)
                m_new = jnp.maximum(ms[g], jnp.max(s, axis=0, keepdims=True))
                p = jnp.exp(s - m_new).astype(BF16)
                pv = jnp.dot(vg, p, preferred_element_type=F32)
                acc_s[g] = jnp.exp(ms[g] - m_new) * acc_s[g] + pv
                out.append(m_new)
            return tuple(out)

        m_init = tuple(jnp.full((1, ATTN_REP * tq), NEG_BIG, F32) for _ in range(ATTN_KV_HEADS))
        lax.fori_loop(0, nchunks, online_body, m_init)

    for h in range(ATTN_HEADS):
        g, r = divmod(h, ATTN_REP)
        a = acc_s[g, :, r * tq:(r + 1) * tq]
        ot_s[h * ATTN_HEAD_DIM:(h + 1) * ATTN_HEAD_DIM, :] = (
            a[:ATTN_HEAD_DIM] / a[ATTN_HEAD_DIM:ATTN_HEAD_DIM + 1])
    o_ref[...] = ot_s[...].T.astype(o_ref.dtype)


def _dsa(qt, iqt, iwt, kk, vt, ik, batch, seq):
    n = batch * seq
    tq, kc = DSA_TQ, DSA_KC
    assert seq % tq == 0
    nq = seq // tq
    topk = min(TOPK_MAX, seq // 4)
    nch = seq // kc
    nout = ATTN_HEADS * ATTN_HEAD_DIM
    qmap = lambda b, i: (0, 0, b * nq + i)
    est = (nch * kc * tq * 8 + 3 * DSA_ATT_GROUP * ATTN_REP * kc * tq * 4 + ATTN_HEADS * (VT_ROWS + 8) * tq * 4 + nout * tq * 4
           + ATTN_KV_HEADS * seq * (V7X_LANES + VT_ROWS) * 2 + seq * V7X_LANES * 2
           + 2 * (2 * ATTN_HEADS * ATTN_HEAD_DIM * tq * 2 + 8 * tq * 4 + tq * nout * 2))
    return pl.pallas_call(
        functools.partial(_dsa_kernel, topk=topk),
        out_shape=jax.ShapeDtypeStruct((n, nout), BF16),
        grid=(batch, nq),
        in_specs=[
            pl.BlockSpec((ATTN_HEADS, ATTN_HEAD_DIM, tq), qmap),
            pl.BlockSpec((IDX_HEADS, IDX_DIM, tq), qmap),
            pl.BlockSpec((IDX_HEADS, tq), lambda b, i: (0, b * nq + i)),
            pl.BlockSpec((ATTN_KV_HEADS, seq, ATTN_HEAD_DIM), lambda b, i: (0, b, 0),
                         pipeline_mode=pl.Buffered(1)),
            pl.BlockSpec((ATTN_KV_HEADS, VT_ROWS, seq), lambda b, i: (0, 0, b),
                         pipeline_mode=pl.Buffered(1)),
            pl.BlockSpec((seq, IDX_DIM), lambda b, i: (b, 0), pipeline_mode=pl.Buffered(1)),
        ],
        out_specs=pl.BlockSpec((tq, nout), lambda b, i: (b * nq + i, 0)),
        scratch_shapes=[
            pltpu.VMEM((nch, kc, tq), jnp.int32),
            pltpu.VMEM((nch, kc, tq), jnp.int16),
            pltpu.VMEM((nch, kc, tq), jnp.int16),
            pltpu.VMEM((V7X_BF16_ROWS, tq), jnp.int16),
            pltpu.VMEM((1, tq), F32),
            pltpu.VMEM((DSA_ATT_GROUP, kc, tq), F32),
            pltpu.VMEM((ATTN_KV_HEADS, VT_ROWS, ATTN_REP * tq), F32),
            pltpu.VMEM((nout, tq), F32),
        ],
        compiler_params=pltpu.CompilerParams(
            dimension_semantics=("arbitrary", "arbitrary"), vmem_limit_bytes=_vmem_limit(est)),
    )(qt, iqt, iwt, kk, vt, ik)


def _gla_kernel(gqk_ref, gv_ref, la_ref, gg_ref, gn_ref, o_ref, st_s):
    @pl.when(pl.program_id(1) == 0)
    def _():
        st_s[...] = jnp.zeros_like(st_s)

    c = GLA_CHUNK
    nqk = GLA_HEADS * GLA_DK
    r_i = lax.broadcasted_iota(jnp.int32, (c, c), 0)
    c_i = lax.broadcasted_iota(jnp.int32, (c, c), 1)
    tri = r_i >= c_i
    tri_f = tri.astype(F32)
    gn = gn_ref[...]
    nchunk = GLA_ROWS // c
    heads = range(GLA_HEADS)
    hs = [slice(h * GLA_DK, (h + 1) * GLA_DK) for h in heads]
    vs = [slice(h * GLA_DV, (h + 1) * GLA_DV) for h in heads]

    local = []
    for ci in range(nchunk):
        rows = slice(ci * c, (ci + 1) * c)
        la = la_ref[rows, :]
        b = jnp.dot(tri_f, la, preferred_element_type=F32, precision=lax.Precision.HIGHEST)
        b_last = b[c - 1:c, :]
        q = gqk_ref[rows, :nqk]
        k = gqk_ref[rows, nqk:]
        q_dec = (q * jnp.exp(b)).astype(BF16)
        k_in = (k * jnp.exp(-b)).astype(BF16)
        k_out = (k * jnp.exp(b_last - b)).astype(BF16)
        decay = jnp.exp(b_last)
        intra, upd = [], []
        for h in heads:
            v = gv_ref[rows, vs[h]]
            a = lax.dot_general(q_dec[:, hs[h]], k_in[:, hs[h]], NT_DIMS, preferred_element_type=F32)
            a = jnp.where(tri, a, 0.0).astype(BF16)
            intra.append(jnp.dot(a, v, preferred_element_type=F32))
            upd.append(lax.dot_general(v, k_out[:, hs[h]], TN_DIMS, preferred_element_type=F32))
        local.append((rows, q_dec, decay, intra, upd))

    st = [st_s[h] for h in heads]
    for rows, q_dec, decay, intra, upd in local:
        for h in heads:
            o = intra[h] + lax.dot_general(q_dec[:, hs[h]], st[h].astype(BF16), NT_DIMS,
                                           preferred_element_type=F32)
            st[h] = st[h] * decay[:, hs[h]] + upd[h]
            gate = gg_ref[rows, vs[h]]
            o_ref[rows, vs[h]] = (_rms(o, gn) * (gate * jax.nn.sigmoid(gate))).astype(o_ref.dtype)
    for h in heads:
        st_s[h] = st[h]


def _gla(gqk, gv, la, gg, g_norm, batch, seq):
    n = batch * seq
    t = GLA_ROWS
    assert seq % t == 0
    ns = seq // t
    row = lambda b, i: (b * ns + i, 0)
    nqk = GLA_HEADS * GLA_DK
    nv = GLA_HEADS * GLA_DV
    return pl.pallas_call(
        _gla_kernel,
        out_shape=jax.ShapeDtypeStruct((n, nv), BF16),
        grid=(batch, ns),
        in_specs=[
            pl.BlockSpec((t, 2 * nqk), row),
            pl.BlockSpec((t, nv), row),
            pl.BlockSpec((t, nqk), row),
            pl.BlockSpec((t, nv), row),
            _const_spec((1, GLA_DV)),
        ],
        out_specs=pl.BlockSpec((t, nv), row),
        scratch_shapes=[pltpu.VMEM((GLA_HEADS, GLA_DV, GLA_DK), F32)],
        compiler_params=pltpu.CompilerParams(dimension_semantics=("arbitrary", "arbitrary")),
    )(gqk, gv, la, gg, g_norm.reshape(1, GLA_DV))


def kernel(x, g_ffn1_pre, w_ffn1_gate, w_ffn1_up, w_ffn1_down, g_ffn1_post, g_mix_pre, w_in, w_gla_a2,
           b_gla_a, g_gla_norm, w_out, g_mix_post, g_ffn2_pre, w_ffn2_gate, w_ffn2_up, w_ffn2_down,
           g_ffn2_post):
    batch, seq, d = x.shape
    h = x.reshape(batch * seq, d)
    for l in range(g_ffn1_pre.shape[0]):
        h = _ffn(h, g_ffn1_pre[l], w_ffn1_gate[l], w_ffn1_up[l], w_ffn1_down[l], g_ffn1_post[l])
        qt, iqt, vt, iwt, kk, ik, gqk, gv, gg, la = _inproj(h, g_mix_pre[l], w_in[l], w_gla_a2[l], b_gla_a[l])
        oa = _dsa(qt, iqt, iwt, kk, vt, ik, batch, seq)
        og = _gla(gqk, gv, la, gg, g_gla_norm[l], batch, seq)
        h = _mix_ffn(h, oa, og, w_out[l], g_mix_post[l],
                     g_ffn2_pre[l], w_ffn2_gate[l], w_ffn2_up[l], w_ffn2_down[l], g_ffn2_post[l])
    return h.reshape(batch, seq, d)
```

```python
import functools

import jax
import jax.numpy as jnp
from jax import lax
from jax.experimental import pallas as pl
from jax.experimental.pallas import tpu as pltpu

ATTN_HEADS = 8
ATTN_KV_HEADS = 2
ATTN_HEAD_DIM = 64
ATTN_REP = ATTN_HEADS // ATTN_KV_HEADS
IDX_HEADS = 8
IDX_DIM = 64
TOPK_MAX = 256
GLA_HEADS = 4
GLA_DK = 64
GLA_DV = 128
GLA_GATE_RANK = 16
GLA_TAU = 16.0
GLA_CHUNK = 64
EPS = 1e-6

V7X_LANES = 128
V7X_SUBLANES = 8
V7X_BF16_ROWS = 16
V7X_MXU_DIM = 256
V7X_VMEM_BYTES = 64 * 2**20

FFN_ROWS = 512
FF_CHUNK = V7X_MXU_DIM
DSA_TQ = 256
DSA_KC = DSA_TQ
DSA_SCORE_GROUP = 4
DSA_COUNT_GROUP = 8
DSA_ATT_GROUP = 4
DSA_ATT_LEAD = 3
GLA_ROWS = 512
VT_ROWS = ATTN_HEAD_DIM + V7X_BF16_ROWS

INT_MIN = -2**31
INT16_MIN, INT16_MAX = -2**15, 2**15 - 1
NEG_BIG = -1e30
SOFTMAX_NORM_MIN = 2.0 ** -60
SOFTMAX_NORM_MAX = 2.0 ** 100

F32 = jnp.float32
BF16 = jnp.bfloat16
NT_DIMS = (((1,), (1,)), ((), ()))
TN_DIMS = (((0,), (0,)), ((), ()))


def _vmem_limit(nbytes):
    return int(min(nbytes * 1.25 + (8 << 20), V7X_VMEM_BYTES - (6 << 20)))


def _rms(x, g):
    return x * lax.rsqrt(jnp.mean(x * x, axis=-1, keepdims=True) + EPS) * g


def _const_spec(shape):
    nd = len(shape)
    return pl.BlockSpec(shape, lambda *_: (0,) * nd, pipeline_mode=pl.Buffered(1))


def _ffn_kernel(x_ref, gpre_ref, wg_ref, wu_ref, wd_ref, gpost_ref, o_ref):
    o_ref[...] = _ffn_residual(x_ref[...], gpre_ref, wg_ref, wu_ref, wd_ref, gpost_ref)


def _mix_ffn_kernel(x_ref, oa_ref, og_ref, wo_ref, gmix_ref, gpre_ref, wg_ref, wu_ref, wd_ref, gpost_ref, o_ref):
    m = jnp.dot(oa_ref[...], wo_ref[0], preferred_element_type=F32)
    m = m + jnp.dot(og_ref[...], wo_ref[1], preferred_element_type=F32)
    x2 = x_ref[...] + _rms(m, gmix_ref[...])
    o_ref[...] = _ffn_residual(x2, gpre_ref, wg_ref, wu_ref, wd_ref, gpost_ref)


def _ffn_residual(x, gpre_ref, wg_ref, wu_ref, wd_ref, gpost_ref):
    xn = _rms(x, gpre_ref[...]).astype(BF16)

    def cols(c):
        return slice(c * FF_CHUNK, (c + 1) * FF_CHUNK)

    def gate_up(c):
        return (jnp.dot(xn, wg_ref[:, cols(c)], preferred_element_type=F32),
                jnp.dot(xn, wu_ref[:, cols(c)], preferred_element_type=F32))

    nch = wg_ref.shape[1] // FF_CHUNK
    acc = None
    nxt = gate_up(0)
    for c in range(nch):
        g, u = nxt
        if c + 1 < nch:
            nxt = gate_up(c + 1)
        a = (g * jax.nn.sigmoid(g) * u).astype(BF16)
        d = jnp.dot(a, wd_ref[cols(c), :], preferred_element_type=F32)
        acc = d if acc is None else acc + d
    return x + 0.5 * _rms(acc, gpost_ref[...])


def _ffn_operands(d, g_pre, w_gate, w_up, w_down, g_post):
    dff = w_gate.shape[1]
    assert dff % FF_CHUNK == 0
    wg, wu, wd = w_gate.astype(BF16), w_up.astype(BF16), w_down.astype(BF16)
    specs = [_const_spec((1, d)), _const_spec(wg.shape), _const_spec(wu.shape), _const_spec(wd.shape),
             _const_spec((1, d))]
    args = (g_pre.reshape(1, d), wg, wu, wd, g_post.reshape(1, d))
    est = 4 * FFN_ROWS * d * 4 + 3 * d * dff * 2 + FFN_ROWS * d * 8 + 4 * FFN_ROWS * FF_CHUNK * 4
    return specs, args, est


def _ffn(x, g_pre, w_gate, w_up, w_down, g_post):
    n, d = x.shape
    assert n % FFN_ROWS == 0
    ffn_specs, ffn_args, est = _ffn_operands(d, g_pre, w_gate, w_up, w_down, g_post)
    x_spec = pl.BlockSpec((FFN_ROWS, d), lambda i: (i, 0))
    return pl.pallas_call(
        _ffn_kernel,
        out_shape=jax.ShapeDtypeStruct((n, d), F32),
        grid=(n // FFN_ROWS,),
        in_specs=[x_spec] + ffn_specs,
        out_specs=x_spec,
        compiler_params=pltpu.CompilerParams(
            dimension_semantics=("arbitrary",), vmem_limit_bytes=_vmem_limit(est)),
    )(x, *ffn_args)


def _mix_ffn(x, oa, og, w_out, g_mix, g_pre, w_gate, w_up, w_down, g_post):
    n, d = x.shape
    assert n % FFN_ROWS == 0
    ffn_specs, ffn_args, est = _ffn_operands(d, g_pre, w_gate, w_up, w_down, g_post)
    half = oa.shape[1]
    assert og.shape[1] == half and w_out.shape == (2 * half, d)
    wo = w_out.reshape(2, half, d).astype(BF16)
    row = lambda i: (i, 0)
    x_spec = pl.BlockSpec((FFN_ROWS, d), row)
    o_spec = pl.BlockSpec((FFN_ROWS, half), row)
    est += 4 * FFN_ROWS * half * 2 + 2 * half * d * 2 + FFN_ROWS * d * 4
    return pl.pallas_call(
        _mix_ffn_kernel,
        out_shape=jax.ShapeDtypeStruct((n, d), F32),
        grid=(n // FFN_ROWS,),
        in_specs=[x_spec, o_spec, o_spec, _const_spec(wo.shape), _const_spec((1, d))] + ffn_specs,
        out_specs=x_spec,
        compiler_params=pltpu.CompilerParams(
            dimension_semantics=("arbitrary",), vmem_limit_bytes=_vmem_limit(est)),
    )(x, oa, og, wo, g_mix.reshape(1, d), *ffn_args)


_K_OFF, _MISC_OFF, _GQ_OFF, _GK_OFF, _GV_OFF, _GG_OFF, _TOK_COLS = 0, 128, 256, 512, 768, 1280, 1792
_GA_LANE = 72
_QT_OFF, _IQT_OFF, _VT_OFF, _IWT_OFF, _T_ROWS = 0, 512, 1024, 1152, 1168


def _inproj_kernel(x_ref, g_ref, wtok_ref, wt_ref, wa2_ref, ba_ref,
                   qt_ref, iqt_ref, vt_ref, iwt_ref, kk_ref, ik_ref, gqk_ref, gv_ref, gg_ref, la_ref):
    h = _rms(x_ref[...], g_ref[...]).astype(BF16)
    t = h.shape[0]
    pt = lax.dot_general(wt_ref[...], h, NT_DIMS, preferred_element_type=F32)
    for i in range(ATTN_HEADS):
        qt_ref[i] = pt[_QT_OFF + 64 * i:_QT_OFF + 64 * (i + 1)].astype(BF16)
    for i in range(IDX_HEADS):
        iqt_ref[i] = pt[_IQT_OFF + 64 * i:_IQT_OFF + 64 * (i + 1)].astype(BF16)
    ones_row = (lax.broadcasted_iota(jnp.int32, (V7X_BF16_ROWS, t), 0) == 0).astype(F32)
    for g in range(ATTN_KV_HEADS):
        v_t = pt[_VT_OFF + 64 * g:_VT_OFF + 64 * (g + 1)]
        vt_ref[g] = jnp.concatenate([v_t, ones_row], axis=0).astype(BF16)
    iwt_ref[...] = pt[_IWT_OFF:_IWT_OFF + IDX_HEADS] * (IDX_HEADS ** -0.5)

    proj = jnp.dot(h, wtok_ref[...], preferred_element_type=F32)
    for g in range(ATTN_KV_HEADS):
        kk_ref[g] = proj[:, _K_OFF + 64 * g:_K_OFF + 64 * (g + 1)].astype(BF16)
    misc = proj[:, _MISC_OFF:_MISC_OFF + V7X_LANES]
    ik_ref[...] = misc[:, :IDX_DIM].astype(BF16)
    gqk_ref[...] = proj[:, _GQ_OFF:_GV_OFF]
    gv_ref[...] = proj[:, _GV_OFF:_GG_OFF].astype(BF16)
    gg_ref[...] = proj[:, _GG_OFF:_GG_OFF + GLA_HEADS * GLA_DV]
    z = jnp.dot(misc, wa2_ref[...], preferred_element_type=F32, precision=lax.Precision.HIGHEST)
    z = z + ba_ref[...]
    log_sig = jnp.minimum(z, 0.0) - jnp.log1p(jnp.exp(-jnp.abs(z)))
    la_ref[...] = log_sig * (1.0 / GLA_TAU)


def _pack_w_in(w_in):
    d = w_in.shape[0]
    sizes = (512, 128, 128, 512, 64, 8, 256, 256, 512, 16, 512)
    offs = [0]
    for s in sizes:
        offs.append(offs[-1] + s)
    aq, ak, av, iq, ik, iw, gq, gk, gv, ga, gg = [w_in[:, offs[i]:offs[i + 1]] for i in range(len(sizes))]
    z = lambda n: jnp.zeros((d, n), w_in.dtype)
    w_tok = jnp.concatenate([ak, ik, z(8), ga, z(40), gq * (GLA_DK ** -0.5), gk, gv, gg], axis=1)
    w_t = jnp.concatenate([aq * (ATTN_HEAD_DIM ** -0.5), iq * (IDX_DIM ** -0.5), av, iw, z(8)], axis=1).T
    assert w_tok.shape[1] == _TOK_COLS and w_t.shape[0] == _T_ROWS
    return w_tok.astype(BF16), w_t.astype(BF16)


def _inproj(x1, g_mix_pre, w_in, w_gla_a2, b_gla_a):
    n, d = x1.shape
    t = FFN_ROWS
    assert n % t == 0
    w_tok, w_t = _pack_w_in(w_in)
    nqk = GLA_HEADS * GLA_DK
    nv = GLA_HEADS * GLA_DV
    wa2 = jnp.zeros((V7X_LANES, nqk), F32).at[_GA_LANE:_GA_LANE + GLA_GATE_RANK].set(w_gla_a2)
    row = lambda i: (i, 0)
    row3 = lambda i: (0, i, 0)
    col3 = lambda i: (0, 0, i)
    out_shape = (
        jax.ShapeDtypeStruct((ATTN_HEADS, ATTN_HEAD_DIM, n), BF16),
        jax.ShapeDtypeStruct((IDX_HEADS, IDX_DIM, n), BF16),
        jax.ShapeDtypeStruct((ATTN_KV_HEADS, VT_ROWS, n), BF16),
        jax.ShapeDtypeStruct((IDX_HEADS, n), F32),
        jax.ShapeDtypeStruct((ATTN_KV_HEADS, n, ATTN_HEAD_DIM), BF16),
        jax.ShapeDtypeStruct((n, IDX_DIM), BF16),
        jax.ShapeDtypeStruct((n, 2 * nqk), F32),
        jax.ShapeDtypeStruct((n, nv), BF16),
        jax.ShapeDtypeStruct((n, nv), F32),
        jax.ShapeDtypeStruct((n, nqk), F32),
    )
    out_specs = (
        pl.BlockSpec((ATTN_HEADS, ATTN_HEAD_DIM, t), col3),
        pl.BlockSpec((IDX_HEADS, IDX_DIM, t), col3),
        pl.BlockSpec((ATTN_KV_HEADS, VT_ROWS, t), col3),
        pl.BlockSpec((IDX_HEADS, t), lambda i: (0, i)),
        pl.BlockSpec((ATTN_KV_HEADS, t, ATTN_HEAD_DIM), row3),
        pl.BlockSpec((t, IDX_DIM), row),
        pl.BlockSpec((t, 2 * nqk), row),
        pl.BlockSpec((t, nv), row),
        pl.BlockSpec((t, nv), row),
        pl.BlockSpec((t, nqk), row),
    )
    est = (2 * t * d * 4 + d * (_TOK_COLS + _T_ROWS) * 2 + t * (_TOK_COLS + _T_ROWS) * 4 * 2
           + 2 * t * 8 * 1024)
    return pl.pallas_call(
        _inproj_kernel,
        out_shape=out_shape,
        grid=(n // t,),
        in_specs=[
            pl.BlockSpec((t, d), row),
            _const_spec((1, d)),
            _const_spec(w_tok.shape),
            _const_spec(w_t.shape),
            _const_spec(wa2.shape),
            _const_spec((1, nqk)),
        ],
        out_specs=out_specs,
        compiler_params=pltpu.CompilerParams(
            dimension_semantics=("arbitrary",), vmem_limit_bytes=_vmem_limit(est)),
    )(x1, g_mix_pre.reshape(1, d), w_tok, w_t, wa2, b_gla_a.reshape(1, nqk))


def _dsa_kernel(qt_ref, iqt_ref, iwt_ref, kk_ref, vt_ref, ik_ref, o_ref,
                key_s, half_s, low_s, cnt_s, seen_s, bias_s, acc_s, ot_s, *, topk):
    tq = DSA_TQ
    kc = DSA_KC
    sub = V7X_SUBLANES
    i = pl.program_id(1)
    nchunks = i + 1

    def score_logits(j):
        ikc = ik_ref[pl.ds(pl.multiple_of(j * kc, kc), kc), :]
        iqcat = jnp.concatenate([iqt_ref[h] for h in range(IDX_HEADS)], axis=1)
        return jnp.dot(ikc, iqcat, preferred_element_type=F32)

    def score_keys(j, lg, diagonal):
        sc = jnp.zeros((kc, tq), F32)
        for h in range(IDX_HEADS):
            sc = sc + jnp.maximum(lg[:, h * tq:(h + 1) * tq], 0.0) * iwt_ref[h:h + 1, :]
        bits = pltpu.bitcast(sc, jnp.int32)
        key = jnp.where(bits < 0, bits ^ 0x7FFFFFFF, bits)
        if diagonal:
            kpos = lax.broadcasted_iota(jnp.int32, (kc, tq), 0)
            qpos = lax.broadcasted_iota(jnp.int32, (kc, tq), 1)
            key = jnp.where(kpos <= qpos, key, INT_MIN)
        key_s[j] = key
        half_s[j] = lax.shift_right_arithmetic(key, 16).astype(jnp.int16)
        low_s[j] = key.astype(jnp.int16) ^ jnp.int16(INT16_MIN)

    def sweep_chunks(n, group_fn, gsz):
        assert gsz & (gsz - 1) == 0

        def trip(q, c):
            group_fn(q * gsz, gsz)
            return c

        lax.fori_loop(0, lax.shift_right_logical(n, gsz.bit_length() - 1), trip, 0)
        done = n & -gsz
        part = gsz // 2
        while part:
            pl.when((n & part) != 0)(functools.partial(group_fn, done, part))
            done = done + (n & part)
            part //= 2

    def score_group(j0, count):
        lgs = [score_logits(j0 + s) for s in range(count)]
        for s in range(count):
            score_keys(j0 + s, lgs[s], False)

    sweep_chunks(i, score_group, DSA_SCORE_GROUP)
    score_keys(i, score_logits(i), True)

    qidx = i * tq + lax.broadcasted_iota(jnp.int32, (1, tq), 1)
    krow = jnp.minimum(topk, qidx + 1).astype(F32)

    def count(ind_fn):
        def body(j, acc):
            parts = [acc, None, None, None]
            for s in range(kc // sub):
                ind = ind_fn(key_s[j, s * sub:(s + 1) * sub, :], j * kc + s * sub)
                p = s % 4
                parts[p] = ind if parts[p] is None else parts[p] + ind
            return (parts[0] + parts[1]) + (parts[2] + parts[3])
        acc = lax.fori_loop(0, nchunks, body, jnp.zeros((sub, tq), F32))
        return jnp.sum(acc, axis=0, keepdims=True)

    def count_ge(cand):
        cb = jnp.broadcast_to(cand, (sub, tq))
        return count(lambda k, base: jnp.where(k >= cb, 1.0, 0.0))

    rows16 = V7X_BF16_ROWS

    def count16(cand):
        cb = jnp.broadcast_to(cand, (rows16, tq)).astype(jnp.int16)
        one, nil = jnp.int16(1), jnp.int16(0)
        cnt_s[...] = jnp.zeros_like(cnt_s)

        def group(j0, count):
            parts = [None] * 4
            for s in range(count * kc // rows16):
                c, r = divmod(s, kc // rows16)
                ind = jnp.where(half_s[j0 + c, r * rows16:(r + 1) * rows16, :] >= cb, one, nil)
                p = s % 4
                parts[p] = ind if parts[p] is None else parts[p] + ind
            cnt_s[...] += (parts[0] + parts[1]) + (parts[2] + parts[3])

        sweep_chunks(nchunks, group, DSA_COUNT_GROUP)
        return jnp.sum(cnt_s[...].astype(F32), axis=0, keepdims=True)

    def search16(count_at_min):
        zero = jnp.zeros((1, tq), jnp.int32)
        c0 = count16(zero)
        start = (jnp.where(c0 >= krow, zero, INT16_MIN), jnp.where(c0 >= krow, c0, count_at_min))

        def bit_body(it, carry):
            th, cth = carry
            cand = th | lax.shift_left(jnp.int32(1), 14 - it)
            c = count16(cand)
            return jnp.where(c >= krow, cand, th), jnp.where(c >= krow, c, cth)

        return lax.fori_loop(0, 15, bit_body, start)

    t_hi, c_hi = search16(jnp.full((1, tq), 1.0, F32) * (nchunks * kc).astype(F32))
    th16 = jnp.broadcast_to(t_hi, (kc, tq)).astype(jnp.int16)

    def low_halves(j0, count):
        for j in range(count):
            hi = half_s[j0 + j]
            half_s[j0 + j] = jnp.where(hi == th16, low_s[j0 + j],
                                       jnp.where(hi > th16, jnp.int16(INT16_MAX), jnp.int16(INT16_MIN)))

    sweep_chunks(nchunks, low_halves, DSA_SCORE_GROUP)
    t_lo, c_ge = search16(c_hi)
    t = lax.shift_left(t_hi, 16) | (t_lo - INT16_MIN)

    @pl.when(jnp.max(c_ge - krow) > 0.5)
    def _():
        saturated = jnp.max(jnp.where(t_lo == INT16_MAX, 1.0, 0.0)) > 0.5
        c_gt = lax.cond(saturated, lambda: count_ge(t + 1), lambda: count16(jnp.minimum(t_lo + 1, INT16_MAX)))
        need = krow - c_gt
        tb_full = jnp.broadcast_to(t, (kc, tq))
        tri = (lax.broadcasted_iota(jnp.int32, (kc, kc), 0)
               >= lax.broadcasted_iota(jnp.int32, (kc, kc), 1)).astype(BF16)

        seen_s[...] = jnp.zeros_like(seen_s)

        def drop_surplus(j0, count):
            ks = [key_s[j0 + s] for s in range(count)]
            prefix = [jnp.dot(tri, jnp.where(k == tb_full, 1.0, 0.0).astype(BF16), preferred_element_type=F32)
                      for k in ks]
            seen = seen_s[...]
            for s in range(count):
                rank = seen + prefix[s]
                key_s[j0 + s] = jnp.where(ks[s] == tb_full, jnp.where(rank > need, INT_MIN, ks[s]), ks[s])
                seen = rank[kc - 1:kc, :]
            seen_s[...] = seen

        sweep_chunks(nchunks, drop_surplus, DSA_SCORE_GROUP)

    def masked_scores(j, g, slot=0):
        koff = pl.multiple_of(j * kc, kc)
        qcat = jnp.concatenate([qt_ref[g * ATTN_REP + r] for r in range(ATTN_REP)], axis=1)
        bias = jnp.concatenate([bias_s[slot]] * ATTN_REP, axis=1)
        kg = kk_ref[g, pl.ds(koff, kc), :]
        vg = vt_ref[g, :, pl.ds(koff, kc)]
        return jnp.dot(kg, qcat, preferred_element_type=F32) + bias, vg

    def set_bias(j, slot=0):
        bias_s[slot] = jnp.where(key_s[j] >= jnp.broadcast_to(t, (kc, tq)), 0.0, NEG_BIG)

    acc_s[...] = jnp.zeros_like(acc_s)

    def att_group(j0, count):
        units = [(slot, g) for slot in range(count) for g in range(ATTN_KV_HEADS)]
        for slot in range(count):
            set_bias(j0 + slot, slot)
        scores = {}

        def issue(u):
            slot, g = units[u]
            scores[u] = masked_scores(j0 + slot, g, slot)

        for u in range(min(DSA_ATT_LEAD, len(units))):
            issue(u)
        for u, (slot, g) in enumerate(units):
            s, vg = scores.pop(u)
            acc_s[g] += jnp.dot(vg, jnp.exp(s).astype(BF16), preferred_element_type=F32)
            if u + DSA_ATT_LEAD < len(units):
                issue(u + DSA_ATT_LEAD)

    sweep_chunks(nchunks, att_group, DSA_ATT_GROUP)

    norm = acc_s[:, ATTN_HEAD_DIM:ATTN_HEAD_DIM + 1, :]
    in_range = jnp.where(norm >= SOFTMAX_NORM_MIN, jnp.where(norm <= SOFTMAX_NORM_MAX, 1.0, 0.0), 0.0)

    @pl.when(jnp.min(in_range) < 0.5)
    def _():
        acc_s[...] = jnp.zeros_like(acc_s)

        def online_body(j, ms):
            set_bias(j)
            out = []
            for g in range(ATTN_KV_HEADS):
                s, vg = masked_scores(j, g)
                m_new = jnp.maximum(ms[g], jnp.max(s, axis=0, keepdims=True))
                p = jnp.exp(s - m_new).astype(BF16)
                pv = jnp.dot(vg, p, preferred_element_type=F32)
                acc_s[g] = jnp.exp(ms[g] - m_new) * acc_s[g] + pv
                out.append(m_new)
            return tuple(out)

        m_init = tuple(jnp.full((1, ATTN_REP * tq), NEG_BIG, F32) for _ in range(ATTN_KV_HEADS))
        lax.fori_loop(0, nchunks, online_body, m_init)

    for h in range(ATTN_HEADS):
        g, r = divmod(h, ATTN_REP)
        a = acc_s[g, :, r * tq:(r + 1) * tq]
        ot_s[h * ATTN_HEAD_DIM:(h + 1) * ATTN_HEAD_DIM, :] = (
            a[:ATTN_HEAD_DIM] / a[ATTN_HEAD_DIM:ATTN_HEAD_DIM + 1])
    o_ref[...] = ot_s[...].T.astype(o_ref.dtype)


def _dsa(qt, iqt, iwt, kk, vt, ik, batch, seq):
    n = batch * seq
    tq, kc = DSA_TQ, DSA_KC
    assert seq % tq == 0
    nq = seq // tq
    topk = min(TOPK_MAX, seq // 4)
    nch = seq // kc
    nout = ATTN_HEADS * ATTN_HEAD_DIM
    qmap = lambda b, i: (0, 0, b * nq + i)
    est = (nch * kc * tq * 8 + 3 * DSA_ATT_GROUP * ATTN_REP * kc * tq * 4 + ATTN_HEADS * (VT_ROWS + 8) * tq * 4 + nout * tq * 4
           + ATTN_KV_HEADS * seq * (V7X_LANES + VT_ROWS) * 2 + seq * V7X_LANES * 2
           + 2 * (2 * ATTN_HEADS * ATTN_HEAD_DIM * tq * 2 + 8 * tq * 4 + tq * nout * 2))
    return pl.pallas_call(
        functools.partial(_dsa_kernel, topk=topk),
        out_shape=jax.ShapeDtypeStruct((n, nout), BF16),
        grid=(batch, nq),
        in_specs=[
            pl.BlockSpec((ATTN_HEADS, ATTN_HEAD_DIM, tq), qmap),
            pl.BlockSpec((IDX_HEADS, IDX_DIM, tq), qmap),
            pl.BlockSpec((IDX_HEADS, tq), lambda b, i: (0, b * nq + i)),
            pl.BlockSpec((ATTN_KV_HEADS, seq, ATTN_HEAD_DIM), lambda b, i: (0, b, 0),
                         pipeline_mode=pl.Buffered(1)),
            pl.BlockSpec((ATTN_KV_HEADS, VT_ROWS, seq), lambda b, i: (0, 0, b),
                         pipeline_mode=pl.Buffered(1)),
            pl.BlockSpec((seq, IDX_DIM), lambda b, i: (b, 0), pipeline_mode=pl.Buffered(1)),
        ],
        out_specs=pl.BlockSpec((tq, nout), lambda b, i: (b * nq + i, 0)),
        scratch_shapes=[
            pltpu.VMEM((nch, kc, tq), jnp.int32),
            pltpu.VMEM((nch, kc, tq), jnp.int16),
            pltpu.VMEM((nch, kc, tq), jnp.int16),
            pltpu.VMEM((V7X_BF16_ROWS, tq), jnp.int16),
            pltpu.VMEM((1, tq), F32),
            pltpu.VMEM((DSA_ATT_GROUP, kc, tq), F32),
            pltpu.VMEM((ATTN_KV_HEADS, VT_ROWS, ATTN_REP * tq), F32),
            pltpu.VMEM((nout, tq), F32),
        ],
        compiler_params=pltpu.CompilerParams(
            dimension_semantics=("arbitrary", "arbitrary"), vmem_limit_bytes=_vmem_limit(est)),
    )(qt, iqt, iwt, kk, vt, ik)


def _gla_kernel(gqk_ref, gv_ref, la_ref, gg_ref, gn_ref, o_ref, st_s):
    @pl.when(pl.program_id(1) == 0)
    def _():
        st_s[...] = jnp.zeros_like(st_s)

    c = GLA_CHUNK
    nqk = GLA_HEADS * GLA_DK
    r_i = lax.broadcasted_iota(jnp.int32, (c, c), 0)
    c_i = lax.broadcasted_iota(jnp.int32, (c, c), 1)
    tri = r_i >= c_i
    tri_f = tri.astype(F32)
    gn = gn_ref[...]
    nchunk = GLA_ROWS // c
    heads = range(GLA_HEADS)
    hs = [slice(h * GLA_DK, (h + 1) * GLA_DK) for h in heads]
    vs = [slice(h * GLA_DV, (h + 1) * GLA_DV) for h in heads]

    local = []
    for ci in range(nchunk):
        rows = slice(ci * c, (ci + 1) * c)
        la = la_ref[rows, :]
        b = jnp.dot(tri_f, la, preferred_element_type=F32, precision=lax.Precision.HIGHEST)
        b_last = b[c - 1:c, :]
        q = gqk_ref[rows, :nqk]
        k = gqk_ref[rows, nqk:]
        q_dec = (q * jnp.exp(b)).astype(BF16)
        k_in = (k * jnp.exp(-b)).astype(BF16)
        k_out = (k * jnp.exp(b_last - b)).astype(BF16)
        decay = jnp.exp(b_last)
        intra, upd = [], []
        for h in heads:
            v = gv_ref[rows, vs[h]]
            a = lax.dot_general(q_dec[:, hs[h]], k_in[:, hs[h]], NT_DIMS, preferred_element_type=F32)
            a = jnp.where(tri, a, 0.0).astype(BF16)
            intra.append(jnp.dot(a, v, preferred_element_type=F32))
            upd.append(lax.dot_general(v, k_out[:, hs[h]], TN_DIMS, preferred_element_type=F32))
        local.append((rows, q_dec, decay, intra, upd))

    st = [st_s[h] for h in heads]
    for rows, q_dec, decay, intra, upd in local:
        for h in heads:
            o = intra[h] + lax.dot_general(q_dec[:, hs[h]], st[h].astype(BF16), NT_DIMS,
                                           preferred_element_type=F32)
            st[h] = st[h] * decay[:, hs[h]] + upd[h]
            gate = gg_ref[rows, vs[h]]
            o_ref[rows, vs[h]] = (_rms(o, gn) * (gate * jax.nn.sigmoid(gate))).astype(o_ref.dtype)
    for h in heads:
        st_s[h] = st[h]


def _gla(gqk, gv, la, gg, g_norm, batch, seq):
    n = batch * seq
    t = GLA_ROWS
    assert seq % t == 0
    ns = seq // t
    row = lambda b, i: (b * ns + i, 0)
    nqk = GLA_HEADS * GLA_DK
    nv = GLA_HEADS * GLA_DV
    return pl.pallas_call(
        _gla_kernel,
        out_shape=jax.ShapeDtypeStruct((n, nv), BF16),
        grid=(batch, ns),
        in_specs=[
            pl.BlockSpec((t, 2 * nqk), row),
            pl.BlockSpec((t, nv), row),
            pl.BlockSpec((t, nqk), row),
            pl.BlockSpec((t, nv), row),
            _const_spec((1, GLA_DV)),
        ],
        out_specs=pl.BlockSpec((t, nv), row),
        scratch_shapes=[pltpu.VMEM((GLA_HEADS, GLA_DV, GLA_DK), F32)],
        compiler_params=pltpu.CompilerParams(dimension_semantics=("arbitrary", "arbitrary")),
    )(gqk, gv, la, gg, g_norm.reshape(1, GLA_DV))


def kernel(x, g_ffn1_pre, w_ffn1_gate, w_ffn1_up, w_ffn1_down, g_ffn1_post, g_mix_pre, w_in, w_gla_a2,
           b_gla_a, g_gla_norm, w_out, g_mix_post, g_ffn2_pre, w_ffn2_gate, w_ffn2_up, w_ffn2_down,
           g_ffn2_post):
    batch, seq, d = x.shape
    h = x.reshape(batch * seq, d)
    for l in range(g_ffn1_pre.shape[0]):
        h = _ffn(h, g_ffn1_pre[l], w_ffn1_gate[l], w_ffn1_up[l], w_ffn1_down[l], g_ffn1_post[l])
        qt, iqt, vt, iwt, kk, ik, gqk, gv, gg, la = _inproj(h, g_mix_pre[l], w_in[l], w_gla_a2[l], b_gla_a[l])
        oa = _dsa(qt, iqt, iwt, kk, vt, ik, batch, seq)
        og = _gla(gqk, gv, la, gg, g_gla_norm[l], batch, seq)
        h = _mix_ffn(h, oa, og, w_out[l], g_mix_post[l],
                     g_ffn2_pre[l], w_ffn2_gate[l], w_ffn2_up[l], w_ffn2_down[l], g_ffn2_post[l])
    return h.reshape(batch, seq, d)
```

```python
import functools

import jax
import jax.numpy as jnp
from jax import lax
from jax.experimental import pallas as pl
from jax.experimental.pallas import tpu as pltpu

ATTN_HEADS = 8
ATTN_KV_HEADS = 2
ATTN_HEAD_DIM = 64
ATTN_REP = ATTN_HEADS // ATTN_KV_HEADS
IDX_HEADS = 8
IDX_DIM = 64
TOPK_MAX = 256
GLA_HEADS = 4
GLA_DK = 64
GLA_DV = 128
GLA_GATE_RANK = 16
GLA_TAU = 16.0
GLA_CHUNK = 64
EPS = 1e-6

V7X_LANES = 128
V7X_SUBLANES = 8
V7X_BF16_ROWS = 16
V7X_MXU_DIM = 256
V7X_VMEM_BYTES = 64 * 2**20

FFN_ROWS = 512
FF_CHUNK = V7X_MXU_DIM
DSA_TQ = 256
DSA_KC = DSA_TQ
DSA_SCORE_GROUP = 8
DSA_COUNT_GROUP = 8
DSA_ATT_GROUP = 4
DSA_ATT_LEAD = 3
GLA_ROWS = 512
VT_ROWS = ATTN_HEAD_DIM + V7X_BF16_ROWS

INT_MIN = -2**31
INT16_MIN, INT16_MAX = -2**15, 2**15 - 1
NEG_BIG = -1e30
SOFTMAX_NORM_MIN = 2.0 ** -60
SOFTMAX_NORM_MAX = 2.0 ** 100

F32 = jnp.float32
BF16 = jnp.bfloat16
NT_DIMS = (((1,), (1,)), ((), ()))
TN_DIMS = (((0,), (0,)), ((), ()))


def _vmem_limit(nbytes):
    return int(min(nbytes * 1.25 + (8 << 20), V7X_VMEM_BYTES - (6 << 20)))


def _rms(x, g):
    return x * lax.rsqrt(jnp.mean(x * x, axis=-1, keepdims=True) + EPS) * g


def _const_spec(shape):
    nd = len(shape)
    return pl.BlockSpec(shape, lambda *_: (0,) * nd, pipeline_mode=pl.Buffered(1))


def _ffn_kernel(x_ref, gpre_ref, wg_ref, wu_ref, wd_ref, gpost_ref, o_ref):
    o_ref[...] = _ffn_residual(x_ref[...], gpre_ref, wg_ref, wu_ref, wd_ref, gpost_ref)


def _mix_ffn_kernel(x_ref, oa_ref, og_ref, wo_ref, gmix_ref, gpre_ref, wg_ref, wu_ref, wd_ref, gpost_ref, o_ref):
    m = jnp.dot(oa_ref[...], wo_ref[0], preferred_element_type=F32)
    m = m + jnp.dot(og_ref[...], wo_ref[1], preferred_element_type=F32)
    x2 = x_ref[...] + _rms(m, gmix_ref[...])
    o_ref[...] = _ffn_residual(x2, gpre_ref, wg_ref, wu_ref, wd_ref, gpost_ref)


def _ffn_residual(x, gpre_ref, wg_ref, wu_ref, wd_ref, gpost_ref):
    xn = _rms(x, gpre_ref[...]).astype(BF16)

    def cols(c):
        return slice(c * FF_CHUNK, (c + 1) * FF_CHUNK)

    def gate_up(c):
        return (jnp.dot(xn, wg_ref[:, cols(c)], preferred_element_type=F32),
                jnp.dot(xn, wu_ref[:, cols(c)], preferred_element_type=F32))

    nch = wg_ref.shape[1] // FF_CHUNK
    acc = None
    nxt = gate_up(0)
    for c in range(nch):
        g, u = nxt
        if c + 1 < nch:
            nxt = gate_up(c + 1)
        a = (g * jax.nn.sigmoid(g) * u).astype(BF16)
        d = jnp.dot(a, wd_ref[cols(c), :], preferred_element_type=F32)
        acc = d if acc is None else acc + d
    return x + 0.5 * _rms(acc, gpost_ref[...])


def _ffn_operands(d, g_pre, w_gate, w_up, w_down, g_post):
    dff = w_gate.shape[1]
    assert dff % FF_CHUNK == 0
    wg, wu, wd = w_gate.astype(BF16), w_up.astype(BF16), w_down.astype(BF16)
    specs = [_const_spec((1, d)), _const_spec(wg.shape), _const_spec(wu.shape), _const_spec(wd.shape),
             _const_spec((1, d))]
    args = (g_pre.reshape(1, d), wg, wu, wd, g_post.reshape(1, d))
    est = 4 * FFN_ROWS * d * 4 + 3 * d * dff * 2 + FFN_ROWS * d * 8 + 4 * FFN_ROWS * FF_CHUNK * 4
    return specs, args, est


def _ffn(x, g_pre, w_gate, w_up, w_down, g_post):
    n, d = x.shape
    assert n % FFN_ROWS == 0
    ffn_specs, ffn_args, est = _ffn_operands(d, g_pre, w_gate, w_up, w_down, g_post)
    x_spec = pl.BlockSpec((FFN_ROWS, d), lambda i: (i, 0))
    return pl.pallas_call(
        _ffn_kernel,
        out_shape=jax.ShapeDtypeStruct((n, d), F32),
        grid=(n // FFN_ROWS,),
        in_specs=[x_spec] + ffn_specs,
        out_specs=x_spec,
        compiler_params=pltpu.CompilerParams(
            dimension_semantics=("arbitrary",), vmem_limit_bytes=_vmem_limit(est)),
    )(x, *ffn_args)


def _mix_ffn(x, oa, og, w_out, g_mix, g_pre, w_gate, w_up, w_down, g_post):
    n, d = x.shape
    assert n % FFN_ROWS == 0
    ffn_specs, ffn_args, est = _ffn_operands(d, g_pre, w_gate, w_up, w_down, g_post)
    half = oa.shape[1]
    assert og.shape[1] == half and w_out.shape == (2 * half, d)
    wo = w_out.reshape(2, half, d).astype(BF16)
    row = lambda i: (i, 0)
    x_spec = pl.BlockSpec((FFN_ROWS, d), row)
    o_spec = pl.BlockSpec((FFN_ROWS, half), row)
    est += 4 * FFN_ROWS * half * 2 + 2 * half * d * 2 + FFN_ROWS * d * 4
    return pl.pallas_call(
        _mix_ffn_kernel,
        out_shape=jax.ShapeDtypeStruct((n, d), F32),
        grid=(n // FFN_ROWS,),
        in_specs=[x_spec, o_spec, o_spec, _const_spec(wo.shape), _const_spec((1, d))] + ffn_specs,
        out_specs=x_spec,
        compiler_params=pltpu.CompilerParams(
            dimension_semantics=("arbitrary",), vmem_limit_bytes=_vmem_limit(est)),
    )(x, oa, og, wo, g_mix.reshape(1, d), *ffn_args)


_K_OFF, _MISC_OFF, _GQ_OFF, _GK_OFF, _GV_OFF, _GG_OFF, _TOK_COLS = 0, 128, 256, 512, 768, 1280, 1792
_GA_LANE = 72
_QT_OFF, _IQT_OFF, _VT_OFF, _IWT_OFF, _T_ROWS = 0, 512, 1024, 1152, 1168


def _inproj_kernel(x_ref, g_ref, wtok_ref, wt_ref, wa2_ref, ba_ref,
                   qt_ref, iqt_ref, vt_ref, iwt_ref, kk_ref, ik_ref, gqk_ref, gv_ref, gg_ref, la_ref):
    h = _rms(x_ref[...], g_ref[...]).astype(BF16)
    t = h.shape[0]
    pt = lax.dot_general(wt_ref[...], h, NT_DIMS, preferred_element_type=F32)
    for i in range(ATTN_HEADS):
        qt_ref[i] = pt[_QT_OFF + 64 * i:_QT_OFF + 64 * (i + 1)].astype(BF16)
    for i in range(IDX_HEADS):
        iqt_ref[i] = pt[_IQT_OFF + 64 * i:_IQT_OFF + 64 * (i + 1)].astype(BF16)
    ones_row = (lax.broadcasted_iota(jnp.int32, (V7X_BF16_ROWS, t), 0) == 0).astype(F32)
    for g in range(ATTN_KV_HEADS):
        v_t = pt[_VT_OFF + 64 * g:_VT_OFF + 64 * (g + 1)]
        vt_ref[g] = jnp.concatenate([v_t, ones_row], axis=0).astype(BF16)
    iwt_ref[...] = pt[_IWT_OFF:_IWT_OFF + IDX_HEADS] * (IDX_HEADS ** -0.5)

    proj = jnp.dot(h, wtok_ref[...], preferred_element_type=F32)
    for g in range(ATTN_KV_HEADS):
        kk_ref[g] = proj[:, _K_OFF + 64 * g:_K_OFF + 64 * (g + 1)].astype(BF16)
    misc = proj[:, _MISC_OFF:_MISC_OFF + V7X_LANES]
    ik_ref[...] = misc[:, :IDX_DIM].astype(BF16)
    gqk_ref[...] = proj[:, _GQ_OFF:_GV_OFF]
    gv_ref[...] = proj[:, _GV_OFF:_GG_OFF].astype(BF16)
    gg_ref[...] = proj[:, _GG_OFF:_GG_OFF + GLA_HEADS * GLA_DV]
    z = jnp.dot(misc, wa2_ref[...], preferred_element_type=F32, precision=lax.Precision.HIGHEST)
    z = z + ba_ref[...]
    log_sig = jnp.minimum(z, 0.0) - jnp.log1p(jnp.exp(-jnp.abs(z)))
    la_ref[...] = log_sig * (1.0 / GLA_TAU)


def _pack_w_in(w_in):
    d = w_in.shape[0]
    sizes = (512, 128, 128, 512, 64, 8, 256, 256, 512, 16, 512)
    offs = [0]
    for s in sizes:
        offs.append(offs[-1] + s)
    aq, ak, av, iq, ik, iw, gq, gk, gv, ga, gg = [w_in[:, offs[i]:offs[i + 1]] for i in range(len(sizes))]
    z = lambda n: jnp.zeros((d, n), w_in.dtype)
    w_tok = jnp.concatenate([ak, ik, z(8), ga, z(40), gq * (GLA_DK ** -0.5), gk, gv, gg], axis=1)
    w_t = jnp.concatenate([aq * (ATTN_HEAD_DIM ** -0.5), iq * (IDX_DIM ** -0.5), av, iw, z(8)], axis=1).T
    assert w_tok.shape[1] == _TOK_COLS and w_t.shape[0] == _T_ROWS
    return w_tok.astype(BF16), w_t.astype(BF16)


def _inproj(x1, g_mix_pre, w_in, w_gla_a2, b_gla_a):
    n, d = x1.shape
    t = FFN_ROWS
    assert n % t == 0
    w_tok, w_t = _pack_w_in(w_in)
    nqk = GLA_HEADS * GLA_DK
    nv = GLA_HEADS * GLA_DV
    wa2 = jnp.zeros((V7X_LANES, nqk), F32).at[_GA_LANE:_GA_LANE + GLA_GATE_RANK].set(w_gla_a2)
    row = lambda i: (i, 0)
    row3 = lambda i: (0, i, 0)
    col3 = lambda i: (0, 0, i)
    out_shape = (
        jax.ShapeDtypeStruct((ATTN_HEADS, ATTN_HEAD_DIM, n), BF16),
        jax.ShapeDtypeStruct((IDX_HEADS, IDX_DIM, n), BF16),
        jax.ShapeDtypeStruct((ATTN_KV_HEADS, VT_ROWS, n), BF16),
        jax.ShapeDtypeStruct((IDX_HEADS, n), F32),
        jax.ShapeDtypeStruct((ATTN_KV_HEADS, n, ATTN_HEAD_DIM), BF16),
        jax.ShapeDtypeStruct((n, IDX_DIM), BF16),
        jax.ShapeDtypeStruct((n, 2 * nqk), F32),
        jax.ShapeDtypeStruct((n, nv), BF16),
        jax.ShapeDtypeStruct((n, nv), F32),
        jax.ShapeDtypeStruct((n, nqk), F32),
    )
    out_specs = (
        pl.BlockSpec((ATTN_HEADS, ATTN_HEAD_DIM, t), col3),
        pl.BlockSpec((IDX_HEADS, IDX_DIM, t), col3),
        pl.BlockSpec((ATTN_KV_HEADS, VT_ROWS, t), col3),
        pl.BlockSpec((IDX_HEADS, t), lambda i: (0, i)),
        pl.BlockSpec((ATTN_KV_HEADS, t, ATTN_HEAD_DIM), row3),
        pl.BlockSpec((t, IDX_DIM), row),
        pl.BlockSpec((t, 2 * nqk), row),
        pl.BlockSpec((t, nv), row),
        pl.BlockSpec((t, nv), row),
        pl.BlockSpec((t, nqk), row),
    )
    est = (2 * t * d * 4 + d * (_TOK_COLS + _T_ROWS) * 2 + t * (_TOK_COLS + _T_ROWS) * 4 * 2
           + 2 * t * 8 * 1024)
    return pl.pallas_call(
        _inproj_kernel,
        out_shape=out_shape,
        grid=(n // t,),
        in_specs=[
            pl.BlockSpec((t, d), row),
            _const_spec((1, d)),
            _const_spec(w_tok.shape),
            _const_spec(w_t.shape),
            _const_spec(wa2.shape),
            _const_spec((1, nqk)),
        ],
        out_specs=out_specs,
        compiler_params=pltpu.CompilerParams(
            dimension_semantics=("arbitrary",), vmem_limit_bytes=_vmem_limit(est)),
    )(x1, g_mix_pre.reshape(1, d), w_tok, w_t, wa2, b_gla_a.reshape(1, nqk))


def _dsa_kernel(qt_ref, iqt_ref, iwt_ref, kk_ref, vt_ref, ik_ref, o_ref,
                key_s, half_s, low_s, cnt_s, seen_s, bias_s, acc_s, ot_s, *, topk):
    tq = DSA_TQ
    kc = DSA_KC
    sub = V7X_SUBLANES
    i = pl.program_id(1)
    nchunks = i + 1

    def score_logits(j):
        ikc = ik_ref[pl.ds(pl.multiple_of(j * kc, kc), kc), :]
        iqcat = jnp.concatenate([iqt_ref[h] for h in range(IDX_HEADS)], axis=1)
        return jnp.dot(ikc, iqcat, preferred_element_type=F32)

    def score_keys(j, lg, diagonal):
        sc = jnp.zeros((kc, tq), F32)
        for h in range(IDX_HEADS):
            sc = sc + jnp.maximum(lg[:, h * tq:(h + 1) * tq], 0.0) * iwt_ref[h:h + 1, :]
        bits = pltpu.bitcast(sc, jnp.int32)
        key = jnp.where(bits < 0, bits ^ 0x7FFFFFFF, bits)
        if diagonal:
            kpos = lax.broadcasted_iota(jnp.int32, (kc, tq), 0)
            qpos = lax.broadcasted_iota(jnp.int32, (kc, tq), 1)
            key = jnp.where(kpos <= qpos, key, INT_MIN)
        key_s[j] = key
        half_s[j] = lax.shift_right_arithmetic(key, 16).astype(jnp.int16)
        low_s[j] = key.astype(jnp.int16) ^ jnp.int16(INT16_MIN)

    def sweep_chunks(n, group_fn, gsz):
        assert gsz & (gsz - 1) == 0

        def trip(q, c):
            group_fn(q * gsz, gsz)
            return c

        lax.fori_loop(0, lax.shift_right_logical(n, gsz.bit_length() - 1), trip, 0)
        done = n & -gsz
        part = gsz // 2
        while part:
            pl.when((n & part) != 0)(functools.partial(group_fn, done, part))
            done = done + (n & part)
            part //= 2

    def score_group(j0, count):
        lgs = [score_logits(j0 + s) for s in range(count)]
        for s in range(count):
            score_keys(j0 + s, lgs[s], False)

    sweep_chunks(i, score_group, DSA_SCORE_GROUP)
    score_keys(i, score_logits(i), True)

    qidx = i * tq + lax.broadcasted_iota(jnp.int32, (1, tq), 1)
    krow = jnp.minimum(topk, qidx + 1).astype(F32)

    def count(ind_fn):
        def body(j, acc):
            parts = [acc, None, None, None]
            for s in range(kc // sub):
                ind = ind_fn(key_s[j, s * sub:(s + 1) * sub, :], j * kc + s * sub)
                p = s % 4
                parts[p] = ind if parts[p] is None else parts[p] + ind
            return (parts[0] + parts[1]) + (parts[2] + parts[3])
        acc = lax.fori_loop(0, nchunks, body, jnp.zeros((sub, tq), F32))
        return jnp.sum(acc, axis=0, keepdims=True)

    def count_ge(cand):
        cb = jnp.broadcast_to(cand, (sub, tq))
        return count(lambda k, base: jnp.where(k >= cb, 1.0, 0.0))

    rows16 = V7X_BF16_ROWS

    def count16(cand):
        cb = jnp.broadcast_to(cand, (rows16, tq)).astype(jnp.int16)
        one, nil = jnp.int16(1), jnp.int16(0)
        cnt_s[...] = jnp.zeros_like(cnt_s)

        def group(j0, count):
            parts = [None] * 4
            for s in range(count * kc // rows16):
                c, r = divmod(s, kc // rows16)
                ind = jnp.where(half_s[j0 + c, r * rows16:(r + 1) * rows16, :] >= cb, one, nil)
                p = s % 4
                parts[p] = ind if parts[p] is None else parts[p] + ind
            cnt_s[...] += (parts[0] + parts[1]) + (parts[2] + parts[3])

        sweep_chunks(nchunks, group, DSA_COUNT_GROUP)
        return jnp.sum(cnt_s[...].astype(F32), axis=0, keepdims=True)

    def search16(count_at_min):
        zero = jnp.zeros((1, tq), jnp.int32)
        c0 = count16(zero)
        start = (jnp.where(c0 >= krow, zero, INT16_MIN), jnp.where(c0 >= krow, c0, count_at_min))

        def bit_body(it, carry):
            th, cth = carry
            cand = th | lax.shift_left(jnp.int32(1), 14 - it)
            c = count16(cand)
            return jnp.where(c >= krow, cand, th), jnp.where(c >= krow, c, cth)

        return lax.fori_loop(0, 15, bit_body, start)

    t_hi, c_hi = search16(jnp.full((1, tq), 1.0, F32) * (nchunks * kc).astype(F32))
    th16 = jnp.broadcast_to(t_hi, (kc, tq)).astype(jnp.int16)

    def low_halves(j0, count):
        for j in range(count):
            hi = half_s[j0 + j]
            half_s[j0 + j] = jnp.where(hi == th16, low_s[j0 + j],
                                       jnp.where(hi > th16, jnp.int16(INT16_MAX), jnp.int16(INT16_MIN)))

    sweep_chunks(nchunks, low_halves, DSA_SCORE_GROUP)
    t_lo, c_ge = search16(c_hi)
    t = lax.shift_left(t_hi, 16) | (t_lo - INT16_MIN)

    @pl.when(jnp.max(c_ge - krow) > 0.5)
    def _():
        saturated = jnp.max(jnp.where(t_lo == INT16_MAX, 1.0, 0.0)) > 0.5
        c_gt = lax.cond(saturated, lambda: count_ge(t + 1), lambda: count16(jnp.minimum(t_lo + 1, INT16_MAX)))
        need = krow - c_gt
        tb_full = jnp.broadcast_to(t, (kc, tq))
        tri = (lax.broadcasted_iota(jnp.int32, (kc, kc), 0)
               >= lax.broadcasted_iota(jnp.int32, (kc, kc), 1)).astype(BF16)

        seen_s[...] = jnp.zeros_like(seen_s)

        def drop_surplus(j0, count):
            ks = [key_s[j0 + s] for s in range(count)]
            prefix = [jnp.dot(tri, jnp.where(k == tb_full, 1.0, 0.0).astype(BF16), preferred_element_type=F32)
                      for k in ks]
            seen = seen_s[...]
            for s in range(count):
                rank = seen + prefix[s]
                key_s[j0 + s] = jnp.where(ks[s] == tb_full, jnp.where(rank > need, INT_MIN, ks[s]), ks[s])
                seen = rank[kc - 1:kc, :]
            seen_s[...] = seen

        sweep_chunks(nchunks, drop_surplus, DSA_SCORE_GROUP)

    def masked_scores(j, g, slot=0):
        koff = pl.multiple_of(j * kc, kc)
        qcat = jnp.concatenate([qt_ref[g * ATTN_REP + r] for r in range(ATTN_REP)], axis=1)
        bias = jnp.concatenate([bias_s[slot]] * ATTN_REP, axis=1)
        kg = kk_ref[g, pl.ds(koff, kc), :]
        vg = vt_ref[g, :, pl.ds(koff, kc)]
        return jnp.dot(kg, qcat, preferred_element_type=F32) + bias, vg

    def set_bias(j, slot=0):
        bias_s[slot] = jnp.where(key_s[j] >= jnp.broadcast_to(t, (kc, tq)), 0.0, NEG_BIG)

    acc_s[...] = jnp.zeros_like(acc_s)

    def att_group(j0, count):
        units = [(slot, g) for slot in range(count) for g in range(ATTN_KV_HEADS)]
        for slot in range(count):
            set_bias(j0 + slot, slot)
        scores = {}

        def issue(u):
            slot, g = units[u]
            scores[u] = masked_scores(j0 + slot, g, slot)

        for u in range(min(DSA_ATT_LEAD, len(units))):
            issue(u)
        for u, (slot, g) in enumerate(units):
            s, vg = scores.pop(u)
            acc_s[g] += jnp.dot(vg, jnp.exp(s).astype(BF16), preferred_element_type=F32)
            if u + DSA_ATT_LEAD < len(units):
                issue(u + DSA_ATT_LEAD)

    sweep_chunks(nchunks, att_group, DSA_ATT_GROUP)

    norm = acc_s[:, ATTN_HEAD_DIM:ATTN_HEAD_DIM + 1, :]
    in_range = jnp.where(norm >= SOFTMAX_NORM_MIN, jnp.where(norm <= SOFTMAX_NORM_MAX, 1.0, 0.0), 0.0)

    @pl.when(jnp.min(in_range) < 0.5)
    def _():
        acc_s[...] = jnp.zeros_like(acc_s)

        def online_body(j, ms):
            set_bias(j)
            out = []
            for g in range(ATTN_KV_HEADS):
                s, vg = masked_scores(j, g)
                m_new = jnp.maximum(ms[g], jnp.max(s, axis=0, keepdims=True))
                p = jnp.exp(s - m_new).astype(BF16)
                pv = jnp.dot(vg, p, preferred_element_type=F32)
                acc_s[g] = jnp.exp(ms[g] - m_new) * acc_s[g] + pv
                out.append(m_new)
            return tuple(out)

        m_init = tuple(jnp.full((1, ATTN_REP * tq), NEG_BIG, F32) for _ in range(ATTN_KV_HEADS))
        lax.fori_loop(0, nchunks, online_body, m_init)

    for h in range(ATTN_HEADS):
        g, r = divmod(h, ATTN_REP)
        a = acc_s[g, :, r * tq:(r + 1) * tq]
        ot_s[h * ATTN_HEAD_DIM:(h + 1) * ATTN_HEAD_DIM, :] = (
            a[:ATTN_HEAD_DIM] / a[ATTN_HEAD_DIM:ATTN_HEAD_DIM + 1])
    o_ref[...] = ot_s[...].T.astype(o_ref.dtype)


def _dsa(qt, iqt, iwt, kk, vt, ik, batch, seq):
    n = batch * seq
    tq, kc = DSA_TQ, DSA_KC
    assert seq % tq == 0
    nq = seq // tq
    topk = min(TOPK_MAX, seq // 4)
    nch = seq // kc
    nout = ATTN_HEADS * ATTN_HEAD_DIM
    qmap = lambda b, i: (0, 0, b * nq + i)
    est = (nch * kc * tq * 8 + 3 * DSA_ATT_GROUP * ATTN_REP * kc * tq * 4 + ATTN_HEADS * (VT_ROWS + 8) * tq * 4 + nout * tq * 4
           + ATTN_KV_HEADS * seq * (V7X_LANES + VT_ROWS) * 2 + seq * V7X_LANES * 2
           + 2 * (2 * ATTN_HEADS * ATTN_HEAD_DIM * tq * 2 + 8 * tq * 4 + tq * nout * 2))
    return pl.pallas_call(
        functools.partial(_dsa_kernel, topk=topk),
        out_shape=jax.ShapeDtypeStruct((n, nout), BF16),
        grid=(batch, nq),
        in_specs=[
            pl.BlockSpec((ATTN_HEADS, ATTN_HEAD_DIM, tq), qmap),
            pl.BlockSpec((IDX_HEADS, IDX_DIM, tq), qmap),
            pl.BlockSpec((IDX_HEADS, tq), lambda b, i: (0, b * nq + i)),
            pl.BlockSpec((ATTN_KV_HEADS, seq, ATTN_HEAD_DIM), lambda b, i: (0, b, 0),
                         pipeline_mode=pl.Buffered(1)),
            pl.BlockSpec((ATTN_KV_HEADS, VT_ROWS, seq), lambda b, i: (0, 0, b),
                         pipeline_mode=pl.Buffered(1)),
            pl.BlockSpec((seq, IDX_DIM), lambda b, i: (b, 0), pipeline_mode=pl.Buffered(1)),
        ],
        out_specs=pl.BlockSpec((tq, nout), lambda b, i: (b * nq + i, 0)),
        scratch_shapes=[
            pltpu.VMEM((nch, kc, tq), jnp.int32),
            pltpu.VMEM((nch, kc, tq), jnp.int16),
            pltpu.VMEM((nch, kc, tq), jnp.int16),
            pltpu.VMEM((V7X_BF16_ROWS, tq), jnp.int16),
            pltpu.VMEM((1, tq), F32),
            pltpu.VMEM((DSA_ATT_GROUP, kc, tq), F32),
            pltpu.VMEM((ATTN_KV_HEADS, VT_ROWS, ATTN_REP * tq), F32),
            pltpu.VMEM((nout, tq), F32),
        ],
        compiler_params=pltpu.CompilerParams(
            dimension_semantics=("arbitrary", "arbitrary"), vmem_limit_bytes=_vmem_limit(est)),
    )(qt, iqt, iwt, kk, vt, ik)


def _gla_kernel(gqk_ref, gv_ref, la_ref, gg_ref, gn_ref, o_ref, st_s):
    @pl.when(pl.program_id(1) == 0)
    def _():
        st_s[...] = jnp.zeros_like(st_s)

    c = GLA_CHUNK
    nqk = GLA_HEADS * GLA_DK
    r_i = lax.broadcasted_iota(jnp.int32, (c, c), 0)
    c_i = lax.broadcasted_iota(jnp.int32, (c, c), 1)
    tri = r_i >= c_i
    tri_f = tri.astype(F32)
    gn = gn_ref[...]
    nchunk = GLA_ROWS // c
    heads = range(GLA_HEADS)
    hs = [slice(h * GLA_DK, (h + 1) * GLA_DK) for h in heads]
    vs = [slice(h * GLA_DV, (h + 1) * GLA_DV) for h in heads]

    local = []
    for ci in range(nchunk):
        rows = slice(ci * c, (ci + 1) * c)
        la = la_ref[rows, :]
        b = jnp.dot(tri_f, la, preferred_element_type=F32, precision=lax.Precision.HIGHEST)
        b_last = b[c - 1:c, :]
        q = gqk_ref[rows, :nqk]
        k = gqk_ref[rows, nqk:]
        q_dec = (q * jnp.exp(b)).astype(BF16)
        k_in = (k * jnp.exp(-b)).astype(BF16)
        k_out = (k * jnp.exp(b_last - b)).astype(BF16)
        decay = jnp.exp(b_last)
        intra, upd = [], []
        for h in heads:
            v = gv_ref[rows, vs[h]]
            a = lax.dot_general(q_dec[:, hs[h]], k_in[:, hs[h]], NT_DIMS, preferred_element_type=F32)
            a = jnp.where(tri, a, 0.0).astype(BF16)
            intra.append(jnp.dot(a, v, preferred_element_type=F32))
            upd.append(lax.dot_general(v, k_out[:, hs[h]], TN_DIMS, preferred_element_type=F32))
        local.append((rows, q_dec, decay, intra, upd))

    st = [st_s[h] for h in heads]
    for rows, q_dec, decay, intra, upd in local:
        for h in heads:
            o = intra[h] + lax.dot_general(q_dec[:, hs[h]], st[h].astype(BF16), NT_DIMS,
                                           preferred_element_type=F32)
            st[h] = st[h] * decay[:, hs[h]] + upd[h]
            gate = gg_ref[rows, vs[h]]
            o_ref[rows, vs[h]] = (_rms(o, gn) * (gate * jax.nn.sigmoid(gate))).astype(o_ref.dtype)
    for h in heads:
        st_s[h] = st[h]


def _gla(gqk, gv, la, gg, g_norm, batch, seq):
    n = batch * seq
    t = GLA_ROWS
    assert seq % t == 0
    ns = seq // t
    row = lambda b, i: (b * ns + i, 0)
    nqk = GLA_HEADS * GLA_DK
    nv = GLA_HEADS * GLA_DV
    return pl.pallas_call(
        _gla_kernel,
        out_shape=jax.ShapeDtypeStruct((n, nv), BF16),
        grid=(batch, ns),
        in_specs=[
            pl.BlockSpec((t, 2 * nqk), row),
            pl.BlockSpec((t, nv), row),
            pl.BlockSpec((t, nqk), row),
            pl.BlockSpec((t, nv), row),
            _const_spec((1, GLA_DV)),
        ],
        out_specs=pl.BlockSpec((t, nv), row),
        scratch_shapes=[pltpu.VMEM((GLA_HEADS, GLA_DV, GLA_DK), F32)],
        compiler_params=pltpu.CompilerParams(dimension_semantics=("arbitrary", "arbitrary")),
    )(gqk, gv, la, gg, g_norm.reshape(1, GLA_DV))


def kernel(x, g_ffn1_pre, w_ffn1_gate, w_ffn1_up, w_ffn1_down, g_ffn1_post, g_mix_pre, w_in, w_gla_a2,
           b_gla_a, g_gla_norm, w_out, g_mix_post, g_ffn2_pre, w_ffn2_gate, w_ffn2_up, w_ffn2_down,
           g_ffn2_post):
    batch, seq, d = x.shape
    h = x.reshape(batch * seq, d)
    for l in range(g_ffn1_pre.shape[0]):
        h = _ffn(h, g_ffn1_pre[l], w_ffn1_gate[l], w_ffn1_up[l], w_ffn1_down[l], g_ffn1_post[l])
        qt, iqt, vt, iwt, kk, ik, gqk, gv, gg, la = _inproj(h, g_mix_pre[l], w_in[l], w_gla_a2[l], b_gla_a[l])
        oa = _dsa(qt, iqt, iwt, kk, vt, ik, batch, seq)
        og = _gla(gqk, gv, la, gg, g_gla_norm[l], batch, seq)
        h = _mix_ffn(h, oa, og, w_out[l], g_mix_post[l],
                     g_ffn2_pre[l], w_ffn2_gate[l], w_ffn2_up[l], w_ffn2_down[l], g_ffn2_post[l])
    return h.reshape(batch, seq, d)
```

```python
import functools

import jax
import jax.numpy as jnp
from jax import lax
from jax.experimental import pallas as pl
from jax.experimental.pallas import tpu as pltpu

ATTN_HEADS = 8
ATTN_KV_HEADS = 2
ATTN_HEAD_DIM = 64
ATTN_REP = ATTN_HEADS // ATTN_KV_HEADS
IDX_HEADS = 8
IDX_DIM = 64
TOPK_MAX = 256
GLA_HEADS = 4
GLA_DK = 64
GLA_DV = 128
GLA_GATE_RANK = 16
GLA_TAU = 16.0
GLA_CHUNK = 64
EPS = 1e-6

V7X_LANES = 128
V7X_SUBLANES = 8
V7X_BF16_ROWS = 16
V7X_MXU_DIM = 256
V7X_VMEM_BYTES = 64 * 2**20

FFN_ROWS = 512
FF_CHUNK = V7X_MXU_DIM
DSA_TQ = 256
DSA_KC = DSA_TQ
DSA_SCORE_GROUP = 8
DSA_COUNT_GROUP = 8
DSA_ATT_GROUP = 4
DSA_ATT_LEAD = 3
GLA_ROWS = 1024
VT_ROWS = ATTN_HEAD_DIM + V7X_BF16_ROWS

INT_MIN = -2**31
INT16_MIN, INT16_MAX = -2**15, 2**15 - 1
NEG_BIG = -1e30
SOFTMAX_NORM_MIN = 2.0 ** -60
SOFTMAX_NORM_MAX = 2.0 ** 100

F32 = jnp.float32
BF16 = jnp.bfloat16
NT_DIMS = (((1,), (1,)), ((), ()))
TN_DIMS = (((0,), (0,)), ((), ()))


def _vmem_limit(nbytes):
    return int(min(nbytes * 1.25 + (8 << 20), V7X_VMEM_BYTES - (6 << 20)))


def _rms(x, g):
    return x * lax.rsqrt(jnp.mean(x * x, axis=-1, keepdims=True) + EPS) * g


def _const_spec(shape):
    nd = len(shape)
    return pl.BlockSpec(shape, lambda *_: (0,) * nd, pipeline_mode=pl.Buffered(1))


def _ffn_kernel(x_ref, gpre_ref, wg_ref, wu_ref, wd_ref, gpost_ref, o_ref):
    o_ref[...] = _ffn_residual(x_ref[...], gpre_ref, wg_ref, wu_ref, wd_ref, gpost_ref)


def _mix_ffn_kernel(x_ref, oa_ref, og_ref, wo_ref, gmix_ref, gpre_ref, wg_ref, wu_ref, wd_ref, gpost_ref, o_ref):
    m = jnp.dot(oa_ref[...], wo_ref[0], preferred_element_type=F32)
    m = m + jnp.dot(og_ref[...], wo_ref[1], preferred_element_type=F32)
    x2 = x_ref[...] + _rms(m, gmix_ref[...])
    o_ref[...] = _ffn_residual(x2, gpre_ref, wg_ref, wu_ref, wd_ref, gpost_ref)


def _ffn_residual(x, gpre_ref, wg_ref, wu_ref, wd_ref, gpost_ref):
    xn = _rms(x, gpre_ref[...]).astype(BF16)

    def cols(c):
        return slice(c * FF_CHUNK, (c + 1) * FF_CHUNK)

    def gate_up(c):
        return (jnp.dot(xn, wg_ref[:, cols(c)], preferred_element_type=F32),
                jnp.dot(xn, wu_ref[:, cols(c)], preferred_element_type=F32))

    nch = wg_ref.shape[1] // FF_CHUNK
    acc = None
    nxt = gate_up(0)
    for c in range(nch):
        g, u = nxt
        if c + 1 < nch:
            nxt = gate_up(c + 1)
        a = (g * jax.nn.sigmoid(g) * u).astype(BF16)
        d = jnp.dot(a, wd_ref[cols(c), :], preferred_element_type=F32)
        acc = d if acc is None else acc + d
    return x + 0.5 * _rms(acc, gpost_ref[...])


def _ffn_operands(d, g_pre, w_gate, w_up, w_down, g_post):
    dff = w_gate.shape[1]
    assert dff % FF_CHUNK == 0
    wg, wu, wd = w_gate.astype(BF16), w_up.astype(BF16), w_down.astype(BF16)
    specs = [_const_spec((1, d)), _const_spec(wg.shape), _const_spec(wu.shape), _const_spec(wd.shape),
             _const_spec((1, d))]
    args = (g_pre.reshape(1, d), wg, wu, wd, g_post.reshape(1, d))
    est = 4 * FFN_ROWS * d * 4 + 3 * d * dff * 2 + FFN_ROWS * d * 8 + 4 * FFN_ROWS * FF_CHUNK * 4
    return specs, args, est


def _ffn(x, g_pre, w_gate, w_up, w_down, g_post):
    n, d = x.shape
    assert n % FFN_ROWS == 0
    ffn_specs, ffn_args, est = _ffn_operands(d, g_pre, w_gate, w_up, w_down, g_post)
    x_spec = pl.BlockSpec((FFN_ROWS, d), lambda i: (i, 0))
    return pl.pallas_call(
        _ffn_kernel,
        out_shape=jax.ShapeDtypeStruct((n, d), F32),
        grid=(n // FFN_ROWS,),
        in_specs=[x_spec] + ffn_specs,
        out_specs=x_spec,
        compiler_params=pltpu.CompilerParams(
            dimension_semantics=("arbitrary",), vmem_limit_bytes=_vmem_limit(est)),
    )(x, *ffn_args)


def _mix_ffn(x, oa, og, w_out, g_mix, g_pre, w_gate, w_up, w_down, g_post):
    n, d = x.shape
    assert n % FFN_ROWS == 0
    ffn_specs, ffn_args, est = _ffn_operands(d, g_pre, w_gate, w_up, w_down, g_post)
    half = oa.shape[1]
    assert og.shape[1] == half and w_out.shape == (2 * half, d)
    wo = w_out.reshape(2, half, d).astype(BF16)
    row = lambda i: (i, 0)
    x_spec = pl.BlockSpec((FFN_ROWS, d), row)
    o_spec = pl.BlockSpec((FFN_ROWS, half), row)
    est += 4 * FFN_ROWS * half * 2 + 2 * half * d * 2 + FFN_ROWS * d * 4
    return pl.pallas_call(
        _mix_ffn_kernel,
        out_shape=jax.ShapeDtypeStruct((n, d), F32),
        grid=(n // FFN_ROWS,),
        in_specs=[x_spec, o_spec, o_spec, _const_spec(wo.shape), _const_spec((1, d))] + ffn_specs,
        out_specs=x_spec,
        compiler_params=pltpu.CompilerParams(
            dimension_semantics=("arbitrary",), vmem_limit_bytes=_vmem_limit(est)),
    )(x, oa, og, wo, g_mix.reshape(1, d), *ffn_args)


_K_OFF, _MISC_OFF, _GQ_OFF, _GK_OFF, _GV_OFF, _GG_OFF, _TOK_COLS = 0, 128, 256, 512, 768, 1280, 1792
_GA_LANE = 72
_QT_OFF, _IQT_OFF, _VT_OFF, _IWT_OFF, _T_ROWS = 0, 512, 1024, 1152, 1168


def _inproj_kernel(x_ref, g_ref, wtok_ref, wt_ref, wa2_ref, ba_ref,
                   qt_ref, iqt_ref, vt_ref, iwt_ref, kk_ref, ik_ref, gqk_ref, gv_ref, gg_ref, la_ref):
    h = _rms(x_ref[...], g_ref[...]).astype(BF16)
    t = h.shape[0]
    pt = lax.dot_general(wt_ref[...], h, NT_DIMS, preferred_element_type=F32)
    for i in range(ATTN_HEADS):
        qt_ref[i] = pt[_QT_OFF + 64 * i:_QT_OFF + 64 * (i + 1)].astype(BF16)
    for i in range(IDX_HEADS):
        iqt_ref[i] = pt[_IQT_OFF + 64 * i:_IQT_OFF + 64 * (i + 1)].astype(BF16)
    ones_row = (lax.broadcasted_iota(jnp.int32, (V7X_BF16_ROWS, t), 0) == 0).astype(F32)
    for g in range(ATTN_KV_HEADS):
        v_t = pt[_VT_OFF + 64 * g:_VT_OFF + 64 * (g + 1)]
        vt_ref[g] = jnp.concatenate([v_t, ones_row], axis=0).astype(BF16)
    iwt_ref[...] = pt[_IWT_OFF:_IWT_OFF + IDX_HEADS] * (IDX_HEADS ** -0.5)

    proj = jnp.dot(h, wtok_ref[...], preferred_element_type=F32)
    for g in range(ATTN_KV_HEADS):
        kk_ref[g] = proj[:, _K_OFF + 64 * g:_K_OFF + 64 * (g + 1)].astype(BF16)
    misc = proj[:, _MISC_OFF:_MISC_OFF + V7X_LANES]
    ik_ref[...] = misc[:, :IDX_DIM].astype(BF16)
    gqk_ref[...] = proj[:, _GQ_OFF:_GV_OFF]
    gv_ref[...] = proj[:, _GV_OFF:_GG_OFF].astype(BF16)
    gg_ref[...] = proj[:, _GG_OFF:_GG_OFF + GLA_HEADS * GLA_DV]
    z = jnp.dot(misc, wa2_ref[...], preferred_element_type=F32, precision=lax.Precision.HIGHEST)
    z = z + ba_ref[...]
    log_sig = jnp.minimum(z, 0.0) - jnp.log1p(jnp.exp(-jnp.abs(z)))
    la_ref[...] = log_sig * (1.0 / GLA_TAU)


def _pack_w_in(w_in):
    d = w_in.shape[0]
    sizes = (512, 128, 128, 512, 64, 8, 256, 256, 512, 16, 512)
    offs = [0]
    for s in sizes:
        offs.append(offs[-1] + s)
    aq, ak, av, iq, ik, iw, gq, gk, gv, ga, gg = [w_in[:, offs[i]:offs[i + 1]] for i in range(len(sizes))]
    z = lambda n: jnp.zeros((d, n), w_in.dtype)
    w_tok = jnp.concatenate([ak, ik, z(8), ga, z(40), gq * (GLA_DK ** -0.5), gk, gv, gg], axis=1)
    w_t = jnp.concatenate([aq * (ATTN_HEAD_DIM ** -0.5), iq * (IDX_DIM ** -0.5), av, iw, z(8)], axis=1).T
    assert w_tok.shape[1] == _TOK_COLS and w_t.shape[0] == _T_ROWS
    return w_tok.astype(BF16), w_t.astype(BF16)


def _inproj(x1, g_mix_pre, w_in, w_gla_a2, b_gla_a):
    n, d = x1.shape
    t = FFN_ROWS
    assert n % t == 0
    w_tok, w_t = _pack_w_in(w_in)
    nqk = GLA_HEADS * GLA_DK
    nv = GLA_HEADS * GLA_DV
    wa2 = jnp.zeros((V7X_LANES, nqk), F32).at[_GA_LANE:_GA_LANE + GLA_GATE_RANK].set(w_gla_a2)
    row = lambda i: (i, 0)
    row3 = lambda i: (0, i, 0)
    col3 = lambda i: (0, 0, i)
    out_shape = (
        jax.ShapeDtypeStruct((ATTN_HEADS, ATTN_HEAD_DIM, n), BF16),
        jax.ShapeDtypeStruct((IDX_HEADS, IDX_DIM, n), BF16),
        jax.ShapeDtypeStruct((ATTN_KV_HEADS, VT_ROWS, n), BF16),
        jax.ShapeDtypeStruct((IDX_HEADS, n), F32),
        jax.ShapeDtypeStruct((ATTN_KV_HEADS, n, ATTN_HEAD_DIM), BF16),
        jax.ShapeDtypeStruct((n, IDX_DIM), BF16),
        jax.ShapeDtypeStruct((n, 2 * nqk), F32),
        jax.ShapeDtypeStruct((n, nv), BF16),
        jax.ShapeDtypeStruct((n, nv), F32),
        jax.ShapeDtypeStruct((n, nqk), F32),
    )
    out_specs = (
        pl.BlockSpec((ATTN_HEADS, ATTN_HEAD_DIM, t), col3),
        pl.BlockSpec((IDX_HEADS, IDX_DIM, t), col3),
        pl.BlockSpec((ATTN_KV_HEADS, VT_ROWS, t), col3),
        pl.BlockSpec((IDX_HEADS, t), lambda i: (0, i)),
        pl.BlockSpec((ATTN_KV_HEADS, t, ATTN_HEAD_DIM), row3),
        pl.BlockSpec((t, IDX_DIM), row),
        pl.BlockSpec((t, 2 * nqk), row),
        pl.BlockSpec((t, nv), row),
        pl.BlockSpec((t, nv), row),
        pl.BlockSpec((t, nqk), row),
    )
    est = (2 * t * d * 4 + d * (_TOK_COLS + _T_ROWS) * 2 + t * (_TOK_COLS + _T_ROWS) * 4 * 2
           + 2 * t * 8 * 1024)
    return pl.pallas_call(
        _inproj_kernel,
        out_shape=out_shape,
        grid=(n // t,),
        in_specs=[
            pl.BlockSpec((t, d), row),
            _const_spec((1, d)),
            _const_spec(w_tok.shape),
            _const_spec(w_t.shape),
            _const_spec(wa2.shape),
            _const_spec((1, nqk)),
        ],
        out_specs=out_specs,
        compiler_params=pltpu.CompilerParams(
            dimension_semantics=("arbitrary",), vmem_limit_bytes=_vmem_limit(est)),
    )(x1, g_mix_pre.reshape(1, d), w_tok, w_t, wa2, b_gla_a.reshape(1, nqk))


def _dsa_kernel(qt_ref, iqt_ref, iwt_ref, kk_ref, vt_ref, ik_ref, o_ref,
                key_s, half_s, low_s, cnt_s, seen_s, bias_s, acc_s, ot_s, *, topk):
    tq = DSA_TQ
    kc = DSA_KC
    sub = V7X_SUBLANES
    i = pl.program_id(1)
    nchunks = i + 1

    def score_logits(j):
        ikc = ik_ref[pl.ds(pl.multiple_of(j * kc, kc), kc), :]
        iqcat = jnp.concatenate([iqt_ref[h] for h in range(IDX_HEADS)], axis=1)
        return jnp.dot(ikc, iqcat, preferred_element_type=F32)

    def score_keys(j, lg, diagonal):
        sc = jnp.zeros((kc, tq), F32)
        for h in range(IDX_HEADS):
            sc = sc + jnp.maximum(lg[:, h * tq:(h + 1) * tq], 0.0) * iwt_ref[h:h + 1, :]
        bits = pltpu.bitcast(sc, jnp.int32)
        key = jnp.where(bits < 0, bits ^ 0x7FFFFFFF, bits)
        if diagonal:
            kpos = lax.broadcasted_iota(jnp.int32, (kc, tq), 0)
            qpos = lax.broadcasted_iota(jnp.int32, (kc, tq), 1)
            key = jnp.where(kpos <= qpos, key, INT_MIN)
        key_s[j] = key
        half_s[j] = lax.shift_right_arithmetic(key, 16).astype(jnp.int16)
        low_s[j] = key.astype(jnp.int16) ^ jnp.int16(INT16_MIN)

    def sweep_chunks(n, group_fn, gsz):
        assert gsz & (gsz - 1) == 0

        def trip(q, c):
            group_fn(q * gsz, gsz)
            return c

        lax.fori_loop(0, lax.shift_right_logical(n, gsz.bit_length() - 1), trip, 0)
        done = n & -gsz
        part = gsz // 2
        while part:
            pl.when((n & part) != 0)(functools.partial(group_fn, done, part))
            done = done + (n & part)
            part //= 2

    def score_group(j0, count):
        lgs = [score_logits(j0 + s) for s in range(count)]
        for s in range(count):
            score_keys(j0 + s, lgs[s], False)

    sweep_chunks(i, score_group, DSA_SCORE_GROUP)
    score_keys(i, score_logits(i), True)

    qidx = i * tq + lax.broadcasted_iota(jnp.int32, (1, tq), 1)
    krow = jnp.minimum(topk, qidx + 1).astype(F32)

    def count(ind_fn):
        def body(j, acc):
            parts = [acc, None, None, None]
            for s in range(kc // sub):
                ind = ind_fn(key_s[j, s * sub:(s + 1) * sub, :], j * kc + s * sub)
                p = s % 4
                parts[p] = ind if parts[p] is None else parts[p] + ind
            return (parts[0] + parts[1]) + (parts[2] + parts[3])
        acc = lax.fori_loop(0, nchunks, body, jnp.zeros((sub, tq), F32))
        return jnp.sum(acc, axis=0, keepdims=True)

    def count_ge(cand):
        cb = jnp.broadcast_to(cand, (sub, tq))
        return count(lambda k, base: jnp.where(k >= cb, 1.0, 0.0))

    rows16 = V7X_BF16_ROWS

    def count16(cand):
        cb = jnp.broadcast_to(cand, (rows16, tq)).astype(jnp.int16)
        one, nil = jnp.int16(1), jnp.int16(0)
        cnt_s[...] = jnp.zeros_like(cnt_s)

        def group(j0, count):
            parts = [None] * 4
            for s in range(count * kc // rows16):
                c, r = divmod(s, kc // rows16)
                ind = jnp.where(half_s[j0 + c, r * rows16:(r + 1) * rows16, :] >= cb, one, nil)
                p = s % 4
                parts[p] = ind if parts[p] is None else parts[p] + ind
            cnt_s[...] += (parts[0] + parts[1]) + (parts[2] + parts[3])

        sweep_chunks(nchunks, group, DSA_COUNT_GROUP)
        return jnp.sum(cnt_s[...].astype(F32), axis=0, keepdims=True)

    def search16(count_at_min):
        zero = jnp.zeros((1, tq), jnp.int32)
        c0 = count16(zero)
        start = (jnp.where(c0 >= krow, zero, INT16_MIN), jnp.where(c0 >= krow, c0, count_at_min))

        def bit_body(it, carry):
            th, cth = carry
            cand = th | lax.shift_left(jnp.int32(1), 14 - it)
            c = count16(cand)
            return jnp.where(c >= krow, cand, th), jnp.where(c >= krow, c, cth)

        return lax.fori_loop(0, 15, bit_body, start)

    t_hi, c_hi = search16(jnp.full((1, tq), 1.0, F32) * (nchunks * kc).astype(F32))
    th16 = jnp.broadcast_to(t_hi, (kc, tq)).astype(jnp.int16)

    def low_halves(j0, count):
        for j in range(count):
            hi = half_s[j0 + j]
            half_s[j0 + j] = jnp.where(hi == th16, low_s[j0 + j],
                                       jnp.where(hi > th16, jnp.int16(INT16_MAX), jnp.int16(INT16_MIN)))

    sweep_chunks(nchunks, low_halves, DSA_SCORE_GROUP)
    t_lo, c_ge = search16(c_hi)
    t = lax.shift_left(t_hi, 16) | (t_lo - INT16_MIN)

    @pl.when(jnp.max(c_ge - krow) > 0.5)
    def _():
        saturated = jnp.max(jnp.where(t_lo == INT16_MAX, 1.0, 0.0)) > 0.5
        c_gt = lax.cond(saturated, lambda: count_ge(t + 1), lambda: count16(jnp.minimum(t_lo + 1, INT16_MAX)))
        need = krow - c_gt
        tb_full = jnp.broadcast_to(t, (kc, tq))
        tri = (lax.broadcasted_iota(jnp.int32, (kc, kc), 0)
               >= lax.broadcasted_iota(jnp.int32, (kc, kc), 1)).astype(BF16)

        seen_s[...] = jnp.zeros_like(seen_s)

        def drop_surplus(j0, count):
            ks = [key_s[j0 + s] for s in range(count)]
            prefix = [jnp.dot(tri, jnp.where(k == tb_full, 1.0, 0.0).astype(BF16), preferred_element_type=F32)
                      for k in ks]
            seen = seen_s[...]
            for s in range(count):
                rank = seen + prefix[s]
                key_s[j0 + s] = jnp.where(ks[s] == tb_full, jnp.where(rank > need, INT_MIN, ks[s]), ks[s])
                seen = rank[kc - 1:kc, :]
            seen_s[...] = seen

        sweep_chunks(nchunks, drop_surplus, DSA_SCORE_GROUP)

    def masked_scores(j, g, slot=0):
        koff = pl.multiple_of(j * kc, kc)
        qcat = jnp.concatenate([qt_ref[g * ATTN_REP + r] for r in range(ATTN_REP)], axis=1)
        bias = jnp.concatenate([bias_s[slot]] * ATTN_REP, axis=1)
        kg = kk_ref[g, pl.ds(koff, kc), :]
        vg = vt_ref[g, :, pl.ds(koff, kc)]
        return jnp.dot(kg, qcat, preferred_element_type=F32) + bias, vg

    def set_bias(j, slot=0):
        bias_s[slot] = jnp.where(key_s[j] >= jnp.broadcast_to(t, (kc, tq)), 0.0, NEG_BIG)

    acc_s[...] = jnp.zeros_like(acc_s)

    def att_group(j0, count):
        units = [(slot, g) for slot in range(count) for g in range(ATTN_KV_HEADS)]
        for slot in range(count):
            set_bias(j0 + slot, slot)
        scores = {}

        def issue(u):
            slot, g = units[u]
            scores[u] = masked_scores(j0 + slot, g, slot)

        for u in range(min(DSA_ATT_LEAD, len(units))):
            issue(u)
        for u, (slot, g) in enumerate(units):
            s, vg = scores.pop(u)
            acc_s[g] += jnp.dot(vg, jnp.exp(s).astype(BF16), preferred_element_type=F32)
            if u + DSA_ATT_LEAD < len(units):
                issue(u + DSA_ATT_LEAD)

    sweep_chunks(nchunks, att_group, DSA_ATT_GROUP)

    norm = acc_s[:, ATTN_HEAD_DIM:ATTN_HEAD_DIM + 1, :]
    in_range = jnp.where(norm >= SOFTMAX_NORM_MIN, jnp.where(norm <= SOFTMAX_NORM_MAX, 1.0, 0.0), 0.0)

    @pl.when(jnp.min(in_range) < 0.5)
    def _():
        acc_s[...] = jnp.zeros_like(acc_s)

        def online_body(j, ms):
            set_bias(j)
            out = []
            for g in range(ATTN_KV_HEADS):
                s, vg = masked_scores(j, g)
                m_new = jnp.maximum(ms[g], jnp.max(s, axis=0, keepdims=True))
                p = jnp.exp(s - m_new).astype(BF16)
                pv = jnp.dot(vg, p, preferred_element_type=F32)
                acc_s[g] = jnp.exp(ms[g] - m_new) * acc_s[g] + pv
                out.append(m_new)
            return tuple(out)

        m_init = tuple(jnp.full((1, ATTN_REP * tq), NEG_BIG, F32) for _ in range(ATTN_KV_HEADS))
        lax.fori_loop(0, nchunks, online_body, m_init)

    for h in range(ATTN_HEADS):
        g, r = divmod(h, ATTN_REP)
        a = acc_s[g, :, r * tq:(r + 1) * tq]
        ot_s[h * ATTN_HEAD_DIM:(h + 1) * ATTN_HEAD_DIM, :] = (
            a[:ATTN_HEAD_DIM] / a[ATTN_HEAD_DIM:ATTN_HEAD_DIM + 1])
    o_ref[...] = ot_s[...].T.astype(o_ref.dtype)


def _dsa(qt, iqt, iwt, kk, vt, ik, batch, seq):
    n = batch * seq
    tq, kc = DSA_TQ, DSA_KC
    assert seq % tq == 0
    nq = seq // tq
    topk = min(TOPK_MAX, seq // 4)
    nch = seq // kc
    nout = ATTN_HEADS * ATTN_HEAD_DIM
    qmap = lambda b, i: (0, 0, b * nq + i)
    est = (nch * kc * tq * 8 + 3 * DSA_ATT_GROUP * ATTN_REP * kc * tq * 4 + ATTN_HEADS * (VT_ROWS + 8) * tq * 4 + nout * tq * 4
           + ATTN_KV_HEADS * seq * (V7X_LANES + VT_ROWS) * 2 + seq * V7X_LANES * 2
           + 2 * (2 * ATTN_HEADS * ATTN_HEAD_DIM * tq * 2 + 8 * tq * 4 + tq * nout * 2))
    return pl.pallas_call(
        functools.partial(_dsa_kernel, topk=topk),
        out_shape=jax.ShapeDtypeStruct((n, nout), BF16),
        grid=(batch, nq),
        in_specs=[
            pl.BlockSpec((ATTN_HEADS, ATTN_HEAD_DIM, tq), qmap),
            pl.BlockSpec((IDX_HEADS, IDX_DIM, tq), qmap),
            pl.BlockSpec((IDX_HEADS, tq), lambda b, i: (0, b * nq + i)),
            pl.BlockSpec((ATTN_KV_HEADS, seq, ATTN_HEAD_DIM), lambda b, i: (0, b, 0),
                         pipeline_mode=pl.Buffered(1)),
            pl.BlockSpec((ATTN_KV_HEADS, VT_ROWS, seq), lambda b, i: (0, 0, b),
                         pipeline_mode=pl.Buffered(1)),
            pl.BlockSpec((seq, IDX_DIM), lambda b, i: (b, 0), pipeline_mode=pl.Buffered(1)),
        ],
        out_specs=pl.BlockSpec((tq, nout), lambda b, i: (b * nq + i, 0)),
        scratch_shapes=[
            pltpu.VMEM((nch, kc, tq), jnp.int32),
            pltpu.VMEM((nch, kc, tq), jnp.int16),
            pltpu.VMEM((nch, kc, tq), jnp.int16),
            pltpu.VMEM((V7X_BF16_ROWS, tq), jnp.int16),
            pltpu.VMEM((1, tq), F32),
            pltpu.VMEM((DSA_ATT_GROUP, kc, tq), F32),
            pltpu.VMEM((ATTN_KV_HEADS, VT_ROWS, ATTN_REP * tq), F32),
            pltpu.VMEM((nout, tq), F32),
        ],
        compiler_params=pltpu.CompilerParams(
            dimension_semantics=("arbitrary", "arbitrary"), vmem_limit_bytes=_vmem_limit(est)),
    )(qt, iqt, iwt, kk, vt, ik)


def _gla_kernel(gqk_ref, gv_ref, la_ref, gg_ref, gn_ref, o_ref, st_s):
    @pl.when(pl.program_id(1) == 0)
    def _():
        st_s[...] = jnp.zeros_like(st_s)

    c = GLA_CHUNK
    nqk = GLA_HEADS * GLA_DK
    r_i = lax.broadcasted_iota(jnp.int32, (c, c), 0)
    c_i = lax.broadcasted_iota(jnp.int32, (c, c), 1)
    tri = r_i >= c_i
    tri_f = tri.astype(F32)
    gn = gn_ref[...]
    nchunk = GLA_ROWS // c
    heads = range(GLA_HEADS)
    hs = [slice(h * GLA_DK, (h + 1) * GLA_DK) for h in heads]
    vs = [slice(h * GLA_DV, (h + 1) * GLA_DV) for h in heads]

    local = []
    for ci in range(nchunk):
        rows = slice(ci * c, (ci + 1) * c)
        la = la_ref[rows, :]
        b = jnp.dot(tri_f, la, preferred_element_type=F32, precision=lax.Precision.HIGHEST)
        b_last = b[c - 1:c, :]
        q = gqk_ref[rows, :nqk]
        k = gqk_ref[rows, nqk:]
        q_dec = (q * jnp.exp(b)).astype(BF16)
        k_in = (k * jnp.exp(-b)).astype(BF16)
        k_out = (k * jnp.exp(b_last - b)).astype(BF16)
        decay = jnp.exp(b_last)
        intra, upd = [], []
        for h in heads:
            v = gv_ref[rows, vs[h]]
            a = lax.dot_general(q_dec[:, hs[h]], k_in[:, hs[h]], NT_DIMS, preferred_element_type=F32)
            a = jnp.where(tri, a, 0.0).astype(BF16)
            intra.append(jnp.dot(a, v, preferred_element_type=F32))
            upd.append(lax.dot_general(v, k_out[:, hs[h]], TN_DIMS, preferred_element_type=F32))
        local.append((rows, q_dec, decay, intra, upd))

    st = [st_s[h] for h in heads]
    for rows, q_dec, decay, intra, upd in local:
        for h in heads:
            o = intra[h] + lax.dot_general(q_dec[:, hs[h]], st[h].astype(BF16), NT_DIMS,
                                           preferred_element_type=F32)
            st[h] = st[h] * decay[:, hs[h]] + upd[h]
            gate = gg_ref[rows, vs[h]]
            o_ref[rows, vs[h]] = (_rms(o, gn) * (gate * jax.nn.sigmoid(gate))).astype(o_ref.dtype)
    for h in heads:
        st_s[h] = st[h]


def _gla(gqk, gv, la, gg, g_norm, batch, seq):
    n = batch * seq
    t = GLA_ROWS
    assert seq % t == 0
    ns = seq // t
    row = lambda b, i: (b * ns + i, 0)
    nqk = GLA_HEADS * GLA_DK
    nv = GLA_HEADS * GLA_DV
    return pl.pallas_call(
        _gla_kernel,
        out_shape=jax.ShapeDtypeStruct((n, nv), BF16),
        grid=(batch, ns),
        in_specs=[
            pl.BlockSpec((t, 2 * nqk), row),
            pl.BlockSpec((t, nv), row),
            pl.BlockSpec((t, nqk), row),
            pl.BlockSpec((t, nv), row),
            _const_spec((1, GLA_DV)),
        ],
        out_specs=pl.BlockSpec((t, nv), row),
        scratch_shapes=[pltpu.VMEM((GLA_HEADS, GLA_DV, GLA_DK), F32)],
        compiler_params=pltpu.CompilerParams(dimension_semantics=("arbitrary", "arbitrary")),
    )(gqk, gv, la, gg, g_norm.reshape(1, GLA_DV))


def kernel(x, g_ffn1_pre, w_ffn1_gate, w_ffn1_up, w_ffn1_down, g_ffn1_post, g_mix_pre, w_in, w_gla_a2,
           b_gla_a, g_gla_norm, w_out, g_mix_post, g_ffn2_pre, w_ffn2_gate, w_ffn2_up, w_ffn2_down,
           g_ffn2_post):
    batch, seq, d = x.shape
    h = x.reshape(batch * seq, d)
    for l in range(g_ffn1_pre.shape[0]):
        h = _ffn(h, g_ffn1_pre[l], w_ffn1_gate[l], w_ffn1_up[l], w_ffn1_down[l], g_ffn1_post[l])
        qt, iqt, vt, iwt, kk, ik, gqk, gv, gg, la = _inproj(h, g_mix_pre[l], w_in[l], w_gla_a2[l], b_gla_a[l])
        oa = _dsa(qt, iqt, iwt, kk, vt, ik, batch, seq)
        og = _gla(gqk, gv, la, gg, g_gla_norm[l], batch, seq)
        h = _mix_ffn(h, oa, og, w_out[l], g_mix_post[l],
                     g_ffn2_pre[l], w_ffn2_gate[l], w_ffn2_up[l], w_ffn2_down[l], g_ffn2_post[l])
    return h.reshape(batch, seq, d)
```

```python
import functools

import jax
import jax.numpy as jnp
from jax import lax
from jax.experimental import pallas as pl
from jax.experimental.pallas import tpu as pltpu

ATTN_HEADS = 8
ATTN_KV_HEADS = 2
ATTN_HEAD_DIM = 64
ATTN_REP = ATTN_HEADS // ATTN_KV_HEADS
IDX_HEADS = 8
IDX_DIM = 64
TOPK_MAX = 256
GLA_HEADS = 4
GLA_DK = 64
GLA_DV = 128
GLA_GATE_RANK = 16
GLA_TAU = 16.0
GLA_CHUNK = 64
EPS = 1e-6

V7X_LANES = 128
V7X_SUBLANES = 8
V7X_BF16_ROWS = 16
V7X_MXU_DIM = 256
V7X_VMEM_BYTES = 64 * 2**20

FFN_ROWS = 1024
FF_CHUNK = V7X_MXU_DIM
DSA_TQ = 256
DSA_KC = DSA_TQ
DSA_SCORE_GROUP = 8
DSA_COUNT_GROUP = 8
DSA_ATT_GROUP = 4
DSA_ATT_LEAD = 3
GLA_ROWS = 1024
VT_ROWS = ATTN_HEAD_DIM + V7X_BF16_ROWS

INT_MIN = -2**31
INT16_MIN, INT16_MAX = -2**15, 2**15 - 1
NEG_BIG = -1e30
SOFTMAX_NORM_MIN = 2.0 ** -60
SOFTMAX_NORM_MAX = 2.0 ** 100

F32 = jnp.float32
BF16 = jnp.bfloat16
NT_DIMS = (((1,), (1,)), ((), ()))
TN_DIMS = (((0,), (0,)), ((), ()))


def _vmem_limit(nbytes):
    return int(min(nbytes * 1.25 + (8 << 20), V7X_VMEM_BYTES - (6 << 20)))


def _rms(x, g):
    return x * lax.rsqrt(jnp.mean(x * x, axis=-1, keepdims=True) + EPS) * g


def _const_spec(shape):
    nd = len(shape)
    return pl.BlockSpec(shape, lambda *_: (0,) * nd, pipeline_mode=pl.Buffered(1))


def _ffn_kernel(x_ref, gpre_ref, wg_ref, wu_ref, wd_ref, gpost_ref, o_ref):
    o_ref[...] = _ffn_residual(x_ref[...], gpre_ref, wg_ref, wu_ref, wd_ref, gpost_ref)


def _mix_ffn_kernel(x_ref, oa_ref, og_ref, wo_ref, gmix_ref, gpre_ref, wg_ref, wu_ref, wd_ref, gpost_ref, o_ref):
    m = jnp.dot(oa_ref[...], wo_ref[0], preferred_element_type=F32)
    m = m + jnp.dot(og_ref[...], wo_ref[1], preferred_element_type=F32)
    x2 = x_ref[...] + _rms(m, gmix_ref[...])
    o_ref[...] = _ffn_residual(x2, gpre_ref, wg_ref, wu_ref, wd_ref, gpost_ref)


def _ffn_residual(x, gpre_ref, wg_ref, wu_ref, wd_ref, gpost_ref):
    xn = _rms(x, gpre_ref[...]).astype(BF16)

    def cols(c):
        return slice(c * FF_CHUNK, (c + 1) * FF_CHUNK)

    def gate_up(c):
        return (jnp.dot(xn, wg_ref[:, cols(c)], preferred_element_type=F32),
                jnp.dot(xn, wu_ref[:, cols(c)], preferred_element_type=F32))

    nch = wg_ref.shape[1] // FF_CHUNK
    acc = None
    nxt = gate_up(0)
    for c in range(nch):
        g, u = nxt
        if c + 1 < nch:
            nxt = gate_up(c + 1)
        a = (g * jax.nn.sigmoid(g) * u).astype(BF16)
        d = jnp.dot(a, wd_ref[cols(c), :], preferred_element_type=F32)
        acc = d if acc is None else acc + d
    return x + 0.5 * _rms(acc, gpost_ref[...])


def _ffn_operands(d, g_pre, w_gate, w_up, w_down, g_post):
    dff = w_gate.shape[1]
    assert dff % FF_CHUNK == 0
    wg, wu, wd = w_gate.astype(BF16), w_up.astype(BF16), w_down.astype(BF16)
    specs = [_const_spec((1, d)), _const_spec(wg.shape), _const_spec(wu.shape), _const_spec(wd.shape),
             _const_spec((1, d))]
    args = (g_pre.reshape(1, d), wg, wu, wd, g_post.reshape(1, d))
    est = 4 * FFN_ROWS * d * 4 + 3 * d * dff * 2 + FFN_ROWS * d * 8 + 4 * FFN_ROWS * FF_CHUNK * 4
    return specs, args, est


def _ffn(x, g_pre, w_gate, w_up, w_down, g_post):
    n, d = x.shape
    assert n % FFN_ROWS == 0
    ffn_specs, ffn_args, est = _ffn_operands(d, g_pre, w_gate, w_up, w_down, g_post)
    x_spec = pl.BlockSpec((FFN_ROWS, d), lambda i: (i, 0))
    return pl.pallas_call(
        _ffn_kernel,
        out_shape=jax.ShapeDtypeStruct((n, d), F32),
        grid=(n // FFN_ROWS,),
        in_specs=[x_spec] + ffn_specs,
        out_specs=x_spec,
        compiler_params=pltpu.CompilerParams(
            dimension_semantics=("arbitrary",), vmem_limit_bytes=_vmem_limit(est)),
    )(x, *ffn_args)


def _mix_ffn(x, oa, og, w_out, g_mix, g_pre, w_gate, w_up, w_down, g_post):
    n, d = x.shape
    assert n % FFN_ROWS == 0
    ffn_specs, ffn_args, est = _ffn_operands(d, g_pre, w_gate, w_up, w_down, g_post)
    half = oa.shape[1]
    assert og.shape[1] == half and w_out.shape == (2 * half, d)
    wo = w_out.reshape(2, half, d).astype(BF16)
    row = lambda i: (i, 0)
    x_spec = pl.BlockSpec((FFN_ROWS, d), row)
    o_spec = pl.BlockSpec((FFN_ROWS, half), row)
    est += 4 * FFN_ROWS * half * 2 + 2 * half * d * 2 + FFN_ROWS * d * 4
    return pl.pallas_call(
        _mix_ffn_kernel,
        out_shape=jax.ShapeDtypeStruct((n, d), F32),
        grid=(n // FFN_ROWS,),
        in_specs=[x_spec, o_spec, o_spec, _const_spec(wo.shape), _const_spec((1, d))] + ffn_specs,
        out_specs=x_spec,
        compiler_params=pltpu.CompilerParams(
            dimension_semantics=("arbitrary",), vmem_limit_bytes=_vmem_limit(est)),
    )(x, oa, og, wo, g_mix.reshape(1, d), *ffn_args)


_K_OFF, _MISC_OFF, _GQ_OFF, _GK_OFF, _GV_OFF, _GG_OFF, _TOK_COLS = 0, 128, 256, 512, 768, 1280, 1792
_GA_LANE = 72
_QT_OFF, _IQT_OFF, _VT_OFF, _IWT_OFF, _T_ROWS = 0, 512, 1024, 1152, 1168


def _inproj_kernel(x_ref, g_ref, wtok_ref, wt_ref, wa2_ref, ba_ref,
                   qt_ref, iqt_ref, vt_ref, iwt_ref, kk_ref, ik_ref, gqk_ref, gv_ref, gg_ref, la_ref):
    h = _rms(x_ref[...], g_ref[...]).astype(BF16)
    t = h.shape[0]
    pt = lax.dot_general(wt_ref[...], h, NT_DIMS, preferred_element_type=F32)
    for i in range(ATTN_HEADS):
        qt_ref[i] = pt[_QT_OFF + 64 * i:_QT_OFF + 64 * (i + 1)].astype(BF16)
    for i in range(IDX_HEADS):
        iqt_ref[i] = pt[_IQT_OFF + 64 * i:_IQT_OFF + 64 * (i + 1)].astype(BF16)
    ones_row = (lax.broadcasted_iota(jnp.int32, (V7X_BF16_ROWS, t), 0) == 0).astype(F32)
    for g in range(ATTN_KV_HEADS):
        v_t = pt[_VT_OFF + 64 * g:_VT_OFF + 64 * (g + 1)]
        vt_ref[g] = jnp.concatenate([v_t, ones_row], axis=0).astype(BF16)
    iwt_ref[...] = pt[_IWT_OFF:_IWT_OFF + IDX_HEADS] * (IDX_HEADS ** -0.5)

    proj = jnp.dot(h, wtok_ref[...], preferred_element_type=F32)
    for g in range(ATTN_KV_HEADS):
        kk_ref[g] = proj[:, _K_OFF + 64 * g:_K_OFF + 64 * (g + 1)].astype(BF16)
    misc = proj[:, _MISC_OFF:_MISC_OFF + V7X_LANES]
    ik_ref[...] = misc[:, :IDX_DIM].astype(BF16)
    gqk_ref[...] = proj[:, _GQ_OFF:_GV_OFF]
    gv_ref[...] = proj[:, _GV_OFF:_GG_OFF].astype(BF16)
    gg_ref[...] = proj[:, _GG_OFF:_GG_OFF + GLA_HEADS * GLA_DV]
    z = jnp.dot(misc, wa2_ref[...], preferred_element_type=F32, precision=lax.Precision.HIGHEST)
    z = z + ba_ref[...]
    log_sig = jnp.minimum(z, 0.0) - jnp.log1p(jnp.exp(-jnp.abs(z)))
    la_ref[...] = log_sig * (1.0 / GLA_TAU)


def _pack_w_in(w_in):
    d = w_in.shape[0]
    sizes = (512, 128, 128, 512, 64, 8, 256, 256, 512, 16, 512)
    offs = [0]
    for s in sizes:
        offs.append(offs[-1] + s)
    aq, ak, av, iq, ik, iw, gq, gk, gv, ga, gg = [w_in[:, offs[i]:offs[i + 1]] for i in range(len(sizes))]
    z = lambda n: jnp.zeros((d, n), w_in.dtype)
    w_tok = jnp.concatenate([ak, ik, z(8), ga, z(40), gq * (GLA_DK ** -0.5), gk, gv, gg], axis=1)
    w_t = jnp.concatenate([aq * (ATTN_HEAD_DIM ** -0.5), iq * (IDX_DIM ** -0.5), av, iw, z(8)], axis=1).T
    assert w_tok.shape[1] == _TOK_COLS and w_t.shape[0] == _T_ROWS
    return w_tok.astype(BF16), w_t.astype(BF16)


def _inproj(x1, g_mix_pre, w_in, w_gla_a2, b_gla_a):
    n, d = x1.shape
    t = FFN_ROWS
    assert n % t == 0
    w_tok, w_t = _pack_w_in(w_in)
    nqk = GLA_HEADS * GLA_DK
    nv = GLA_HEADS * GLA_DV
    wa2 = jnp.zeros((V7X_LANES, nqk), F32).at[_GA_LANE:_GA_LANE + GLA_GATE_RANK].set(w_gla_a2)
    row = lambda i: (i, 0)
    row3 = lambda i: (0, i, 0)
    col3 = lambda i: (0, 0, i)
    out_shape = (
        jax.ShapeDtypeStruct((ATTN_HEADS, ATTN_HEAD_DIM, n), BF16),
        jax.ShapeDtypeStruct((IDX_HEADS, IDX_DIM, n), BF16),
        jax.ShapeDtypeStruct((ATTN_KV_HEADS, VT_ROWS, n), BF16),
        jax.ShapeDtypeStruct((IDX_HEADS, n), F32),
        jax.ShapeDtypeStruct((ATTN_KV_HEADS, n, ATTN_HEAD_DIM), BF16),
        jax.ShapeDtypeStruct((n, IDX_DIM), BF16),
        jax.ShapeDtypeStruct((n, 2 * nqk), F32),
        jax.ShapeDtypeStruct((n, nv), BF16),
        jax.ShapeDtypeStruct((n, nv), F32),
        jax.ShapeDtypeStruct((n, nqk), F32),
    )
    out_specs = (
        pl.BlockSpec((ATTN_HEADS, ATTN_HEAD_DIM, t), col3),
        pl.BlockSpec((IDX_HEADS, IDX_DIM, t), col3),
        pl.BlockSpec((ATTN_KV_HEADS, VT_ROWS, t), col3),
        pl.BlockSpec((IDX_HEADS, t), lambda i: (0, i)),
        pl.BlockSpec((ATTN_KV_HEADS, t, ATTN_HEAD_DIM), row3),
        pl.BlockSpec((t, IDX_DIM), row),
        pl.BlockSpec((t, 2 * nqk), row),
        pl.BlockSpec((t, nv), row),
        pl.BlockSpec((t, nv), row),
        pl.BlockSpec((t, nqk), row),
    )
    est = (2 * t * d * 4 + d * (_TOK_COLS + _T_ROWS) * 2 + t * (_TOK_COLS + _T_ROWS) * 4 * 2
           + 2 * t * 8 * 1024)
    return pl.pallas_call(
        _inproj_kernel,
        out_shape=out_shape,
        grid=(n // t,),
        in_specs=[
            pl.BlockSpec((t, d), row),
            _const_spec((1, d)),
            _const_spec(w_tok.shape),
            _const_spec(w_t.shape),
            _const_spec(wa2.shape),
            _const_spec((1, nqk)),
        ],
        out_specs=out_specs,
        compiler_params=pltpu.CompilerParams(
            dimension_semantics=("arbitrary",), vmem_limit_bytes=_vmem_limit(est)),
    )(x1, g_mix_pre.reshape(1, d), w_tok, w_t, wa2, b_gla_a.reshape(1, nqk))


def _dsa_kernel(qt_ref, iqt_ref, iwt_ref, kk_ref, vt_ref, ik_ref, o_ref,
                key_s, half_s, low_s, cnt_s, seen_s, bias_s, acc_s, ot_s, *, topk):
    tq = DSA_TQ
    kc = DSA_KC
    sub = V7X_SUBLANES
    i = pl.program_id(1)
    nchunks = i + 1

    def score_logits(j):
        ikc = ik_ref[pl.ds(pl.multiple_of(j * kc, kc), kc), :]
        iqcat = jnp.concatenate([iqt_ref[h] for h in range(IDX_HEADS)], axis=1)
        return jnp.dot(ikc, iqcat, preferred_element_type=F32)

    def score_keys(j, lg, diagonal):
        sc = jnp.zeros((kc, tq), F32)
        for h in range(IDX_HEADS):
            sc = sc + jnp.maximum(lg[:, h * tq:(h + 1) * tq], 0.0) * iwt_ref[h:h + 1, :]
        bits = pltpu.bitcast(sc, jnp.int32)
        key = jnp.where(bits < 0, bits ^ 0x7FFFFFFF, bits)
        if diagonal:
            kpos = lax.broadcasted_iota(jnp.int32, (kc, tq), 0)
            qpos = lax.broadcasted_iota(jnp.int32, (kc, tq), 1)
            key = jnp.where(kpos <= qpos, key, INT_MIN)
        key_s[j] = key
        half_s[j] = lax.shift_right_arithmetic(key, 16).astype(jnp.int16)
        low_s[j] = key.astype(jnp.int16) ^ jnp.int16(INT16_MIN)

    def sweep_chunks(n, group_fn, gsz):
        assert gsz & (gsz - 1) == 0

        def trip(q, c):
            group_fn(q * gsz, gsz)
            return c

        lax.fori_loop(0, lax.shift_right_logical(n, gsz.bit_length() - 1), trip, 0)
        done = n & -gsz
        part = gsz // 2
        while part:
            pl.when((n & part) != 0)(functools.partial(group_fn, done, part))
            done = done + (n & part)
            part //= 2

    def score_group(j0, count):
        lgs = [score_logits(j0 + s) for s in range(count)]
        for s in range(count):
            score_keys(j0 + s, lgs[s], False)

    sweep_chunks(i, score_group, DSA_SCORE_GROUP)
    score_keys(i, score_logits(i), True)

    qidx = i * tq + lax.broadcasted_iota(jnp.int32, (1, tq), 1)
    krow = jnp.minimum(topk, qidx + 1).astype(F32)

    def count(ind_fn):
        def body(j, acc):
            parts = [acc, None, None, None]
            for s in range(kc // sub):
                ind = ind_fn(key_s[j, s * sub:(s + 1) * sub, :], j * kc + s * sub)
                p = s % 4
                parts[p] = ind if parts[p] is None else parts[p] + ind
            return (parts[0] + parts[1]) + (parts[2] + parts[3])
        acc = lax.fori_loop(0, nchunks, body, jnp.zeros((sub, tq), F32))
        return jnp.sum(acc, axis=0, keepdims=True)

    def count_ge(cand):
        cb = jnp.broadcast_to(cand, (sub, tq))
        return count(lambda k, base: jnp.where(k >= cb, 1.0, 0.0))

    rows16 = V7X_BF16_ROWS

    def count16(cand):
        cb = jnp.broadcast_to(cand, (rows16, tq)).astype(jnp.int16)
        one, nil = jnp.int16(1), jnp.int16(0)
        cnt_s[...] = jnp.zeros_like(cnt_s)

        def group(j0, count):
            parts = [None] * 4
            for s in range(count * kc // rows16):
                c, r = divmod(s, kc // rows16)
                ind = jnp.where(half_s[j0 + c, r * rows16:(r + 1) * rows16, :] >= cb, one, nil)
                p = s % 4
                parts[p] = ind if parts[p] is None else parts[p] + ind
            cnt_s[...] += (parts[0] + parts[1]) + (parts[2] + parts[3])

        sweep_chunks(nchunks, group, DSA_COUNT_GROUP)
        return jnp.sum(cnt_s[...].astype(F32), axis=0, keepdims=True)

    def search16(count_at_min):
        zero = jnp.zeros((1, tq), jnp.int32)
        c0 = count16(zero)
        start = (jnp.where(c0 >= krow, zero, INT16_MIN), jnp.where(c0 >= krow, c0, count_at_min))

        def bit_body(it, carry):
            th, cth = carry
            cand = th | lax.shift_left(jnp.int32(1), 14 - it)
            c = count16(cand)
            return jnp.where(c >= krow, cand, th), jnp.where(c >= krow, c, cth)

        return lax.fori_loop(0, 15, bit_body, start)

    t_hi, c_hi = search16(jnp.full((1, tq), 1.0, F32) * (nchunks * kc).astype(F32))
    th16 = jnp.broadcast_to(t_hi, (kc, tq)).astype(jnp.int16)

    def low_halves(j0, count):
        for j in range(count):
            hi = half_s[j0 + j]
            half_s[j0 + j] = jnp.where(hi == th16, low_s[j0 + j],
                                       jnp.where(hi > th16, jnp.int16(INT16_MAX), jnp.int16(INT16_MIN)))

    sweep_chunks(nchunks, low_halves, DSA_SCORE_GROUP)
    t_lo, c_ge = search16(c_hi)
    t = lax.shift_left(t_hi, 16) | (t_lo - INT16_MIN)

    @pl.when(jnp.max(c_ge - krow) > 0.5)
    def _():
        saturated = jnp.max(jnp.where(t_lo == INT16_MAX, 1.0, 0.0)) > 0.5
        c_gt = lax.cond(saturated, lambda: count_ge(t + 1), lambda: count16(jnp.minimum(t_lo + 1, INT16_MAX)))
        need = krow - c_gt
        tb_full = jnp.broadcast_to(t, (kc, tq))
        tri = (lax.broadcasted_iota(jnp.int32, (kc, kc), 0)
               >= lax.broadcasted_iota(jnp.int32, (kc, kc), 1)).astype(BF16)

        seen_s[...] = jnp.zeros_like(seen_s)

        def drop_surplus(j0, count):
            ks = [key_s[j0 + s] for s in range(count)]
            prefix = [jnp.dot(tri, jnp.where(k == tb_full, 1.0, 0.0).astype(BF16), preferred_element_type=F32)
                      for k in ks]
            seen = seen_s[...]
            for s in range(count):
                rank = seen + prefix[s]
                key_s[j0 + s] = jnp.where(ks[s] == tb_full, jnp.where(rank > need, INT_MIN, ks[s]), ks[s])
                seen = rank[kc - 1:kc, :]
            seen_s[...] = seen

        sweep_chunks(nchunks, drop_surplus, DSA_SCORE_GROUP)

    def masked_scores(j, g, slot=0):
        koff = pl.multiple_of(j * kc, kc)
        qcat = jnp.concatenate([qt_ref[g * ATTN_REP + r] for r in range(ATTN_REP)], axis=1)
        bias = jnp.concatenate([bias_s[slot]] * ATTN_REP, axis=1)
        kg = kk_ref[g, pl.ds(koff, kc), :]
        vg = vt_ref[g, :, pl.ds(koff, kc)]
        return jnp.dot(kg, qcat, preferred_element_type=F32) + bias, vg

    def set_bias(j, slot=0):
        bias_s[slot] = jnp.where(key_s[j] >= jnp.broadcast_to(t, (kc, tq)), 0.0, NEG_BIG)

    acc_s[...] = jnp.zeros_like(acc_s)

    def att_group(j0, count):
        units = [(slot, g) for slot in range(count) for g in range(ATTN_KV_HEADS)]
        for slot in range(count):
            set_bias(j0 + slot, slot)
        scores = {}

        def issue(u):
            slot, g = units[u]
            scores[u] = masked_scores(j0 + slot, g, slot)

        for u in range(min(DSA_ATT_LEAD, len(units))):
            issue(u)
        for u, (slot, g) in enumerate(units):
            s, vg = scores.pop(u)
            acc_s[g] += jnp.dot(vg, jnp.exp(s).astype(BF16), preferred_element_type=F32)
            if u + DSA_ATT_LEAD < len(units):
                issue(u + DSA_ATT_LEAD)

    sweep_chunks(nchunks, att_group, DSA_ATT_GROUP)

    norm = acc_s[:, ATTN_HEAD_DIM:ATTN_HEAD_DIM + 1, :]
    in_range = jnp.where(norm >= SOFTMAX_NORM_MIN, jnp.where(norm <= SOFTMAX_NORM_MAX, 1.0, 0.0), 0.0)

    @pl.when(jnp.min(in_range) < 0.5)
    def _():
        acc_s[...] = jnp.zeros_like(acc_s)

        def online_body(j, ms):
            set_bias(j)
            out = []
            for g in range(ATTN_KV_HEADS):
                s, vg = masked_scores(j, g)
                m_new = jnp.maximum(ms[g], jnp.max(s, axis=0, keepdims=True))
                p = jnp.exp(s - m_new).astype(BF16)
                pv = jnp.dot(vg, p, preferred_element_type=F32)
                acc_s[g] = jnp.exp(ms[g] - m_new) * acc_s[g] + pv
                out.append(m_new)
            return tuple(out)

        m_init = tuple(jnp.full((1, ATTN_REP * tq), NEG_BIG, F32) for _ in range(ATTN_KV_HEADS))
        lax.fori_loop(0, nchunks, online_body, m_init)

    for h in range(ATTN_HEADS):
        g, r = divmod(h, ATTN_REP)
        a = acc_s[g, :, r * tq:(r + 1) * tq]
        ot_s[h * ATTN_HEAD_DIM:(h + 1) * ATTN_HEAD_DIM, :] = (
            a[:ATTN_HEAD_DIM] / a[ATTN_HEAD_DIM:ATTN_HEAD_DIM + 1])
    o_ref[...] = ot_s[...].T.astype(o_ref.dtype)


def _dsa(qt, iqt, iwt, kk, vt, ik, batch, seq):
    n = batch * seq
    tq, kc = DSA_TQ, DSA_KC
    assert seq % tq == 0
    nq = seq // tq
    topk = min(TOPK_MAX, seq // 4)
    nch = seq // kc
    nout = ATTN_HEADS * ATTN_HEAD_DIM
    qmap = lambda b, i: (0, 0, b * nq + i)
    est = (nch * kc * tq * 8 + 3 * DSA_ATT_GROUP * ATTN_REP * kc * tq * 4 + ATTN_HEADS * (VT_ROWS + 8) * tq * 4 + nout * tq * 4
           + ATTN_KV_HEADS * seq * (V7X_LANES + VT_ROWS) * 2 + seq * V7X_LANES * 2
           + 2 * (2 * ATTN_HEADS * ATTN_HEAD_DIM * tq * 2 + 8 * tq * 4 + tq * nout * 2))
    return pl.pallas_call(
        functools.partial(_dsa_kernel, topk=topk),
        out_shape=jax.ShapeDtypeStruct((n, nout), BF16),
        grid=(batch, nq),
        in_specs=[
            pl.BlockSpec((ATTN_HEADS, ATTN_HEAD_DIM, tq), qmap),
            pl.BlockSpec((IDX_HEADS, IDX_DIM, tq), qmap),
            pl.BlockSpec((IDX_HEADS, tq), lambda b, i: (0, b * nq + i)),
            pl.BlockSpec((ATTN_KV_HEADS, seq, ATTN_HEAD_DIM), lambda b, i: (0, b, 0),
                         pipeline_mode=pl.Buffered(1)),
            pl.BlockSpec((ATTN_KV_HEADS, VT_ROWS, seq), lambda b, i: (0, 0, b),
                         pipeline_mode=pl.Buffered(1)),
            pl.BlockSpec((seq, IDX_DIM), lambda b, i: (b, 0), pipeline_mode=pl.Buffered(1)),
        ],
        out_specs=pl.BlockSpec((tq, nout), lambda b, i: (b * nq + i, 0)),
        scratch_shapes=[
            pltpu.VMEM((nch, kc, tq), jnp.int32),
            pltpu.VMEM((nch, kc, tq), jnp.int16),
            pltpu.VMEM((nch, kc, tq), jnp.int16),
            pltpu.VMEM((V7X_BF16_ROWS, tq), jnp.int16),
            pltpu.VMEM((1, tq), F32),
            pltpu.VMEM((DSA_ATT_GROUP, kc, tq), F32),
            pltpu.VMEM((ATTN_KV_HEADS, VT_ROWS, ATTN_REP * tq), F32),
            pltpu.VMEM((nout, tq), F32),
        ],
        compiler_params=pltpu.CompilerParams(
            dimension_semantics=("arbitrary", "arbitrary"), vmem_limit_bytes=_vmem_limit(est)),
    )(qt, iqt, iwt, kk, vt, ik)


def _gla_kernel(gqk_ref, gv_ref, la_ref, gg_ref, gn_ref, o_ref, st_s):
    @pl.when(pl.program_id(1) == 0)
    def _():
        st_s[...] = jnp.zeros_like(st_s)

    c = GLA_CHUNK
    nqk = GLA_HEADS * GLA_DK
    r_i = lax.broadcasted_iota(jnp.int32, (c, c), 0)
    c_i = lax.broadcasted_iota(jnp.int32, (c, c), 1)
    tri = r_i >= c_i
    tri_f = tri.astype(F32)
    gn = gn_ref[...]
    nchunk = GLA_ROWS // c
    heads = range(GLA_HEADS)
    hs = [slice(h * GLA_DK, (h + 1) * GLA_DK) for h in heads]
    vs = [slice(h * GLA_DV, (h + 1) * GLA_DV) for h in heads]

    local = []
    for ci in range(nchunk):
        rows = slice(ci * c, (ci + 1) * c)
        la = la_ref[rows, :]
        b = jnp.dot(tri_f, la, preferred_element_type=F32, precision=lax.Precision.HIGHEST)
        b_last = b[c - 1:c, :]
        q = gqk_ref[rows, :nqk]
        k = gqk_ref[rows, nqk:]
        q_dec = (q * jnp.exp(b)).astype(BF16)
        k_in = (k * jnp.exp(-b)).astype(BF16)
        k_out = (k * jnp.exp(b_last - b)).astype(BF16)
        decay = jnp.exp(b_last)
        intra, upd = [], []
        for h in heads:
            v = gv_ref[rows, vs[h]]
            a = lax.dot_general(q_dec[:, hs[h]], k_in[:, hs[h]], NT_DIMS, preferred_element_type=F32)
            a = jnp.where(tri, a, 0.0).astype(BF16)
            intra.append(jnp.dot(a, v, preferred_element_type=F32))
            upd.append(lax.dot_general(v, k_out[:, hs[h]], TN_DIMS, preferred_element_type=F32))
        local.append((rows, q_dec, decay, intra, upd))

    st = [st_s[h] for h in heads]
    for rows, q_dec, decay, intra, upd in local:
        for h in heads:
            o = intra[h] + lax.dot_general(q_dec[:, hs[h]], st[h].astype(BF16), NT_DIMS,
                                           preferred_element_type=F32)
            st[h] = st[h] * decay[:, hs[h]] + upd[h]
            gate = gg_ref[rows, vs[h]]
            o_ref[rows, vs[h]] = (_rms(o, gn) * (gate * jax.nn.sigmoid(gate))).astype(o_ref.dtype)
    for h in heads:
        st_s[h] = st[h]


def _gla(gqk, gv, la, gg, g_norm, batch, seq):
    n = batch * seq
    t = GLA_ROWS
    assert seq % t == 0
    ns = seq // t
    row = lambda b, i: (b * ns + i, 0)
    nqk = GLA_HEADS * GLA_DK
    nv = GLA_HEADS * GLA_DV
    return pl.pallas_call(
        _gla_kernel,
        out_shape=jax.ShapeDtypeStruct((n, nv), BF16),
        grid=(batch, ns),
        in_specs=[
            pl.BlockSpec((t, 2 * nqk), row),
            pl.BlockSpec((t, nv), row),
            pl.BlockSpec((t, nqk), row),
            pl.BlockSpec((t, nv), row),
            _const_spec((1, GLA_DV)),
        ],
        out_specs=pl.BlockSpec((t, nv), row),
        scratch_shapes=[pltpu.VMEM((GLA_HEADS, GLA_DV, GLA_DK), F32)],
        compiler_params=pltpu.CompilerParams(dimension_semantics=("arbitrary", "arbitrary")),
    )(gqk, gv, la, gg, g_norm.reshape(1, GLA_DV))


def kernel(x, g_ffn1_pre, w_ffn1_gate, w_ffn1_up, w_ffn1_down, g_ffn1_post, g_mix_pre, w_in, w_gla_a2,
           b_gla_a, g_gla_norm, w_out, g_mix_post, g_ffn2_pre, w_ffn2_gate, w_ffn2_up, w_ffn2_down,
           g_ffn2_post):
    batch, seq, d = x.shape
    h = x.reshape(batch * seq, d)
    for l in range(g_ffn1_pre.shape[0]):
        h = _ffn(h, g_ffn1_pre[l], w_ffn1_gate[l], w_ffn1_up[l], w_ffn1_down[l], g_ffn1_post[l])
        qt, iqt, vt, iwt, kk, ik, gqk, gv, gg, la = _inproj(h, g_mix_pre[l], w_in[l], w_gla_a2[l], b_gla_a[l])
        oa = _dsa(qt, iqt, iwt, kk, vt, ik, batch, seq)
        og = _gla(gqk, gv, la, gg, g_gla_norm[l], batch, seq)
        h = _mix_ffn(h, oa, og, w_out[l], g_mix_post[l],
                     g_ffn2_pre[l], w_ffn2_gate[l], w_ffn2_up[l], w_ffn2_down[l], g_ffn2_post[l])
    return h.reshape(batch, seq, d)
```

```python
import functools

import jax
import jax.numpy as jnp
from jax import lax
from jax.experimental import pallas as pl
from jax.experimental.pallas import tpu as pltpu

ATTN_HEADS = 8
ATTN_KV_HEADS = 2
ATTN_HEAD_DIM = 64
ATTN_REP = ATTN_HEADS // ATTN_KV_HEADS
IDX_HEADS = 8
IDX_DIM = 64
TOPK_MAX = 256
GLA_HEADS = 4
GLA_DK = 64
GLA_DV = 128
GLA_GATE_RANK = 16
GLA_TAU = 16.0
GLA_CHUNK = 64
EPS = 1e-6

V7X_LANES = 128
V7X_SUBLANES = 8
V7X_BF16_ROWS = 16
V7X_MXU_DIM = 256
V7X_VMEM_BYTES = 64 * 2**20

FFN_ROWS = 1024
FF_CHUNK = V7X_MXU_DIM
DSA_TQ = 256
DSA_KC = DSA_TQ
DSA_SCORE_GROUP = 8
DSA_COUNT_GROUP = 16
DSA_ATT_GROUP = 4
DSA_ATT_LEAD = 3
GLA_ROWS = 1024
VT_ROWS = ATTN_HEAD_DIM + V7X_BF16_ROWS

INT_MIN = -2**31
INT16_MIN, INT16_MAX = -2**15, 2**15 - 1
NEG_BIG = -1e30
SOFTMAX_NORM_MIN = 2.0 ** -60
SOFTMAX_NORM_MAX = 2.0 ** 100

F32 = jnp.float32
BF16 = jnp.bfloat16
NT_DIMS = (((1,), (1,)), ((), ()))
TN_DIMS = (((0,), (0,)), ((), ()))


def _vmem_limit(nbytes):
    return int(min(nbytes * 1.25 + (8 << 20), V7X_VMEM_BYTES - (6 << 20)))


def _rms(x, g):
    return x * lax.rsqrt(jnp.mean(x * x, axis=-1, keepdims=True) + EPS) * g


def _const_spec(shape):
    nd = len(shape)
    return pl.BlockSpec(shape, lambda *_: (0,) * nd, pipeline_mode=pl.Buffered(1))


def _ffn_kernel(x_ref, gpre_ref, wg_ref, wu_ref, wd_ref, gpost_ref, o_ref):
    o_ref[...] = _ffn_residual(x_ref[...], gpre_ref, wg_ref, wu_ref, wd_ref, gpost_ref)


def _mix_ffn_kernel(x_ref, oa_ref, og_ref, wo_ref, gmix_ref, gpre_ref, wg_ref, wu_ref, wd_ref, gpost_ref, o_ref):
    m = jnp.dot(oa_ref[...], wo_ref[0], preferred_element_type=F32)
    m = m + jnp.dot(og_ref[...], wo_ref[1], preferred_element_type=F32)
    x2 = x_ref[...] + _rms(m, gmix_ref[...])
    o_ref[...] = _ffn_residual(x2, gpre_ref, wg_ref, wu_ref, wd_ref, gpost_ref)


def _ffn_residual(x, gpre_ref, wg_ref, wu_ref, wd_ref, gpost_ref):
    xn = _rms(x, gpre_ref[...]).astype(BF16)

    def cols(c):
        return slice(c * FF_CHUNK, (c + 1) * FF_CHUNK)

    def gate_up(c):
        return (jnp.dot(xn, wg_ref[:, cols(c)], preferred_element_type=F32),
                jnp.dot(xn, wu_ref[:, cols(c)], preferred_element_type=F32))

    nch = wg_ref.shape[1] // FF_CHUNK
    acc = None
    nxt = gate_up(0)
    for c in range(nch):
        g, u = nxt
        if c + 1 < nch:
            nxt = gate_up(c + 1)
        a = (g * jax.nn.sigmoid(g) * u).astype(BF16)
        d = jnp.dot(a, wd_ref[cols(c), :], preferred_element_type=F32)
        acc = d if acc is None else acc + d
    return x + 0.5 * _rms(acc, gpost_ref[...])


def _ffn_operands(d, g_pre, w_gate, w_up, w_down, g_post):
    dff = w_gate.shape[1]
    assert dff % FF_CHUNK == 0
    wg, wu, wd = w_gate.astype(BF16), w_up.astype(BF16), w_down.astype(BF16)
    specs = [_const_spec((1, d)), _const_spec(wg.shape), _const_spec(wu.shape), _const_spec(wd.shape),
             _const_spec((1, d))]
    args = (g_pre.reshape(1, d), wg, wu, wd, g_post.reshape(1, d))
    est = 4 * FFN_ROWS * d * 4 + 3 * d * dff * 2 + FFN_ROWS * d * 8 + 4 * FFN_ROWS * FF_CHUNK * 4
    return specs, args, est


def _ffn(x, g_pre, w_gate, w_up, w_down, g_post):
    n, d = x.shape
    assert n % FFN_ROWS == 0
    ffn_specs, ffn_args, est = _ffn_operands(d, g_pre, w_gate, w_up, w_down, g_post)
    x_spec = pl.BlockSpec((FFN_ROWS, d), lambda i: (i, 0))
    return pl.pallas_call(
        _ffn_kernel,
        out_shape=jax.ShapeDtypeStruct((n, d), F32),
        grid=(n // FFN_ROWS,),
        in_specs=[x_spec] + ffn_specs,
        out_specs=x_spec,
        compiler_params=pltpu.CompilerParams(
            dimension_semantics=("arbitrary",), vmem_limit_bytes=_vmem_limit(est)),
    )(x, *ffn_args)


def _mix_ffn(x, oa, og, w_out, g_mix, g_pre, w_gate, w_up, w_down, g_post):
    n, d = x.shape
    assert n % FFN_ROWS == 0
    ffn_specs, ffn_args, est = _ffn_operands(d, g_pre, w_gate, w_up, w_down, g_post)
    half = oa.shape[1]
    assert og.shape[1] == half and w_out.shape == (2 * half, d)
    wo = w_out.reshape(2, half, d).astype(BF16)
    row = lambda i: (i, 0)
    x_spec = pl.BlockSpec((FFN_ROWS, d), row)
    o_spec = pl.BlockSpec((FFN_ROWS, half), row)
    est += 4 * FFN_ROWS * half * 2 + 2 * half * d * 2 + FFN_ROWS * d * 4
    return pl.pallas_call(
        _mix_ffn_kernel,
        out_shape=jax.ShapeDtypeStruct((n, d), F32),
        grid=(n // FFN_ROWS,),
        in_specs=[x_spec, o_spec, o_spec, _const_spec(wo.shape), _const_spec((1, d))] + ffn_specs,
        out_specs=x_spec,
        compiler_params=pltpu.CompilerParams(
            dimension_semantics=("arbitrary",), vmem_limit_bytes=_vmem_limit(est)),
    )(x, oa, og, wo, g_mix.reshape(1, d), *ffn_args)


_K_OFF = 0
_MISC_OFF = _K_OFF + ATTN_KV_HEADS * ATTN_HEAD_DIM
_GA_LANE = IDX_DIM + IDX_HEADS
_GQ_OFF = _MISC_OFF + V7X_LANES
_GK_OFF = _GQ_OFF + GLA_HEADS * GLA_DK
_GV_OFF = _GK_OFF + GLA_HEADS * GLA_DK
_GG_OFF = _GV_OFF + GLA_HEADS * GLA_DV
_TOK_COLS = _GG_OFF + GLA_HEADS * GLA_DV
_QT_OFF = 0
_IQT_OFF = _QT_OFF + ATTN_HEADS * ATTN_HEAD_DIM
_VT_OFF = _IQT_OFF + IDX_HEADS * IDX_DIM
_IWT_OFF = _VT_OFF + ATTN_KV_HEADS * ATTN_HEAD_DIM
_T_ROWS = _IWT_OFF + V7X_BF16_ROWS


def _inproj_kernel(x_ref, g_ref, wtok_ref, wt_ref, wa2_ref, ba_ref,
                   qt_ref, iqt_ref, vt_ref, iwt_ref, kk_ref, ik_ref, gqk_ref, gv_ref, gg_ref, la_ref):
    h = _rms(x_ref[...], g_ref[...]).astype(BF16)
    t = h.shape[0]
    pt = lax.dot_general(wt_ref[...], h, NT_DIMS, preferred_element_type=F32)
    hd, idim = ATTN_HEAD_DIM, IDX_DIM
    for i in range(ATTN_HEADS):
        qt_ref[i] = pt[_QT_OFF + hd * i:_QT_OFF + hd * (i + 1)].astype(BF16)
    for i in range(IDX_HEADS):
        iqt_ref[i] = pt[_IQT_OFF + idim * i:_IQT_OFF + idim * (i + 1)].astype(BF16)
    ones_row = (lax.broadcasted_iota(jnp.int32, (V7X_BF16_ROWS, t), 0) == 0).astype(F32)
    for g in range(ATTN_KV_HEADS):
        v_t = pt[_VT_OFF + hd * g:_VT_OFF + hd * (g + 1)]
        vt_ref[g] = jnp.concatenate([v_t, ones_row], axis=0).astype(BF16)
    iwt_ref[...] = pt[_IWT_OFF:_IWT_OFF + IDX_HEADS] * (IDX_HEADS ** -0.5)

    proj = jnp.dot(h, wtok_ref[...], preferred_element_type=F32)
    for g in range(ATTN_KV_HEADS):
        kk_ref[g] = proj[:, _K_OFF + hd * g:_K_OFF + hd * (g + 1)].astype(BF16)
    misc = proj[:, _MISC_OFF:_MISC_OFF + V7X_LANES]
    ik_ref[...] = misc[:, :IDX_DIM].astype(BF16)
    gqk_ref[...] = proj[:, _GQ_OFF:_GV_OFF]
    gv_ref[...] = proj[:, _GV_OFF:_GG_OFF].astype(BF16)
    gg_ref[...] = proj[:, _GG_OFF:_GG_OFF + GLA_HEADS * GLA_DV]
    z = jnp.dot(misc, wa2_ref[...], preferred_element_type=F32, precision=lax.Precision.HIGHEST)
    z = z + ba_ref[...]
    log_sig = jnp.minimum(z, 0.0) - jnp.log1p(jnp.exp(-jnp.abs(z)))
    la_ref[...] = log_sig * (1.0 / GLA_TAU)


def _pack_w_in(w_in):
    d = w_in.shape[0]
    kv_w = ATTN_KV_HEADS * ATTN_HEAD_DIM
    sizes = (ATTN_HEADS * ATTN_HEAD_DIM, kv_w, kv_w, IDX_HEADS * IDX_DIM, IDX_DIM, IDX_HEADS,
             GLA_HEADS * GLA_DK, GLA_HEADS * GLA_DK, GLA_HEADS * GLA_DV, GLA_GATE_RANK, GLA_HEADS * GLA_DV)
    assert sum(sizes) == w_in.shape[1]
    offs = [0]
    for s in sizes:
        offs.append(offs[-1] + s)
    aq, ak, av, iq, ik, iw, gq, gk, gv, ga, gg = [w_in[:, offs[i]:offs[i + 1]] for i in range(len(sizes))]
    z = lambda n: jnp.zeros((d, n), w_in.dtype)
    misc_pad = V7X_LANES - _GA_LANE - GLA_GATE_RANK
    w_tok = jnp.concatenate([ak, ik, z(IDX_HEADS), ga, z(misc_pad), gq * (GLA_DK ** -0.5), gk, gv, gg], axis=1)
    w_t = jnp.concatenate([aq * (ATTN_HEAD_DIM ** -0.5), iq * (IDX_DIM ** -0.5), av, iw,
                           z(V7X_BF16_ROWS - IDX_HEADS)], axis=1).T
    assert w_tok.shape[1] == _TOK_COLS and w_t.shape[0] == _T_ROWS
    return w_tok.astype(BF16), w_t.astype(BF16)


def _inproj(x1, g_mix_pre, w_in, w_gla_a2, b_gla_a):
    n, d = x1.shape
    t = FFN_ROWS
    assert n % t == 0
    w_tok, w_t = _pack_w_in(w_in)
    nqk = GLA_HEADS * GLA_DK
    nv = GLA_HEADS * GLA_DV
    wa2 = jnp.zeros((V7X_LANES, nqk), F32).at[_GA_LANE:_GA_LANE + GLA_GATE_RANK].set(w_gla_a2)
    row = lambda i: (i, 0)
    row3 = lambda i: (0, i, 0)
    col3 = lambda i: (0, 0, i)
    out_shape = (
        jax.ShapeDtypeStruct((ATTN_HEADS, ATTN_HEAD_DIM, n), BF16),
        jax.ShapeDtypeStruct((IDX_HEADS, IDX_DIM, n), BF16),
        jax.ShapeDtypeStruct((ATTN_KV_HEADS, VT_ROWS, n), BF16),
        jax.ShapeDtypeStruct((IDX_HEADS, n), F32),
        jax.ShapeDtypeStruct((ATTN_KV_HEADS, n, ATTN_HEAD_DIM), BF16),
        jax.ShapeDtypeStruct((n, IDX_DIM), BF16),
        jax.ShapeDtypeStruct((n, 2 * nqk), F32),
        jax.ShapeDtypeStruct((n, nv), BF16),
        jax.ShapeDtypeStruct((n, nv), F32),
        jax.ShapeDtypeStruct((n, nqk), F32),
    )
    out_specs = (
        pl.BlockSpec((ATTN_HEADS, ATTN_HEAD_DIM, t), col3),
        pl.BlockSpec((IDX_HEADS, IDX_DIM, t), col3),
        pl.BlockSpec((ATTN_KV_HEADS, VT_ROWS, t), col3),
        pl.BlockSpec((IDX_HEADS, t), lambda i: (0, i)),
        pl.BlockSpec((ATTN_KV_HEADS, t, ATTN_HEAD_DIM), row3),
        pl.BlockSpec((t, IDX_DIM), row),
        pl.BlockSpec((t, 2 * nqk), row),
        pl.BlockSpec((t, nv), row),
        pl.BlockSpec((t, nv), row),
        pl.BlockSpec((t, nqk), row),
    )
    est = (2 * t * d * 4 + d * (_TOK_COLS + _T_ROWS) * 2 + t * (_TOK_COLS + _T_ROWS) * 4 * 2
           + 2 * t * 8 * 1024)
    return pl.pallas_call(
        _inproj_kernel,
        out_shape=out_shape,
        grid=(n // t,),
        in_specs=[
            pl.BlockSpec((t, d), row),
            _const_spec((1, d)),
            _const_spec(w_tok.shape),
            _const_spec(w_t.shape),
            _const_spec(wa2.shape),
            _const_spec((1, nqk)),
        ],
        out_specs=out_specs,
        compiler_params=pltpu.CompilerParams(
            dimension_semantics=("arbitrary",), vmem_limit_bytes=_vmem_limit(est)),
    )(x1, g_mix_pre.reshape(1, d), w_tok, w_t, wa2, b_gla_a.reshape(1, nqk))


def _dsa_kernel(qt_ref, iqt_ref, iwt_ref, kk_ref, vt_ref, ik_ref, o_ref,
                key_s, half_s, low_s, cnt_s, seen_s, bias_s, acc_s, ot_s, *, topk):
    tq = DSA_TQ
    kc = DSA_KC
    sub = V7X_SUBLANES
    i = pl.program_id(1)
    nchunks = i + 1

    def score_logits(j):
        ikc = ik_ref[pl.ds(pl.multiple_of(j * kc, kc), kc), :]
        iqcat = jnp.concatenate([iqt_ref[h] for h in range(IDX_HEADS)], axis=1)
        return jnp.dot(ikc, iqcat, preferred_element_type=F32)

    def score_keys(j, lg, diagonal):
        sc = jnp.zeros((kc, tq), F32)
        for h in range(IDX_HEADS):
            sc = sc + jnp.maximum(lg[:, h * tq:(h + 1) * tq], 0.0) * iwt_ref[h:h + 1, :]
        bits = pltpu.bitcast(sc, jnp.int32)
        key = jnp.where(bits < 0, bits ^ 0x7FFFFFFF, bits)
        if diagonal:
            kpos = lax.broadcasted_iota(jnp.int32, (kc, tq), 0)
            qpos = lax.broadcasted_iota(jnp.int32, (kc, tq), 1)
            key = jnp.where(kpos <= qpos, key, INT_MIN)
        key_s[j] = key
        half_s[j] = lax.shift_right_arithmetic(key, 16).astype(jnp.int16)
        low_s[j] = key.astype(jnp.int16) ^ jnp.int16(INT16_MIN)

    def sweep_chunks(n, group_fn, gsz):
        assert gsz & (gsz - 1) == 0

        def trip(q, c):
            group_fn(q * gsz, gsz)
            return c

        lax.fori_loop(0, lax.shift_right_logical(n, gsz.bit_length() - 1), trip, 0)
        done = n & -gsz
        part = gsz // 2
        while part:
            pl.when((n & part) != 0)(functools.partial(group_fn, done, part))
            done = done + (n & part)
            part //= 2

    def score_group(j0, count):
        lgs = [score_logits(j0 + s) for s in range(count)]
        for s in range(count):
            score_keys(j0 + s, lgs[s], False)

    sweep_chunks(i, score_group, DSA_SCORE_GROUP)
    score_keys(i, score_logits(i), True)

    qidx = i * tq + lax.broadcasted_iota(jnp.int32, (1, tq), 1)
    krow = jnp.minimum(topk, qidx + 1).astype(F32)

    def count(ind_fn):
        def body(j, acc):
            parts = [acc, None, None, None]
            for s in range(kc // sub):
                ind = ind_fn(key_s[j, s * sub:(s + 1) * sub, :], j * kc + s * sub)
                p = s % 4
                parts[p] = ind if parts[p] is None else parts[p] + ind
            return (parts[0] + parts[1]) + (parts[2] + parts[3])
        acc = lax.fori_loop(0, nchunks, body, jnp.zeros((sub, tq), F32))
        return jnp.sum(acc, axis=0, keepdims=True)

    def count_ge(cand):
        cb = jnp.broadcast_to(cand, (sub, tq))
        return count(lambda k, base: jnp.where(k >= cb, 1.0, 0.0))

    rows16 = V7X_BF16_ROWS

    def count16(cand):
        cb = jnp.broadcast_to(cand, (rows16, tq)).astype(jnp.int16)
        one, nil = jnp.int16(1), jnp.int16(0)
        cnt_s[...] = jnp.zeros_like(cnt_s)

        def group(j0, count):
            parts = [None] * 4
            for s in range(count * kc // rows16):
                c, r = divmod(s, kc // rows16)
                ind = jnp.where(half_s[j0 + c, r * rows16:(r + 1) * rows16, :] >= cb, one, nil)
                p = s % 4
                parts[p] = ind if parts[p] is None else parts[p] + ind
            cnt_s[...] += (parts[0] + parts[1]) + (parts[2] + parts[3])

        sweep_chunks(nchunks, group, DSA_COUNT_GROUP)
        return jnp.sum(cnt_s[...].astype(F32), axis=0, keepdims=True)

    def search16(count_at_min):
        zero = jnp.zeros((1, tq), jnp.int32)
        c0 = count16(zero)
        start = (jnp.where(c0 >= krow, zero, INT16_MIN), jnp.where(c0 >= krow, c0, count_at_min))

        def bit_body(it, carry):
            th, cth = carry
            cand = th | lax.shift_left(jnp.int32(1), 14 - it)
            c = count16(cand)
            return jnp.where(c >= krow, cand, th), jnp.where(c >= krow, c, cth)

        return lax.fori_loop(0, 15, bit_body, start)

    t_hi, c_hi = search16(jnp.full((1, tq), 1.0, F32) * (nchunks * kc).astype(F32))
    th16 = jnp.broadcast_to(t_hi, (kc, tq)).astype(jnp.int16)

    def low_halves(j0, count):
        for j in range(count):
            hi = half_s[j0 + j]
            half_s[j0 + j] = jnp.where(hi == th16, low_s[j0 + j],
                                       jnp.where(hi > th16, jnp.int16(INT16_MAX), jnp.int16(INT16_MIN)))

    sweep_chunks(nchunks, low_halves, DSA_SCORE_GROUP)
    t_lo, c_ge = search16(c_hi)
    t = lax.shift_left(t_hi, 16) | (t_lo - INT16_MIN)

    @pl.when(jnp.max(c_ge - krow) > 0.5)
    def _():
        saturated = jnp.max(jnp.where(t_lo == INT16_MAX, 1.0, 0.0)) > 0.5
        c_gt = lax.cond(saturated, lambda: count_ge(t + 1), lambda: count16(jnp.minimum(t_lo + 1, INT16_MAX)))
        need = krow - c_gt
        tb_full = jnp.broadcast_to(t, (kc, tq))
        tri = (lax.broadcasted_iota(jnp.int32, (kc, kc), 0)
               >= lax.broadcasted_iota(jnp.int32, (kc, kc), 1)).astype(BF16)

        seen_s[...] = jnp.zeros_like(seen_s)

        def drop_surplus(j0, count):
            ks = [key_s[j0 + s] for s in range(count)]
            prefix = [jnp.dot(tri, jnp.where(k == tb_full, 1.0, 0.0).astype(BF16), preferred_element_type=F32)
                      for k in ks]
            seen = seen_s[...]
            for s in range(count):
                rank = seen + prefix[s]
                key_s[j0 + s] = jnp.where(ks[s] == tb_full, jnp.where(rank > need, INT_MIN, ks[s]), ks[s])
                seen = rank[kc - 1:kc, :]
            seen_s[...] = seen

        sweep_chunks(nchunks, drop_surplus, DSA_SCORE_GROUP)

    def masked_scores(j, g, slot=0):
        koff = pl.multiple_of(j * kc, kc)
        qcat = jnp.concatenate([qt_ref[g * ATTN_REP + r] for r in range(ATTN_REP)], axis=1)
        bias = jnp.concatenate([bias_s[slot]] * ATTN_REP, axis=1)
        kg = kk_ref[g, pl.ds(koff, kc), :]
        vg = vt_ref[g, :, pl.ds(koff, kc)]
        return jnp.dot(kg, qcat, preferred_element_type=F32) + bias, vg

    def set_bias(j, slot=0):
        bias_s[slot] = jnp.where(key_s[j] >= jnp.broadcast_to(t, (kc, tq)), 0.0, NEG_BIG)

    acc_s[...] = jnp.zeros_like(acc_s)

    def att_group(j0, count):
        units = [(slot, g) for slot in range(count) for g in range(ATTN_KV_HEADS)]
        for slot in range(count):
            set_bias(j0 + slot, slot)
        scores = {}

        def issue(u):
            slot, g = units[u]
            scores[u] = masked_scores(j0 + slot, g, slot)

        for u in range(min(DSA_ATT_LEAD, len(units))):
            issue(u)
        for u, (slot, g) in enumerate(units):
            s, vg = scores.pop(u)
            acc_s[g] += jnp.dot(vg, jnp.exp(s).astype(BF16), preferred_element_type=F32)
            if u + DSA_ATT_LEAD < len(units):
                issue(u + DSA_ATT_LEAD)

    sweep_chunks(nchunks, att_group, DSA_ATT_GROUP)

    norm = acc_s[:, ATTN_HEAD_DIM:ATTN_HEAD_DIM + 1, :]
    in_range = jnp.where(norm >= SOFTMAX_NORM_MIN, jnp.where(norm <= SOFTMAX_NORM_MAX, 1.0, 0.0), 0.0)

    @pl.when(jnp.min(in_range) < 0.5)
    def _():
        acc_s[...] = jnp.zeros_like(acc_s)

        def online_body(j, ms):
            set_bias(j)
            out = []
            for g in range(ATTN_KV_HEADS):
                s, vg = masked_scores(j, g)
                m_new = jnp.maximum(ms[g], jnp.max(s, axis=0, keepdims=True))
                p = jnp.exp(s - m_new).astype(BF16)
                pv = jnp.dot(vg, p, preferred_element_type=F32)
                acc_s[g] = jnp.exp(ms[g] - m_new) * acc_s[g] + pv
                out.append(m_new)
            return tuple(out)

        m_init = tuple(jnp.full((1, ATTN_REP * tq), NEG_BIG, F32) for _ in range(ATTN_KV_HEADS))
        lax.fori_loop(0, nchunks, online_body, m_init)

    for h in range(ATTN_HEADS):
        g, r = divmod(h, ATTN_REP)
        a = acc_s[g, :, r * tq:(r + 1) * tq]
        ot_s[h * ATTN_HEAD_DIM:(h + 1) * ATTN_HEAD_DIM, :] = (
            a[:ATTN_HEAD_DIM] / a[ATTN_HEAD_DIM:ATTN_HEAD_DIM + 1])
    o_ref[...] = ot_s[...].T.astype(o_ref.dtype)


def _dsa(qt, iqt, iwt, kk, vt, ik, batch, seq):
    n = batch * seq
    tq, kc = DSA_TQ, DSA_KC
    assert seq % tq == 0
    nq = seq // tq
    topk = min(TOPK_MAX, seq // 4)
    nch = seq // kc
    nout = ATTN_HEADS * ATTN_HEAD_DIM
    qmap = lambda b, i: (0, 0, b * nq + i)
    est = (nch * kc * tq * 8 + 3 * DSA_ATT_GROUP * ATTN_REP * kc * tq * 4 + ATTN_HEADS * (VT_ROWS + 8) * tq * 4 + nout * tq * 4
           + ATTN_KV_HEADS * seq * (V7X_LANES + VT_ROWS) * 2 + seq * V7X_LANES * 2
           + 2 * (2 * ATTN_HEADS * ATTN_HEAD_DIM * tq * 2 + 8 * tq * 4 + tq * nout * 2))
    return pl.pallas_call(
        functools.partial(_dsa_kernel, topk=topk),
        out_shape=jax.ShapeDtypeStruct((n, nout), BF16),
        grid=(batch, nq),
        in_specs=[
            pl.BlockSpec((ATTN_HEADS, ATTN_HEAD_DIM, tq), qmap),
            pl.BlockSpec((IDX_HEADS, IDX_DIM, tq), qmap),
            pl.BlockSpec((IDX_HEADS, tq), lambda b, i: (0, b * nq + i)),
            pl.BlockSpec((ATTN_KV_HEADS, seq, ATTN_HEAD_DIM), lambda b, i: (0, b, 0),
                         pipeline_mode=pl.Buffered(1)),
            pl.BlockSpec((ATTN_KV_HEADS, VT_ROWS, seq), lambda b, i: (0, 0, b),
                         pipeline_mode=pl.Buffered(1)),
            pl.BlockSpec((seq, IDX_DIM), lambda b, i: (b, 0), pipeline_mode=pl.Buffered(1)),
        ],
        out_specs=pl.BlockSpec((tq, nout), lambda b, i: (b * nq + i, 0)),
        scratch_shapes=[
            pltpu.VMEM((nch, kc, tq), jnp.int32),
            pltpu.VMEM((nch, kc, tq), jnp.int16),
            pltpu.VMEM((nch, kc, tq), jnp.int16),
            pltpu.VMEM((V7X_BF16_ROWS, tq), jnp.int16),
            pltpu.VMEM((1, tq), F32),
            pltpu.VMEM((DSA_ATT_GROUP, kc, tq), F32),
            pltpu.VMEM((ATTN_KV_HEADS, VT_ROWS, ATTN_REP * tq), F32),
            pltpu.VMEM((nout, tq), F32),
        ],
        compiler_params=pltpu.CompilerParams(
            dimension_semantics=("arbitrary", "arbitrary"), vmem_limit_bytes=_vmem_limit(est)),
    )(qt, iqt, iwt, kk, vt, ik)


def _gla_kernel(gqk_ref, gv_ref, la_ref, gg_ref, gn_ref, o_ref, st_s):
    @pl.when(pl.program_id(1) == 0)
    def _():
        st_s[...] = jnp.zeros_like(st_s)

    c = GLA_CHUNK
    nqk = GLA_HEADS * GLA_DK
    r_i = lax.broadcasted_iota(jnp.int32, (c, c), 0)
    c_i = lax.broadcasted_iota(jnp.int32, (c, c), 1)
    tri = r_i >= c_i
    tri_f = tri.astype(F32)
    gn = gn_ref[...]
    nchunk = GLA_ROWS // c
    heads = range(GLA_HEADS)
    hs = [slice(h * GLA_DK, (h + 1) * GLA_DK) for h in heads]
    vs = [slice(h * GLA_DV, (h + 1) * GLA_DV) for h in heads]

    local = []
    for ci in range(nchunk):
        rows = slice(ci * c, (ci + 1) * c)
        la = la_ref[rows, :]
        b = jnp.dot(tri_f, la, preferred_element_type=F32, precision=lax.Precision.HIGHEST)
        b_last = b[c - 1:c, :]
        q = gqk_ref[rows, :nqk]
        k = gqk_ref[rows, nqk:]
        q_dec = (q * jnp.exp(b)).astype(BF16)
        k_in = (k * jnp.exp(-b)).astype(BF16)
        k_out = (k * jnp.exp(b_last - b)).astype(BF16)
        decay = jnp.exp(b_last)
        intra, upd = [], []
        for h in heads:
            v = gv_ref[rows, vs[h]]
            a = lax.dot_general(q_dec[:, hs[h]], k_in[:, hs[h]], NT_DIMS, preferred_element_type=F32)
            a = jnp.where(tri, a, 0.0).astype(BF16)
            intra.append(jnp.dot(a, v, preferred_element_type=F32))
            upd.append(lax.dot_general(v, k_out[:, hs[h]], TN_DIMS, preferred_element_type=F32))
        local.append((rows, q_dec, decay, intra, upd))

    st = [st_s[h] for h in heads]
    for rows, q_dec, decay, intra, upd in local:
        for h in heads:
            o = intra[h] + lax.dot_general(q_dec[:, hs[h]], st[h].astype(BF16), NT_DIMS,
                                           preferred_element_type=F32)
            st[h] = st[h] * decay[:, hs[h]] + upd[h]
            gate = gg_ref[rows, vs[h]]
            o_ref[rows, vs[h]] = (_rms(o, gn) * (gate * jax.nn.sigmoid(gate))).astype(o_ref.dtype)
    for h in heads:
        st_s[h] = st[h]


def _gla(gqk, gv, la, gg, g_norm, batch, seq):
    n = batch * seq
    t = GLA_ROWS
    assert seq % t == 0
    ns = seq // t
    row = lambda b, i: (b * ns + i, 0)
    nqk = GLA_HEADS * GLA_DK
    nv = GLA_HEADS * GLA_DV
    return pl.pallas_call(
        _gla_kernel,
        out_shape=jax.ShapeDtypeStruct((n, nv), BF16),
        grid=(batch, ns),
        in_specs=[
            pl.BlockSpec((t, 2 * nqk), row),
            pl.BlockSpec((t, nv), row),
            pl.BlockSpec((t, nqk), row),
            pl.BlockSpec((t, nv), row),
            _const_spec((1, GLA_DV)),
        ],
        out_specs=pl.BlockSpec((t, nv), row),
        scratch_shapes=[pltpu.VMEM((GLA_HEADS, GLA_DV, GLA_DK), F32)],
        compiler_params=pltpu.CompilerParams(dimension_semantics=("arbitrary", "arbitrary")),
    )(gqk, gv, la, gg, g_norm.reshape(1, GLA_DV))


def kernel(x, g_ffn1_pre, w_ffn1_gate, w_ffn1_up, w_ffn1_down, g_ffn1_post, g_mix_pre, w_in, w_gla_a2,
           b_gla_a, g_gla_norm, w_out, g_mix_post, g_ffn2_pre, w_ffn2_gate, w_ffn2_up, w_ffn2_down,
           g_ffn2_post):
    batch, seq, d = x.shape
    h = x.reshape(batch * seq, d)
    for l in range(g_ffn1_pre.shape[0]):
        h = _ffn(h, g_ffn1_pre[l], w_ffn1_gate[l], w_ffn1_up[l], w_ffn1_down[l], g_ffn1_post[l])
        qt, iqt, vt, iwt, kk, ik, gqk, gv, gg, la = _inproj(h, g_mix_pre[l], w_in[l], w_gla_a2[l], b_gla_a[l])
        oa = _dsa(qt, iqt, iwt, kk, vt, ik, batch, seq)
        og = _gla(gqk, gv, la, gg, g_gla_norm[l], batch, seq)
        h = _mix_ffn(h, oa, og, w_out[l], g_mix_post[l],
                     g_ffn2_pre[l], w_ffn2_gate[l], w_ffn2_up[l], w_ffn2_down[l], g_ffn2_post[l])
    return h.reshape(batch, seq, d)
```

```python
import functools

import jax
import jax.numpy as jnp
from jax import lax
from jax.experimental import pallas as pl
from jax.experimental.pallas import tpu as pltpu

ATTN_HEADS = 8
ATTN_KV_HEADS = 2
ATTN_HEAD_DIM = 64
ATTN_REP = ATTN_HEADS // ATTN_KV_HEADS
IDX_HEADS = 8
IDX_DIM = 64
TOPK_MAX = 256
GLA_HEADS = 4
GLA_DK = 64
GLA_DV = 128
GLA_GATE_RANK = 16
GLA_TAU = 16.0
GLA_CHUNK = 64
EPS = 1e-6

V7X_LANES = 128
V7X_SUBLANES = 8
V7X_BF16_ROWS = 16
V7X_MXU_DIM = 256
V7X_VMEM_BYTES = 64 * 2**20

FFN_ROWS = 1024
FF_CHUNK = V7X_MXU_DIM
DSA_TQ = 256
DSA_KC = DSA_TQ
DSA_SCORE_GROUP = 16
DSA_COUNT_GROUP = 16
DSA_ATT_GROUP = 4
DSA_ATT_LEAD = 3
GLA_ROWS = 1024
VT_ROWS = ATTN_HEAD_DIM + V7X_BF16_ROWS

INT_MIN = -2**31
INT16_MIN, INT16_MAX = -2**15, 2**15 - 1
NEG_BIG = -1e30
SOFTMAX_NORM_MIN = 2.0 ** -60
SOFTMAX_NORM_MAX = 2.0 ** 100

F32 = jnp.float32
BF16 = jnp.bfloat16
NT_DIMS = (((1,), (1,)), ((), ()))
TN_DIMS = (((0,), (0,)), ((), ()))


def _vmem_limit(nbytes):
    return int(min(nbytes * 1.25 + (8 << 20), V7X_VMEM_BYTES - (6 << 20)))


def _rms(x, g):
    return x * lax.rsqrt(jnp.mean(x * x, axis=-1, keepdims=True) + EPS) * g


def _const_spec(shape):
    nd = len(shape)
    return pl.BlockSpec(shape, lambda *_: (0,) * nd, pipeline_mode=pl.Buffered(1))


def _ffn_kernel(x_ref, gpre_ref, wg_ref, wu_ref, wd_ref, gpost_ref, o_ref):
    o_ref[...] = _ffn_residual(x_ref[...], gpre_ref, wg_ref, wu_ref, wd_ref, gpost_ref)


def _mix_ffn_kernel(x_ref, oa_ref, og_ref, wo_ref, gmix_ref, gpre_ref, wg_ref, wu_ref, wd_ref, gpost_ref, o_ref):
    m = jnp.dot(oa_ref[...], wo_ref[0], preferred_element_type=F32)
    m = m + jnp.dot(og_ref[...], wo_ref[1], preferred_element_type=F32)
    x2 = x_ref[...] + _rms(m, gmix_ref[...])
    o_ref[...] = _ffn_residual(x2, gpre_ref, wg_ref, wu_ref, wd_ref, gpost_ref)


def _ffn_residual(x, gpre_ref, wg_ref, wu_ref, wd_ref, gpost_ref):
    xn = _rms(x, gpre_ref[...]).astype(BF16)

    def cols(c):
        return slice(c * FF_CHUNK, (c + 1) * FF_CHUNK)

    def gate_up(c):
        return (jnp.dot(xn, wg_ref[:, cols(c)], preferred_element_type=F32),
                jnp.dot(xn, wu_ref[:, cols(c)], preferred_element_type=F32))

    nch = wg_ref.shape[1] // FF_CHUNK
    acc = None
    nxt = gate_up(0)
    for c in range(nch):
        g, u = nxt
        if c + 1 < nch:
            nxt = gate_up(c + 1)
        a = (g * jax.nn.sigmoid(g) * u).astype(BF16)
        d = jnp.dot(a, wd_ref[cols(c), :], preferred_element_type=F32)
        acc = d if acc is None else acc + d
    return x + 0.5 * _rms(acc, gpost_ref[...])


def _ffn_operands(d, g_pre, w_gate, w_up, w_down, g_post):
    dff = w_gate.shape[1]
    assert dff % FF_CHUNK == 0
    wg, wu, wd = w_gate.astype(BF16), w_up.astype(BF16), w_down.astype(BF16)
    specs = [_const_spec((1, d)), _const_spec(wg.shape), _const_spec(wu.shape), _const_spec(wd.shape),
             _const_spec((1, d))]
    args = (g_pre.reshape(1, d), wg, wu, wd, g_post.reshape(1, d))
    est = 4 * FFN_ROWS * d * 4 + 3 * d * dff * 2 + FFN_ROWS * d * 8 + 4 * FFN_ROWS * FF_CHUNK * 4
    return specs, args, est


def _ffn(x, g_pre, w_gate, w_up, w_down, g_post):
    n, d = x.shape
    assert n % FFN_ROWS == 0
    ffn_specs, ffn_args, est = _ffn_operands(d, g_pre, w_gate, w_up, w_down, g_post)
    x_spec = pl.BlockSpec((FFN_ROWS, d), lambda i: (i, 0))
    return pl.pallas_call(
        _ffn_kernel,
        out_shape=jax.ShapeDtypeStruct((n, d), F32),
        grid=(n // FFN_ROWS,),
        in_specs=[x_spec] + ffn_specs,
        out_specs=x_spec,
        compiler_params=pltpu.CompilerParams(
            dimension_semantics=("arbitrary",), vmem_limit_bytes=_vmem_limit(est)),
    )(x, *ffn_args)


def _mix_ffn(x, oa, og, w_out, g_mix, g_pre, w_gate, w_up, w_down, g_post):
    n, d = x.shape
    assert n % FFN_ROWS == 0
    ffn_specs, ffn_args, est = _ffn_operands(d, g_pre, w_gate, w_up, w_down, g_post)
    half = oa.shape[1]
    assert og.shape[1] == half and w_out.shape == (2 * half, d)
    wo = w_out.reshape(2, half, d).astype(BF16)
    row = lambda i: (i, 0)
    x_spec = pl.BlockSpec((FFN_ROWS, d), row)
    o_spec = pl.BlockSpec((FFN_ROWS, half), row)
    est += 4 * FFN_ROWS * half * 2 + 2 * half * d * 2 + FFN_ROWS * d * 4
    return pl.pallas_call(
        _mix_ffn_kernel,
        out_shape=jax.ShapeDtypeStruct((n, d), F32),
        grid=(n // FFN_ROWS,),
        in_specs=[x_spec, o_spec, o_spec, _const_spec(wo.shape), _const_spec((1, d))] + ffn_specs,
        out_specs=x_spec,
        compiler_params=pltpu.CompilerParams(
            dimension_semantics=("arbitrary",), vmem_limit_bytes=_vmem_limit(est)),
    )(x, oa, og, wo, g_mix.reshape(1, d), *ffn_args)


_K_OFF = 0
_MISC_OFF = _K_OFF + ATTN_KV_HEADS * ATTN_HEAD_DIM
_GA_LANE = IDX_DIM + IDX_HEADS
_GQ_OFF = _MISC_OFF + V7X_LANES
_GK_OFF = _GQ_OFF + GLA_HEADS * GLA_DK
_GV_OFF = _GK_OFF + GLA_HEADS * GLA_DK
_GG_OFF = _GV_OFF + GLA_HEADS * GLA_DV
_TOK_COLS = _GG_OFF + GLA_HEADS * GLA_DV
_QT_OFF = 0
_IQT_OFF = _QT_OFF + ATTN_HEADS * ATTN_HEAD_DIM
_VT_OFF = _IQT_OFF + IDX_HEADS * IDX_DIM
_IWT_OFF = _VT_OFF + ATTN_KV_HEADS * ATTN_HEAD_DIM
_T_ROWS = _IWT_OFF + V7X_BF16_ROWS


def _inproj_kernel(x_ref, g_ref, wtok_ref, wt_ref, wa2_ref, ba_ref,
                   qt_ref, iqt_ref, vt_ref, iwt_ref, kk_ref, ik_ref, gqk_ref, gv_ref, gg_ref, la_ref):
    h = _rms(x_ref[...], g_ref[...]).astype(BF16)
    t = h.shape[0]
    pt = lax.dot_general(wt_ref[...], h, NT_DIMS, preferred_element_type=F32)
    hd, idim = ATTN_HEAD_DIM, IDX_DIM
    for i in range(ATTN_HEADS):
        qt_ref[i] = pt[_QT_OFF + hd * i:_QT_OFF + hd * (i + 1)].astype(BF16)
    for i in range(IDX_HEADS):
        iqt_ref[i] = pt[_IQT_OFF + idim * i:_IQT_OFF + idim * (i + 1)].astype(BF16)
    ones_row = (lax.broadcasted_iota(jnp.int32, (V7X_BF16_ROWS, t), 0) == 0).astype(F32)
    for g in range(ATTN_KV_HEADS):
        v_t = pt[_VT_OFF + hd * g:_VT_OFF + hd * (g + 1)]
        vt_ref[g] = jnp.concatenate([v_t, ones_row], axis=0).astype(BF16)
    iwt_ref[...] = pt[_IWT_OFF:_IWT_OFF + IDX_HEADS] * (IDX_HEADS ** -0.5)

    proj = jnp.dot(h, wtok_ref[...], preferred_element_type=F32)
    for g in range(ATTN_KV_HEADS):
        kk_ref[g] = proj[:, _K_OFF + hd * g:_K_OFF + hd * (g + 1)].astype(BF16)
    misc = proj[:, _MISC_OFF:_MISC_OFF + V7X_LANES]
    ik_ref[...] = misc[:, :IDX_DIM].astype(BF16)
    gqk_ref[...] = proj[:, _GQ_OFF:_GV_OFF]
    gv_ref[...] = proj[:, _GV_OFF:_GG_OFF].astype(BF16)
    gg_ref[...] = proj[:, _GG_OFF:_GG_OFF + GLA_HEADS * GLA_DV]
    z = jnp.dot(misc, wa2_ref[...], preferred_element_type=F32, precision=lax.Precision.HIGHEST)
    z = z + ba_ref[...]
    log_sig = jnp.minimum(z, 0.0) - jnp.log1p(jnp.exp(-jnp.abs(z)))
    la_ref[...] = log_sig * (1.0 / GLA_TAU)


def _pack_w_in(w_in):
    d = w_in.shape[0]
    kv_w = ATTN_KV_HEADS * ATTN_HEAD_DIM
    sizes = (ATTN_HEADS * ATTN_HEAD_DIM, kv_w, kv_w, IDX_HEADS * IDX_DIM, IDX_DIM, IDX_HEADS,
             GLA_HEADS * GLA_DK, GLA_HEADS * GLA_DK, GLA_HEADS * GLA_DV, GLA_GATE_RANK, GLA_HEADS * GLA_DV)
    assert sum(sizes) == w_in.shape[1]
    offs = [0]
    for s in sizes:
        offs.append(offs[-1] + s)
    aq, ak, av, iq, ik, iw, gq, gk, gv, ga, gg = [w_in[:, offs[i]:offs[i + 1]] for i in range(len(sizes))]
    z = lambda n: jnp.zeros((d, n), w_in.dtype)
    misc_pad = V7X_LANES - _GA_LANE - GLA_GATE_RANK
    w_tok = jnp.concatenate([ak, ik, z(IDX_HEADS), ga, z(misc_pad), gq * (GLA_DK ** -0.5), gk, gv, gg], axis=1)
    w_t = jnp.concatenate([aq * (ATTN_HEAD_DIM ** -0.5), iq * (IDX_DIM ** -0.5), av, iw,
                           z(V7X_BF16_ROWS - IDX_HEADS)], axis=1).T
    assert w_tok.shape[1] == _TOK_COLS and w_t.shape[0] == _T_ROWS
    return w_tok.astype(BF16), w_t.astype(BF16)


def _inproj(x1, g_mix_pre, w_in, w_gla_a2, b_gla_a):
    n, d = x1.shape
    t = FFN_ROWS
    assert n % t == 0
    w_tok, w_t = _pack_w_in(w_in)
    nqk = GLA_HEADS * GLA_DK
    nv = GLA_HEADS * GLA_DV
    wa2 = jnp.zeros((V7X_LANES, nqk), F32).at[_GA_LANE:_GA_LANE + GLA_GATE_RANK].set(w_gla_a2)
    row = lambda i: (i, 0)
    row3 = lambda i: (0, i, 0)
    col3 = lambda i: (0, 0, i)
    out_shape = (
        jax.ShapeDtypeStruct((ATTN_HEADS, ATTN_HEAD_DIM, n), BF16),
        jax.ShapeDtypeStruct((IDX_HEADS, IDX_DIM, n), BF16),
        jax.ShapeDtypeStruct((ATTN_KV_HEADS, VT_ROWS, n), BF16),
        jax.ShapeDtypeStruct((IDX_HEADS, n), F32),
        jax.ShapeDtypeStruct((ATTN_KV_HEADS, n, ATTN_HEAD_DIM), BF16),
        jax.ShapeDtypeStruct((n, IDX_DIM), BF16),
        jax.ShapeDtypeStruct((n, 2 * nqk), F32),
        jax.ShapeDtypeStruct((n, nv), BF16),
        jax.ShapeDtypeStruct((n, nv), F32),
        jax.ShapeDtypeStruct((n, nqk), F32),
    )
    out_specs = (
        pl.BlockSpec((ATTN_HEADS, ATTN_HEAD_DIM, t), col3),
        pl.BlockSpec((IDX_HEADS, IDX_DIM, t), col3),
        pl.BlockSpec((ATTN_KV_HEADS, VT_ROWS, t), col3),
        pl.BlockSpec((IDX_HEADS, t), lambda i: (0, i)),
        pl.BlockSpec((ATTN_KV_HEADS, t, ATTN_HEAD_DIM), row3),
        pl.BlockSpec((t, IDX_DIM), row),
        pl.BlockSpec((t, 2 * nqk), row),
        pl.BlockSpec((t, nv), row),
        pl.BlockSpec((t, nv), row),
        pl.BlockSpec((t, nqk), row),
    )
    est = (2 * t * d * 4 + d * (_TOK_COLS + _T_ROWS) * 2 + t * (_TOK_COLS + _T_ROWS) * 4 * 2
           + 2 * t * 8 * 1024)
    return pl.pallas_call(
        _inproj_kernel,
        out_shape=out_shape,
        grid=(n // t,),
        in_specs=[
            pl.BlockSpec((t, d), row),
            _const_spec((1, d)),
            _const_spec(w_tok.shape),
            _const_spec(w_t.shape),
            _const_spec(wa2.shape),
            _const_spec((1, nqk)),
        ],
        out_specs=out_specs,
        compiler_params=pltpu.CompilerParams(
            dimension_semantics=("arbitrary",), vmem_limit_bytes=_vmem_limit(est)),
    )(x1, g_mix_pre.reshape(1, d), w_tok, w_t, wa2, b_gla_a.reshape(1, nqk))


def _dsa_kernel(qt_ref, iqt_ref, iwt_ref, kk_ref, vt_ref, ik_ref, o_ref,
                key_s, half_s, low_s, cnt_s, seen_s, bias_s, acc_s, ot_s, *, topk):
    tq = DSA_TQ
    kc = DSA_KC
    sub = V7X_SUBLANES
    i = pl.program_id(1)
    nchunks = i + 1

    def score_logits(j):
        ikc = ik_ref[pl.ds(pl.multiple_of(j * kc, kc), kc), :]
        iqcat = jnp.concatenate([iqt_ref[h] for h in range(IDX_HEADS)], axis=1)
        return jnp.dot(ikc, iqcat, preferred_element_type=F32)

    def score_keys(j, lg, diagonal):
        sc = jnp.zeros((kc, tq), F32)
        for h in range(IDX_HEADS):
            sc = sc + jnp.maximum(lg[:, h * tq:(h + 1) * tq], 0.0) * iwt_ref[h:h + 1, :]
        bits = pltpu.bitcast(sc, jnp.int32)
        key = jnp.where(bits < 0, bits ^ 0x7FFFFFFF, bits)
        if diagonal:
            kpos = lax.broadcasted_iota(jnp.int32, (kc, tq), 0)
            qpos = lax.broadcasted_iota(jnp.int32, (kc, tq), 1)
            key = jnp.where(kpos <= qpos, key, INT_MIN)
        key_s[j] = key
        half_s[j] = lax.shift_right_arithmetic(key, 16).astype(jnp.int16)
        low_s[j] = key.astype(jnp.int16) ^ jnp.int16(INT16_MIN)

    def sweep_chunks(n, group_fn, gsz):
        assert gsz & (gsz - 1) == 0

        def trip(q, c):
            group_fn(q * gsz, gsz)
            return c

        lax.fori_loop(0, lax.shift_right_logical(n, gsz.bit_length() - 1), trip, 0)
        done = n & -gsz
        part = gsz // 2
        while part:
            pl.when((n & part) != 0)(functools.partial(group_fn, done, part))
            done = done + (n & part)
            part //= 2

    def score_group(j0, count):
        lgs = [score_logits(j0 + s) for s in range(count)]
        for s in range(count):
            score_keys(j0 + s, lgs[s], False)

    sweep_chunks(i, score_group, DSA_SCORE_GROUP)
    score_keys(i, score_logits(i), True)

    qidx = i * tq + lax.broadcasted_iota(jnp.int32, (1, tq), 1)
    krow = jnp.minimum(topk, qidx + 1).astype(F32)

    def count(ind_fn):
        def body(j, acc):
            parts = [acc, None, None, None]
            for s in range(kc // sub):
                ind = ind_fn(key_s[j, s * sub:(s + 1) * sub, :], j * kc + s * sub)
                p = s % 4
                parts[p] = ind if parts[p] is None else parts[p] + ind
            return (parts[0] + parts[1]) + (parts[2] + parts[3])
        acc = lax.fori_loop(0, nchunks, body, jnp.zeros((sub, tq), F32))
        return jnp.sum(acc, axis=0, keepdims=True)

    def count_ge(cand):
        cb = jnp.broadcast_to(cand, (sub, tq))
        return count(lambda k, base: jnp.where(k >= cb, 1.0, 0.0))

    rows16 = V7X_BF16_ROWS

    def count16(cand):
        cb = jnp.broadcast_to(cand, (rows16, tq)).astype(jnp.int16)
        one, nil = jnp.int16(1), jnp.int16(0)
        cnt_s[...] = jnp.zeros_like(cnt_s)

        def group(j0, count):
            parts = [None] * 4
            for s in range(count * kc // rows16):
                c, r = divmod(s, kc // rows16)
                ind = jnp.where(half_s[j0 + c, r * rows16:(r + 1) * rows16, :] >= cb, one, nil)
                p = s % 4
                parts[p] = ind if parts[p] is None else parts[p] + ind
            cnt_s[...] += (parts[0] + parts[1]) + (parts[2] + parts[3])

        sweep_chunks(nchunks, group, DSA_COUNT_GROUP)
        return jnp.sum(cnt_s[...].astype(F32), axis=0, keepdims=True)

    def search16(count_at_min):
        zero = jnp.zeros((1, tq), jnp.int32)
        c0 = count16(zero)
        start = (jnp.where(c0 >= krow, zero, INT16_MIN), jnp.where(c0 >= krow, c0, count_at_min))

        def bit_body(it, carry):
            th, cth = carry
            cand = th | lax.shift_left(jnp.int32(1), 14 - it)
            c = count16(cand)
            return jnp.where(c >= krow, cand, th), jnp.where(c >= krow, c, cth)

        return lax.fori_loop(0, 15, bit_body, start)

    t_hi, c_hi = search16(jnp.full((1, tq), 1.0, F32) * (nchunks * kc).astype(F32))
    th16 = jnp.broadcast_to(t_hi, (kc, tq)).astype(jnp.int16)

    def low_halves(j0, count):
        for j in range(count):
            hi = half_s[j0 + j]
            half_s[j0 + j] = jnp.where(hi == th16, low_s[j0 + j],
                                       jnp.where(hi > th16, jnp.int16(INT16_MAX), jnp.int16(INT16_MIN)))

    sweep_chunks(nchunks, low_halves, DSA_SCORE_GROUP)
    t_lo, c_ge = search16(c_hi)
    t = lax.shift_left(t_hi, 16) | (t_lo - INT16_MIN)

    @pl.when(jnp.max(c_ge - krow) > 0.5)
    def _():
        saturated = jnp.max(jnp.where(t_lo == INT16_MAX, 1.0, 0.0)) > 0.5
        c_gt = lax.cond(saturated, lambda: count_ge(t + 1), lambda: count16(jnp.minimum(t_lo + 1, INT16_MAX)))
        need = krow - c_gt
        tb_full = jnp.broadcast_to(t, (kc, tq))
        tri = (lax.broadcasted_iota(jnp.int32, (kc, kc), 0)
               >= lax.broadcasted_iota(jnp.int32, (kc, kc), 1)).astype(BF16)

        seen_s[...] = jnp.zeros_like(seen_s)

        def drop_surplus(j0, count):
            ks = [key_s[j0 + s] for s in range(count)]
            prefix = [jnp.dot(tri, jnp.where(k == tb_full, 1.0, 0.0).astype(BF16), preferred_element_type=F32)
                      for k in ks]
            seen = seen_s[...]
            for s in range(count):
                rank = seen + prefix[s]
                key_s[j0 + s] = jnp.where(ks[s] == tb_full, jnp.where(rank > need, INT_MIN, ks[s]), ks[s])
                seen = rank[kc - 1:kc, :]
            seen_s[...] = seen

        sweep_chunks(nchunks, drop_surplus, DSA_SCORE_GROUP)

    def masked_scores(j, g, slot=0):
        koff = pl.multiple_of(j * kc, kc)
        qcat = jnp.concatenate([qt_ref[g * ATTN_REP + r] for r in range(ATTN_REP)], axis=1)
        bias = jnp.concatenate([bias_s[slot]] * ATTN_REP, axis=1)
        kg = kk_ref[g, pl.ds(koff, kc), :]
        vg = vt_ref[g, :, pl.ds(koff, kc)]
        return jnp.dot(kg, qcat, preferred_element_type=F32) + bias, vg

    def set_bias(j, slot=0):
        bias_s[slot] = jnp.where(key_s[j] >= jnp.broadcast_to(t, (kc, tq)), 0.0, NEG_BIG)

    acc_s[...] = jnp.zeros_like(acc_s)

    def att_group(j0, count):
        units = [(slot, g) for slot in range(count) for g in range(ATTN_KV_HEADS)]
        for slot in range(count):
            set_bias(j0 + slot, slot)
        scores = {}

        def issue(u):
            slot, g = units[u]
            scores[u] = masked_scores(j0 + slot, g, slot)

        for u in range(min(DSA_ATT_LEAD, len(units))):
            issue(u)
        for u, (slot, g) in enumerate(units):
            s, vg = scores.pop(u)
            acc_s[g] += jnp.dot(vg, jnp.exp(s).astype(BF16), preferred_element_type=F32)
            if u + DSA_ATT_LEAD < len(units):
                issue(u + DSA_ATT_LEAD)

    sweep_chunks(nchunks, att_group, DSA_ATT_GROUP)

    norm = acc_s[:, ATTN_HEAD_DIM:ATTN_HEAD_DIM + 1, :]
    in_range = jnp.where(norm >= SOFTMAX_NORM_MIN, jnp.where(norm <= SOFTMAX_NORM_MAX, 1.0, 0.0), 0.0)

    @pl.when(jnp.min(in_range) < 0.5)
    def _():
        acc_s[...] = jnp.zeros_like(acc_s)

        def online_body(j, ms):
            set_bias(j)
            out = []
            for g in range(ATTN_KV_HEADS):
                s, vg = masked_scores(j, g)
                m_new = jnp.maximum(ms[g], jnp.max(s, axis=0, keepdims=True))
                p = jnp.exp(s - m_new).astype(BF16)
                pv = jnp.dot(vg, p, preferred_element_type=F32)
                acc_s[g] = jnp.exp(ms[g] - m_new) * acc_s[g] + pv
                out.append(m_new)
            return tuple(out)

        m_init = tuple(jnp.full((1, ATTN_REP * tq), NEG_BIG, F32) for _ in range(ATTN_KV_HEADS))
        lax.fori_loop(0, nchunks, online_body, m_init)

    for h in range(ATTN_HEADS):
        g, r = divmod(h, ATTN_REP)
        a = acc_s[g, :, r * tq:(r + 1) * tq]
        ot_s[h * ATTN_HEAD_DIM:(h + 1) * ATTN_HEAD_DIM, :] = (
            a[:ATTN_HEAD_DIM] / a[ATTN_HEAD_DIM:ATTN_HEAD_DIM + 1])
    o_ref[...] = ot_s[...].T.astype(o_ref.dtype)


def _dsa(qt, iqt, iwt, kk, vt, ik, batch, seq):
    n = batch * seq
    tq, kc = DSA_TQ, DSA_KC
    assert seq % tq == 0
    nq = seq // tq
    topk = min(TOPK_MAX, seq // 4)
    nch = seq // kc
    nout = ATTN_HEADS * ATTN_HEAD_DIM
    qmap = lambda b, i: (0, 0, b * nq + i)
    est = (nch * kc * tq * 8 + 3 * DSA_ATT_GROUP * ATTN_REP * kc * tq * 4 + ATTN_HEADS * (VT_ROWS + 8) * tq * 4 + nout * tq * 4
           + ATTN_KV_HEADS * seq * (V7X_LANES + VT_ROWS) * 2 + seq * V7X_LANES * 2
           + 2 * (2 * ATTN_HEADS * ATTN_HEAD_DIM * tq * 2 + 8 * tq * 4 + tq * nout * 2))
    return pl.pallas_call(
        functools.partial(_dsa_kernel, topk=topk),
        out_shape=jax.ShapeDtypeStruct((n, nout), BF16),
        grid=(batch, nq),
        in_specs=[
            pl.BlockSpec((ATTN_HEADS, ATTN_HEAD_DIM, tq), qmap),
            pl.BlockSpec((IDX_HEADS, IDX_DIM, tq), qmap),
            pl.BlockSpec((IDX_HEADS, tq), lambda b, i: (0, b * nq + i)),
            pl.BlockSpec((ATTN_KV_HEADS, seq, ATTN_HEAD_DIM), lambda b, i: (0, b, 0),
                         pipeline_mode=pl.Buffered(1)),
            pl.BlockSpec((ATTN_KV_HEADS, VT_ROWS, seq), lambda b, i: (0, 0, b),
                         pipeline_mode=pl.Buffered(1)),
            pl.BlockSpec((seq, IDX_DIM), lambda b, i: (b, 0), pipeline_mode=pl.Buffered(1)),
        ],
        out_specs=pl.BlockSpec((tq, nout), lambda b, i: (b * nq + i, 0)),
        scratch_shapes=[
            pltpu.VMEM((nch, kc, tq), jnp.int32),
            pltpu.VMEM((nch, kc, tq), jnp.int16),
            pltpu.VMEM((nch, kc, tq), jnp.int16),
            pltpu.VMEM((V7X_BF16_ROWS, tq), jnp.int16),
            pltpu.VMEM((1, tq), F32),
            pltpu.VMEM((DSA_ATT_GROUP, kc, tq), F32),
            pltpu.VMEM((ATTN_KV_HEADS, VT_ROWS, ATTN_REP * tq), F32),
            pltpu.VMEM((nout, tq), F32),
        ],
        compiler_params=pltpu.CompilerParams(
            dimension_semantics=("arbitrary", "arbitrary"), vmem_limit_bytes=_vmem_limit(est)),
    )(qt, iqt, iwt, kk, vt, ik)


def _gla_kernel(gqk_ref, gv_ref, la_ref, gg_ref, gn_ref, o_ref, st_s):
    @pl.when(pl.program_id(1) == 0)
    def _():
        st_s[...] = jnp.zeros_like(st_s)

    c = GLA_CHUNK
    nqk = GLA_HEADS * GLA_DK
    r_i = lax.broadcasted_iota(jnp.int32, (c, c), 0)
    c_i = lax.broadcasted_iota(jnp.int32, (c, c), 1)
    tri = r_i >= c_i
    tri_f = tri.astype(F32)
    gn = gn_ref[...]
    nchunk = GLA_ROWS // c
    heads = range(GLA_HEADS)
    hs = [slice(h * GLA_DK, (h + 1) * GLA_DK) for h in heads]
    vs = [slice(h * GLA_DV, (h + 1) * GLA_DV) for h in heads]

    local = []
    for ci in range(nchunk):
        rows = slice(ci * c, (ci + 1) * c)
        la = la_ref[rows, :]
        b = jnp.dot(tri_f, la, preferred_element_type=F32, precision=lax.Precision.HIGHEST)
        b_last = b[c - 1:c, :]
        q = gqk_ref[rows, :nqk]
        k = gqk_ref[rows, nqk:]
        q_dec = (q * jnp.exp(b)).astype(BF16)
        k_in = (k * jnp.exp(-b)).astype(BF16)
        k_out = (k * jnp.exp(b_last - b)).astype(BF16)
        decay = jnp.exp(b_last)
        intra, upd = [], []
        for h in heads:
            v = gv_ref[rows, vs[h]]
            a = lax.dot_general(q_dec[:, hs[h]], k_in[:, hs[h]], NT_DIMS, preferred_element_type=F32)
            a = jnp.where(tri, a, 0.0).astype(BF16)
            intra.append(jnp.dot(a, v, preferred_element_type=F32))
            upd.append(lax.dot_general(v, k_out[:, hs[h]], TN_DIMS, preferred_element_type=F32))
        local.append((rows, q_dec, decay, intra, upd))

    st = [st_s[h] for h in heads]
    for rows, q_dec, decay, intra, upd in local:
        for h in heads:
            o = intra[h] + lax.dot_general(q_dec[:, hs[h]], st[h].astype(BF16), NT_DIMS,
                                           preferred_element_type=F32)
            st[h] = st[h] * decay[:, hs[h]] + upd[h]
            gate = gg_ref[rows, vs[h]]
            o_ref[rows, vs[h]] = (_rms(o, gn) * (gate * jax.nn.sigmoid(gate))).astype(o_ref.dtype)
    for h in heads:
        st_s[h] = st[h]


def _gla(gqk, gv, la, gg, g_norm, batch, seq):
    n = batch * seq
    t = GLA_ROWS
    assert seq % t == 0
    ns = seq // t
    row = lambda b, i: (b * ns + i, 0)
    nqk = GLA_HEADS * GLA_DK
    nv = GLA_HEADS * GLA_DV
    return pl.pallas_call(
        _gla_kernel,
        out_shape=jax.ShapeDtypeStruct((n, nv), BF16),
        grid=(batch, ns),
        in_specs=[
            pl.BlockSpec((t, 2 * nqk), row),
            pl.BlockSpec((t, nv), row),
            pl.BlockSpec((t, nqk), row),
            pl.BlockSpec((t, nv), row),
            _const_spec((1, GLA_DV)),
        ],
        out_specs=pl.BlockSpec((t, nv), row),
        scratch_shapes=[pltpu.VMEM((GLA_HEADS, GLA_DV, GLA_DK), F32)],
        compiler_params=pltpu.CompilerParams(dimension_semantics=("arbitrary", "arbitrary")),
    )(gqk, gv, la, gg, g_norm.reshape(1, GLA_DV))


def kernel(x, g_ffn1_pre, w_ffn1_gate, w_ffn1_up, w_ffn1_down, g_ffn1_post, g_mix_pre, w_in, w_gla_a2,
           b_gla_a, g_gla_norm, w_out, g_mix_post, g_ffn2_pre, w_ffn2_gate, w_ffn2_up, w_ffn2_down,
           g_ffn2_post):
    batch, seq, d = x.shape
    h = x.reshape(batch * seq, d)
    for l in range(g_ffn1_pre.shape[0]):
        h = _ffn(h, g_ffn1_pre[l], w_ffn1_gate[l], w_ffn1_up[l], w_ffn1_down[l], g_ffn1_post[l])
        qt, iqt, vt, iwt, kk, ik, gqk, gv, gg, la = _inproj(h, g_mix_pre[l], w_in[l], w_gla_a2[l], b_gla_a[l])
        oa = _dsa(qt, iqt, iwt, kk, vt, ik, batch, seq)
        og = _gla(gqk, gv, la, gg, g_gla_norm[l], batch, seq)
        h = _mix_ffn(h, oa, og, w_out[l], g_mix_post[l],
                     g_ffn2_pre[l], w_ffn2_gate[l], w_ffn2_up[l], w_ffn2_down[l], g_ffn2_post[l])
    return h.reshape(batch, seq, d)
```

```python
import functools

import jax
import jax.numpy as jnp
from jax import lax
from jax.experimental import pallas as pl
from jax.experimental.pallas import tpu as pltpu

ATTN_HEADS = 8
ATTN_KV_HEADS = 2
ATTN_HEAD_DIM = 64
ATTN_REP = ATTN_HEADS // ATTN_KV_HEADS
IDX_HEADS = 8
IDX_DIM = 64
TOPK_MAX = 256
GLA_HEADS = 4
GLA_DK = 64
GLA_DV = 128
GLA_GATE_RANK = 16
GLA_TAU = 16.0
GLA_CHUNK = 64
EPS = 1e-6

V7X_LANES = 128
V7X_SUBLANES = 8
V7X_BF16_ROWS = 16
V7X_MXU_DIM = 256
V7X_VMEM_BYTES = 64 * 2**20

FFN_ROWS = 1024
FF_CHUNK = V7X_MXU_DIM
DSA_TQ = 256
DSA_KC = DSA_TQ
DSA_SCORE_GROUP = 16
DSA_COUNT_GROUP = 16
DSA_ATT_GROUP = 4
DSA_ATT_LEAD = 3
GLA_ROWS = 1024
VT_ROWS = ATTN_HEAD_DIM + V7X_BF16_ROWS

INT_MIN = -2**31
INT16_MIN, INT16_MAX = -2**15, 2**15 - 1
NEG_BIG = -1e30
SOFTMAX_NORM_MIN = 2.0 ** -60
SOFTMAX_NORM_MAX = 2.0 ** 100

F32 = jnp.float32
BF16 = jnp.bfloat16
NT_DIMS = (((1,), (1,)), ((), ()))
TN_DIMS = (((0,), (0,)), ((), ()))


def _vmem_limit(nbytes):
    return int(min(nbytes * 1.25 + (8 << 20), V7X_VMEM_BYTES - (6 << 20)))


def _rms(x, g):
    return x * lax.rsqrt(jnp.mean(x * x, axis=-1, keepdims=True) + EPS) * g


def _const_spec(shape):
    nd = len(shape)
    return pl.BlockSpec(shape, lambda *_: (0,) * nd, pipeline_mode=pl.Buffered(1))


def _ffn_kernel(x_ref, gpre_ref, wg_ref, wu_ref, wd_ref, gpost_ref, o_ref):
    o_ref[...] = _ffn_residual(x_ref[...], gpre_ref, wg_ref, wu_ref, wd_ref, gpost_ref)


def _mix_ffn_kernel(x_ref, oa_ref, og_ref, wo_ref, gmix_ref, gpre_ref, wg_ref, wu_ref, wd_ref, gpost_ref, o_ref):
    m = jnp.dot(oa_ref[...], wo_ref[0], preferred_element_type=F32)
    m = m + jnp.dot(og_ref[...], wo_ref[1], preferred_element_type=F32)
    x2 = x_ref[...] + _rms(m, gmix_ref[...])
    o_ref[...] = _ffn_residual(x2, gpre_ref, wg_ref, wu_ref, wd_ref, gpost_ref)


def _ffn_residual(x, gpre_ref, wg_ref, wu_ref, wd_ref, gpost_ref):
    xn = _rms(x, gpre_ref[...]).astype(BF16)

    def cols(c):
        return slice(c * FF_CHUNK, (c + 1) * FF_CHUNK)

    def gate_up(c):
        return (jnp.dot(xn, wg_ref[:, cols(c)], preferred_element_type=F32),
                jnp.dot(xn, wu_ref[:, cols(c)], preferred_element_type=F32))

    nch = wg_ref.shape[1] // FF_CHUNK
    acc = None
    nxt = gate_up(0)
    for c in range(nch):
        g, u = nxt
        if c + 1 < nch:
            nxt = gate_up(c + 1)
        a = (g * jax.nn.sigmoid(g) * u).astype(BF16)
        d = jnp.dot(a, wd_ref[cols(c), :], preferred_element_type=F32)
        acc = d if acc is None else acc + d
    return x + 0.5 * _rms(acc, gpost_ref[...])


def _ffn_operands(d, g_pre, w_gate, w_up, w_down, g_post):
    dff = w_gate.shape[1]
    assert dff % FF_CHUNK == 0
    wg, wu, wd = w_gate.astype(BF16), w_up.astype(BF16), w_down.astype(BF16)
    specs = [_const_spec((1, d)), _const_spec(wg.shape), _const_spec(wu.shape), _const_spec(wd.shape),
             _const_spec((1, d))]
    args = (g_pre.reshape(1, d), wg, wu, wd, g_post.reshape(1, d))
    est = 4 * FFN_ROWS * d * 4 + 3 * d * dff * 2 + FFN_ROWS * d * 8 + 4 * FFN_ROWS * FF_CHUNK * 4
    return specs, args, est


def _ffn(x, g_pre, w_gate, w_up, w_down, g_post):
    n, d = x.shape
    assert n % FFN_ROWS == 0
    ffn_specs, ffn_args, est = _ffn_operands(d, g_pre, w_gate, w_up, w_down, g_post)
    x_spec = pl.BlockSpec((FFN_ROWS, d), lambda i: (i, 0))
    return pl.pallas_call(
        _ffn_kernel,
        out_shape=jax.ShapeDtypeStruct((n, d), F32),
        grid=(n // FFN_ROWS,),
        in_specs=[x_spec] + ffn_specs,
        out_specs=x_spec,
        compiler_params=pltpu.CompilerParams(
            dimension_semantics=("arbitrary",), vmem_limit_bytes=_vmem_limit(est)),
    )(x, *ffn_args)


def _mix_ffn(x, oa, og, w_out, g_mix, g_pre, w_gate, w_up, w_down, g_post):
    n, d = x.shape
    assert n % FFN_ROWS == 0
    ffn_specs, ffn_args, est = _ffn_operands(d, g_pre, w_gate, w_up, w_down, g_post)
    half = oa.shape[1]
    assert og.shape[1] == half and w_out.shape == (2 * half, d)
    wo = w_out.reshape(2, half, d).astype(BF16)
    row = lambda i: (i, 0)
    x_spec = pl.BlockSpec((FFN_ROWS, d), row)
    o_spec = pl.BlockSpec((FFN_ROWS, half), row)
    est += 4 * FFN_ROWS * half * 2 + 2 * half * d * 2 + FFN_ROWS * d * 4
    return pl.pallas_call(
        _mix_ffn_kernel,
        out_shape=jax.ShapeDtypeStruct((n, d), F32),
        grid=(n // FFN_ROWS,),
        in_specs=[x_spec, o_spec, o_spec, _const_spec(wo.shape), _const_spec((1, d))] + ffn_specs,
        out_specs=x_spec,
        compiler_params=pltpu.CompilerParams(
            dimension_semantics=("arbitrary",), vmem_limit_bytes=_vmem_limit(est)),
    )(x, oa, og, wo, g_mix.reshape(1, d), *ffn_args)


_K_OFF = 0
_MISC_OFF = _K_OFF + ATTN_KV_HEADS * ATTN_HEAD_DIM
_GA_LANE = IDX_DIM + IDX_HEADS
_GQ_OFF = _MISC_OFF + V7X_LANES
_GK_OFF = _GQ_OFF + GLA_HEADS * GLA_DK
_GV_OFF = _GK_OFF + GLA_HEADS * GLA_DK
_GG_OFF = _GV_OFF + GLA_HEADS * GLA_DV
_TOK_COLS = _GG_OFF + GLA_HEADS * GLA_DV
_QT_OFF = 0
_IQT_OFF = _QT_OFF + ATTN_HEADS * ATTN_HEAD_DIM
_VT_OFF = _IQT_OFF + IDX_HEADS * IDX_DIM
_IWT_OFF = _VT_OFF + ATTN_KV_HEADS * ATTN_HEAD_DIM
_T_ROWS = _IWT_OFF + V7X_BF16_ROWS


def _inproj_kernel(x_ref, g_ref, wtok_ref, wt_ref, wa2_ref, ba_ref,
                   qt_ref, iqt_ref, vt_ref, iwt_ref, kk_ref, ik_ref, gqk_ref, gv_ref, gg_ref, la_ref):
    h = _rms(x_ref[...], g_ref[...]).astype(BF16)
    t = h.shape[0]
    pt = lax.dot_general(wt_ref[...], h, NT_DIMS, preferred_element_type=F32)
    hd, idim = ATTN_HEAD_DIM, IDX_DIM
    for i in range(ATTN_HEADS):
        qt_ref[i] = pt[_QT_OFF + hd * i:_QT_OFF + hd * (i + 1)].astype(BF16)
    for i in range(IDX_HEADS):
        iqt_ref[i] = pt[_IQT_OFF + idim * i:_IQT_OFF + idim * (i + 1)].astype(BF16)
    ones_row = (lax.broadcasted_iota(jnp.int32, (V7X_BF16_ROWS, t), 0) == 0).astype(F32)
    for g in range(ATTN_KV_HEADS):
        v_t = pt[_VT_OFF + hd * g:_VT_OFF + hd * (g + 1)]
        vt_ref[g] = jnp.concatenate([v_t, ones_row], axis=0).astype(BF16)
    iwt_ref[...] = pt[_IWT_OFF:_IWT_OFF + IDX_HEADS] * (IDX_HEADS ** -0.5)

    proj = jnp.dot(h, wtok_ref[...], preferred_element_type=F32)
    for g in range(ATTN_KV_HEADS):
        kk_ref[g] = proj[:, _K_OFF + hd * g:_K_OFF + hd * (g + 1)].astype(BF16)
    misc = proj[:, _MISC_OFF:_MISC_OFF + V7X_LANES]
    ik_ref[...] = misc[:, :IDX_DIM].astype(BF16)
    gqk_ref[...] = proj[:, _GQ_OFF:_GV_OFF]
    gv_ref[...] = proj[:, _GV_OFF:_GG_OFF].astype(BF16)
    gg_ref[...] = proj[:, _GG_OFF:_GG_OFF + GLA_HEADS * GLA_DV]
    z = jnp.dot(misc, wa2_ref[...], preferred_element_type=F32, precision=lax.Precision.HIGHEST)
    z = z + ba_ref[...]
    log_sig = jnp.minimum(z, 0.0) - jnp.log1p(jnp.exp(-jnp.abs(z)))
    la_ref[...] = log_sig * (1.0 / GLA_TAU)


def _pack_w_in(w_in):
    d = w_in.shape[0]
    kv_w = ATTN_KV_HEADS * ATTN_HEAD_DIM
    sizes = (ATTN_HEADS * ATTN_HEAD_DIM, kv_w, kv_w, IDX_HEADS * IDX_DIM, IDX_DIM, IDX_HEADS,
             GLA_HEADS * GLA_DK, GLA_HEADS * GLA_DK, GLA_HEADS * GLA_DV, GLA_GATE_RANK, GLA_HEADS * GLA_DV)
    assert sum(sizes) == w_in.shape[1]
    offs = [0]
    for s in sizes:
        offs.append(offs[-1] + s)
    aq, ak, av, iq, ik, iw, gq, gk, gv, ga, gg = [w_in[:, offs[i]:offs[i + 1]] for i in range(len(sizes))]
    z = lambda n: jnp.zeros((d, n), w_in.dtype)
    misc_pad = V7X_LANES - _GA_LANE - GLA_GATE_RANK
    w_tok = jnp.concatenate([ak, ik, z(IDX_HEADS), ga, z(misc_pad), gq * (GLA_DK ** -0.5), gk, gv, gg], axis=1)
    w_t = jnp.concatenate([aq * (ATTN_HEAD_DIM ** -0.5), iq * (IDX_DIM ** -0.5), av, iw,
                           z(V7X_BF16_ROWS - IDX_HEADS)], axis=1).T
    assert w_tok.shape[1] == _TOK_COLS and w_t.shape[0] == _T_ROWS
    return w_tok.astype(BF16), w_t.astype(BF16)


def _inproj(x1, g_mix_pre, w_in, w_gla_a2, b_gla_a):
    n, d = x1.shape
    t = FFN_ROWS
    assert n % t == 0
    w_tok, w_t = _pack_w_in(w_in)
    nqk = GLA_HEADS * GLA_DK
    nv = GLA_HEADS * GLA_DV
    wa2 = jnp.zeros((V7X_LANES, nqk), F32).at[_GA_LANE:_GA_LANE + GLA_GATE_RANK].set(w_gla_a2)
    row = lambda i: (i, 0)
    row3 = lambda i: (0, i, 0)
    col3 = lambda i: (0, 0, i)
    out_shape = (
        jax.ShapeDtypeStruct((ATTN_HEADS, ATTN_HEAD_DIM, n), BF16),
        jax.ShapeDtypeStruct((IDX_HEADS, IDX_DIM, n), BF16),
        jax.ShapeDtypeStruct((ATTN_KV_HEADS, VT_ROWS, n), BF16),
        jax.ShapeDtypeStruct((IDX_HEADS, n), F32),
        jax.ShapeDtypeStruct((ATTN_KV_HEADS, n, ATTN_HEAD_DIM), BF16),
        jax.ShapeDtypeStruct((n, IDX_DIM), BF16),
        jax.ShapeDtypeStruct((n, 2 * nqk), F32),
        jax.ShapeDtypeStruct((n, nv), BF16),
        jax.ShapeDtypeStruct((n, nv), F32),
        jax.ShapeDtypeStruct((n, nqk), F32),
    )
    out_specs = (
        pl.BlockSpec((ATTN_HEADS, ATTN_HEAD_DIM, t), col3),
        pl.BlockSpec((IDX_HEADS, IDX_DIM, t), col3),
        pl.BlockSpec((ATTN_KV_HEADS, VT_ROWS, t), col3),
        pl.BlockSpec((IDX_HEADS, t), lambda i: (0, i)),
        pl.BlockSpec((ATTN_KV_HEADS, t, ATTN_HEAD_DIM), row3),
        pl.BlockSpec((t, IDX_DIM), row),
        pl.BlockSpec((t, 2 * nqk), row),
        pl.BlockSpec((t, nv), row),
        pl.BlockSpec((t, nv), row),
        pl.BlockSpec((t, nqk), row),
    )
    est = (2 * t * d * 4 + d * (_TOK_COLS + _T_ROWS) * 2 + t * (_TOK_COLS + _T_ROWS) * 4 * 2
           + 2 * t * 8 * 1024)
    return pl.pallas_call(
        _inproj_kernel,
        out_shape=out_shape,
        grid=(n // t,),
        in_specs=[
            pl.BlockSpec((t, d), row),
            _const_spec((1, d)),
            _const_spec(w_tok.shape),
            _const_spec(w_t.shape),
            _const_spec(wa2.shape),
            _const_spec((1, nqk)),
        ],
        out_specs=out_specs,
        compiler_params=pltpu.CompilerParams(
            dimension_semantics=("arbitrary",), vmem_limit_bytes=_vmem_limit(est)),
    )(x1, g_mix_pre.reshape(1, d), w_tok, w_t, wa2, b_gla_a.reshape(1, nqk))


def _dsa_kernel(qt_ref, iqt_ref, iwt_ref, kk_ref, vt_ref, ik_ref, o_ref,
                key_s, half_s, low_s, cnt_s, seen_s, bias_s, acc_s, ot_s, *, topk):
    tq = DSA_TQ
    kc = DSA_KC
    sub = V7X_SUBLANES
    i = pl.program_id(1)
    nchunks = i + 1

    def score_logits(j):
        ikc = ik_ref[pl.ds(pl.multiple_of(j * kc, kc), kc), :]
        iqcat = jnp.concatenate([iqt_ref[h] for h in range(IDX_HEADS)], axis=1)
        return jnp.dot(ikc, iqcat, preferred_element_type=F32)

    def score_keys(j, lg, diagonal):
        sc = None
        for h in range(IDX_HEADS):
            term = jnp.maximum(lg[:, h * tq:(h + 1) * tq], 0.0) * iwt_ref[h:h + 1, :]
            sc = term if sc is None else sc + term
        bits = pltpu.bitcast(sc, jnp.int32)
        key = jnp.where(bits < 0, INT_MIN - bits, bits)
        if diagonal:
            kpos = lax.broadcasted_iota(jnp.int32, (kc, tq), 0)
            qpos = lax.broadcasted_iota(jnp.int32, (kc, tq), 1)
            key = jnp.where(kpos <= qpos, key, INT_MIN)
        key_s[j] = key
        half_s[j] = lax.shift_right_arithmetic(key, 16).astype(jnp.int16)
        low_s[j] = key.astype(jnp.int16) ^ jnp.int16(INT16_MIN)

    def sweep_chunks(n, group_fn, gsz):
        assert gsz & (gsz - 1) == 0

        def trip(q, c):
            group_fn(q * gsz, gsz)
            return c

        lax.fori_loop(0, lax.shift_right_logical(n, gsz.bit_length() - 1), trip, 0)
        done = n & -gsz
        part = gsz // 2
        while part:
            pl.when((n & part) != 0)(functools.partial(group_fn, done, part))
            done = done + (n & part)
            part //= 2

    def score_group(j0, count):
        lgs = [score_logits(j0 + s) for s in range(count)]
        for s in range(count):
            score_keys(j0 + s, lgs[s], False)

    sweep_chunks(i, score_group, DSA_SCORE_GROUP)
    score_keys(i, score_logits(i), True)

    qidx = i * tq + lax.broadcasted_iota(jnp.int32, (1, tq), 1)
    krow = jnp.minimum(topk, qidx + 1).astype(F32)

    def count(ind_fn):
        def body(j, acc):
            parts = [acc, None, None, None]
            for s in range(kc // sub):
                ind = ind_fn(key_s[j, s * sub:(s + 1) * sub, :], j * kc + s * sub)
                p = s % 4
                parts[p] = ind if parts[p] is None else parts[p] + ind
            return (parts[0] + parts[1]) + (parts[2] + parts[3])
        acc = lax.fori_loop(0, nchunks, body, jnp.zeros((sub, tq), F32))
        return jnp.sum(acc, axis=0, keepdims=True)

    def count_ge(cand):
        cb = jnp.broadcast_to(cand, (sub, tq))
        return count(lambda k, base: jnp.where(k >= cb, 1.0, 0.0))

    rows16 = V7X_BF16_ROWS

    def count16(cand):
        cb = jnp.broadcast_to(cand, (rows16, tq)).astype(jnp.int16)
        one, nil = jnp.int16(1), jnp.int16(0)
        cnt_s[...] = jnp.zeros_like(cnt_s)

        def group(j0, count):
            parts = [None] * 4
            for s in range(count * kc // rows16):
                c, r = divmod(s, kc // rows16)
                ind = jnp.where(half_s[j0 + c, r * rows16:(r + 1) * rows16, :] >= cb, one, nil)
                p = s % 4
                parts[p] = ind if parts[p] is None else parts[p] + ind
            cnt_s[...] += (parts[0] + parts[1]) + (parts[2] + parts[3])

        sweep_chunks(nchunks, group, DSA_COUNT_GROUP)
        return jnp.sum(cnt_s[...].astype(F32), axis=0, keepdims=True)

    def search16(count_at_min):
        zero = jnp.zeros((1, tq), jnp.int32)
        c0 = count16(zero)
        start = (jnp.where(c0 >= krow, zero, INT16_MIN), jnp.where(c0 >= krow, c0, count_at_min))

        def bit_body(it, carry):
            th, cth = carry
            cand = th | lax.shift_left(jnp.int32(1), 14 - it)
            c = count16(cand)
            return jnp.where(c >= krow, cand, th), jnp.where(c >= krow, c, cth)

        return lax.fori_loop(0, 15, bit_body, start)

    t_hi, c_hi = search16(jnp.full((1, tq), 1.0, F32) * (nchunks * kc).astype(F32))
    th16 = jnp.broadcast_to(t_hi, (kc, tq)).astype(jnp.int16)

    def low_halves(j0, count):
        for j in range(count):
            hi = half_s[j0 + j]
            half_s[j0 + j] = jnp.where(hi == th16, low_s[j0 + j],
                                       jnp.where(hi > th16, jnp.int16(INT16_MAX), jnp.int16(INT16_MIN)))

    sweep_chunks(nchunks, low_halves, DSA_SCORE_GROUP)
    t_lo, c_ge = search16(c_hi)
    t = lax.shift_left(t_hi, 16) | (t_lo - INT16_MIN)

    @pl.when(jnp.max(c_ge - krow) > 0.5)
    def _():
        saturated = jnp.max(jnp.where(t_lo == INT16_MAX, 1.0, 0.0)) > 0.5
        c_gt = lax.cond(saturated, lambda: count_ge(t + 1), lambda: count16(jnp.minimum(t_lo + 1, INT16_MAX)))
        need = krow - c_gt
        tb_full = jnp.broadcast_to(t, (kc, tq))
        tri = (lax.broadcasted_iota(jnp.int32, (kc, kc), 0)
               >= lax.broadcasted_iota(jnp.int32, (kc, kc), 1)).astype(BF16)

        seen_s[...] = jnp.zeros_like(seen_s)

        def drop_surplus(j0, count):
            ks = [key_s[j0 + s] for s in range(count)]
            prefix = [jnp.dot(tri, jnp.where(k == tb_full, 1.0, 0.0).astype(BF16), preferred_element_type=F32)
                      for k in ks]
            seen = seen_s[...]
            for s in range(count):
                rank = seen + prefix[s]
                key_s[j0 + s] = jnp.where(ks[s] == tb_full, jnp.where(rank > need, INT_MIN, ks[s]), ks[s])
                seen = rank[kc - 1:kc, :]
            seen_s[...] = seen

        sweep_chunks(nchunks, drop_surplus, DSA_SCORE_GROUP)

    def masked_scores(j, g, slot=0):
        koff = pl.multiple_of(j * kc, kc)
        qcat = jnp.concatenate([qt_ref[g * ATTN_REP + r] for r in range(ATTN_REP)], axis=1)
        bias = jnp.concatenate([bias_s[slot]] * ATTN_REP, axis=1)
        kg = kk_ref[g, pl.ds(koff, kc), :]
        vg = vt_ref[g, :, pl.ds(koff, kc)]
        return jnp.dot(kg, qcat, preferred_element_type=F32) + bias, vg

    def set_bias(j, slot=0):
        bias_s[slot] = jnp.where(key_s[j] >= jnp.broadcast_to(t, (kc, tq)), 0.0, NEG_BIG)

    acc_s[...] = jnp.zeros_like(acc_s)

    def att_group(j0, count):
        units = [(slot, g) for slot in range(count) for g in range(ATTN_KV_HEADS)]
        for slot in range(count):
            set_bias(j0 + slot, slot)
        scores = {}

        def issue(u):
            slot, g = units[u]
            scores[u] = masked_scores(j0 + slot, g, slot)

        for u in range(min(DSA_ATT_LEAD, len(units))):
            issue(u)
        for u, (slot, g) in enumerate(units):
            s, vg = scores.pop(u)
            acc_s[g] += jnp.dot(vg, jnp.exp(s).astype(BF16), preferred_element_type=F32)
            if u + DSA_ATT_LEAD < len(units):
                issue(u + DSA_ATT_LEAD)

    sweep_chunks(nchunks, att_group, DSA_ATT_GROUP)

    norm = acc_s[:, ATTN_HEAD_DIM:ATTN_HEAD_DIM + 1, :]
    in_range = jnp.where(norm >= SOFTMAX_NORM_MIN, jnp.where(norm <= SOFTMAX_NORM_MAX, 1.0, 0.0), 0.0)

    @pl.when(jnp.min(in_range) < 0.5)
    def _():
        acc_s[...] = jnp.zeros_like(acc_s)

        def online_body(j, ms):
            set_bias(j)
            out = []
            for g in range(ATTN_KV_HEADS):
                s, vg = masked_scores(j, g)
                m_new = jnp.maximum(ms[g], jnp.max(s, axis=0, keepdims=True))
                p = jnp.exp(s - m_new).astype(BF16)
                pv = jnp.dot(vg, p, preferred_element_type=F32)
                acc_s[g] = jnp.exp(ms[g] - m_new) * acc_s[g] + pv
                out.append(m_new)
            return tuple(out)

        m_init = tuple(jnp.full((1, ATTN_REP * tq), NEG_BIG, F32) for _ in range(ATTN_KV_HEADS))
        lax.fori_loop(0, nchunks, online_body, m_init)

    for h in range(ATTN_HEADS):
        g, r = divmod(h, ATTN_REP)
        a = acc_s[g, :, r * tq:(r + 1) * tq]
        ot_s[h * ATTN_HEAD_DIM:(h + 1) * ATTN_HEAD_DIM, :] = (
            a[:ATTN_HEAD_DIM] / a[ATTN_HEAD_DIM:ATTN_HEAD_DIM + 1])
    o_ref[...] = ot_s[...].T.astype(o_ref.dtype)


def _dsa(qt, iqt, iwt, kk, vt, ik, batch, seq):
    n = batch * seq
    tq, kc = DSA_TQ, DSA_KC
    assert seq % tq == 0
    nq = seq // tq
    topk = min(TOPK_MAX, seq // 4)
    nch = seq // kc
    nout = ATTN_HEADS * ATTN_HEAD_DIM
    qmap = lambda b, i: (0, 0, b * nq + i)
    est = (nch * kc * tq * 8 + 3 * DSA_ATT_GROUP * ATTN_REP * kc * tq * 4 + ATTN_HEADS * (VT_ROWS + 8) * tq * 4 + nout * tq * 4
           + ATTN_KV_HEADS * seq * (V7X_LANES + VT_ROWS) * 2 + seq * V7X_LANES * 2
           + 2 * (2 * ATTN_HEADS * ATTN_HEAD_DIM * tq * 2 + 8 * tq * 4 + tq * nout * 2))
    return pl.pallas_call(
        functools.partial(_dsa_kernel, topk=topk),
        out_shape=jax.ShapeDtypeStruct((n, nout), BF16),
        grid=(batch, nq),
        in_specs=[
            pl.BlockSpec((ATTN_HEADS, ATTN_HEAD_DIM, tq), qmap),
            pl.BlockSpec((IDX_HEADS, IDX_DIM, tq), qmap),
            pl.BlockSpec((IDX_HEADS, tq), lambda b, i: (0, b * nq + i)),
            pl.BlockSpec((ATTN_KV_HEADS, seq, ATTN_HEAD_DIM), lambda b, i: (0, b, 0),
                         pipeline_mode=pl.Buffered(1)),
            pl.BlockSpec((ATTN_KV_HEADS, VT_ROWS, seq), lambda b, i: (0, 0, b),
                         pipeline_mode=pl.Buffered(1)),
            pl.BlockSpec((seq, IDX_DIM), lambda b, i: (b, 0), pipeline_mode=pl.Buffered(1)),
        ],
        out_specs=pl.BlockSpec((tq, nout), lambda b, i: (b * nq + i, 0)),
        scratch_shapes=[
            pltpu.VMEM((nch, kc, tq), jnp.int32),
            pltpu.VMEM((nch, kc, tq), jnp.int16),
            pltpu.VMEM((nch, kc, tq), jnp.int16),
            pltpu.VMEM((V7X_BF16_ROWS, tq), jnp.int16),
            pltpu.VMEM((1, tq), F32),
            pltpu.VMEM((DSA_ATT_GROUP, kc, tq), F32),
            pltpu.VMEM((ATTN_KV_HEADS, VT_ROWS, ATTN_REP * tq), F32),
            pltpu.VMEM((nout, tq), F32),
        ],
        compiler_params=pltpu.CompilerParams(
            dimension_semantics=("arbitrary", "arbitrary"), vmem_limit_bytes=_vmem_limit(est)),
    )(qt, iqt, iwt, kk, vt, ik)


def _gla_kernel(gqk_ref, gv_ref, la_ref, gg_ref, gn_ref, o_ref, st_s):
    @pl.when(pl.program_id(1) == 0)
    def _():
        st_s[...] = jnp.zeros_like(st_s)

    c = GLA_CHUNK
    nqk = GLA_HEADS * GLA_DK
    r_i = lax.broadcasted_iota(jnp.int32, (c, c), 0)
    c_i = lax.broadcasted_iota(jnp.int32, (c, c), 1)
    tri = r_i >= c_i
    tri_f = tri.astype(F32)
    gn = gn_ref[...]
    nchunk = GLA_ROWS // c
    heads = range(GLA_HEADS)
    hs = [slice(h * GLA_DK, (h + 1) * GLA_DK) for h in heads]
    vs = [slice(h * GLA_DV, (h + 1) * GLA_DV) for h in heads]

    local = []
    for ci in range(nchunk):
        rows = slice(ci * c, (ci + 1) * c)
        la = la_ref[rows, :]
        b = jnp.dot(tri_f, la, preferred_element_type=F32, precision=lax.Precision.HIGHEST)
        b_last = b[c - 1:c, :]
        q = gqk_ref[rows, :nqk]
        k = gqk_ref[rows, nqk:]
        q_dec = (q * jnp.exp(b)).astype(BF16)
        k_in = (k * jnp.exp(-b)).astype(BF16)
        k_out = (k * jnp.exp(b_last - b)).astype(BF16)
        decay = jnp.exp(b_last)
        intra, upd = [], []
        for h in heads:
            v = gv_ref[rows, vs[h]]
            a = lax.dot_general(q_dec[:, hs[h]], k_in[:, hs[h]], NT_DIMS, preferred_element_type=F32)
            a = jnp.where(tri, a, 0.0).astype(BF16)
            intra.append(jnp.dot(a, v, preferred_element_type=F32))
            upd.append(lax.dot_general(v, k_out[:, hs[h]], TN_DIMS, preferred_element_type=F32))
        local.append((rows, q_dec, decay, intra, upd))

    st = [st_s[h] for h in heads]
    for rows, q_dec, decay, intra, upd in local:
        for h in heads:
            o = intra[h] + lax.dot_general(q_dec[:, hs[h]], st[h].astype(BF16), NT_DIMS,
                                           preferred_element_type=F32)
            st[h] = st[h] * decay[:, hs[h]] + upd[h]
            gate = gg_ref[rows, vs[h]]
            o_ref[rows, vs[h]] = (_rms(o, gn) * (gate * jax.nn.sigmoid(gate))).astype(o_ref.dtype)
    for h in heads:
        st_s[h] = st[h]


def _gla(gqk, gv, la, gg, g_norm, batch, seq):
    n = batch * seq
    t = GLA_ROWS
    assert seq % t == 0
    ns = seq // t
    row = lambda b, i: (b * ns + i, 0)
    nqk = GLA_HEADS * GLA_DK
    nv = GLA_HEADS * GLA_DV
    return pl.pallas_call(
        _gla_kernel,
        out_shape=jax.ShapeDtypeStruct((n, nv), BF16),
        grid=(batch, ns),
        in_specs=[
            pl.BlockSpec((t, 2 * nqk), row),
            pl.BlockSpec((t, nv), row),
            pl.BlockSpec((t, nqk), row),
            pl.BlockSpec((t, nv), row),
            _const_spec((1, GLA_DV)),
        ],
        out_specs=pl.BlockSpec((t, nv), row),
        scratch_shapes=[pltpu.VMEM((GLA_HEADS, GLA_DV, GLA_DK), F32)],
        compiler_params=pltpu.CompilerParams(dimension_semantics=("arbitrary", "arbitrary")),
    )(gqk, gv, la, gg, g_norm.reshape(1, GLA_DV))


def kernel(x, g_ffn1_pre, w_ffn1_gate, w_ffn1_up, w_ffn1_down, g_ffn1_post, g_mix_pre, w_in, w_gla_a2,
           b_gla_a, g_gla_norm, w_out, g_mix_post, g_ffn2_pre, w_ffn2_gate, w_ffn2_up, w_ffn2_down,
           g_ffn2_post):
    batch, seq, d = x.shape
    h = x.reshape(batch * seq, d)
    for l in range(g_ffn1_pre.shape[0]):
        h = _ffn(h, g_ffn1_pre[l], w_ffn1_gate[l], w_ffn1_up[l], w_ffn1_down[l], g_ffn1_post[l])
        qt, iqt, vt, iwt, kk, ik, gqk, gv, gg, la = _inproj(h, g_mix_pre[l], w_in[l], w_gla_a2[l], b_gla_a[l])
        oa = _dsa(qt, iqt, iwt, kk, vt, ik, batch, seq)
        og = _gla(gqk, gv, la, gg, g_gla_norm[l], batch, seq)
        h = _mix_ffn(h, oa, og, w_out[l], g_mix_post[l],
                     g_ffn2_pre[l], w_ffn2_gate[l], w_ffn2_up[l], w_ffn2_down[l], g_ffn2_post[l])
    return h.reshape(batch, seq, d)
```

```python
import functools

import jax
import jax.numpy as jnp
from jax import lax
from jax.experimental import pallas as pl
from jax.experimental.pallas import tpu as pltpu

ATTN_HEADS = 8
ATTN_KV_HEADS = 2
ATTN_HEAD_DIM = 64
ATTN_REP = ATTN_HEADS // ATTN_KV_HEADS
IDX_HEADS = 8
IDX_DIM = 64
TOPK_MAX = 256
GLA_HEADS = 4
GLA_DK = 64
GLA_DV = 128
GLA_GATE_RANK = 16
GLA_TAU = 16.0
GLA_CHUNK = 64
EPS = 1e-6

V7X_LANES = 128
V7X_SUBLANES = 8
V7X_BF16_ROWS = 16
V7X_MXU_DIM = 256
V7X_VMEM_BYTES = 64 * 2**20

FFN_ROWS = 1024
FF_CHUNK = V7X_MXU_DIM
DSA_TQ = 256
DSA_KC = DSA_TQ
DSA_SCORE_GROUP = 16
DSA_COUNT_GROUP = 16
DSA_ATT_GROUP = 4
DSA_ATT_LEAD = 3
GLA_ROWS = 1024
VT_ROWS = ATTN_HEAD_DIM + V7X_BF16_ROWS

INT_MIN = -2**31
INT16_MIN, INT16_MAX = -2**15, 2**15 - 1
NEG_BIG = -1e30
SOFTMAX_NORM_MIN = 2.0 ** -60
SOFTMAX_NORM_MAX = 2.0 ** 100

F32 = jnp.float32
BF16 = jnp.bfloat16
NT_DIMS = (((1,), (1,)), ((), ()))
TN_DIMS = (((0,), (0,)), ((), ()))


def _vmem_limit(nbytes):
    return int(min(nbytes * 1.25 + (8 << 20), V7X_VMEM_BYTES - (6 << 20)))


def _rms(x, g):
    return x * lax.rsqrt(jnp.mean(x * x, axis=-1, keepdims=True) + EPS) * g


def _const_spec(shape):
    nd = len(shape)
    return pl.BlockSpec(shape, lambda *_: (0,) * nd, pipeline_mode=pl.Buffered(1))


def _ffn_kernel(x_ref, gpre_ref, wg_ref, wu_ref, wd_ref, gpost_ref, o_ref):
    o_ref[...] = _ffn_residual(x_ref[...], gpre_ref, wg_ref, wu_ref, wd_ref, gpost_ref)


def _mix_ffn_kernel(x_ref, oa_ref, og_ref, wo_ref, gmix_ref, gpre_ref, wg_ref, wu_ref, wd_ref, gpost_ref, o_ref):
    m = jnp.dot(oa_ref[...], wo_ref[0], preferred_element_type=F32)
    m = m + jnp.dot(og_ref[...], wo_ref[1], preferred_element_type=F32)
    x2 = x_ref[...] + _rms(m, gmix_ref[...])
    o_ref[...] = _ffn_residual(x2, gpre_ref, wg_ref, wu_ref, wd_ref, gpost_ref)


def _ffn_residual(x, gpre_ref, wg_ref, wu_ref, wd_ref, gpost_ref):
    xn = _rms(x, gpre_ref[...]).astype(BF16)

    def cols(c):
        return slice(c * FF_CHUNK, (c + 1) * FF_CHUNK)

    def gate_up(c):
        return (jnp.dot(xn, wg_ref[:, cols(c)], preferred_element_type=F32),
                jnp.dot(xn, wu_ref[:, cols(c)], preferred_element_type=F32))

    nch = wg_ref.shape[1] // FF_CHUNK
    acc = None
    nxt = gate_up(0)
    for c in range(nch):
        g, u = nxt
        if c + 1 < nch:
            nxt = gate_up(c + 1)
        a = (g * jax.nn.sigmoid(g) * u).astype(BF16)
        d = jnp.dot(a, wd_ref[cols(c), :], preferred_element_type=F32)
        acc = d if acc is None else acc + d
    return x + 0.5 * _rms(acc, gpost_ref[...])


def _ffn_operands(d, g_pre, w_gate, w_up, w_down, g_post):
    dff = w_gate.shape[1]
    assert dff % FF_CHUNK == 0
    wg, wu, wd = w_gate.astype(BF16), w_up.astype(BF16), w_down.astype(BF16)
    specs = [_const_spec((1, d)), _const_spec(wg.shape), _const_spec(wu.shape), _const_spec(wd.shape),
             _const_spec((1, d))]
    args = (g_pre.reshape(1, d), wg, wu, wd, g_post.reshape(1, d))
    est = 4 * FFN_ROWS * d * 4 + 3 * d * dff * 2 + FFN_ROWS * d * 8 + 4 * FFN_ROWS * FF_CHUNK * 4
    return specs, args, est


def _ffn(x, g_pre, w_gate, w_up, w_down, g_post):
    n, d = x.shape
    assert n % FFN_ROWS == 0
    ffn_specs, ffn_args, est = _ffn_operands(d, g_pre, w_gate, w_up, w_down, g_post)
    x_spec = pl.BlockSpec((FFN_ROWS, d), lambda i: (i, 0))
    return pl.pallas_call(
        _ffn_kernel,
        out_shape=jax.ShapeDtypeStruct((n, d), F32),
        grid=(n // FFN_ROWS,),
        in_specs=[x_spec] + ffn_specs,
        out_specs=x_spec,
        compiler_params=pltpu.CompilerParams(
            dimension_semantics=("arbitrary",), vmem_limit_bytes=_vmem_limit(est)),
    )(x, *ffn_args)


def _mix_ffn(x, oa, og, w_out, g_mix, g_pre, w_gate, w_up, w_down, g_post):
    n, d = x.shape
    assert n % FFN_ROWS == 0
    ffn_specs, ffn_args, est = _ffn_operands(d, g_pre, w_gate, w_up, w_down, g_post)
    half = oa.shape[1]
    assert og.shape[1] == half and w_out.shape == (2 * half, d)
    wo = w_out.reshape(2, half, d).astype(BF16)
    row = lambda i: (i, 0)
    x_spec = pl.BlockSpec((FFN_ROWS, d), row)
    o_spec = pl.BlockSpec((FFN_ROWS, half), row)
    est += 4 * FFN_ROWS * half * 2 + 2 * half * d * 2 + FFN_ROWS * d * 4
    return pl.pallas_call(
        _mix_ffn_kernel,
        out_shape=jax.ShapeDtypeStruct((n, d), F32),
        grid=(n // FFN_ROWS,),
        in_specs=[x_spec, o_spec, o_spec, _const_spec(wo.shape), _const_spec((1, d))] + ffn_specs,
        out_specs=x_spec,
        compiler_params=pltpu.CompilerParams(
            dimension_semantics=("arbitrary",), vmem_limit_bytes=_vmem_limit(est)),
    )(x, oa, og, wo, g_mix.reshape(1, d), *ffn_args)


_K_OFF = 0
_MISC_OFF = _K_OFF + ATTN_KV_HEADS * ATTN_HEAD_DIM
_GA_LANE = IDX_DIM + IDX_HEADS
_GQ_OFF = _MISC_OFF + V7X_LANES
_GK_OFF = _GQ_OFF + GLA_HEADS * GLA_DK
_GV_OFF = _GK_OFF + GLA_HEADS * GLA_DK
_GG_OFF = _GV_OFF + GLA_HEADS * GLA_DV
_TOK_COLS = _GG_OFF + GLA_HEADS * GLA_DV
_QT_OFF = 0
_IQT_OFF = _QT_OFF + ATTN_HEADS * ATTN_HEAD_DIM
_VT_OFF = _IQT_OFF + IDX_HEADS * IDX_DIM
_IWT_OFF = _VT_OFF + ATTN_KV_HEADS * ATTN_HEAD_DIM
_T_ROWS = _IWT_OFF + V7X_BF16_ROWS


def _inproj_kernel(x_ref, g_ref, wtok_ref, wt_ref, wa2_ref, ba_ref,
                   qt_ref, iqt_ref, vt_ref, iwt_ref, kk_ref, ik_ref, gqk_ref, gv_ref, gg_ref, la_ref):
    h = _rms(x_ref[...], g_ref[...]).astype(BF16)
    t = h.shape[0]
    pt = lax.dot_general(wt_ref[...], h, NT_DIMS, preferred_element_type=F32)
    hd, idim = ATTN_HEAD_DIM, IDX_DIM
    for i in range(ATTN_HEADS):
        qt_ref[i] = pt[_QT_OFF + hd * i:_QT_OFF + hd * (i + 1)].astype(BF16)
    for i in range(IDX_HEADS):
        iqt_ref[i] = pt[_IQT_OFF + idim * i:_IQT_OFF + idim * (i + 1)].astype(BF16)
    ones_row = (lax.broadcasted_iota(jnp.int32, (V7X_BF16_ROWS, t), 0) == 0).astype(F32)
    for g in range(ATTN_KV_HEADS):
        v_t = pt[_VT_OFF + hd * g:_VT_OFF + hd * (g + 1)]
        vt_ref[g] = jnp.concatenate([v_t, ones_row], axis=0).astype(BF16)
    iwt_ref[...] = pt[_IWT_OFF:_IWT_OFF + IDX_HEADS] * (IDX_HEADS ** -0.5)

    proj = jnp.dot(h, wtok_ref[...], preferred_element_type=F32)
    for g in range(ATTN_KV_HEADS):
        kk_ref[g] = proj[:, _K_OFF + hd * g:_K_OFF + hd * (g + 1)].astype(BF16)
    misc = proj[:, _MISC_OFF:_MISC_OFF + V7X_LANES]
    ik_ref[...] = misc[:, :IDX_DIM].astype(BF16)
    gqk_ref[...] = proj[:, _GQ_OFF:_GV_OFF]
    gv_ref[...] = proj[:, _GV_OFF:_GG_OFF].astype(BF16)
    gg_ref[...] = proj[:, _GG_OFF:_GG_OFF + GLA_HEADS * GLA_DV]
    z = jnp.dot(misc, wa2_ref[...], preferred_element_type=F32, precision=lax.Precision.HIGHEST)
    z = z + ba_ref[...]
    log_sig = jnp.minimum(z, 0.0) - jnp.log1p(jnp.exp(-jnp.abs(z)))
    la_ref[...] = log_sig * (1.0 / GLA_TAU)


def _pack_w_in(w_in):
    d = w_in.shape[0]
    kv_w = ATTN_KV_HEADS * ATTN_HEAD_DIM
    sizes = (ATTN_HEADS * ATTN_HEAD_DIM, kv_w, kv_w, IDX_HEADS * IDX_DIM, IDX_DIM, IDX_HEADS,
             GLA_HEADS * GLA_DK, GLA_HEADS * GLA_DK, GLA_HEADS * GLA_DV, GLA_GATE_RANK, GLA_HEADS * GLA_DV)
    assert sum(sizes) == w_in.shape[1]
    offs = [0]
    for s in sizes:
        offs.append(offs[-1] + s)
    aq, ak, av, iq, ik, iw, gq, gk, gv, ga, gg = [w_in[:, offs[i]:offs[i + 1]] for i in range(len(sizes))]
    z = lambda n: jnp.zeros((d, n), w_in.dtype)
    misc_pad = V7X_LANES - _GA_LANE - GLA_GATE_RANK
    w_tok = jnp.concatenate([ak, ik, z(IDX_HEADS), ga, z(misc_pad), gq * (GLA_DK ** -0.5), gk, gv, gg], axis=1)
    w_t = jnp.concatenate([aq * (ATTN_HEAD_DIM ** -0.5), iq * (IDX_DIM ** -0.5), av, iw,
                           z(V7X_BF16_ROWS - IDX_HEADS)], axis=1).T
    assert w_tok.shape[1] == _TOK_COLS and w_t.shape[0] == _T_ROWS
    return w_tok.astype(BF16), w_t.astype(BF16)


def _inproj(x1, g_mix_pre, w_in, w_gla_a2, b_gla_a):
    n, d = x1.shape
    t = FFN_ROWS
    assert n % t == 0
    w_tok, w_t = _pack_w_in(w_in)
    nqk = GLA_HEADS * GLA_DK
    nv = GLA_HEADS * GLA_DV
    wa2 = jnp.zeros((V7X_LANES, nqk), F32).at[_GA_LANE:_GA_LANE + GLA_GATE_RANK].set(w_gla_a2)
    row = lambda i: (i, 0)
    row3 = lambda i: (0, i, 0)
    col3 = lambda i: (0, 0, i)
    out_shape = (
        jax.ShapeDtypeStruct((ATTN_HEADS, ATTN_HEAD_DIM, n), BF16),
        jax.ShapeDtypeStruct((IDX_HEADS, IDX_DIM, n), BF16),
        jax.ShapeDtypeStruct((ATTN_KV_HEADS, VT_ROWS, n), BF16),
        jax.ShapeDtypeStruct((IDX_HEADS, n), F32),
        jax.ShapeDtypeStruct((ATTN_KV_HEADS, n, ATTN_HEAD_DIM), BF16),
        jax.ShapeDtypeStruct((n, IDX_DIM), BF16),
        jax.ShapeDtypeStruct((n, 2 * nqk), F32),
        jax.ShapeDtypeStruct((n, nv), BF16),
        jax.ShapeDtypeStruct((n, nv), F32),
        jax.ShapeDtypeStruct((n, nqk), F32),
    )
    out_specs = (
        pl.BlockSpec((ATTN_HEADS, ATTN_HEAD_DIM, t), col3),
        pl.BlockSpec((IDX_HEADS, IDX_DIM, t), col3),
        pl.BlockSpec((ATTN_KV_HEADS, VT_ROWS, t), col3),
        pl.BlockSpec((IDX_HEADS, t), lambda i: (0, i)),
        pl.BlockSpec((ATTN_KV_HEADS, t, ATTN_HEAD_DIM), row3),
        pl.BlockSpec((t, IDX_DIM), row),
        pl.BlockSpec((t, 2 * nqk), row),
        pl.BlockSpec((t, nv), row),
        pl.BlockSpec((t, nv), row),
        pl.BlockSpec((t, nqk), row),
    )
    est = (2 * t * d * 4 + d * (_TOK_COLS + _T_ROWS) * 2 + t * (_TOK_COLS + _T_ROWS) * 4 * 2
           + 2 * t * 8 * 1024)
    return pl.pallas_call(
        _inproj_kernel,
        out_shape=out_shape,
        grid=(n // t,),
        in_specs=[
            pl.BlockSpec((t, d), row),
            _const_spec((1, d)),
            _const_spec(w_tok.shape),
            _const_spec(w_t.shape),
            _const_spec(wa2.shape),
            _const_spec((1, nqk)),
        ],
        out_specs=out_specs,
        compiler_params=pltpu.CompilerParams(
            dimension_semantics=("arbitrary",), vmem_limit_bytes=_vmem_limit(est)),
    )(x1, g_mix_pre.reshape(1, d), w_tok, w_t, wa2, b_gla_a.reshape(1, nqk))


def _dsa_kernel(qt_ref, iqt_ref, iwt_ref, kk_ref, vt_ref, ik_ref, o_ref,
                key_s, half_s, low_s, cnt_s, seen_s, bias_s, acc_s, ot_s, *, topk):
    tq = DSA_TQ
    kc = DSA_KC
    sub = V7X_SUBLANES
    i = pl.program_id(1)
    nchunks = i + 1

    def score_logits(j):
        ikc = ik_ref[pl.ds(pl.multiple_of(j * kc, kc), kc), :]
        iqcat = jnp.concatenate([iqt_ref[h] for h in range(IDX_HEADS)], axis=1)
        return jnp.dot(ikc, iqcat, preferred_element_type=F32)

    def score_keys(j, lg, diagonal):
        sc = None
        for h in range(IDX_HEADS):
            term = jnp.maximum(lg[:, h * tq:(h + 1) * tq], 0.0) * iwt_ref[h:h + 1, :]
            sc = term if sc is None else sc + term
        bits = pltpu.bitcast(sc, jnp.int32)
        key = jnp.where(bits < 0, INT_MIN - bits, bits)
        if diagonal:
            kpos = lax.broadcasted_iota(jnp.int32, (kc, tq), 0)
            qpos = lax.broadcasted_iota(jnp.int32, (kc, tq), 1)
            key = jnp.where(kpos <= qpos, key, INT_MIN)
        key_s[j] = key
        half_s[j] = lax.shift_right_arithmetic(key, 16).astype(jnp.int16)
        low_s[j] = key.astype(jnp.int16) ^ jnp.int16(INT16_MIN)

    def sweep_chunks(n, group_fn, gsz):
        assert gsz & (gsz - 1) == 0

        def trip(q, c):
            group_fn(q * gsz, gsz)
            return c

        lax.fori_loop(0, lax.shift_right_logical(n, gsz.bit_length() - 1), trip, 0)
        done = n & -gsz
        part = gsz // 2
        while part:
            pl.when((n & part) != 0)(functools.partial(group_fn, done, part))
            done = done + (n & part)
            part //= 2

    def score_group(j0, count):
        lgs = [score_logits(j0 + s) for s in range(count)]
        for s in range(count):
            score_keys(j0 + s, lgs[s], False)

    sweep_chunks(i, score_group, DSA_SCORE_GROUP)
    score_keys(i, score_logits(i), True)

    qidx = i * tq + lax.broadcasted_iota(jnp.int32, (1, tq), 1)
    krow = jnp.minimum(topk, qidx + 1).astype(F32)

    def count(ind_fn):
        def body(j, acc):
            parts = [acc, None, None, None]
            for s in range(kc // sub):
                ind = ind_fn(key_s[j, s * sub:(s + 1) * sub, :], j * kc + s * sub)
                p = s % 4
                parts[p] = ind if parts[p] is None else parts[p] + ind
            return (parts[0] + parts[1]) + (parts[2] + parts[3])
        acc = lax.fori_loop(0, nchunks, body, jnp.zeros((sub, tq), F32))
        return jnp.sum(acc, axis=0, keepdims=True)

    def count_ge(cand):
        cb = jnp.broadcast_to(cand, (sub, tq))
        return count(lambda k, base: jnp.where(k >= cb, 1.0, 0.0))

    rows16 = V7X_BF16_ROWS

    def count16(cand):
        cb = jnp.broadcast_to(cand, (rows16, tq)).astype(jnp.int16)
        one, nil = jnp.int16(1), jnp.int16(0)
        cnt_s[...] = jnp.zeros_like(cnt_s)

        def group(j0, count):
            parts = [None] * 4
            for s in range(count * kc // rows16):
                c, r = divmod(s, kc // rows16)
                ind = jnp.where(half_s[j0 + c, r * rows16:(r + 1) * rows16, :] >= cb, one, nil)
                p = s % 4
                parts[p] = ind if parts[p] is None else parts[p] + ind
            cnt_s[...] += (parts[0] + parts[1]) + (parts[2] + parts[3])

        sweep_chunks(nchunks, group, DSA_COUNT_GROUP)
        return jnp.sum(cnt_s[...].astype(F32), axis=0, keepdims=True)

    def search16(count_at_min):
        zero = jnp.zeros((1, tq), jnp.int32)
        c0 = count16(zero)
        start = (jnp.where(c0 >= krow, zero, INT16_MIN), jnp.where(c0 >= krow, c0, count_at_min))

        def bit_body(it, carry):
            th, cth = carry
            cand = th | lax.shift_left(jnp.int32(1), 14 - it)
            c = count16(cand)
            return jnp.where(c >= krow, cand, th), jnp.where(c >= krow, c, cth)

        return lax.fori_loop(0, 15, bit_body, start)

    t_hi, c_hi = search16(jnp.full((1, tq), 1.0, F32) * (nchunks * kc).astype(F32))
    th16 = jnp.broadcast_to(t_hi, (kc, tq)).astype(jnp.int16)

    def low_halves(j0, count):
        for j in range(count):
            hi = half_s[j0 + j]
            half_s[j0 + j] = jnp.where(hi == th16, low_s[j0 + j],
                                       jnp.where(hi > th16, jnp.int16(INT16_MAX), jnp.int16(INT16_MIN)))

    sweep_chunks(nchunks, low_halves, DSA_SCORE_GROUP)
    t_lo, c_ge = search16(c_hi)
    t = lax.shift_left(t_hi, 16) | (t_lo - INT16_MIN)

    @pl.when(jnp.max(c_ge - krow) > 0.5)
    def _():
        saturated = jnp.max(jnp.where(t_lo == INT16_MAX, 1.0, 0.0)) > 0.5
        c_gt = lax.cond(saturated, lambda: count_ge(t + 1), lambda: count16(jnp.minimum(t_lo + 1, INT16_MAX)))
        need = krow - c_gt
        tb_full = jnp.broadcast_to(t, (kc, tq))
        tri = (lax.broadcasted_iota(jnp.int32, (kc, kc), 0)
               >= lax.broadcasted_iota(jnp.int32, (kc, kc), 1)).astype(BF16)

        seen_s[...] = jnp.zeros_like(seen_s)

        def drop_surplus(j0, count):
            ks = [key_s[j0 + s] for s in range(count)]
            prefix = [jnp.dot(tri, jnp.where(k == tb_full, 1.0, 0.0).astype(BF16), preferred_element_type=F32)
                      for k in ks]
            seen = seen_s[...]
            for s in range(count):
                rank = seen + prefix[s]
                key_s[j0 + s] = jnp.where(ks[s] == tb_full, jnp.where(rank > need, INT_MIN, ks[s]), ks[s])
                seen = rank[kc - 1:kc, :]
            seen_s[...] = seen

        sweep_chunks(nchunks, drop_surplus, DSA_SCORE_GROUP)

    def masked_scores(j, g, slot=0):
        koff = pl.multiple_of(j * kc, kc)
        qcat = jnp.concatenate([qt_ref[g * ATTN_REP + r] for r in range(ATTN_REP)], axis=1)
        bias = jnp.concatenate([bias_s[slot]] * ATTN_REP, axis=1)
        kg = kk_ref[g, pl.ds(koff, kc), :]
        vg = vt_ref[g, :, pl.ds(koff, kc)]
        return jnp.dot(kg, qcat, preferred_element_type=F32) + bias, vg

    def set_bias(j, slot=0):
        bias_s[slot] = jnp.where(key_s[j] >= jnp.broadcast_to(t, (kc, tq)), 0.0, NEG_BIG)

    acc_s[...] = jnp.zeros_like(acc_s)

    def att_group(j0, count):
        units = [(slot, g) for slot in range(count) for g in range(ATTN_KV_HEADS)]
        for slot in range(count):
            set_bias(j0 + slot, slot)
        scores = {}

        def issue(u):
            slot, g = units[u]
            scores[u] = masked_scores(j0 + slot, g, slot)

        for u in range(min(DSA_ATT_LEAD, len(units))):
            issue(u)
        for u, (slot, g) in enumerate(units):
            s, vg = scores.pop(u)
            acc_s[g] += jnp.dot(vg, jnp.exp(s).astype(BF16), preferred_element_type=F32)
            if u + DSA_ATT_LEAD < len(units):
                issue(u + DSA_ATT_LEAD)

    sweep_chunks(nchunks, att_group, DSA_ATT_GROUP)

    norm = acc_s[:, ATTN_HEAD_DIM:ATTN_HEAD_DIM + 1, :]
    in_range = jnp.where(norm >= SOFTMAX_NORM_MIN, jnp.where(norm <= SOFTMAX_NORM_MAX, 1.0, 0.0), 0.0)

    @pl.when(jnp.min(in_range) < 0.5)
    def _():
        acc_s[...] = jnp.zeros_like(acc_s)

        def online_body(j, ms):
            set_bias(j)
            out = []
            for g in range(ATTN_KV_HEADS):
                s, vg = masked_scores(j, g)
                m_new = jnp.maximum(ms[g], jnp.max(s, axis=0, keepdims=True))
                p = jnp.exp(s - m_new).astype(BF16)
                pv = jnp.dot(vg, p, preferred_element_type=F32)
                acc_s[g] = jnp.exp(ms[g] - m_new) * acc_s[g] + pv
                out.append(m_new)
            return tuple(out)

        m_init = tuple(jnp.full((1, ATTN_REP * tq), NEG_BIG, F32) for _ in range(ATTN_KV_HEADS))
        lax.fori_loop(0, nchunks, online_body, m_init)

    for h in range(ATTN_HEADS):
        g, r = divmod(h, ATTN_REP)
        a = acc_s[g, :, r * tq:(r + 1) * tq]
        ot_s[h * ATTN_HEAD_DIM:(h + 1) * ATTN_HEAD_DIM, :] = (
            a[:ATTN_HEAD_DIM] / a[ATTN_HEAD_DIM:ATTN_HEAD_DIM + 1])
    o_ref[...] = ot_s[...].T.astype(o_ref.dtype)


def _dsa(qt, iqt, iwt, kk, vt, ik, batch, seq):
    n = batch * seq
    tq, kc = DSA_TQ, DSA_KC
    assert seq % tq == 0
    nq = seq // tq
    topk = min(TOPK_MAX, seq // 4)
    nch = seq // kc
    nout = ATTN_HEADS * ATTN_HEAD_DIM
    qmap = lambda b, i: (0, 0, b * nq + i)
    est = (nch * kc * tq * 8 + 3 * DSA_ATT_GROUP * ATTN_REP * kc * tq * 4 + ATTN_HEADS * (VT_ROWS + 8) * tq * 4 + nout * tq * 4
           + ATTN_KV_HEADS * seq * (V7X_LANES + VT_ROWS) * 2 + seq * V7X_LANES * 2
           + 2 * (2 * ATTN_HEADS * ATTN_HEAD_DIM * tq * 2 + 8 * tq * 4 + tq * nout * 2))
    return pl.pallas_call(
        functools.partial(_dsa_kernel, topk=topk),
        out_shape=jax.ShapeDtypeStruct((n, nout), BF16),
        grid=(batch, nq),
        in_specs=[
            pl.BlockSpec((ATTN_HEADS, ATTN_HEAD_DIM, tq), qmap),
            pl.BlockSpec((IDX_HEADS, IDX_DIM, tq), qmap),
            pl.BlockSpec((IDX_HEADS, tq), lambda b, i: (0, b * nq + i)),
            pl.BlockSpec((ATTN_KV_HEADS, seq, ATTN_HEAD_DIM), lambda b, i: (0, b, 0),
                         pipeline_mode=pl.Buffered(1)),
            pl.BlockSpec((ATTN_KV_HEADS, VT_ROWS, seq), lambda b, i: (0, 0, b),
                         pipeline_mode=pl.Buffered(1)),
            pl.BlockSpec((seq, IDX_DIM), lambda b, i: (b, 0), pipeline_mode=pl.Buffered(1)),
        ],
        out_specs=pl.BlockSpec((tq, nout), lambda b, i: (b * nq + i, 0)),
        scratch_shapes=[
            pltpu.VMEM((nch, kc, tq), jnp.int32),
            pltpu.VMEM((nch, kc, tq), jnp.int16),
            pltpu.VMEM((nch, kc, tq), jnp.int16),
            pltpu.VMEM((V7X_BF16_ROWS, tq), jnp.int16),
            pltpu.VMEM((1, tq), F32),
            pltpu.VMEM((DSA_ATT_GROUP, kc, tq), F32),
            pltpu.VMEM((ATTN_KV_HEADS, VT_ROWS, ATTN_REP * tq), F32),
            pltpu.VMEM((nout, tq), F32),
        ],
        compiler_params=pltpu.CompilerParams(
            dimension_semantics=("arbitrary", "arbitrary"), vmem_limit_bytes=_vmem_limit(est)),
    )(qt, iqt, iwt, kk, vt, ik)


def _gla_kernel(gqk_ref, gv_ref, la_ref, gg_ref, gn_ref, o_ref, st_s):
    @pl.when(pl.program_id(1) == 0)
    def _():
        st_s[...] = jnp.zeros_like(st_s)

    c = GLA_CHUNK
    nqk = GLA_HEADS * GLA_DK
    r_i = lax.broadcasted_iota(jnp.int32, (c, c), 0)
    c_i = lax.broadcasted_iota(jnp.int32, (c, c), 1)
    tri = r_i >= c_i
    gn = gn_ref[...]
    nchunk = GLA_ROWS // c
    heads = range(GLA_HEADS)
    hs = [slice(h * GLA_DK, (h + 1) * GLA_DK) for h in heads]
    vs = [slice(h * GLA_DV, (h + 1) * GLA_DV) for h in heads]

    local = []
    for ci in range(nchunk):
        rows = slice(ci * c, (ci + 1) * c)
        la = la_ref[rows, :]
        b = la
        for sh in [1 << p for p in range((c - 1).bit_length())]:
            b = b + jnp.where(r_i[:, :1] >= sh, pltpu.roll(b, sh, axis=0), 0.0)
        b_last = b[c - 1:c, :]
        q = gqk_ref[rows, :nqk]
        k = gqk_ref[rows, nqk:]
        q_dec = (q * jnp.exp(b)).astype(BF16)
        k_in = (k * jnp.exp(-b)).astype(BF16)
        k_out = (k * jnp.exp(b_last - b)).astype(BF16)
        decay = jnp.exp(b_last)
        intra, upd = [], []
        for h in heads:
            v = gv_ref[rows, vs[h]]
            a = lax.dot_general(q_dec[:, hs[h]], k_in[:, hs[h]], NT_DIMS, preferred_element_type=F32)
            a = jnp.where(tri, a, 0.0).astype(BF16)
            intra.append(jnp.dot(a, v, preferred_element_type=F32))
            upd.append(lax.dot_general(v, k_out[:, hs[h]], TN_DIMS, preferred_element_type=F32))
        local.append((rows, q_dec, decay, intra, upd))

    st = [st_s[h] for h in heads]
    for rows, q_dec, decay, intra, upd in local:
        for h in heads:
            o = intra[h] + lax.dot_general(q_dec[:, hs[h]], st[h].astype(BF16), NT_DIMS,
                                           preferred_element_type=F32)
            st[h] = st[h] * decay[:, hs[h]] + upd[h]
            gate = gg_ref[rows, vs[h]]
            o_ref[rows, vs[h]] = (_rms(o, gn) * (gate * jax.nn.sigmoid(gate))).astype(o_ref.dtype)
    for h in heads:
        st_s[h] = st[h]


def _gla(gqk, gv, la, gg, g_norm, batch, seq):
    n = batch * seq
    t = GLA_ROWS
    assert seq % t == 0
    ns = seq // t
    row = lambda b, i: (b * ns + i, 0)
    nqk = GLA_HEADS * GLA_DK
    nv = GLA_HEADS * GLA_DV
    return pl.pallas_call(
        _gla_kernel,
        out_shape=jax.ShapeDtypeStruct((n, nv), BF16),
        grid=(batch, ns),
        in_specs=[
            pl.BlockSpec((t, 2 * nqk), row),
            pl.BlockSpec((t, nv), row),
            pl.BlockSpec((t, nqk), row),
            pl.BlockSpec((t, nv), row),
            _const_spec((1, GLA_DV)),
        ],
        out_specs=pl.BlockSpec((t, nv), row),
        scratch_shapes=[pltpu.VMEM((GLA_HEADS, GLA_DV, GLA_DK), F32)],
        compiler_params=pltpu.CompilerParams(dimension_semantics=("arbitrary", "arbitrary")),
    )(gqk, gv, la, gg, g_norm.reshape(1, GLA_DV))


def kernel(x, g_ffn1_pre, w_ffn1_gate, w_ffn1_up, w_ffn1_down, g_ffn1_post, g_mix_pre, w_in, w_gla_a2,
           b_gla_a, g_gla_norm, w_out, g_mix_post, g_ffn2_pre, w_ffn2_gate, w_ffn2_up, w_ffn2_down,
           g_ffn2_post):
    batch, seq, d = x.shape
    h = x.reshape(batch * seq, d)
    for l in range(g_ffn1_pre.shape[0]):
        h = _ffn(h, g_ffn1_pre[l], w_ffn1_gate[l], w_ffn1_up[l], w_ffn1_down[l], g_ffn1_post[l])
        qt, iqt, vt, iwt, kk, ik, gqk, gv, gg, la = _inproj(h, g_mix_pre[l], w_in[l], w_gla_a2[l], b_gla_a[l])
        oa = _dsa(qt, iqt, iwt, kk, vt, ik, batch, seq)
        og = _gla(gqk, gv, la, gg, g_gla_norm[l], batch, seq)
        h = _mix_ffn(h, oa, og, w_out[l], g_mix_post[l],
                     g_ffn2_pre[l], w_ffn2_gate[l], w_ffn2_up[l], w_ffn2_down[l], g_ffn2_post[l])
    return h.reshape(batch, seq, d)
```
